```python
import jax, jax.numpy as jnp
from jax import lax
import numpy as np

D_MODEL = 1024
BATCH = 2
SEQ = 8192
DEPTH = 1
DEC_BATCH = 128
DEC_SEQ = 4
PAST_LEN = 8192
PAGE_SIZE = 128

N_META = 16
D_MLSTM = D_MODEL // 2
D_SWA = D_MODEL - D_MLSTM
M_HEADS = 4
M_DH = D_MLSTM // M_HEADS
A_HEADS = 8
A_KV_HEADS = 2
A_GROUP = A_HEADS // A_KV_HEADS
A_DH = D_SWA // A_HEADS
KV_W = A_KV_HEADS * A_DH
WINDOW = 128
CHUNK = 128
LN_EPS = 1e-5
DN_ALPHA = (2.0 * DEPTH) ** 0.25
DN_BETA = (8.0 * DEPTH) ** -0.25
PROJ_SIZES = (D_MLSTM, D_MLSTM, D_MLSTM, D_MLSTM, D_MLSTM, M_HEADS, M_HEADS, D_SWA, KV_W, KV_W, D_SWA)
N_IN = sum(PROJ_SIZES)
F_GATE_OFF = 5 * D_MLSTM + M_HEADS

kernel_name = "hymba_mlstm_swa_sink_decode_step"


def layer_norm(x, g, b):
    xf = x.astype(jnp.float32)
    mu = jnp.mean(xf, -1, keepdims=True)
    var = jnp.mean(jnp.square(xf - mu), -1, keepdims=True)
    y = (xf - mu) * lax.rsqrt(var + LN_EPS) * g.astype(jnp.float32) + b.astype(jnp.float32)
    return y.astype(x.dtype)


def split_proj(u):
    parts, off = [], 0
    for n in PROJ_SIZES:
        parts.append(u[..., off:off + n])
        off += n
    return parts


def mlstm_inputs(mq, mk, mv, mi, mf):
    f32 = jnp.float32
    shp = mq.shape[:2] + (M_HEADS, M_DH)
    q = mq.astype(f32).reshape(shp)
    k = mk.astype(f32).reshape(shp) * (M_DH ** -0.5)
    v = mv.astype(f32).reshape(shp)
    log_i = mi.astype(f32)
    log_f = jax.nn.log_sigmoid(mf.astype(f32))
    return q, k, v, log_i, log_f


def mlstm_chunk(state, q, k, v, log_i, log_f):
    c, n, m = state
    L = q.shape[1]
    b = jnp.cumsum(log_f, axis=1)
    dmat = b[:, :, None, :] - b[:, None, :, :] + log_i[:, None, :, :]
    causal = jnp.tril(jnp.ones((L, L), bool))[:, :, None]
    dmat = jnp.where(causal, dmat, -jnp.inf)
    inter = b + m[:, None, :]
    m_t = jnp.maximum(inter, jnp.max(dmat, axis=2))
    w = jnp.exp(dmat - m_t[:, :, None, :])
    s_inter = jnp.exp(inter - m_t)
    qk = jnp.einsum('bthd,bshd->btsh', q, k) * w
    num = jnp.einsum('btsh,bshd->bthd', qk, v) + jnp.einsum('bthk,bhkv->bthv', q, c) * s_inter[..., None]
    den = jnp.sum(qk, axis=2) + jnp.einsum('bthk,bhk->bth', q, n) * s_inter
    h = num / jnp.maximum(jnp.abs(den), jnp.exp(-m_t))[..., None]
    b_last = b[:, -1]
    m_new = m_t[:, -1]
    w_state = jnp.exp(b_last[:, None, :] - b + log_i - m_new[:, None, :])
    decay = jnp.exp(b_last + m - m_new)
    c_new = decay[..., None, None] * c + jnp.einsum('bsh,bshk,bshv->bhkv', w_state, k, v)
    n_new = decay[..., None] * n + jnp.einsum('bsh,bshk->bhk', w_state, k)
    return (c_new, n_new, m_new), h


def mlstm_prompt(q, k, v, log_i, log_f):
    f32 = jnp.float32
    B, T = q.shape[:2]
    n_chunks = (T - N_META) // CHUNK
    st = (jnp.zeros((B, M_HEADS, M_DH, M_DH), f32), jnp.zeros((B, M_HEADS, M_DH), f32),
          jnp.zeros((B, M_HEADS), f32))
    st, h_meta = mlstm_chunk(st, q[:, :N_META], k[:, :N_META], v[:, :N_META],
                             log_i[:, :N_META], log_f[:, :N_META])

    def to_chunks(a):
        a = a[:, N_META:]
        return jnp.swapaxes(a.reshape((B, n_chunks, CHUNK) + a.shape[2:]), 0, 1)

    def body(carry, xs):
        return mlstm_chunk(carry, *xs)

    st, h_real = lax.scan(body, st, tuple(to_chunks(a) for a in (q, k, v, log_i, log_f)))
    h_real = jnp.swapaxes(h_real, 0, 1).reshape(B, T - N_META, M_HEADS, M_DH)
    return jnp.concatenate([h_meta, h_real], axis=1), st


def swa_inputs(aq, ak, av):
    f32 = jnp.float32
    B, T = aq.shape[:2]
    q = aq.astype(f32).reshape(B, T, A_KV_HEADS, A_GROUP, A_DH)
    k = ak.astype(f32).reshape(B, T, A_KV_HEADS, A_DH)
    v = av.astype(f32).reshape(B, T, A_KV_HEADS, A_DH)
    return q, k, v


def sink_softmax(s, sink):
    sk = sink.astype(jnp.float32)[:, :, None, None]
    mx = jnp.maximum(jnp.max(s, axis=-1, keepdims=True), sk)
    p = jnp.exp(s - mx)
    return p / (jnp.sum(p, axis=-1, keepdims=True) + jnp.exp(sk - mx))


def swa_prompt(q, k, v, sink):
    B, T = q.shape[:2]
    n_real = T - N_META
    nb = n_real // WINDOW
    q = q * (A_DH ** -0.5)
    qm, km, vm = q[:, :N_META], k[:, :N_META], v[:, :N_META]
    s = jnp.einsum('bqkgd,bskd->bkgqs', qm, km)
    s = jnp.where(jnp.tril(jnp.ones((N_META, N_META), bool)), s, -jnp.inf)
    o_meta = jnp.einsum('bkgqs,bskd->bqkgd', sink_softmax(s, sink), vm)
    qb = q[:, N_META:].reshape(B, nb, WINDOW, A_KV_HEADS, A_GROUP, A_DH)
    kb = k[:, N_META:].reshape(B, nb, WINDOW, A_KV_HEADS, A_DH)
    vb = v[:, N_META:].reshape(B, nb, WINDOW, A_KV_HEADS, A_DH)
    pad = ((0, 0), (1, 0), (0, 0), (0, 0), (0, 0))
    kprev = jnp.pad(kb, pad)[:, :-1]
    vprev = jnp.pad(vb, pad)[:, :-1]
    kmeta = jnp.broadcast_to(km[:, None], (B, nb, N_META, A_KV_HEADS, A_DH))
    vmeta = jnp.broadcast_to(vm[:, None], (B, nb, N_META, A_KV_HEADS, A_DH))
    kcat = jnp.concatenate([kmeta, kprev, kb], axis=2)
    vcat = jnp.concatenate([vmeta, vprev, vb], axis=2)
    diff = jnp.arange(WINDOW)[:, None] - (jnp.arange(2 * WINDOW)[None, :] - WINDOW)
    band = (diff >= 0) & (diff < WINDOW)
    has_prev = (jnp.arange(nb) > 0)[:, None, None]
    prev_col = (jnp.arange(2 * WINDOW) < WINDOW)[None, None, :]
    band_b = band[None] & (has_prev | ~prev_col)
    mask = jnp.concatenate([jnp.ones((nb, WINDOW, N_META), bool), band_b], axis=-1)
    s = jnp.einsum('bnqkgd,bnskd->bnkgqs', qb, kcat)
    s = jnp.where(mask[None, :, None, None], s, -jnp.inf)
    o_real = jnp.einsum('bnkgqs,bnskd->bnqkgd', sink_softmax(s, sink), vcat)
    o_real = o_real.reshape(B, n_real, A_KV_HEADS, A_GROUP, A_DH)
    return jnp.concatenate([o_meta, o_real], axis=1)


def swa_sample(q, k, v, ck, cv, sink):
    f32 = jnp.float32
    L = q.shape[1]
    kk = jnp.concatenate([ck.astype(f32), k], axis=1)
    vv = jnp.concatenate([cv.astype(f32), v], axis=1)
    qpos = PAST_LEN + jnp.arange(L)
    buf_pos = PAST_LEN - WINDOW + jnp.arange(WINDOW)
    meta_ok = jnp.ones((L, N_META), bool)
    win_ok = ((qpos[:, None] - buf_pos[None, :]) < WINDOW) & (buf_pos[None, :] >= N_META)
    dnew = jnp.arange(L)[:, None] - jnp.arange(L)[None, :]
    new_ok = (dnew >= 0) & (dnew < WINDOW)
    mask = jnp.concatenate([meta_ok, win_ok, new_ok], axis=1)
    s = jnp.einsum('bqkgd,bskd->bkgqs', q * (A_DH ** -0.5), kk)
    s = jnp.where(mask, s, -jnp.inf)
    o = jnp.einsum('bkgqs,bskd->bqkgd', sink_softmax(s, sink), vv)
    new_k = jnp.concatenate([ck[:, :N_META],
                             jnp.concatenate([ck[:, N_META:], k.astype(ck.dtype)], axis=1)[:, -WINDOW:]], axis=1)
    new_v = jnp.concatenate([cv[:, :N_META],
                             jnp.concatenate([cv[:, N_META:], v.astype(cv.dtype)], axis=1)[:, -WINDOW:]], axis=1)
    return o, new_k, new_v


def mix_and_residual(x, h_m, o_pre, z_m, h_a, z_a, norm_g, w_o, g, b):
    f32 = jnp.float32
    B, T = x.shape[:2]
    h_m = h_m * jax.nn.sigmoid(o_pre.astype(f32)).reshape(h_m.shape)
    mu = jnp.mean(h_m, -1, keepdims=True)
    var = jnp.mean(jnp.square(h_m - mu), -1, keepdims=True)
    h_m = ((h_m - mu) * lax.rsqrt(var + LN_EPS)).reshape(B, T, D_MLSTM) * norm_g.astype(f32)
    y_m = h_m * jax.nn.silu(z_m.astype(f32))
    y_a = h_a.reshape(B, T, D_SWA) * jax.nn.silu(z_a.astype(f32))
    mix = jnp.concatenate([y_m, y_a], axis=-1).astype(w_o.dtype) @ w_o
    return layer_norm(DN_ALPHA * x + mix.astype(x.dtype), g, b)


def setup_inputs(seed: int = 0) -> dict:
    key = jax.random.key(seed)
    ks = jax.random.split(key, 18)
    f32 = jnp.float32

    def nrm(k, shape, scale=1.0):
        return scale * jax.random.normal(k, shape, f32)

    buf = (DEPTH, DEC_BATCH, N_META + WINDOW, A_KV_HEADS, A_DH)
    b_in = nrm(ks[10], (DEPTH, N_IN), 0.02)
    b_in = b_in.at[:, F_GATE_OFF:F_GATE_OFF + M_HEADS].add(jnp.linspace(3.0, 6.0, M_HEADS, dtype=f32))
    return {
        "x_prompt": nrm(ks[0], (BATCH, SEQ, D_MODEL)),
        "x_sample": nrm(ks[1], (DEC_BATCH, DEC_SEQ, D_MODEL)),
        "cache_swa_k": nrm(ks[2], buf),
        "cache_swa_v": nrm(ks[3], buf),
        "state_mlstm_c": nrm(ks[4], (DEPTH, DEC_BATCH, M_HEADS, M_DH, M_DH), 0.3),
        "state_mlstm_n": nrm(ks[5], (DEPTH, DEC_BATCH, M_HEADS, M_DH), 0.3),
        "state_mlstm_m": nrm(ks[6], (DEPTH, DEC_BATCH, M_HEADS), 0.5),
        "meta_tokens": nrm(ks[7], (N_META, D_MODEL)),
        "ln0_g": 1.0 + nrm(ks[8], (D_MODEL,), 0.02),
        "ln0_b": nrm(ks[9], (D_MODEL,), 0.02),
        "w_in": nrm(ks[11], (DEPTH, D_MODEL, N_IN), D_MODEL ** -0.5),
        "b_in": b_in,
        "a_sinks": nrm(ks[12], (DEPTH, A_HEADS), 0.5),
        "m_norm_g": 1.0 + nrm(ks[13], (DEPTH, D_MLSTM), 0.02),
        "w_out": nrm(ks[14], (DEPTH, D_MODEL, D_MODEL), DN_BETA * D_MODEL ** -0.5),
        "ln_g": 1.0 + nrm(ks[15], (DEPTH, D_MODEL), 0.02),
        "ln_b": nrm(ks[16], (DEPTH, D_MODEL), 0.02),
    }


def reference(x_prompt, x_sample, cache_swa_k, cache_swa_v, state_mlstm_c, state_mlstm_n, state_mlstm_m,
              meta_tokens, ln0_g, ln0_b, w_in, b_in, a_sinks, m_norm_g, w_out, ln_g, ln_b):
    f32 = jnp.float32
    B = x_prompt.shape[0]
    meta = jnp.broadcast_to(meta_tokens.astype(x_prompt.dtype)[None], (B, N_META, D_MODEL))
    hp = layer_norm(jnp.concatenate([meta, x_prompt], axis=1), ln0_g, ln0_b)
    hs = layer_norm(x_sample, ln0_g, ln0_b)
    pk, pv, pc, pn, pm = [], [], [], [], []
    sk, sv, sc, sn, sm = [], [], [], [], []
    for l in range(DEPTH):
        sink = a_sinks[l].reshape(A_KV_HEADS, A_GROUP)
        mq, mk, mv, mo, mz, mi, mf, aq, ak, av, az = split_proj(hp @ w_in[l] + b_in[l])
        h_m, (c1, n1, m1) = mlstm_prompt(*mlstm_inputs(mq, mk, mv, mi, mf))
        q_a, k_a, v_a = swa_inputs(aq, ak, av)
        h_a = swa_prompt(q_a, k_a, v_a, sink)
        pk.append(jnp.concatenate([k_a[:, :N_META], k_a[:, -WINDOW:]], axis=1).astype(cache_swa_k.dtype))
        pv.append(jnp.concatenate([v_a[:, :N_META], v_a[:, -WINDOW:]], axis=1).astype(cache_swa_v.dtype))
        pc.append(c1.astype(state_mlstm_c.dtype))
        pn.append(n1.astype(state_mlstm_n.dtype))
        pm.append(m1.astype(state_mlstm_m.dtype))
        hp = mix_and_residual(hp, h_m, mo, mz, h_a, az, m_norm_g[l], w_out[l], ln_g[l], ln_b[l])
        mq, mk, mv, mo, mz, mi, mf, aq, ak, av, az = split_proj(hs @ w_in[l] + b_in[l])
        st0 = (state_mlstm_c[l].astype(f32), state_mlstm_n[l].astype(f32), state_mlstm_m[l].astype(f32))
        (c2, n2, m2), h_m = mlstm_chunk(st0, *mlstm_inputs(mq, mk, mv, mi, mf))
        q_a, k_a, v_a = swa_inputs(aq, ak, av)
        h_a, nk, nv = swa_sample(q_a, k_a, v_a, cache_swa_k[l], cache_swa_v[l], sink)
        sk.append(nk)
        sv.append(nv)
        sc.append(c2.astype(state_mlstm_c.dtype))
        sn.append(n2.astype(state_mlstm_n.dtype))
        sm.append(m2.astype(state_mlstm_m.dtype))
        hs = mix_and_residual(hs, h_m, mo, mz, h_a, az, m_norm_g[l], w_out[l], ln_g[l], ln_b[l])
    y_prompt = hp[:, N_META:]
    y_sample = hs
    return (y_prompt, y_sample,
            jnp.stack(pk), jnp.stack(pv), jnp.stack(pc), jnp.stack(pn), jnp.stack(pm),
            jnp.stack(sk), jnp.stack(sv), jnp.stack(sc), jnp.stack(sn), jnp.stack(sm))
```

```python
import functools

import jax
import jax.numpy as jnp
from jax import lax
from jax.experimental import pallas as pl
from jax.experimental.pallas import tpu as pltpu

F32 = jnp.float32
BF16 = jnp.bfloat16

D_MODEL = 1024
N_META = 16
M_HEADS = 4
M_DH = 128
D_MLSTM = M_HEADS * M_DH
A_HEADS = 8
A_KV_HEADS = 2
A_GROUP = A_HEADS // A_KV_HEADS
A_DH = 64
D_SWA = A_HEADS * A_DH
KV_W = A_KV_HEADS * A_DH
WINDOW = 128
CHUNK = 128
LN_EPS = 1e-5
DEPTH = 1
DN_ALPHA = (2.0 * DEPTH) ** 0.25
N_BUF = N_META + WINDOW

C_Q, C_K, C_V, C_O, C_Z = 0, 512, 1024, 1536, 2048
C_G = 2560
C_AQ, C_AK, C_AV, C_AZ = 2688, 3200, 3328, 3456
N_PAD = 3968
N_RAW_GATE_END = 2568

PROJ_COL_STEP = 512
VMEM_LIMIT_BYTES = 56 * 1024 * 1024
NEG_INF = float("-inf")


def _dot(a, b):
    return jnp.dot(a, b, preferred_element_type=F32)


def _dot_nt(a, b):
    return lax.dot_general(a, b, (((1,), (1,)), ((), ())), preferred_element_type=F32)


def _dot_exact(a, b):
    return jnp.dot(a, b, precision=lax.Precision.HIGHEST, preferred_element_type=F32)


def _layer_norm(x, g, b):
    mu = jnp.mean(x, axis=-1, keepdims=True)
    xc = x - mu
    var = jnp.mean(xc * xc, axis=-1, keepdims=True)
    return xc * lax.rsqrt(var + LN_EPS) * g + b


def _log_sigmoid(x):
    return jnp.minimum(x, 0.0) - jnp.log1p(jnp.exp(-jnp.abs(x)))


def _sigmoid(x):
    return 1.0 / (1.0 + jnp.exp(-x))


def _iota2(shape, dim):
    return lax.broadcasted_iota(jnp.int32, shape, dim)


def _project(hb, w_ref, b_ref, u_ref):
    for n0 in range(0, N_PAD, PROJ_COL_STEP):
        n1 = min(n0 + PROJ_COL_STEP, N_PAD)
        u_ref[:, n0:n1] = _dot(hb, w_ref[:, n0:n1]) + b_ref[:, n0:n1]


def _mlstm_intra(q, k, v, li_col, b_col, m_old, allowed):
    n = q.shape[0]
    a_col = li_col - b_col
    eye = _iota2((n, n), 0) == _iota2((n, n), 1)
    a_row = jnp.sum(jnp.where(eye, a_col, 0.0), axis=0, keepdims=True)
    a_mat = jnp.where(allowed, a_row, NEG_INF)
    mm = jnp.maximum(m_old, jnp.max(a_mat, axis=1, keepdims=True))
    w = jnp.exp(a_mat - mm)
    qb = q.astype(BF16)
    qkw = _dot_nt(qb, k.astype(BF16)) * w
    den = jnp.sum(qkw, axis=1, keepdims=True)
    num = _dot(qkw.astype(BF16), v.astype(BF16))
    return num, den, mm, a_col


def _mlstm_shared_state_head(q, k, v, li_col, b_col, m_old, c_old, n_old, allowed, n_valid):
    n = q.shape[0]
    num_i, den_i, mm, a_col = _mlstm_intra(q, k, v, li_col, b_col, m_old, allowed)
    s_inter = jnp.exp(m_old - mm)
    qc = _dot(q.astype(BF16), c_old.astype(BF16))
    qn = jnp.sum(q * n_old, axis=1, keepdims=True)
    num = num_i + qc * s_inter
    den = den_i + qn * s_inter
    m_t = b_col + mm
    h = num / jnp.maximum(jnp.abs(den), jnp.exp(-m_t))
    last = n_valid - 1
    mm_last = mm[last:last + 1, :]
    m_new = b_col[last:last + 1, :] + mm_last
    w_state = jnp.exp(a_col - mm_last)
    if n_valid < n:
        w_state = jnp.where(_iota2((n, 1), 0) < n_valid, w_state, 0.0)
    decay = jnp.exp(m_old - mm_last)
    kw = k * w_state
    c_new = decay * c_old + _dot(kw.T.astype(BF16), v.astype(BF16))
    n_new = decay * n_old + jnp.sum(kw, axis=0, keepdims=True)
    return h, c_new, n_new, m_new


def _gate_mix(h_m, mo, mz, o_a, az, norm_g):
    parts = []
    for hd in range(M_HEADS):
        sl = slice(hd * M_DH, (hd + 1) * M_DH)
        hh = h_m[:, sl] * _sigmoid(mo[:, sl])
        mu = jnp.mean(hh, axis=-1, keepdims=True)
        hc = hh - mu
        var = jnp.mean(hc * hc, axis=-1, keepdims=True)
        hn = hc * lax.rsqrt(var + LN_EPS) * norm_g[:, sl]
        zz = mz[:, sl]
        parts.append((hn * (zz * _sigmoid(zz))).astype(BF16))
    parts.append((o_a * (az * _sigmoid(az))).astype(BF16))
    return jnp.concatenate(parts, axis=-1)


def _out_and_norm(hp, mix, wo_ref, g, b):
    z = DN_ALPHA * hp + _dot(mix, wo_ref[...])
    return _layer_norm(z, g, b)


def _meta_kernel(meta_ref, w_ref, bias_ref, g0_ref, b0_ref,
                 c0_ref, n0_ref, m0_ref, km_ref, vm_ref, u_ref):
    hp = _layer_norm(meta_ref[...], g0_ref[...], b0_ref[...])
    _project(hp.astype(BF16), w_ref, bias_ref, u_ref)
    row = _iota2((CHUNK, CHUNK), 0)
    col = _iota2((CHUNK, CHUNK), 1)
    allowed = (col <= row) & (col < N_META)
    gates = u_ref[:, C_G:C_G + 128]
    b_all = _dot_exact(jnp.where(col <= row, 1.0, 0.0), _log_sigmoid(gates))
    zero_m = jnp.zeros((1, 1), F32)
    zero_c = jnp.zeros((M_DH, M_DH), F32)
    zero_n = jnp.zeros((1, M_DH), F32)
    n0_ref[...] = jnp.zeros(n0_ref.shape, F32)
    m0_ref[...] = jnp.zeros(m0_ref.shape, F32)
    for hd in range(M_HEADS):
        sl = slice(hd * M_DH, (hd + 1) * M_DH)
        q = u_ref[:, C_Q + hd * M_DH:C_Q + (hd + 1) * M_DH]
        k = u_ref[:, C_K + hd * M_DH:C_K + (hd + 1) * M_DH] * (M_DH ** -0.5)
        v = u_ref[:, C_V + hd * M_DH:C_V + (hd + 1) * M_DH]
        del sl
        _, c_new, n_new, m_new = _mlstm_shared_state_head(
            q, k, v, gates[:, hd:hd + 1], b_all[:, M_HEADS + hd:M_HEADS + hd + 1],
            zero_m, zero_c, zero_n, allowed, N_META)
        c0_ref[hd] = c_new
        n0_ref[hd:hd + 1, :] = n_new
        m0_ref[hd:hd + 1, :] = jnp.broadcast_to(m_new, (1, 128))
    km_ref[...] = u_ref[0:N_META, C_AK:C_AK + KV_W]
    vm_ref[...] = u_ref[0:N_META, C_AV:C_AV + KV_W]


def _dup_halves(x, kv):
    swapped = pltpu.roll(x, 64, 1)
    low = _iota2(x.shape, 1) < A_DH
    if kv == 0:
        return jnp.where(low, x, swapped)
    return jnp.where(low, swapped, x)


def _swa_chunk(aq, k_own, v_own, k_prev, v_prev, k_meta, v_meta, sink_ref, has_prev):
    row = _iota2((CHUNK, CHUNK), 0)
    col = _iota2((CHUNK, CHUNK), 1)
    own_ok = col <= row
    prev_ok = (col > row) & has_prev
    low = _iota2((CHUNK, 128), 1) < A_DH
    out_tiles = []
    for kv in range(A_KV_HEADS):
        ko = _dup_halves(k_own, kv).astype(BF16)
        kp = _dup_halves(k_prev, kv).astype(BF16)
        km = _dup_halves(k_meta, kv).astype(BF16)
        vo = _dup_halves(v_own, kv).astype(BF16)
        vp = _dup_halves(v_prev, kv).astype(BF16)
        vm = _dup_halves(v_meta, kv).astype(BF16)
        for pair in range(A_GROUP // 2):
            tile = kv * (A_GROUP // 2) + pair
            q_tile = aq[:, tile * 128:(tile + 1) * 128] * (A_DH ** -0.5)
            outs = []
            for half in range(2):
                head = 2 * tile + half
                qm = jnp.where(low if half == 0 else ~low, q_tile, 0.0).astype(BF16)
                s_own = jnp.where(own_ok, _dot_nt(qm, ko), NEG_INF)
                s_prev = jnp.where(prev_ok, _dot_nt(qm, kp), NEG_INF)
                s_meta = _dot_nt(qm, km)
                sink = sink_ref[head]
                mx = jnp.maximum(jnp.maximum(jnp.max(s_own, axis=1, keepdims=True),
                                             jnp.max(s_prev, axis=1, keepdims=True)),
                                 jnp.maximum(jnp.max(s_meta, axis=1, keepdims=True), sink))
                p_own = jnp.exp(s_own - mx)
                p_prev = jnp.exp(s_prev - mx)
                p_meta = jnp.exp(s_meta - mx)
                den = (jnp.sum(p_own, axis=1, keepdims=True) + jnp.sum(p_prev, axis=1, keepdims=True)
                       + jnp.sum(p_meta, axis=1, keepdims=True) + jnp.exp(sink - mx))
                pv = (_dot(p_own.astype(BF16), vo) + _dot(p_prev.astype(BF16), vp)
                      + _dot(p_meta.astype(BF16), vm))
                outs.append(pv / den)
            out_tiles.append(jnp.where(low, outs[0], outs[1]))
    return jnp.concatenate(out_tiles, axis=-1)


def _prompt_kernel(sink_ref, x_ref, w_ref, bias_ref, g0_ref, b0_ref, c0_ref, n0_ref, m0_ref,
                   km_ref, vm_ref, ng_ref, wo_ref, lng_ref, lnb_ref,
                   y_ref, pk_ref, pv_ref, pc_ref, pn_ref, pm_ref,
                   u_ref, hp_ref, mix_ref, kprev_ref, vprev_ref, *, tb):
    j = pl.program_id(1)
    n_chunks = tb // CHUNK

    @pl.when(j == 0)
    def _():
        pc_ref[0] = c0_ref[...]
        pn_ref[0] = n0_ref[...]
        pm_ref[0] = m0_ref[...]
        kprev_ref[...] = jnp.zeros(kprev_ref.shape, F32)
        vprev_ref[...] = jnp.zeros(vprev_ref.shape, F32)
        pk_ref[0, 0:N_META, :] = km_ref[...]
        pv_ref[0, 0:N_META, :] = vm_ref[...]

    hp = _layer_norm(x_ref[0], g0_ref[...], b0_ref[...])
    hp_ref[...] = hp
    _project(hp.astype(BF16), w_ref, bias_ref, u_ref)

    def chunk_body(ci, carry):
        r0 = pl.multiple_of(ci * CHUNK, CHUNK)
        rows = pl.ds(r0, CHUNK)
        row = _iota2((CHUNK, CHUNK), 0)
        col = _iota2((CHUNK, CHUNK), 1)
        causal = col <= row
        gates = u_ref[rows, C_G:C_G + 128]
        b_all = _dot_exact(jnp.where(causal, 1.0, 0.0), _log_sigmoid(gates))
        h_parts = []
        for hd in range(M_HEADS):
            q = u_ref[rows, C_Q + hd * M_DH:C_Q + (hd + 1) * M_DH]
            k = u_ref[rows, C_K + hd * M_DH:C_K + (hd + 1) * M_DH] * (M_DH ** -0.5)
            v = u_ref[rows, C_V + hd * M_DH:C_V + (hd + 1) * M_DH]
            h, c_new, n_new, m_new = _mlstm_shared_state_head(
                q, k, v, gates[:, hd:hd + 1], b_all[:, M_HEADS + hd:M_HEADS + hd + 1],
                pm_ref[0, hd:hd + 1, 0:1], pc_ref[0, hd], pn_ref[0, hd:hd + 1, :], causal, CHUNK)
            pc_ref[0, hd] = c_new
            pn_ref[0, hd:hd + 1, :] = n_new
            pm_ref[0, hd:hd + 1, :] = jnp.broadcast_to(m_new, (1, 128))
            h_parts.append(h)
        h_m = jnp.concatenate(h_parts, axis=-1)

        k_own = u_ref[rows, C_AK:C_AK + KV_W]
        v_own = u_ref[rows, C_AV:C_AV + KV_W]
        has_prev = (j * n_chunks + ci) > 0
        o_a = _swa_chunk(u_ref[rows, C_AQ:C_AQ + D_SWA], k_own, v_own, kprev_ref[...], vprev_ref[...],
                         km_ref[...], vm_ref[...], sink_ref, has_prev)
        kprev_ref[...] = k_own
        vprev_ref[...] = v_own

        mix_ref[rows, :] = _gate_mix(h_m, u_ref[rows, C_O:C_O + D_MLSTM], u_ref[rows, C_Z:C_Z + D_MLSTM],
                                     o_a, u_ref[rows, C_AZ:C_AZ + D_SWA], ng_ref[...])
        return carry

    lax.fori_loop(0, n_chunks, chunk_body, 0)

    y_ref[0] = _out_and_norm(hp_ref[...], mix_ref[...], wo_ref, lng_ref[...], lnb_ref[...])

    @pl.when(j == pl.num_programs(1) - 1)
    def _():
        pk_ref[0, N_META:N_BUF, :] = u_ref[tb - WINDOW:tb, C_AK:C_AK + KV_W]
        pv_ref[0, N_META:N_BUF, :] = u_ref[tb - WINDOW:tb, C_AV:C_AV + KV_W]


def _const_spec(shape):
    return pl.BlockSpec(shape, lambda *_: (0,) * len(shape))


def _prompt_path(x_prompt, meta_tokens, ln0_g, ln0_b, w_all, b_all, sinks, norm_g, w_o, ln_g, ln_b, tb=512):
    batch, seq, _ = x_prompt.shape
    meta_pad = jnp.pad(meta_tokens.astype(F32), ((0, CHUNK - N_META), (0, 0)))
    c0, n0, m0, km, vm = pl.pallas_call(
        _meta_kernel,
        out_shape=(jax.ShapeDtypeStruct((M_HEADS, M_DH, M_DH), F32),
                   jax.ShapeDtypeStruct((8, M_DH), F32),
                   jax.ShapeDtypeStruct((8, 128), F32),
                   jax.ShapeDtypeStruct((N_META, KV_W), F32),
                   jax.ShapeDtypeStruct((N_META, KV_W), F32)),
        scratch_shapes=[pltpu.VMEM((CHUNK, N_PAD), F32)],
        compiler_params=pltpu.CompilerParams(vmem_limit_bytes=VMEM_LIMIT_BYTES),
        name="meta_tokens",
    )(meta_pad, w_all, b_all, ln0_g, ln0_b)

    nj = seq // tb
    in_specs = [
        pl.BlockSpec(memory_space=pltpu.SMEM),
        pl.BlockSpec((1, tb, D_MODEL), lambda b, j: (b, j, 0)),
        _const_spec((D_MODEL, N_PAD)), _const_spec((1, N_PAD)),
        _const_spec((1, D_MODEL)), _const_spec((1, D_MODEL)),
        _const_spec((M_HEADS, M_DH, M_DH)), _const_spec((8, M_DH)), _const_spec((8, 128)),
        _const_spec((N_META, KV_W)), _const_spec((N_META, KV_W)),
        _const_spec((1, D_MLSTM)),
        _const_spec((D_MODEL, D_MODEL)),
        _const_spec((1, D_MODEL)), _const_spec((1, D_MODEL)),
    ]
    out_specs = [
        pl.BlockSpec((1, tb, D_MODEL), lambda b, j: (b, j, 0)),
        pl.BlockSpec((1, N_BUF, KV_W), lambda b, j: (b, 0, 0)),
        pl.BlockSpec((1, N_BUF, KV_W), lambda b, j: (b, 0, 0)),
        pl.BlockSpec((1, M_HEADS, M_DH, M_DH), lambda b, j: (b, 0, 0, 0)),
        pl.BlockSpec((1, 8, M_DH), lambda b, j: (b, 0, 0)),
        pl.BlockSpec((1, 8, 128), lambda b, j: (b, 0, 0)),
    ]
    out_shape = (
        jax.ShapeDtypeStruct((batch, seq, D_MODEL), F32),
        jax.ShapeDtypeStruct((batch, N_BUF, KV_W), F32),
        jax.ShapeDtypeStruct((batch, N_BUF, KV_W), F32),
        jax.ShapeDtypeStruct((batch, M_HEADS, M_DH, M_DH), F32),
        jax.ShapeDtypeStruct((batch, 8, M_DH), F32),
        jax.ShapeDtypeStruct((batch, 8, 128), F32),
    )
    y, pk, pv, pc, pn, pm = pl.pallas_call(
        functools.partial(_prompt_kernel, tb=tb),
        grid=(batch, nj),
        in_specs=in_specs,
        out_specs=out_specs,
        out_shape=out_shape,
        scratch_shapes=[pltpu.VMEM((tb, N_PAD), F32), pltpu.VMEM((tb, D_MODEL), F32),
                        pltpu.VMEM((tb, D_MODEL), BF16),
                        pltpu.VMEM((CHUNK, KV_W), F32), pltpu.VMEM((CHUNK, KV_W), F32)],
        compiler_params=pltpu.CompilerParams(dimension_semantics=("arbitrary", "arbitrary"),
                                             vmem_limit_bytes=VMEM_LIMIT_BYTES),
        name="prompt_layer",
    )(sinks, x_prompt, w_all, b_all, ln0_g, ln0_b, c0, n0, m0, km, vm, norm_g, w_o, ln_g, ln_b)
    pk = pk.reshape(1, batch, N_BUF, A_KV_HEADS, A_DH)
    pv = pv.reshape(1, batch, N_BUF, A_KV_HEADS, A_DH)
    return y, pk, pv, pc[None], pn[:, :M_HEADS][None], pm[:, :M_HEADS, 0][None]


SEQ_PER_GROUP = 32
SWA_SEQ_PER_STEP = 8


def _sample_proj_kernel(x_ref, w_ref, bias_ref, g0_ref, b0_ref, hs_ref, u_ref):
    hs = _layer_norm(x_ref[...], g0_ref[...], b0_ref[...])
    hs_ref[...] = hs
    _project(hs.astype(BF16), w_ref, bias_ref, u_ref)


def _sample_mlstm_kernel(q_ref, k_ref, v_ref, g_ref, c_ref, n_ref, m_ref,
                         h_ref, cn_ref, nn_ref, mn_ref, decay_ref, kwt_ref, *, dec_seq):
    hd = pl.program_id(1)
    n = CHUNK
    nb = n // dec_seq
    q = q_ref[...]
    k = k_ref[...] * (M_DH ** -0.5)
    v = v_ref[...]
    gates = g_ref[...]
    row = _iota2((n, n), 0)
    col = _iota2((n, n), 1)
    row_seq = row // dec_seq
    col_seq = col // dec_seq
    allowed = (row_seq == col_seq) & (col <= row)
    b_all = _dot_exact(jnp.where(allowed, 1.0, 0.0), _log_sigmoid(gates))
    li_col = jnp.sum(jnp.where(col == hd, gates, 0.0), axis=1, keepdims=True)
    b_col = jnp.sum(jnp.where(col == hd + M_HEADS, b_all, 0.0), axis=1, keepdims=True)

    expand = jnp.where(_iota2((n, nb), 0) // dec_seq == _iota2((n, nb), 1), 1.0, 0.0)
    m_seq = m_ref[0]
    n_seq = n_ref[0]
    m_old = _dot_exact(expand, m_seq)[:, 0:1]
    n_rows = _dot_exact(expand, n_seq)

    num_i, den_i, mm, a_col = _mlstm_intra(q, k, v, li_col, b_col, m_old, allowed)
    sel_last = jnp.where(col == row_seq * dec_seq + (dec_seq - 1), 1.0, 0.0)
    packed = jnp.where(col == 0, mm, jnp.where(col == 1, b_col, 0.0))
    last_vals = _dot_exact(sel_last, packed)
    mm_last = last_vals[:, 0:1]
    b_last = last_vals[:, 1:2]
    m_t = b_col + mm
    w_state = jnp.exp(a_col - mm_last)
    decay_col = jnp.exp(m_old - mm_last)
    kw = k * w_state
    kwt_ref[...] = kw.T
    decay_ref[...] = jnp.broadcast_to(decay_col, (n, n))

    pick = jnp.where(_iota2((nb, n), 1) == _iota2((nb, n), 0) * dec_seq + (dec_seq - 1), 1.0, 0.0)
    seg = jnp.where(_iota2((nb, n), 1) // dec_seq == _iota2((nb, n), 0), 1.0, 0.0)
    decay_seq = _dot_exact(pick, decay_ref[...])
    nn_ref[0] = decay_seq * n_seq + _dot_exact(seg, kw)
    mn_ref[0] = _dot_exact(pick, jnp.broadcast_to(b_last + mm_last, (n, n)))

    qb = q.astype(BF16)
    vb = v.astype(BF16)

    def seq_body(s, acc):
        c_s = c_ref[s, 0]
        qc = _dot(qb, c_s.astype(BF16))
        acc = acc + jnp.where(row_seq == s, qc, 0.0)
        kwt_s = jnp.where(col_seq == s, kwt_ref[...], 0.0).astype(BF16)
        decay_s = decay_ref[pl.ds(s * dec_seq + (dec_seq - 1), 1), :]
        cn_ref[s, 0] = decay_s * c_s + _dot(kwt_s, vb)
        return acc

    qc_all = lax.fori_loop(0, nb, seq_body, jnp.zeros((n, n), F32))
    s_inter = jnp.exp(m_old - mm)
    qn = jnp.sum(q * n_rows, axis=1, keepdims=True)
    num = num_i + qc_all * s_inter
    den = den_i + qn * s_inter
    h_ref[...] = num / jnp.maximum(jnp.abs(den), jnp.exp(-m_t))


def _sample_swa_kernel(qz_ref, kn_ref, vn_ref, ck_ref, cv_ref, sink_ref, o_ref, nk_ref, nv_ref, *, dec_seq):
    n_rows = qz_ref.shape[1]
    t_c = _iota2((n_rows, N_BUF), 0) % dec_seq
    i_c = _iota2((n_rows, N_BUF), 1)
    ok_c = (i_c < N_META) | (i_c > t_c + N_META)
    t_n = _iota2((n_rows, 8), 0) % dec_seq
    i_n = _iota2((n_rows, 8), 1)
    ok_n = (i_n <= t_n) & (i_n < dec_seq)
    sink = sink_ref[:, 0:1]

    def seq_body(s, carry):
        qz = (qz_ref[s] * (A_DH ** -0.5)).astype(BF16)
        kc = ck_ref[s]
        vc = cv_ref[s]
        kn = kn_ref[s]
        vn = vn_ref[s]
        s_c = jnp.where(ok_c, _dot_nt(qz, kc.astype(BF16)), NEG_INF)
        s_n = jnp.where(ok_n, _dot_nt(qz, kn.astype(BF16)), NEG_INF)
        mx = jnp.maximum(jnp.maximum(jnp.max(s_c, axis=1, keepdims=True),
                                     jnp.max(s_n, axis=1, keepdims=True)), sink)
        p_c = jnp.exp(s_c - mx)
        p_n = jnp.exp(s_n - mx)
        den = jnp.sum(p_c, axis=1, keepdims=True) + jnp.sum(p_n, axis=1, keepdims=True) + jnp.exp(sink - mx)
        o_ref[s] = (_dot(p_c.astype(BF16), vc.astype(BF16)) + _dot(p_n.astype(BF16), vn.astype(BF16))) / den
        nk_ref[s, 0:N_META, :] = kc[0:N_META]
        nk_ref[s, N_META:N_BUF - dec_seq, :] = kc[N_META + dec_seq:N_BUF]
        nk_ref[s, N_BUF - dec_seq:N_BUF, :] = kn[0:dec_seq]
        nv_ref[s, 0:N_META, :] = vc[0:N_META]
        nv_ref[s, N_META:N_BUF - dec_seq, :] = vc[N_META + dec_seq:N_BUF]
        nv_ref[s, N_BUF - dec_seq:N_BUF, :] = vn[0:dec_seq]
        return carry

    lax.fori_loop(0, qz_ref.shape[0], seq_body, 0)


def _sample_out_kernel(hs_ref, u_ref, hm_ref, oa_ref, ng_ref, wo_ref, lng_ref, lnb_ref, y_ref):
    mix = _gate_mix(hm_ref[...], u_ref[:, C_O:C_O + D_MLSTM], u_ref[:, C_Z:C_Z + D_MLSTM],
                    oa_ref[...], u_ref[:, C_AZ:C_AZ + D_SWA], ng_ref[...])
    y_ref[...] = _out_and_norm(hs_ref[...], mix, wo_ref, lng_ref[...], lnb_ref[...])


def _sample_path(x_sample, cache_k, cache_v, state_c, state_n, state_m,
                 ln0_g, ln0_b, w_all, b_all, a_sinks, norm_g, w_o, ln_g, ln_b):
    db, dec_seq, _ = x_sample.shape
    rows = db * dec_seq
    params = pltpu.CompilerParams(vmem_limit_bytes=VMEM_LIMIT_BYTES)
    hs, u = pl.pallas_call(
        _sample_proj_kernel,
        out_shape=(jax.ShapeDtypeStruct((rows, D_MODEL), F32), jax.ShapeDtypeStruct((rows, N_PAD), F32)),
        compiler_params=params,
        name="sample_proj",
    )(x_sample.reshape(rows, D_MODEL), w_all, b_all, ln0_g, ln0_b)

    n_groups = db // SEQ_PER_GROUP
    n_t = jnp.transpose(state_n, (1, 0, 2))
    m_t = jnp.broadcast_to(jnp.transpose(state_m, (1, 0))[:, :, None], (M_HEADS, db, 128))

    def col_spec(col0):
        return pl.BlockSpec((CHUNK, M_DH), lambda g, h: (g, col0 // M_DH + h))

    state_spec = pl.BlockSpec((SEQ_PER_GROUP, 1, M_DH, M_DH), lambda g, h: (g, h, 0, 0))
    vec_spec = pl.BlockSpec((1, SEQ_PER_GROUP, M_DH), lambda g, h: (h, g, 0))
    h_m, c_new, n_new, m_new = pl.pallas_call(
        functools.partial(_sample_mlstm_kernel, dec_seq=dec_seq),
        grid=(n_groups, M_HEADS),
        in_specs=[col_spec(C_Q), col_spec(C_K), col_spec(C_V),
                  pl.BlockSpec((CHUNK, 128), lambda g, h: (g, C_G // 128)),
                  state_spec, vec_spec, vec_spec],
        out_specs=[pl.BlockSpec((CHUNK, M_DH), lambda g, h: (g, h)), state_spec, vec_spec, vec_spec],
        out_shape=(jax.ShapeDtypeStruct((rows, D_MLSTM), F32),
                   jax.ShapeDtypeStruct(state_c.shape, F32),
                   jax.ShapeDtypeStruct((M_HEADS, db, M_DH), F32),
                   jax.ShapeDtypeStruct((M_HEADS, db, 128), F32)),
        scratch_shapes=[pltpu.VMEM((CHUNK, CHUNK), F32), pltpu.VMEM((CHUNK, CHUNK), F32)],
        compiler_params=pltpu.CompilerParams(dimension_semantics=("arbitrary", "arbitrary"),
                                             vmem_limit_bytes=VMEM_LIMIT_BYTES),
        name="sample_mlstm",
    )(u, u, u, u, state_c, n_t, m_t)

    aq = u[:, C_AQ:C_AQ + D_SWA].reshape(db, dec_seq, A_KV_HEADS, A_GROUP, A_DH)
    aq = jnp.transpose(aq, (0, 2, 3, 1, 4)).reshape(db, A_KV_HEADS, A_GROUP * dec_seq, A_DH)
    zeros = jnp.zeros_like(aq[:, 0])
    qz = jnp.stack([jnp.concatenate([aq[:, 0], zeros], axis=-1),
                    jnp.concatenate([zeros, aq[:, 1]], axis=-1)], axis=1)
    n_qrows = A_HEADS * dec_seq
    qz = qz.reshape(db, n_qrows, KV_W)
    k_new = jnp.pad(u[:, C_AK:C_AK + KV_W].reshape(db, dec_seq, KV_W), ((0, 0), (0, 8 - dec_seq), (0, 0)))
    v_new = jnp.pad(u[:, C_AV:C_AV + KV_W].reshape(db, dec_seq, KV_W), ((0, 0), (0, 8 - dec_seq), (0, 0)))
    sink_rows = jnp.broadcast_to(jnp.repeat(a_sinks.astype(F32), dec_seq)[:, None], (n_qrows, 128))
    ck = cache_k.reshape(db, N_BUF, KV_W)
    cv = cache_v.reshape(db, N_BUF, KV_W)
    sb = SWA_SEQ_PER_STEP

    def seq_spec(r):
        return pl.BlockSpec((sb, r, KV_W), lambda i: (i, 0, 0))

    o, nk, nv = pl.pallas_call(
        functools.partial(_sample_swa_kernel, dec_seq=dec_seq),
        grid=(db // sb,),
        in_specs=[seq_spec(n_qrows), seq_spec(8), seq_spec(8), seq_spec(N_BUF), seq_spec(N_BUF),
                  pl.BlockSpec((n_qrows, 128), lambda i: (0, 0))],
        out_specs=[seq_spec(n_qrows), seq_spec(N_BUF), seq_spec(N_BUF)],
        out_shape=(jax.ShapeDtypeStruct((db, n_qrows, KV_W), F32),
                   jax.ShapeDtypeStruct((db, N_BUF, KV_W), F32),
                   jax.ShapeDtypeStruct((db, N_BUF, KV_W), F32)),
        compiler_params=pltpu.CompilerParams(dimension_semantics=("arbitrary",),
                                             vmem_limit_bytes=VMEM_LIMIT_BYTES),
        name="sample_swa",
    )(qz, k_new, v_new, ck, cv, sink_rows)
    o = o.reshape(db, A_KV_HEADS, A_GROUP, dec_seq, A_KV_HEADS, A_DH)
    o = jnp.stack([o[:, 0, :, :, 0, :], o[:, 1, :, :, 1, :]], axis=1)
    o_a = jnp.transpose(o, (0, 3, 1, 2, 4)).reshape(rows, D_SWA)

    y = pl.pallas_call(
        _sample_out_kernel,
        out_shape=jax.ShapeDtypeStruct((rows, D_MODEL), F32),
        compiler_params=params,
        name="sample_out",
    )(hs, u, h_m, o_a, norm_g, w_o, ln_g, ln_b)

    shape5 = (1, db, N_BUF, A_KV_HEADS, A_DH)
    return (y.reshape(db, dec_seq, D_MODEL), nk.reshape(shape5), nv.reshape(shape5), c_new[None],
            jnp.transpose(n_new, (1, 0, 2))[None], jnp.transpose(m_new[:, :, 0], (1, 0))[None])


def kernel(x_prompt, x_sample, cache_swa_k, cache_swa_v, state_mlstm_c, state_mlstm_n, state_mlstm_m,
           meta_tokens, ln0_g, ln0_b, w_in, b_in, a_sinks, m_norm_g, w_out, ln_g, ln_b):
    assert w_in.shape[0] == DEPTH and x_prompt.shape[-1] == D_MODEL
    w = w_in[0]
    pad_w = jnp.zeros((D_MODEL, C_AQ - N_RAW_GATE_END), w.dtype)
    w_all = jnp.concatenate([w[:, :N_RAW_GATE_END], pad_w, w[:, N_RAW_GATE_END:]], axis=1).astype(BF16)
    b = b_in[0].astype(F32)
    b_all = jnp.concatenate([b[:N_RAW_GATE_END], jnp.zeros((C_AQ - N_RAW_GATE_END,), F32),
                             b[N_RAW_GATE_END:]])[None]
    w_o = w_out[0].astype(BF16)
    g0 = ln0_g.astype(F32)[None]
    b0 = ln0_b.astype(F32)[None]
    lg = ln_g[0].astype(F32)[None]
    lb = ln_b[0].astype(F32)[None]
    norm_g = m_norm_g[0].astype(F32)[None]
    sinks = a_sinks[0].astype(F32)

    y_p, pk, pv, pc, pn, pm = _prompt_path(x_prompt, meta_tokens, g0, b0, w_all, b_all, sinks, norm_g, w_o, lg, lb)
    y_s, sk, sv, sc, sn, sm = _sample_path(x_sample, cache_swa_k[0], cache_swa_v[0], state_mlstm_c[0],
                                           state_mlstm_n[0], state_mlstm_m[0],
                                           g0, b0, w_all, b_all, sinks, norm_g, w_o, lg, lb)
    return (y_p, y_s, pk, pv, pc, pn, pm, sk, sv, sc, sn, sm)
```

```python
import functools

import jax
import jax.numpy as jnp
from jax import lax
from jax.experimental import pallas as pl
from jax.experimental.pallas import tpu as pltpu

F32 = jnp.float32
BF16 = jnp.bfloat16

D_MODEL = 1024
N_META = 16
M_HEADS = 4
M_DH = 128
D_MLSTM = M_HEADS * M_DH
A_HEADS = 8
A_KV_HEADS = 2
A_GROUP = A_HEADS // A_KV_HEADS
A_DH = 64
D_SWA = A_HEADS * A_DH
KV_W = A_KV_HEADS * A_DH
WINDOW = 128
CHUNK = 128
LN_EPS = 1e-5
DEPTH = 1
DN_ALPHA = (2.0 * DEPTH) ** 0.25
N_BUF = N_META + WINDOW

C_Q, C_K, C_V, C_O, C_Z = 0, 512, 1024, 1536, 2048
C_G = 2560
C_AQ, C_AK, C_AV, C_AZ = 2688, 3200, 3328, 3456
N_PAD = 3968
N_RAW_GATE_END = 2568

PROJ_COL_STEP = 512
VMEM_LIMIT_BYTES = 56 * 1024 * 1024
NEG_INF = float("-inf")


def _dot(a, b):
    return jnp.dot(a, b, preferred_element_type=F32)


def _dot_nt(a, b):
    return lax.dot_general(a, b, (((1,), (1,)), ((), ())), preferred_element_type=F32)


def _dot_exact(a, b):
    return jnp.dot(a, b, precision=lax.Precision.HIGHEST, preferred_element_type=F32)


def _layer_norm(x, g, b):
    mu = jnp.mean(x, axis=-1, keepdims=True)
    xc = x - mu
    var = jnp.mean(xc * xc, axis=-1, keepdims=True)
    return xc * lax.rsqrt(var + LN_EPS) * g + b


def _log_sigmoid(x):
    return jnp.minimum(x, 0.0) - jnp.log1p(jnp.exp(-jnp.abs(x)))


def _sigmoid(x):
    return 1.0 / (1.0 + jnp.exp(-x))


def _iota2(shape, dim):
    return lax.broadcasted_iota(jnp.int32, shape, dim)


def _project(hb, w_ref, b_ref, u_ref):
    for n0 in range(0, N_PAD, PROJ_COL_STEP):
        n1 = min(n0 + PROJ_COL_STEP, N_PAD)
        u_ref[:, n0:n1] = _dot(hb, w_ref[:, n0:n1]) + b_ref[:, n0:n1]


def _mlstm_intra(q, k, v, li_col, b_col, m_old, allowed):
    n = q.shape[0]
    a_col = li_col - b_col
    eye = _iota2((n, n), 0) == _iota2((n, n), 1)
    a_row = jnp.sum(jnp.where(eye, a_col, 0.0), axis=0, keepdims=True)
    a_mat = jnp.where(allowed, a_row, NEG_INF)
    mm = jnp.maximum(m_old, jnp.max(a_mat, axis=1, keepdims=True))
    w = jnp.exp(a_mat - mm)
    qb = q.astype(BF16)
    qkw = _dot_nt(qb, k.astype(BF16)) * w
    den = jnp.sum(qkw, axis=1, keepdims=True)
    num = _dot(qkw.astype(BF16), v.astype(BF16))
    return num, den, mm, a_col


def _gate_rows(gates):
    g_t = gates.T[0:8, :]
    x = _log_sigmoid(g_t)
    lane = _iota2(x.shape, 1)
    shift = 1
    while shift < x.shape[1]:
        x = x + jnp.where(lane >= shift, pltpu.roll(x, shift, 1), 0.0)
        shift *= 2
    return g_t, x


def _mlstm_chunk_t(q, k, v, li_row, b_row, m_old, ct_aug, key_ok, n_valid):
    carry, ct_new, n_new, m_new = _mlstm_scores_and_state(q, k, v, li_row, b_row, m_old, ct_aug, key_ok, n_valid)
    return _mlstm_finish(*_mlstm_weighted_values(*carry)), ct_new, n_new, m_new


def _mlstm_scores_and_state(q, k, v, li_row, b_row, m_old, ct_aug, key_ok, n_valid):
    n = q.shape[0]
    row = _iota2((n, n), 0)
    col = _iota2((n, n), 1)
    a_row = li_row - b_row
    a_keys = jnp.broadcast_to(jnp.sum(jnp.where(row == col, a_row, 0.0), axis=1, keepdims=True), (n, n))
    a_t = jnp.where(key_ok, a_keys, NEG_INF)
    mm = jnp.maximum(m_old, jnp.max(a_t, axis=0, keepdims=True))
    w_t = jnp.exp(a_t - mm)
    lane = _iota2((1, n), 1)
    last = n_valid - 1
    mm_last = jnp.max(jnp.where(lane == last, mm, NEG_INF), axis=1, keepdims=True)
    m_new = jnp.sum(jnp.where(lane == last, b_row, 0.0), axis=1, keepdims=True) + mm_last
    w_state = jnp.exp(a_keys - mm_last)
    if n_valid < n:
        w_state = jnp.where(row < n_valid, w_state, 0.0)
    decay = jnp.exp(m_old - mm_last)
    kw = k * w_state
    qb = q.astype(BF16)
    vtb = v.T.astype(BF16)
    scores_t = _dot_nt(k.astype(BF16), qb)
    inter = _dot_nt(ct_aug.astype(BF16), qb)
    ct_new = decay * ct_aug[0:M_DH] + _dot(vtb, kw.astype(BF16))
    n_new = decay * ct_aug[M_DH:M_DH + 1] + jnp.sum(kw, axis=0, keepdims=True)
    s_inter = jnp.exp(m_old - mm)
    floor = jnp.exp(-(b_row + mm))
    return (scores_t, w_t, vtb, inter, s_inter, floor), ct_new, n_new, m_new


def _mlstm_weighted_values(scores_t, w_t, vtb, inter, s_inter, floor):
    qkw_t = scores_t * w_t
    den = jnp.sum(qkw_t, axis=0, keepdims=True)
    num_t = _dot(vtb, qkw_t.astype(BF16))
    return num_t, den, inter, s_inter, floor


def _mlstm_finish(num_t, den, inter, s_inter, floor):
    num_t = num_t + inter[0:M_DH] * s_inter
    den = den + inter[M_DH:M_DH + 1] * s_inter
    return num_t * (1.0 / jnp.maximum(jnp.abs(den), floor))


def _silu(x):
    return x * _sigmoid(x)


def _mlstm_gate_head(h, mo, mz, norm_g):
    hh = h * _sigmoid(mo)
    mu = jnp.mean(hh, axis=-1, keepdims=True)
    hc = hh - mu
    var = jnp.mean(hc * hc, axis=-1, keepdims=True)
    return (hc * lax.rsqrt(var + LN_EPS) * norm_g * _silu(mz)).astype(BF16)


def _gate_mix(h_m, mo, mz, o_a, az, norm_g):
    parts = []
    for hd in range(M_HEADS):
        sl = slice(hd * M_DH, (hd + 1) * M_DH)
        parts.append(_mlstm_gate_head(h_m[:, sl], mo[:, sl], mz[:, sl], norm_g[:, sl]))
    parts.append((o_a * _silu(az)).astype(BF16))
    return jnp.concatenate(parts, axis=-1)


def _out_and_norm(hp, mix, wo_ref, g, b):
    z = DN_ALPHA * hp + _dot(mix, wo_ref[...])
    return _layer_norm(z, g, b)


CT_ROWS = M_DH + 8


def _meta_kernel(meta_ref, w_ref, bias_ref, g0_ref, b0_ref,
                 ct0_ref, m0_ref, km_ref, vm_ref, vmt_ref, u_ref):
    hp = _layer_norm(meta_ref[...], g0_ref[...], b0_ref[...])
    _project(hp.astype(BF16), w_ref, bias_ref, u_ref)
    row = _iota2((CHUNK, CHUNK), 0)
    col = _iota2((CHUNK, CHUNK), 1)
    key_ok = (row <= col) & (row < N_META)
    li_rows, b_rows = _gate_rows(u_ref[:, C_G:C_G + 128])
    zero_m = jnp.zeros((1, 128), F32)
    zero_ct = jnp.zeros((CT_ROWS, M_DH), F32)
    ct0_ref[...] = jnp.zeros(ct0_ref.shape, F32)
    m0_ref[...] = jnp.zeros(m0_ref.shape, F32)
    for hd in range(M_HEADS):
        q = u_ref[:, C_Q + hd * M_DH:C_Q + (hd + 1) * M_DH]
        k = u_ref[:, C_K + hd * M_DH:C_K + (hd + 1) * M_DH] * (M_DH ** -0.5)
        v = u_ref[:, C_V + hd * M_DH:C_V + (hd + 1) * M_DH]
        _, ct_new, n_new, m_new = _mlstm_chunk_t(
            q, k, v, li_rows[hd:hd + 1], b_rows[M_HEADS + hd:M_HEADS + hd + 1],
            zero_m, zero_ct, key_ok, N_META)
        ct0_ref[hd, 0:M_DH, :] = ct_new
        ct0_ref[hd, M_DH:M_DH + 1, :] = n_new
        m0_ref[hd:hd + 1, :] = jnp.broadcast_to(m_new, (1, 128))
    km_ref[...] = u_ref[0:N_META, C_AK:C_AK + KV_W]
    vm_ref[...] = u_ref[0:N_META, C_AV:C_AV + KV_W]
    vmt_ref[...] = u_ref[:, C_AV:C_AV + KV_W].T[:, 0:N_META]


def _keep_kv_half(x, kv):
    low = _iota2(x.shape, 1) < A_DH
    return jnp.where(low if kv == 0 else ~low, x, 0.0)


def _swa_scores_t(tile, q_tile, keys):
    kv = tile // (A_GROUP // 2)
    q_scaled = q_tile * (A_DH ** -0.5)
    q_swapped = pltpu.roll(q_scaled, A_DH, 1)
    out = []
    for half in range(2):
        qz = (q_scaled if half == kv else q_swapped).astype(BF16)
        out.append(tuple(_dot_nt(kk, qz) for kk in keys))
    return out


def _swa_weighted_values_t(scores, values_t, sink, own_ok, prev_ok):
    s_own = jnp.where(own_ok, scores[0], NEG_INF)
    s_prev = jnp.where(prev_ok, scores[1], NEG_INF)
    s_meta = scores[2]
    mx = jnp.maximum(jnp.maximum(jnp.max(s_own, axis=0, keepdims=True),
                                 jnp.max(s_prev, axis=0, keepdims=True)),
                     jnp.maximum(jnp.max(s_meta, axis=0, keepdims=True), sink))
    p_own, p_prev, p_meta = (jnp.exp(s - mx) for s in (s_own, s_prev, s_meta))
    den = (jnp.sum(p_own, axis=0, keepdims=True) + jnp.sum(p_prev, axis=0, keepdims=True)
           + jnp.sum(p_meta, axis=0, keepdims=True) + jnp.exp(sink - mx))
    o_t = (_dot(values_t[0], p_own.astype(BF16)) + _dot(values_t[1], p_prev.astype(BF16))
           + _dot(values_t[2], p_meta.astype(BF16)))
    return o_t, den


def _prompt_kernel(sink_ref, x_ref, w_ref, bias_ref, g0_ref, b0_ref, ct0_ref, m0_ref,
                   km_ref, vm_ref, vmt_ref, ng_ref, wo_ref, lng_ref, lnb_ref,
                   y_ref, pk_ref, pv_ref, pc_ref, pn_ref, pm_ref,
                   u_ref, hp_ref, mix_ref, ct_ref, kprev_ref, vtprev_ref, *, tb):
    j = pl.program_id(1)
    n_chunks = tb // CHUNK

    @pl.when(j == 0)
    def _():
        ct_ref[...] = ct0_ref[...]
        pm_ref[0] = m0_ref[...]
        kprev_ref[...] = jnp.zeros(kprev_ref.shape, F32)
        vtprev_ref[...] = jnp.zeros(vtprev_ref.shape, F32)
        pk_ref[0, 0:N_META, :] = km_ref[...]
        pv_ref[0, 0:N_META, :] = vm_ref[...]

    hp = _layer_norm(x_ref[0], g0_ref[...], b0_ref[...])
    hp_ref[...] = hp
    _project(hp.astype(BF16), w_ref, bias_ref, u_ref)

    def chunk_body(ci, carry):
        r0 = pl.multiple_of(ci * CHUNK, CHUNK)
        rows = pl.ds(r0, CHUNK)
        key = _iota2((CHUNK, CHUNK), 0)
        query = _iota2((CHUNK, CHUNK), 1)
        causal = key <= query

        li_rows, b_rows = _gate_rows(u_ref[rows, C_G:C_G + 128])
        m_carry = []
        for hd in range(M_HEADS):
            q = u_ref[rows, C_Q + hd * M_DH:C_Q + (hd + 1) * M_DH]
            k = u_ref[rows, C_K + hd * M_DH:C_K + (hd + 1) * M_DH] * (M_DH ** -0.5)
            v = u_ref[rows, C_V + hd * M_DH:C_V + (hd + 1) * M_DH]
            carry_hd, ct_new, n_new, m_new = _mlstm_scores_and_state(
                q, k, v, li_rows[hd:hd + 1], b_rows[M_HEADS + hd:M_HEADS + hd + 1],
                pm_ref[0, hd:hd + 1, :], ct_ref[hd], causal, CHUNK)
            ct_ref[hd, 0:M_DH, :] = ct_new
            ct_ref[hd, M_DH:M_DH + 1, :] = n_new
            pm_ref[0, hd:hd + 1, :] = jnp.broadcast_to(m_new, (1, 128))
            m_carry.append(carry_hd)

        k_own = u_ref[rows, C_AK:C_AK + KV_W]
        vt_own = u_ref[rows, C_AV:C_AV + KV_W].T
        k_prev = kprev_ref[...]
        vt_prev = vtprev_ref[...]
        prev_ok = (key > query) & ((j * n_chunks + ci) > 0)
        keys_kv = [tuple(_keep_kv_half(x, kv).astype(BF16) for x in (k_own, k_prev, km_ref[...]))
                   for kv in range(A_KV_HEADS)]
        values_kv = [tuple(x[kv * A_DH:(kv + 1) * A_DH].astype(BF16) for x in (vt_own, vt_prev, vmt_ref[...]))
                     for kv in range(A_KV_HEADS)]
        a_scores = []
        for tile in range(A_HEADS // 2):
            a_scores += _swa_scores_t(tile, u_ref[rows, C_AQ + tile * 128:C_AQ + (tile + 1) * 128],
                                      keys_kv[tile // (A_GROUP // 2)])
        kprev_ref[...] = k_own
        vtprev_ref[...] = vt_own

        m_carry = [_mlstm_weighted_values(*c) for c in m_carry]
        a_out = [_swa_weighted_values_t(a_scores[hd], values_kv[hd // A_GROUP], sink_ref[hd], causal, prev_ok)
                 for hd in range(A_HEADS)]

        for hd in range(M_HEADS):
            sl = slice(hd * M_DH, (hd + 1) * M_DH)
            mix_ref[rows, sl] = _mlstm_gate_head(
                _mlstm_finish(*m_carry[hd]).T, u_ref[rows, C_O + hd * M_DH:C_O + (hd + 1) * M_DH],
                u_ref[rows, C_Z + hd * M_DH:C_Z + (hd + 1) * M_DH], ng_ref[:, sl])
        for tile in range(A_HEADS // 2):
            o_tile = jnp.concatenate([o_t * (1.0 / den) for o_t, den in a_out[2 * tile:2 * tile + 2]], axis=0).T
            az = u_ref[rows, C_AZ + tile * 128:C_AZ + (tile + 1) * 128]
            mix_ref[rows, D_MLSTM + tile * 128:D_MLSTM + (tile + 1) * 128] = (o_tile * _silu(az)).astype(BF16)
        return carry

    lax.fori_loop(0, n_chunks, chunk_body, 0)

    y_ref[0] = _out_and_norm(hp_ref[...], mix_ref[...], wo_ref, lng_ref[...], lnb_ref[...])

    @pl.when(j == pl.num_programs(1) - 1)
    def _():
        pk_ref[0, N_META:N_BUF, :] = u_ref[tb - WINDOW:tb, C_AK:C_AK + KV_W]
        pv_ref[0, N_META:N_BUF, :] = u_ref[tb - WINDOW:tb, C_AV:C_AV + KV_W]
        for hd in range(M_HEADS):
            pc_ref[0, hd] = ct_ref[hd, 0:M_DH, :].T
            pn_ref[0, hd:hd + 1, :] = ct_ref[hd, M_DH:M_DH + 1, :]
        pn_ref[0, M_HEADS:8, :] = jnp.zeros((8 - M_HEADS, M_DH), F32)


def _const_spec(shape):
    return pl.BlockSpec(shape, lambda *_: (0,) * len(shape))


def _prompt_path(x_prompt, meta_tokens, ln0_g, ln0_b, w_all, b_all, sinks, norm_g, w_o, ln_g, ln_b, tb=512):
    batch, seq, _ = x_prompt.shape
    meta_pad = jnp.pad(meta_tokens.astype(F32), ((0, CHUNK - N_META), (0, 0)))
    ct0, m0, km, vm, vmt = pl.pallas_call(
        _meta_kernel,
        out_shape=(jax.ShapeDtypeStruct((M_HEADS, CT_ROWS, M_DH), F32),
                   jax.ShapeDtypeStruct((8, 128), F32),
                   jax.ShapeDtypeStruct((N_META, KV_W), F32),
                   jax.ShapeDtypeStruct((N_META, KV_W), F32),
                   jax.ShapeDtypeStruct((KV_W, N_META), F32)),
        scratch_shapes=[pltpu.VMEM((CHUNK, N_PAD), F32)],
        compiler_params=pltpu.CompilerParams(vmem_limit_bytes=VMEM_LIMIT_BYTES),
        name="meta_tokens",
    )(meta_pad, w_all, b_all, ln0_g, ln0_b)

    nj = seq // tb
    in_specs = [
        pl.BlockSpec(memory_space=pltpu.SMEM),
        pl.BlockSpec((1, tb, D_MODEL), lambda b, j: (b, j, 0)),
        _const_spec((D_MODEL, N_PAD)), _const_spec((1, N_PAD)),
        _const_spec((1, D_MODEL)), _const_spec((1, D_MODEL)),
        _const_spec((M_HEADS, CT_ROWS, M_DH)), _const_spec((8, 128)),
        _const_spec((N_META, KV_W)), _const_spec((N_META, KV_W)), _const_spec((KV_W, N_META)),
        _const_spec((1, D_MLSTM)),
        _const_spec((D_MODEL, D_MODEL)),
        _const_spec((1, D_MODEL)), _const_spec((1, D_MODEL)),
    ]
    out_specs = [
        pl.BlockSpec((1, tb, D_MODEL), lambda b, j: (b, j, 0)),
        pl.BlockSpec((1, N_BUF, KV_W), lambda b, j: (b, 0, 0)),
        pl.BlockSpec((1, N_BUF, KV_W), lambda b, j: (b, 0, 0)),
        pl.BlockSpec((1, M_HEADS, M_DH, M_DH), lambda b, j: (b, 0, 0, 0)),
        pl.BlockSpec((1, 8, M_DH), lambda b, j: (b, 0, 0)),
        pl.BlockSpec((1, 8, 128), lambda b, j: (b, 0, 0)),
    ]
    out_shape = (
        jax.ShapeDtypeStruct((batch, seq, D_MODEL), F32),
        jax.ShapeDtypeStruct((batch, N_BUF, KV_W), F32),
        jax.ShapeDtypeStruct((batch, N_BUF, KV_W), F32),
        jax.ShapeDtypeStruct((batch, M_HEADS, M_DH, M_DH), F32),
        jax.ShapeDtypeStruct((batch, 8, M_DH), F32),
        jax.ShapeDtypeStruct((batch, 8, 128), F32),
    )
    y, pk, pv, pc, pn, pm = pl.pallas_call(
        functools.partial(_prompt_kernel, tb=tb),
        grid=(batch, nj),
        in_specs=in_specs,
        out_specs=out_specs,
        out_shape=out_shape,
        scratch_shapes=[pltpu.VMEM((tb, N_PAD), F32), pltpu.VMEM((tb, D_MODEL), F32),
                        pltpu.VMEM((tb, D_MODEL), BF16),
                        pltpu.VMEM((M_HEADS, CT_ROWS, M_DH), F32),
                        pltpu.VMEM((CHUNK, KV_W), F32), pltpu.VMEM((KV_W, CHUNK), F32)],
        compiler_params=pltpu.CompilerParams(dimension_semantics=("arbitrary", "arbitrary"),
                                             vmem_limit_bytes=VMEM_LIMIT_BYTES),
        name="prompt_layer",
    )(sinks, x_prompt, w_all, b_all, ln0_g, ln0_b, ct0, m0, km, vm, vmt, norm_g, w_o, ln_g, ln_b)
    pk = pk.reshape(1, batch, N_BUF, A_KV_HEADS, A_DH)
    pv = pv.reshape(1, batch, N_BUF, A_KV_HEADS, A_DH)
    return y, pk, pv, pc[None], pn[:, :M_HEADS][None], pm[:, :M_HEADS, 0][None]


SEQ_PER_GROUP = 32
SWA_SEQ_PER_STEP = 8


def _sample_proj_kernel(x_ref, w_ref, bias_ref, g0_ref, b0_ref, hs_ref, u_ref):
    hs = _layer_norm(x_ref[...], g0_ref[...], b0_ref[...])
    hs_ref[...] = hs
    _project(hs.astype(BF16), w_ref, bias_ref, u_ref)


def _sample_mlstm_kernel(q_ref, k_ref, v_ref, g_ref, c_ref, n_ref, m_ref,
                         h_ref, cn_ref, nn_ref, mn_ref, decay_ref, kwt_ref, *, dec_seq):
    hd = pl.program_id(1)
    n = CHUNK
    nb = n // dec_seq
    q = q_ref[...]
    k = k_ref[...] * (M_DH ** -0.5)
    v = v_ref[...]
    gates = g_ref[...]
    row = _iota2((n, n), 0)
    col = _iota2((n, n), 1)
    row_seq = row // dec_seq
    col_seq = col // dec_seq
    allowed = (row_seq == col_seq) & (col <= row)
    b_all = _dot_exact(jnp.where(allowed, 1.0, 0.0), _log_sigmoid(gates))
    li_col = jnp.sum(jnp.where(col == hd, gates, 0.0), axis=1, keepdims=True)
    b_col = jnp.sum(jnp.where(col == hd + M_HEADS, b_all, 0.0), axis=1, keepdims=True)

    expand = jnp.where(_iota2((n, nb), 0) // dec_seq == _iota2((n, nb), 1), 1.0, 0.0)
    m_seq = m_ref[0]
    n_seq = n_ref[0]
    m_old = _dot_exact(expand, m_seq)[:, 0:1]
    n_rows = _dot_exact(expand, n_seq)

    num_i, den_i, mm, a_col = _mlstm_intra(q, k, v, li_col, b_col, m_old, allowed)
    sel_last = jnp.where(col == row_seq * dec_seq + (dec_seq - 1), 1.0, 0.0)
    packed = jnp.where(col == 0, mm, jnp.where(col == 1, b_col, 0.0))
    last_vals = _dot_exact(sel_last, packed)
    mm_last = last_vals[:, 0:1]
    b_last = last_vals[:, 1:2]
    m_t = b_col + mm
    w_state = jnp.exp(a_col - mm_last)
    decay_col = jnp.exp(m_old - mm_last)
    kw = k * w_state
    kwt_ref[...] = kw.T
    decay_ref[...] = jnp.broadcast_to(decay_col, (n, n))

    pick = jnp.where(_iota2((nb, n), 1) == _iota2((nb, n), 0) * dec_seq + (dec_seq - 1), 1.0, 0.0)
    seg = jnp.where(_iota2((nb, n), 1) // dec_seq == _iota2((nb, n), 0), 1.0, 0.0)
    decay_seq = _dot_exact(pick, decay_ref[...])
    nn_ref[0] = decay_seq * n_seq + _dot_exact(seg, kw)
    mn_ref[0] = _dot_exact(pick, jnp.broadcast_to(b_last + mm_last, (n, n)))

    qb = q.astype(BF16)
    vb = v.astype(BF16)

    def seq_body(s, acc):
        c_s = c_ref[s, 0]
        qc = _dot(qb, c_s.astype(BF16))
        acc = acc + jnp.where(row_seq == s, qc, 0.0)
        kwt_s = jnp.where(col_seq == s, kwt_ref[...], 0.0).astype(BF16)
        decay_s = decay_ref[pl.ds(s * dec_seq + (dec_seq - 1), 1), :]
        cn_ref[s, 0] = decay_s * c_s + _dot(kwt_s, vb)
        return acc

    qc_all = lax.fori_loop(0, nb, seq_body, jnp.zeros((n, n), F32))
    s_inter = jnp.exp(m_old - mm)
    qn = jnp.sum(q * n_rows, axis=1, keepdims=True)
    num = num_i + qc_all * s_inter
    den = den_i + qn * s_inter
    h_ref[...] = num / jnp.maximum(jnp.abs(den), jnp.exp(-m_t))


def _sample_swa_kernel(qz_ref, kn_ref, vn_ref, ck_ref, cv_ref, sink_ref, o_ref, nk_ref, nv_ref, *, dec_seq):
    n_rows = qz_ref.shape[1]
    t_c = _iota2((n_rows, N_BUF), 0) % dec_seq
    i_c = _iota2((n_rows, N_BUF), 1)
    ok_c = (i_c < N_META) | (i_c > t_c + N_META)
    t_n = _iota2((n_rows, 8), 0) % dec_seq
    i_n = _iota2((n_rows, 8), 1)
    ok_n = (i_n <= t_n) & (i_n < dec_seq)
    sink = sink_ref[:, 0:1]

    def seq_body(s, carry):
        qz = (qz_ref[s] * (A_DH ** -0.5)).astype(BF16)
        kc = ck_ref[s]
        vc = cv_ref[s]
        kn = kn_ref[s]
        vn = vn_ref[s]
        s_c = jnp.where(ok_c, _dot_nt(qz, kc.astype(BF16)), NEG_INF)
        s_n = jnp.where(ok_n, _dot_nt(qz, kn.astype(BF16)), NEG_INF)
        mx = jnp.maximum(jnp.maximum(jnp.max(s_c, axis=1, keepdims=True),
                                     jnp.max(s_n, axis=1, keepdims=True)), sink)
        p_c = jnp.exp(s_c - mx)
        p_n = jnp.exp(s_n - mx)
        den = jnp.sum(p_c, axis=1, keepdims=True) + jnp.sum(p_n, axis=1, keepdims=True) + jnp.exp(sink - mx)
        o_ref[s] = (_dot(p_c.astype(BF16), vc.astype(BF16)) + _dot(p_n.astype(BF16), vn.astype(BF16))) / den
        nk_ref[s, 0:N_META, :] = kc[0:N_META]
        nk_ref[s, N_META:N_BUF - dec_seq, :] = kc[N_META + dec_seq:N_BUF]
        nk_ref[s, N_BUF - dec_seq:N_BUF, :] = kn[0:dec_seq]
        nv_ref[s, 0:N_META, :] = vc[0:N_META]
        nv_ref[s, N_META:N_BUF - dec_seq, :] = vc[N_META + dec_seq:N_BUF]
        nv_ref[s, N_BUF - dec_seq:N_BUF, :] = vn[0:dec_seq]
        return carry

    lax.fori_loop(0, qz_ref.shape[0], seq_body, 0)


def _sample_out_kernel(hs_ref, u_ref, hm_ref, oa_ref, ng_ref, wo_ref, lng_ref, lnb_ref, y_ref):
    mix = _gate_mix(hm_ref[...], u_ref[:, C_O:C_O + D_MLSTM], u_ref[:, C_Z:C_Z + D_MLSTM],
                    oa_ref[...], u_ref[:, C_AZ:C_AZ + D_SWA], ng_ref[...])
    y_ref[...] = _out_and_norm(hs_ref[...], mix, wo_ref, lng_ref[...], lnb_ref[...])


def _sample_path(x_sample, cache_k, cache_v, state_c, state_n, state_m,
                 ln0_g, ln0_b, w_all, b_all, a_sinks, norm_g, w_o, ln_g, ln_b):
    db, dec_seq, _ = x_sample.shape
    rows = db * dec_seq
    params = pltpu.CompilerParams(vmem_limit_bytes=VMEM_LIMIT_BYTES)
    hs, u = pl.pallas_call(
        _sample_proj_kernel,
        out_shape=(jax.ShapeDtypeStruct((rows, D_MODEL), F32), jax.ShapeDtypeStruct((rows, N_PAD), F32)),
        compiler_params=params,
        name="sample_proj",
    )(x_sample.reshape(rows, D_MODEL), w_all, b_all, ln0_g, ln0_b)

    n_groups = db // SEQ_PER_GROUP
    n_t = jnp.transpose(state_n, (1, 0, 2))
    m_t = jnp.broadcast_to(jnp.transpose(state_m, (1, 0))[:, :, None], (M_HEADS, db, 128))

    def col_spec(col0):
        return pl.BlockSpec((CHUNK, M_DH), lambda g, h: (g, col0 // M_DH + h))

    state_spec = pl.BlockSpec((SEQ_PER_GROUP, 1, M_DH, M_DH), lambda g, h: (g, h, 0, 0))
    vec_spec = pl.BlockSpec((1, SEQ_PER_GROUP, M_DH), lambda g, h: (h, g, 0))
    h_m, c_new, n_new, m_new = pl.pallas_call(
        functools.partial(_sample_mlstm_kernel, dec_seq=dec_seq),
        grid=(n_groups, M_HEADS),
        in_specs=[col_spec(C_Q), col_spec(C_K), col_spec(C_V),
                  pl.BlockSpec((CHUNK, 128), lambda g, h: (g, C_G // 128)),
                  state_spec, vec_spec, vec_spec],
        out_specs=[pl.BlockSpec((CHUNK, M_DH), lambda g, h: (g, h)), state_spec, vec_spec, vec_spec],
        out_shape=(jax.ShapeDtypeStruct((rows, D_MLSTM), F32),
                   jax.ShapeDtypeStruct(state_c.shape, F32),
                   jax.ShapeDtypeStruct((M_HEADS, db, M_DH), F32),
                   jax.ShapeDtypeStruct((M_HEADS, db, 128), F32)),
        scratch_shapes=[pltpu.VMEM((CHUNK, CHUNK), F32), pltpu.VMEM((CHUNK, CHUNK), F32)],
        compiler_params=pltpu.CompilerParams(dimension_semantics=("arbitrary", "arbitrary"),
                                             vmem_limit_bytes=VMEM_LIMIT_BYTES),
        name="sample_mlstm",
    )(u, u, u, u, state_c, n_t, m_t)

    aq = u[:, C_AQ:C_AQ + D_SWA].reshape(db, dec_seq, A_KV_HEADS, A_GROUP, A_DH)
    aq = jnp.transpose(aq, (0, 2, 3, 1, 4)).reshape(db, A_KV_HEADS, A_GROUP * dec_seq, A_DH)
    zeros = jnp.zeros_like(aq[:, 0])
    qz = jnp.stack([jnp.concatenate([aq[:, 0], zeros], axis=-1),
                    jnp.concatenate([zeros, aq[:, 1]], axis=-1)], axis=1)
    n_qrows = A_HEADS * dec_seq
    qz = qz.reshape(db, n_qrows, KV_W)
    k_new = jnp.pad(u[:, C_AK:C_AK + KV_W].reshape(db, dec_seq, KV_W), ((0, 0), (0, 8 - dec_seq), (0, 0)))
    v_new = jnp.pad(u[:, C_AV:C_AV + KV_W].reshape(db, dec_seq, KV_W), ((0, 0), (0, 8 - dec_seq), (0, 0)))
    sink_rows = jnp.broadcast_to(jnp.repeat(a_sinks.astype(F32), dec_seq)[:, None], (n_qrows, 128))
    ck = cache_k.reshape(db, N_BUF, KV_W)
    cv = cache_v.reshape(db, N_BUF, KV_W)
    sb = SWA_SEQ_PER_STEP

    def seq_spec(r):
        return pl.BlockSpec((sb, r, KV_W), lambda i: (i, 0, 0))

    o, nk, nv = pl.pallas_call(
        functools.partial(_sample_swa_kernel, dec_seq=dec_seq),
        grid=(db // sb,),
        in_specs=[seq_spec(n_qrows), seq_spec(8), seq_spec(8), seq_spec(N_BUF), seq_spec(N_BUF),
                  pl.BlockSpec((n_qrows, 128), lambda i: (0, 0))],
        out_specs=[seq_spec(n_qrows), seq_spec(N_BUF), seq_spec(N_BUF)],
        out_shape=(jax.ShapeDtypeStruct((db, n_qrows, KV_W), F32),
                   jax.ShapeDtypeStruct((db, N_BUF, KV_W), F32),
                   jax.ShapeDtypeStruct((db, N_BUF, KV_W), F32)),
        compiler_params=pltpu.CompilerParams(dimension_semantics=("arbitrary",),
                                             vmem_limit_bytes=VMEM_LIMIT_BYTES),
        name="sample_swa",
    )(qz, k_new, v_new, ck, cv, sink_rows)
    o = o.reshape(db, A_KV_HEADS, A_GROUP, dec_seq, A_KV_HEADS, A_DH)
    o = jnp.stack([o[:, 0, :, :, 0, :], o[:, 1, :, :, 1, :]], axis=1)
    o_a = jnp.transpose(o, (0, 3, 1, 2, 4)).reshape(rows, D_SWA)

    y = pl.pallas_call(
        _sample_out_kernel,
        out_shape=jax.ShapeDtypeStruct((rows, D_MODEL), F32),
        compiler_params=params,
        name="sample_out",
    )(hs, u, h_m, o_a, norm_g, w_o, ln_g, ln_b)

    shape5 = (1, db, N_BUF, A_KV_HEADS, A_DH)
    return (y.reshape(db, dec_seq, D_MODEL), nk.reshape(shape5), nv.reshape(shape5), c_new[None],
            jnp.transpose(n_new, (1, 0, 2))[None], jnp.transpose(m_new[:, :, 0], (1, 0))[None])


def kernel(x_prompt, x_sample, cache_swa_k, cache_swa_v, state_mlstm_c, state_mlstm_n, state_mlstm_m,
           meta_tokens, ln0_g, ln0_b, w_in, b_in, a_sinks, m_norm_g, w_out, ln_g, ln_b):
    assert w_in.shape[0] == DEPTH and x_prompt.shape[-1] == D_MODEL
    w = w_in[0]
    pad_w = jnp.zeros((D_MODEL, C_AQ - N_RAW_GATE_END), w.dtype)
    w_all = jnp.concatenate([w[:, :N_RAW_GATE_END], pad_w, w[:, N_RAW_GATE_END:]], axis=1).astype(BF16)
    b = b_in[0].astype(F32)
    b_all = jnp.concatenate([b[:N_RAW_GATE_END], jnp.zeros((C_AQ - N_RAW_GATE_END,), F32),
                             b[N_RAW_GATE_END:]])[None]
    w_o = w_out[0].astype(BF16)
    g0 = ln0_g.astype(F32)[None]
    b0 = ln0_b.astype(F32)[None]
    lg = ln_g[0].astype(F32)[None]
    lb = ln_b[0].astype(F32)[None]
    norm_g = m_norm_g[0].astype(F32)[None]
    sinks = a_sinks[0].astype(F32)

    y_p, pk, pv, pc, pn, pm = _prompt_path(x_prompt, meta_tokens, g0, b0, w_all, b_all, sinks, norm_g, w_o, lg, lb)
    y_s, sk, sv, sc, sn, sm = _sample_path(x_sample, cache_swa_k[0], cache_swa_v[0], state_mlstm_c[0],
                                           state_mlstm_n[0], state_mlstm_m[0],
                                           g0, b0, w_all, b_all, sinks, norm_g, w_o, lg, lb)
    return (y_p, y_s, pk, pv, pc, pn, pm, sk, sv, sc, sn, sm)
```

```python
import functools

import jax
import jax.numpy as jnp
from jax import lax
from jax.experimental import pallas as pl
from jax.experimental.pallas import tpu as pltpu

F32 = jnp.float32
BF16 = jnp.bfloat16

D_MODEL = 1024
N_META = 16
M_HEADS = 4
M_DH = 128
D_MLSTM = M_HEADS * M_DH
A_HEADS = 8
A_KV_HEADS = 2
A_GROUP = A_HEADS // A_KV_HEADS
A_DH = 64
D_SWA = A_HEADS * A_DH
KV_W = A_KV_HEADS * A_DH
WINDOW = 128
CHUNK = 128
LN_EPS = 1e-5
DEPTH = 1
DN_ALPHA = (2.0 * DEPTH) ** 0.25
N_BUF = N_META + WINDOW

C_Q, C_K, C_V, C_O, C_Z = 0, 512, 1024, 1536, 2048
C_G = 2560
C_AQ, C_AK, C_AV, C_AZ = 2688, 3200, 3328, 3456
N_PAD = 3968
N_RAW_GATE_END = 2568

PROJ_COL_STEP = 512
VMEM_LIMIT_BYTES = 56 * 1024 * 1024
NEG_INF = float("-inf")


def _dot(a, b):
    return jnp.dot(a, b, preferred_element_type=F32)


def _dot_nt(a, b):
    return lax.dot_general(a, b, (((1,), (1,)), ((), ())), preferred_element_type=F32)


def _dot_exact(a, b):
    return jnp.dot(a, b, precision=lax.Precision.HIGHEST, preferred_element_type=F32)


def _layer_norm(x, g, b):
    mu = jnp.mean(x, axis=-1, keepdims=True)
    xc = x - mu
    var = jnp.mean(xc * xc, axis=-1, keepdims=True)
    return xc * lax.rsqrt(var + LN_EPS) * g + b


def _log_sigmoid(x):
    return jnp.minimum(x, 0.0) - jnp.log1p(jnp.exp(-jnp.abs(x)))


def _sigmoid(x):
    return 0.5 * jnp.tanh(0.5 * x) + 0.5


def _iota2(shape, dim):
    return lax.broadcasted_iota(jnp.int32, shape, dim)


def _project(hb, w_ref, b_ref, u_ref):
    slabs = [(C_G, C_AQ)]
    for lo, hi in ((0, C_G), (C_AQ, N_PAD)):
        slabs += [(n0, min(n0 + PROJ_COL_STEP, hi)) for n0 in range(lo, hi, PROJ_COL_STEP)]
    for n0, n1 in slabs:
        u_ref[:, n0:n1] = _dot(hb, w_ref[:, n0:n1]) + b_ref[:, n0:n1]


def _mlstm_intra(q, k, v, li_col, b_col, m_old, allowed):
    n = q.shape[0]
    a_col = li_col - b_col
    eye = _iota2((n, n), 0) == _iota2((n, n), 1)
    a_row = jnp.sum(jnp.where(eye, a_col, 0.0), axis=0, keepdims=True)
    a_mat = jnp.where(allowed, a_row, NEG_INF)
    mm = jnp.maximum(m_old, jnp.max(a_mat, axis=1, keepdims=True))
    w = jnp.exp(a_mat - mm)
    qb = q.astype(BF16)
    qkw = _dot_nt(qb, k.astype(BF16)) * w
    den = jnp.sum(qkw, axis=1, keepdims=True)
    num = _dot(qkw.astype(BF16), v.astype(BF16))
    return num, den, mm, a_col


def _gate_rows(gates):
    g_t = jnp.concatenate([gates[r0:r0 + CHUNK].T[0:8, :] for r0 in range(0, gates.shape[0], CHUNK)], axis=0)
    x = _log_sigmoid(g_t)
    lane = _iota2(x.shape, 1)
    shift = 1
    while shift < x.shape[1]:
        x = x + jnp.where(lane >= shift, pltpu.roll(x, shift, 1), 0.0)
        shift *= 2
    return g_t, x


CT_ROWS = M_DH + 8


def _mlstm_scores_and_state(q, k, v, li_row, b_row, m_old, ct_aug, key_ok, n_valid):
    n = q.shape[0]
    row = _iota2((n, n), 0)
    col = _iota2((n, n), 1)
    a_row = li_row - b_row
    a_keys = jnp.broadcast_to(jnp.sum(jnp.where(row == col, a_row, 0.0), axis=1, keepdims=True), (n, n))
    a_t = jnp.where(key_ok, a_keys, NEG_INF)
    mm = jnp.maximum(m_old, jnp.max(a_t, axis=0, keepdims=True))
    w_t = jnp.exp(a_t - mm)
    lane = _iota2((1, n), 1)
    last = n_valid - 1
    mm_last = jnp.max(jnp.where(lane == last, mm, NEG_INF), axis=1, keepdims=True)
    m_new = jnp.sum(jnp.where(lane == last, b_row, 0.0), axis=1, keepdims=True) + mm_last
    w_state = jnp.exp(a_row - mm_last)
    if n_valid < n:
        w_state = jnp.where(lane < n_valid, w_state, 0.0)
    decay = jnp.exp(m_old - mm_last)
    qb = q.astype(BF16)
    kb = k.astype(BF16)
    v_t = v.T
    scores_t = _dot_nt(kb, qb)
    inter = _dot_nt(ct_aug.astype(BF16), qb)
    ones_row = jnp.where(_iota2((CT_ROWS - M_DH, n), 0) == 0, w_state, 0.0)
    vtw = jnp.concatenate([v_t * w_state, ones_row], axis=0).astype(BF16)
    ct_aug_new = decay * ct_aug + _dot(vtw, kb)
    s_inter = jnp.exp(m_old - mm)
    floor = jnp.exp(-(b_row + mm))
    return (scores_t, w_t, v_t.astype(BF16), inter, s_inter, floor), ct_aug_new, m_new


def _mlstm_weighted_values(scores_t, w_t, vtb, inter, s_inter, floor):
    qkw_t = scores_t * w_t
    den = jnp.sum(qkw_t, axis=0, keepdims=True)
    num_t = _dot(vtb, qkw_t.astype(BF16))
    return num_t, den, inter, s_inter, floor


def _mlstm_finish(num_t, den, inter, s_inter, floor):
    num_t = num_t + inter[0:M_DH] * s_inter
    den = den + inter[M_DH:M_DH + 1] * s_inter
    return num_t * (1.0 / jnp.maximum(jnp.abs(den), floor))


def _silu(x):
    return x * _sigmoid(x)


def _mlstm_gate_head(h, mo, mz, norm_g):
    hh = h * _sigmoid(mo)
    mu = jnp.mean(hh, axis=-1, keepdims=True)
    hc = hh - mu
    var = jnp.mean(hc * hc, axis=-1, keepdims=True)
    return (hc * lax.rsqrt(var + LN_EPS) * norm_g * _silu(mz)).astype(BF16)


def _gate_mix(h_m, mo, mz, o_a, az, norm_g):
    parts = []
    for hd in range(M_HEADS):
        sl = slice(hd * M_DH, (hd + 1) * M_DH)
        parts.append(_mlstm_gate_head(h_m[:, sl], mo[:, sl], mz[:, sl], norm_g[:, sl]))
    parts.append((o_a * _silu(az)).astype(BF16))
    return jnp.concatenate(parts, axis=-1)


def _out_and_norm(hp, mix, wo_ref, g, b):
    z = DN_ALPHA * hp + _dot(mix, wo_ref[...])
    return _layer_norm(z, g, b)


def _meta_kernel(meta_ref, w_ref, bias_ref, g0_ref, b0_ref,
                 ct0_ref, m0_ref, km_ref, vm_ref, vmt_ref, u_ref):
    hp = _layer_norm(meta_ref[...], g0_ref[...], b0_ref[...])
    _project(hp.astype(BF16), w_ref, bias_ref, u_ref)
    row = _iota2((CHUNK, CHUNK), 0)
    col = _iota2((CHUNK, CHUNK), 1)
    key_ok = (row <= col) & (row < N_META)
    li_rows, b_rows = _gate_rows(u_ref[:, C_G:C_G + 128])
    zero_m = jnp.zeros((1, 128), F32)
    zero_ct = jnp.zeros((CT_ROWS, M_DH), F32)
    m0_ref[...] = jnp.zeros(m0_ref.shape, F32)
    for hd in range(M_HEADS):
        q = u_ref[:, C_Q + hd * M_DH:C_Q + (hd + 1) * M_DH]
        k = u_ref[:, C_K + hd * M_DH:C_K + (hd + 1) * M_DH] * (M_DH ** -0.5)
        v = u_ref[:, C_V + hd * M_DH:C_V + (hd + 1) * M_DH]
        _, ct_new, m_new = _mlstm_scores_and_state(
            q, k, v, li_rows[hd:hd + 1], b_rows[M_HEADS + hd:M_HEADS + hd + 1],
            zero_m, zero_ct, key_ok, N_META)
        ct0_ref[hd] = ct_new
        m0_ref[hd:hd + 1, :] = jnp.broadcast_to(m_new, (1, 128))
    km_ref[...] = u_ref[0:N_META, C_AK:C_AK + KV_W]
    vm_ref[...] = u_ref[0:N_META, C_AV:C_AV + KV_W]
    vmt_ref[...] = u_ref[:, C_AV:C_AV + KV_W].T[:, 0:N_META]


def _keep_kv_half(x, kv):
    low = _iota2(x.shape, 1) < A_DH
    return jnp.where(low if kv == 0 else ~low, x, 0.0)


def _swa_scores_t(tile, q_tile, keys):
    kv = tile // (A_GROUP // 2)
    q_scaled = q_tile * (A_DH ** -0.5)
    q_swapped = pltpu.roll(q_scaled, A_DH, 1)
    out = []
    for half in range(2):
        qz = (q_scaled if half == kv else q_swapped).astype(BF16)
        out.append(tuple(_dot_nt(kk, qz) for kk in keys))
    return out


def _swa_weighted_values_t(scores, values_t, sink, own_ok, prev_ok):
    s_own = jnp.where(own_ok, scores[0], NEG_INF)
    s_prev = jnp.where(prev_ok, scores[1], NEG_INF)
    s_meta = scores[2]
    mx = jnp.maximum(jnp.maximum(jnp.max(s_own, axis=0, keepdims=True),
                                 jnp.max(s_prev, axis=0, keepdims=True)),
                     jnp.maximum(jnp.max(s_meta, axis=0, keepdims=True), sink))
    p_own, p_prev, p_meta = (jnp.exp(s - mx) for s in (s_own, s_prev, s_meta))
    den = (jnp.sum(p_own, axis=0, keepdims=True) + jnp.sum(p_prev, axis=0, keepdims=True)
           + jnp.sum(p_meta, axis=0, keepdims=True) + jnp.exp(sink - mx))
    o_t = (_dot(values_t[0], p_own.astype(BF16)) + _dot(values_t[1], p_prev.astype(BF16))
           + _dot(values_t[2], p_meta.astype(BF16)))
    return o_t, den


def _prompt_kernel(sink_ref, x_ref, w_ref, bias_ref, g0_ref, b0_ref, ct0_ref, m0_ref,
                   km_ref, vm_ref, vmt_ref, ng_ref, wo_ref, lng_ref, lnb_ref,
                   y_ref, pk_ref, pv_ref, pc_ref, pn_ref, pm_ref,
                   u_ref, hp_ref, mix_ref, ct_ref, kprev_ref, vtprev_ref, li_ref, cumf_ref, *, tb):
    j = pl.program_id(1)
    n_chunks = tb // CHUNK

    @pl.when(j == 0)
    def _():
        ct_ref[...] = ct0_ref[...]
        pm_ref[0] = m0_ref[...]
        kprev_ref[...] = jnp.zeros(kprev_ref.shape, F32)
        vtprev_ref[...] = jnp.zeros(vtprev_ref.shape, F32)
        pk_ref[0, 0:N_META, :] = km_ref[...]
        pv_ref[0, 0:N_META, :] = vm_ref[...]

    hp = _layer_norm(x_ref[0], g0_ref[...], b0_ref[...])
    hp_ref[...] = hp
    _project(hp.astype(BF16), w_ref, bias_ref, u_ref)
    li_ref[...], cumf_ref[...] = _gate_rows(u_ref[:, C_G:C_G + 128])

    def chunk_body(ci, carry):
        r0 = pl.multiple_of(ci * CHUNK, CHUNK)
        rows = pl.ds(r0, CHUNK)
        key = _iota2((CHUNK, CHUNK), 0)
        query = _iota2((CHUNK, CHUNK), 1)
        causal = key <= query

        gate_rows = pl.ds(pl.multiple_of(ci * 8, 8), 8)
        li_rows = li_ref[gate_rows, :]
        b_rows = cumf_ref[gate_rows, :]
        m_carry = []
        for hd in range(M_HEADS):
            q = u_ref[rows, C_Q + hd * M_DH:C_Q + (hd + 1) * M_DH]
            k = u_ref[rows, C_K + hd * M_DH:C_K + (hd + 1) * M_DH] * (M_DH ** -0.5)
            v = u_ref[rows, C_V + hd * M_DH:C_V + (hd + 1) * M_DH]
            carry_hd, ct_new, m_new = _mlstm_scores_and_state(
                q, k, v, li_rows[hd:hd + 1], b_rows[M_HEADS + hd:M_HEADS + hd + 1],
                pm_ref[0, hd:hd + 1, :], ct_ref[hd], causal, CHUNK)
            ct_ref[hd] = ct_new
            pm_ref[0, hd:hd + 1, :] = jnp.broadcast_to(m_new, (1, 128))
            m_carry.append(carry_hd)

        k_own = u_ref[rows, C_AK:C_AK + KV_W]
        vt_own = u_ref[rows, C_AV:C_AV + KV_W].T
        k_prev = kprev_ref[...]
        vt_prev = vtprev_ref[...]
        prev_ok = (key > query) & ((j * n_chunks + ci) > 0)
        keys_kv = [tuple(_keep_kv_half(x, kv).astype(BF16) for x in (k_own, k_prev, km_ref[...]))
                   for kv in range(A_KV_HEADS)]
        values_kv = [tuple(x[kv * A_DH:(kv + 1) * A_DH].astype(BF16) for x in (vt_own, vt_prev, vmt_ref[...]))
                     for kv in range(A_KV_HEADS)]
        a_scores = []
        for tile in range(A_HEADS // 2):
            a_scores += _swa_scores_t(tile, u_ref[rows, C_AQ + tile * 128:C_AQ + (tile + 1) * 128],
                                      keys_kv[tile // (A_GROUP // 2)])
        kprev_ref[...] = k_own
        vtprev_ref[...] = vt_own

        m_carry = [_mlstm_weighted_values(*c) for c in m_carry]
        a_out = [_swa_weighted_values_t(a_scores[hd], values_kv[hd // A_GROUP], sink_ref[hd], causal, prev_ok)
                 for hd in range(A_HEADS)]

        for hd in range(M_HEADS):
            sl = slice(hd * M_DH, (hd + 1) * M_DH)
            mix_ref[rows, sl] = _mlstm_gate_head(
                _mlstm_finish(*m_carry[hd]).T, u_ref[rows, C_O + hd * M_DH:C_O + (hd + 1) * M_DH],
                u_ref[rows, C_Z + hd * M_DH:C_Z + (hd + 1) * M_DH], ng_ref[:, sl])
        for tile in range(A_HEADS // 2):
            o_tile = jnp.concatenate([o_t * (1.0 / den) for o_t, den in a_out[2 * tile:2 * tile + 2]], axis=0).T
            az = u_ref[rows, C_AZ + tile * 128:C_AZ + (tile + 1) * 128]
            mix_ref[rows, D_MLSTM + tile * 128:D_MLSTM + (tile + 1) * 128] = (o_tile * _silu(az)).astype(BF16)
        return carry

    lax.fori_loop(0, n_chunks, chunk_body, 0)

    y_ref[0] = _out_and_norm(hp_ref[...], mix_ref[...], wo_ref, lng_ref[...], lnb_ref[...])

    @pl.when(j == pl.num_programs(1) - 1)
    def _():
        pk_ref[0, N_META:N_BUF, :] = u_ref[tb - WINDOW:tb, C_AK:C_AK + KV_W]
        pv_ref[0, N_META:N_BUF, :] = u_ref[tb - WINDOW:tb, C_AV:C_AV + KV_W]
        for hd in range(M_HEADS):
            pc_ref[0, hd] = ct_ref[hd, 0:M_DH, :].T
            pn_ref[0, hd:hd + 1, :] = ct_ref[hd, M_DH:M_DH + 1, :]
        pn_ref[0, M_HEADS:8, :] = jnp.zeros((8 - M_HEADS, M_DH), F32)


def _const_spec(shape):
    return pl.BlockSpec(shape, lambda *_: (0,) * len(shape))


def _prompt_path(x_prompt, meta_tokens, ln0_g, ln0_b, w_all, b_all, sinks, norm_g, w_o, ln_g, ln_b, tb=512):
    batch, seq, _ = x_prompt.shape
    meta_pad = jnp.pad(meta_tokens.astype(F32), ((0, CHUNK - N_META), (0, 0)))
    ct0, m0, km, vm, vmt = pl.pallas_call(
        _meta_kernel,
        out_shape=(jax.ShapeDtypeStruct((M_HEADS, CT_ROWS, M_DH), F32),
                   jax.ShapeDtypeStruct((8, 128), F32),
                   jax.ShapeDtypeStruct((N_META, KV_W), F32),
                   jax.ShapeDtypeStruct((N_META, KV_W), F32),
                   jax.ShapeDtypeStruct((KV_W, N_META), F32)),
        scratch_shapes=[pltpu.VMEM((CHUNK, N_PAD), F32)],
        compiler_params=pltpu.CompilerParams(vmem_limit_bytes=VMEM_LIMIT_BYTES),
        name="meta_tokens",
    )(meta_pad, w_all, b_all, ln0_g, ln0_b)

    nj = seq // tb
    in_specs = [
        pl.BlockSpec(memory_space=pltpu.SMEM),
        pl.BlockSpec((1, tb, D_MODEL), lambda b, j: (b, j, 0)),
        _const_spec((D_MODEL, N_PAD)), _const_spec((1, N_PAD)),
        _const_spec((1, D_MODEL)), _const_spec((1, D_MODEL)),
        _const_spec((M_HEADS, CT_ROWS, M_DH)), _const_spec((8, 128)),
        _const_spec((N_META, KV_W)), _const_spec((N_META, KV_W)), _const_spec((KV_W, N_META)),
        _const_spec((1, D_MLSTM)),
        _const_spec((D_MODEL, D_MODEL)),
        _const_spec((1, D_MODEL)), _const_spec((1, D_MODEL)),
    ]
    out_specs = [
        pl.BlockSpec((1, tb, D_MODEL), lambda b, j: (b, j, 0)),
        pl.BlockSpec((1, N_BUF, KV_W), lambda b, j: (b, 0, 0)),
        pl.BlockSpec((1, N_BUF, KV_W), lambda b, j: (b, 0, 0)),
        pl.BlockSpec((1, M_HEADS, M_DH, M_DH), lambda b, j: (b, 0, 0, 0)),
        pl.BlockSpec((1, 8, M_DH), lambda b, j: (b, 0, 0)),
        pl.BlockSpec((1, 8, 128), lambda b, j: (b, 0, 0)),
    ]
    out_shape = (
        jax.ShapeDtypeStruct((batch, seq, D_MODEL), F32),
        jax.ShapeDtypeStruct((batch, N_BUF, KV_W), F32),
        jax.ShapeDtypeStruct((batch, N_BUF, KV_W), F32),
        jax.ShapeDtypeStruct((batch, M_HEADS, M_DH, M_DH), F32),
        jax.ShapeDtypeStruct((batch, 8, M_DH), F32),
        jax.ShapeDtypeStruct((batch, 8, 128), F32),
    )
    y, pk, pv, pc, pn, pm = pl.pallas_call(
        functools.partial(_prompt_kernel, tb=tb),
        grid=(batch, nj),
        in_specs=in_specs,
        out_specs=out_specs,
        out_shape=out_shape,
        scratch_shapes=[pltpu.VMEM((tb, N_PAD), F32), pltpu.VMEM((tb, D_MODEL), F32),
                        pltpu.VMEM((tb, D_MODEL), BF16),
                        pltpu.VMEM((M_HEADS, CT_ROWS, M_DH), F32),
                        pltpu.VMEM((CHUNK, KV_W), F32), pltpu.VMEM((KV_W, CHUNK), F32),
                        pltpu.VMEM((tb // CHUNK * 8, CHUNK), F32), pltpu.VMEM((tb // CHUNK * 8, CHUNK), F32)],
        compiler_params=pltpu.CompilerParams(dimension_semantics=("arbitrary", "arbitrary"),
                                             vmem_limit_bytes=VMEM_LIMIT_BYTES),
        name="prompt_layer",
    )(sinks, x_prompt, w_all, b_all, ln0_g, ln0_b, ct0, m0, km, vm, vmt, norm_g, w_o, ln_g, ln_b)
    pk = pk.reshape(1, batch, N_BUF, A_KV_HEADS, A_DH)
    pv = pv.reshape(1, batch, N_BUF, A_KV_HEADS, A_DH)
    return y, pk, pv, pc[None], pn[:, :M_HEADS][None], pm[:, :M_HEADS, 0][None]


SEQ_PER_GROUP = 32
SWA_SEQ_PER_STEP = 8


def _sample_proj_kernel(x_ref, w_ref, bias_ref, g0_ref, b0_ref, hs_ref, u_ref):
    hs = _layer_norm(x_ref[...], g0_ref[...], b0_ref[...])
    hs_ref[...] = hs
    _project(hs.astype(BF16), w_ref, bias_ref, u_ref)


def _sample_mlstm_kernel(q_ref, k_ref, v_ref, g_ref, c_ref, n_ref, m_ref,
                         h_ref, cn_ref, nn_ref, mn_ref, decay_ref, kwt_ref, *, dec_seq):
    hd = pl.program_id(1)
    n = CHUNK
    nb = n // dec_seq
    q = q_ref[...]
    k = k_ref[...] * (M_DH ** -0.5)
    v = v_ref[...]
    gates = g_ref[...]
    row = _iota2((n, n), 0)
    col = _iota2((n, n), 1)
    row_seq = row // dec_seq
    col_seq = col // dec_seq
    allowed = (row_seq == col_seq) & (col <= row)
    b_all = _dot_exact(jnp.where(allowed, 1.0, 0.0), _log_sigmoid(gates))
    li_col = jnp.sum(jnp.where(col == hd, gates, 0.0), axis=1, keepdims=True)
    b_col = jnp.sum(jnp.where(col == hd + M_HEADS, b_all, 0.0), axis=1, keepdims=True)

    expand = jnp.where(_iota2((n, nb), 0) // dec_seq == _iota2((n, nb), 1), 1.0, 0.0)
    m_seq = m_ref[0]
    n_seq = n_ref[0]
    m_old = _dot_exact(expand, m_seq)[:, 0:1]
    n_rows = _dot_exact(expand, n_seq)

    num_i, den_i, mm, a_col = _mlstm_intra(q, k, v, li_col, b_col, m_old, allowed)
    sel_last = jnp.where(col == row_seq * dec_seq + (dec_seq - 1), 1.0, 0.0)
    packed = jnp.where(col == 0, mm, jnp.where(col == 1, b_col, 0.0))
    last_vals = _dot_exact(sel_last, packed)
    mm_last = last_vals[:, 0:1]
    b_last = last_vals[:, 1:2]
    m_t = b_col + mm
    w_state = jnp.exp(a_col - mm_last)
    decay_col = jnp.exp(m_old - mm_last)
    kw = k * w_state
    kwt_ref[...] = kw.T
    decay_ref[...] = jnp.broadcast_to(decay_col, (n, n))

    pick = jnp.where(_iota2((nb, n), 1) == _iota2((nb, n), 0) * dec_seq + (dec_seq - 1), 1.0, 0.0)
    seg = jnp.where(_iota2((nb, n), 1) // dec_seq == _iota2((nb, n), 0), 1.0, 0.0)
    decay_seq = _dot_exact(pick, decay_ref[...])
    nn_ref[0] = decay_seq * n_seq + _dot_exact(seg, kw)
    mn_ref[0] = _dot_exact(pick, jnp.broadcast_to(b_last + mm_last, (n, n)))

    qb = q.astype(BF16)
    vb = v.astype(BF16)

    def seq_body(s, acc):
        c_s = c_ref[s, 0]
        qc = _dot(qb, c_s.astype(BF16))
        acc = acc + jnp.where(row_seq == s, qc, 0.0)
        kwt_s = jnp.where(col_seq == s, kwt_ref[...], 0.0).astype(BF16)
        decay_s = decay_ref[pl.ds(s * dec_seq + (dec_seq - 1), 1), :]
        cn_ref[s, 0] = decay_s * c_s + _dot(kwt_s, vb)
        return acc

    qc_all = lax.fori_loop(0, nb, seq_body, jnp.zeros((n, n), F32))
    s_inter = jnp.exp(m_old - mm)
    qn = jnp.sum(q * n_rows, axis=1, keepdims=True)
    num = num_i + qc_all * s_inter
    den = den_i + qn * s_inter
    h_ref[...] = num / jnp.maximum(jnp.abs(den), jnp.exp(-m_t))


def _sample_swa_kernel(qz_ref, kn_ref, vn_ref, ck_ref, cv_ref, sink_ref, o_ref, nk_ref, nv_ref, *, dec_seq):
    n_rows = qz_ref.shape[1]
    t_c = _iota2((n_rows, N_BUF), 0) % dec_seq
    i_c = _iota2((n_rows, N_BUF), 1)
    ok_c = (i_c < N_META) | (i_c > t_c + N_META)
    t_n = _iota2((n_rows, 8), 0) % dec_seq
    i_n = _iota2((n_rows, 8), 1)
    ok_n = (i_n <= t_n) & (i_n < dec_seq)
    sink = sink_ref[:, 0:1]

    def seq_body(s, carry):
        qz = (qz_ref[s] * (A_DH ** -0.5)).astype(BF16)
        kc = ck_ref[s]
        vc = cv_ref[s]
        kn = kn_ref[s]
        vn = vn_ref[s]
        s_c = jnp.where(ok_c, _dot_nt(qz, kc.astype(BF16)), NEG_INF)
        s_n = jnp.where(ok_n, _dot_nt(qz, kn.astype(BF16)), NEG_INF)
        mx = jnp.maximum(jnp.maximum(jnp.max(s_c, axis=1, keepdims=True),
                                     jnp.max(s_n, axis=1, keepdims=True)), sink)
        p_c = jnp.exp(s_c - mx)
        p_n = jnp.exp(s_n - mx)
        den = jnp.sum(p_c, axis=1, keepdims=True) + jnp.sum(p_n, axis=1, keepdims=True) + jnp.exp(sink - mx)
        o_ref[s] = (_dot(p_c.astype(BF16), vc.astype(BF16)) + _dot(p_n.astype(BF16), vn.astype(BF16))) / den
        nk_ref[s, 0:N_META, :] = kc[0:N_META]
        nk_ref[s, N_META:N_BUF - dec_seq, :] = kc[N_META + dec_seq:N_BUF]
        nk_ref[s, N_BUF - dec_seq:N_BUF, :] = kn[0:dec_seq]
        nv_ref[s, 0:N_META, :] = vc[0:N_META]
        nv_ref[s, N_META:N_BUF - dec_seq, :] = vc[N_META + dec_seq:N_BUF]
        nv_ref[s, N_BUF - dec_seq:N_BUF, :] = vn[0:dec_seq]
        return carry

    lax.fori_loop(0, qz_ref.shape[0], seq_body, 0)


def _sample_out_kernel(hs_ref, u_ref, hm_ref, oa_ref, ng_ref, wo_ref, lng_ref, lnb_ref, y_ref):
    mix = _gate_mix(hm_ref[...], u_ref[:, C_O:C_O + D_MLSTM], u_ref[:, C_Z:C_Z + D_MLSTM],
                    oa_ref[...], u_ref[:, C_AZ:C_AZ + D_SWA], ng_ref[...])
    y_ref[...] = _out_and_norm(hs_ref[...], mix, wo_ref, lng_ref[...], lnb_ref[...])


def _sample_path(x_sample, cache_k, cache_v, state_c, state_n, state_m,
                 ln0_g, ln0_b, w_all, b_all, a_sinks, norm_g, w_o, ln_g, ln_b):
    db, dec_seq, _ = x_sample.shape
    rows = db * dec_seq
    params = pltpu.CompilerParams(vmem_limit_bytes=VMEM_LIMIT_BYTES)
    hs, u = pl.pallas_call(
        _sample_proj_kernel,
        out_shape=(jax.ShapeDtypeStruct((rows, D_MODEL), F32), jax.ShapeDtypeStruct((rows, N_PAD), F32)),
        compiler_params=params,
        name="sample_proj",
    )(x_sample.reshape(rows, D_MODEL), w_all, b_all, ln0_g, ln0_b)

    n_groups = db // SEQ_PER_GROUP
    n_t = jnp.transpose(state_n, (1, 0, 2))
    m_t = jnp.broadcast_to(jnp.transpose(state_m, (1, 0))[:, :, None], (M_HEADS, db, 128))

    def col_spec(col0):
        return pl.BlockSpec((CHUNK, M_DH), lambda g, h: (g, col0 // M_DH + h))

    state_spec = pl.BlockSpec((SEQ_PER_GROUP, 1, M_DH, M_DH), lambda g, h: (g, h, 0, 0))
    vec_spec = pl.BlockSpec((1, SEQ_PER_GROUP, M_DH), lambda g, h: (h, g, 0))
    h_m, c_new, n_new, m_new = pl.pallas_call(
        functools.partial(_sample_mlstm_kernel, dec_seq=dec_seq),
        grid=(n_groups, M_HEADS),
        in_specs=[col_spec(C_Q), col_spec(C_K), col_spec(C_V),
                  pl.BlockSpec((CHUNK, 128), lambda g, h: (g, C_G // 128)),
                  state_spec, vec_spec, vec_spec],
        out_specs=[pl.BlockSpec((CHUNK, M_DH), lambda g, h: (g, h)), state_spec, vec_spec, vec_spec],
        out_shape=(jax.ShapeDtypeStruct((rows, D_MLSTM), F32),
                   jax.ShapeDtypeStruct(state_c.shape, F32),
                   jax.ShapeDtypeStruct((M_HEADS, db, M_DH), F32),
                   jax.ShapeDtypeStruct((M_HEADS, db, 128), F32)),
        scratch_shapes=[pltpu.VMEM((CHUNK, CHUNK), F32), pltpu.VMEM((CHUNK, CHUNK), F32)],
        compiler_params=pltpu.CompilerParams(dimension_semantics=("arbitrary", "arbitrary"),
                                             vmem_limit_bytes=VMEM_LIMIT_BYTES),
        name="sample_mlstm",
    )(u, u, u, u, state_c, n_t, m_t)

    aq = u[:, C_AQ:C_AQ + D_SWA].reshape(db, dec_seq, A_KV_HEADS, A_GROUP, A_DH)
    aq = jnp.transpose(aq, (0, 2, 3, 1, 4)).reshape(db, A_KV_HEADS, A_GROUP * dec_seq, A_DH)
    zeros = jnp.zeros_like(aq[:, 0])
    qz = jnp.stack([jnp.concatenate([aq[:, 0], zeros], axis=-1),
                    jnp.concatenate([zeros, aq[:, 1]], axis=-1)], axis=1)
    n_qrows = A_HEADS * dec_seq
    qz = qz.reshape(db, n_qrows, KV_W)
    k_new = jnp.pad(u[:, C_AK:C_AK + KV_W].reshape(db, dec_seq, KV_W), ((0, 0), (0, 8 - dec_seq), (0, 0)))
    v_new = jnp.pad(u[:, C_AV:C_AV + KV_W].reshape(db, dec_seq, KV_W), ((0, 0), (0, 8 - dec_seq), (0, 0)))
    sink_rows = jnp.broadcast_to(jnp.repeat(a_sinks.astype(F32), dec_seq)[:, None], (n_qrows, 128))
    ck = cache_k.reshape(db, N_BUF, KV_W)
    cv = cache_v.reshape(db, N_BUF, KV_W)
    sb = SWA_SEQ_PER_STEP

    def seq_spec(r):
        return pl.BlockSpec((sb, r, KV_W), lambda i: (i, 0, 0))

    o, nk, nv = pl.pallas_call(
        functools.partial(_sample_swa_kernel, dec_seq=dec_seq),
        grid=(db // sb,),
        in_specs=[seq_spec(n_qrows), seq_spec(8), seq_spec(8), seq_spec(N_BUF), seq_spec(N_BUF),
                  pl.BlockSpec((n_qrows, 128), lambda i: (0, 0))],
        out_specs=[seq_spec(n_qrows), seq_spec(N_BUF), seq_spec(N_BUF)],
        out_shape=(jax.ShapeDtypeStruct((db, n_qrows, KV_W), F32),
                   jax.ShapeDtypeStruct((db, N_BUF, KV_W), F32),
                   jax.ShapeDtypeStruct((db, N_BUF, KV_W), F32)),
        compiler_params=pltpu.CompilerParams(dimension_semantics=("arbitrary",),
                                             vmem_limit_bytes=VMEM_LIMIT_BYTES),
        name="sample_swa",
    )(qz, k_new, v_new, ck, cv, sink_rows)
    o = o.reshape(db, A_KV_HEADS, A_GROUP, dec_seq, A_KV_HEADS, A_DH)
    o = jnp.stack([o[:, 0, :, :, 0, :], o[:, 1, :, :, 1, :]], axis=1)
    o_a = jnp.transpose(o, (0, 3, 1, 2, 4)).reshape(rows, D_SWA)

    y = pl.pallas_call(
        _sample_out_kernel,
        out_shape=jax.ShapeDtypeStruct((rows, D_MODEL), F32),
        compiler_params=params,
        name="sample_out",
    )(hs, u, h_m, o_a, norm_g, w_o, ln_g, ln_b)

    shape5 = (1, db, N_BUF, A_KV_HEADS, A_DH)
    return (y.reshape(db, dec_seq, D_MODEL), nk.reshape(shape5), nv.reshape(shape5), c_new[None],
            jnp.transpose(n_new, (1, 0, 2))[None], jnp.transpose(m_new[:, :, 0], (1, 0))[None])


def kernel(x_prompt, x_sample, cache_swa_k, cache_swa_v, state_mlstm_c, state_mlstm_n, state_mlstm_m,
           meta_tokens, ln0_g, ln0_b, w_in, b_in, a_sinks, m_norm_g, w_out, ln_g, ln_b):
    assert w_in.shape[0] == DEPTH and x_prompt.shape[-1] == D_MODEL
    w = w_in[0]
    pad_w = jnp.zeros((D_MODEL, C_AQ - N_RAW_GATE_END), w.dtype)
    w_all = jnp.concatenate([w[:, :N_RAW_GATE_END], pad_w, w[:, N_RAW_GATE_END:]], axis=1).astype(BF16)
    b = b_in[0].astype(F32)
    b_all = jnp.concatenate([b[:N_RAW_GATE_END], jnp.zeros((C_AQ - N_RAW_GATE_END,), F32),
                             b[N_RAW_GATE_END:]])[None]
    w_o = w_out[0].astype(BF16)
    g0 = ln0_g.astype(F32)[None]
    b0 = ln0_b.astype(F32)[None]
    lg = ln_g[0].astype(F32)[None]
    lb = ln_b[0].astype(F32)[None]
    norm_g = m_norm_g[0].astype(F32)[None]
    sinks = a_sinks[0].astype(F32)

    y_p, pk, pv, pc, pn, pm = _prompt_path(x_prompt, meta_tokens, g0, b0, w_all, b_all, sinks, norm_g, w_o, lg, lb)
    y_s, sk, sv, sc, sn, sm = _sample_path(x_sample, cache_swa_k[0], cache_swa_v[0], state_mlstm_c[0],
                                           state_mlstm_n[0], state_mlstm_m[0],
                                           g0, b0, w_all, b_all, sinks, norm_g, w_o, lg, lb)
    return (y_p, y_s, pk, pv, pc, pn, pm, sk, sv, sc, sn, sm)
```

```python
import functools

import jax
import jax.numpy as jnp
from jax import lax
from jax.experimental import pallas as pl
from jax.experimental.pallas import tpu as pltpu

F32 = jnp.float32
BF16 = jnp.bfloat16

D_MODEL = 1024
N_META = 16
M_HEADS = 4
M_DH = 128
D_MLSTM = M_HEADS * M_DH
A_HEADS = 8
A_KV_HEADS = 2
A_GROUP = A_HEADS // A_KV_HEADS
A_DH = 64
D_SWA = A_HEADS * A_DH
KV_W = A_KV_HEADS * A_DH
WINDOW = 128
CHUNK = 128
LN_EPS = 1e-5
DEPTH = 1
DN_ALPHA = (2.0 * DEPTH) ** 0.25
N_BUF = N_META + WINDOW

C_Q, C_K, C_V, C_O, C_Z = 0, 512, 1024, 1536, 2048
C_G = 2560
C_AQ, C_AK, C_AV, C_AZ = 2688, 3200, 3328, 3456
N_PAD = 3968
N_RAW_GATE_END = 2568

PROJ_COL_STEP = 512
VMEM_LIMIT_BYTES = 56 * 1024 * 1024
NEG_INF = float("-inf")


def _dot(a, b):
    return jnp.dot(a, b, preferred_element_type=F32)


def _dot_nt(a, b):
    return lax.dot_general(a, b, (((1,), (1,)), ((), ())), preferred_element_type=F32)


def _dot_exact(a, b):
    return jnp.dot(a, b, precision=lax.Precision.HIGHEST, preferred_element_type=F32)


def _layer_norm(x, g, b):
    mu = jnp.mean(x, axis=-1, keepdims=True)
    xc = x - mu
    var = jnp.mean(xc * xc, axis=-1, keepdims=True)
    return xc * lax.rsqrt(var + LN_EPS) * g + b


def _log_sigmoid(x):
    return jnp.minimum(x, 0.0) - jnp.log1p(jnp.exp(-jnp.abs(x)))


def _sigmoid(x):
    return 0.5 * jnp.tanh(0.5 * x) + 0.5


def _iota2(shape, dim):
    return lax.broadcasted_iota(jnp.int32, shape, dim)


def _project(hb, w_ref, b_ref, u_ref):
    slabs = [(C_G, C_AQ)]
    for lo, hi in ((0, C_G), (C_AQ, N_PAD)):
        slabs += [(n0, min(n0 + PROJ_COL_STEP, hi)) for n0 in range(lo, hi, PROJ_COL_STEP)]
    for n0, n1 in slabs:
        u_ref[:, n0:n1] = _dot(hb, w_ref[:, n0:n1]) + b_ref[:, n0:n1]


def _mlstm_intra(q, k, v, li_col, b_col, m_old, allowed):
    n = q.shape[0]
    a_col = li_col - b_col
    eye = _iota2((n, n), 0) == _iota2((n, n), 1)
    a_row = jnp.sum(jnp.where(eye, a_col, 0.0), axis=0, keepdims=True)
    a_mat = jnp.where(allowed, a_row, NEG_INF)
    mm = jnp.maximum(m_old, jnp.max(a_mat, axis=1, keepdims=True))
    w = jnp.exp(a_mat - mm)
    qb = q.astype(BF16)
    qkw = _dot_nt(qb, k.astype(BF16)) * w
    den = jnp.sum(qkw, axis=1, keepdims=True)
    num = _dot(qkw.astype(BF16), v.astype(BF16))
    return num, den, mm, a_col


def _gate_rows(gates):
    g_t = jnp.concatenate([gates[r0:r0 + CHUNK].T[0:8, :] for r0 in range(0, gates.shape[0], CHUNK)], axis=0)
    x = _log_sigmoid(g_t)
    lane = _iota2(x.shape, 1)
    shift = 1
    while shift < x.shape[1]:
        x = x + jnp.where(lane >= shift, pltpu.roll(x, shift, 1), 0.0)
        shift *= 2
    return g_t, x


CT_ROWS = M_DH + 8


def _mlstm_scores_and_state(q, k, v, li_row, b_row, m_old, ct_aug, key_ok, n_valid):
    n = q.shape[0]
    row = _iota2((n, n), 0)
    col = _iota2((n, n), 1)
    a_row = li_row - b_row
    a_keys = jnp.broadcast_to(jnp.sum(jnp.where(row == col, a_row, 0.0), axis=1, keepdims=True), (n, n))
    a_t = jnp.where(key_ok, a_keys, NEG_INF)
    mm = jnp.maximum(m_old, jnp.max(a_t, axis=0, keepdims=True))
    w_t = jnp.exp(a_t - mm)
    lane = _iota2((1, n), 1)
    last = n_valid - 1
    mm_last = jnp.max(jnp.where(lane == last, mm, NEG_INF), axis=1, keepdims=True)
    m_new = jnp.sum(jnp.where(lane == last, b_row, 0.0), axis=1, keepdims=True) + mm_last
    w_state = jnp.exp(a_row - mm_last)
    if n_valid < n:
        w_state = jnp.where(lane < n_valid, w_state, 0.0)
    decay = jnp.exp(m_old - mm_last)
    qb = q.astype(BF16)
    kb = k.astype(BF16)
    v_t = v.T
    scores_t = _dot_nt(kb, qb)
    inter = _dot_nt(ct_aug.astype(BF16), qb)
    ones_row = jnp.where(_iota2((CT_ROWS - M_DH, n), 0) == 0, w_state, 0.0)
    vtw = jnp.concatenate([v_t * w_state, ones_row], axis=0).astype(BF16)
    ct_aug_new = decay * ct_aug + _dot(vtw, kb)
    s_inter = jnp.exp(m_old - mm)
    floor = jnp.exp(-(b_row + mm))
    return (scores_t, w_t, v_t.astype(BF16), inter, s_inter, floor), ct_aug_new, m_new


def _mlstm_weighted_values(scores_t, w_t, vtb, inter, s_inter, floor):
    qkw_t = scores_t * w_t
    den = jnp.sum(qkw_t, axis=0, keepdims=True)
    num_t = _dot(vtb, qkw_t.astype(BF16))
    return num_t, den, inter, s_inter, floor


def _mlstm_finish(num_t, den, inter, s_inter, floor):
    num_t = num_t + inter[0:M_DH] * s_inter
    den = den + inter[M_DH:M_DH + 1] * s_inter
    return num_t * (1.0 / jnp.maximum(jnp.abs(den), floor))


def _silu(x):
    return x * _sigmoid(x)


def _mlstm_gate_head(h, mo, mz, norm_g):
    hh = h * _sigmoid(mo)
    mu = jnp.mean(hh, axis=-1, keepdims=True)
    hc = hh - mu
    var = jnp.mean(hc * hc, axis=-1, keepdims=True)
    return (hc * lax.rsqrt(var + LN_EPS) * norm_g * _silu(mz)).astype(BF16)


def _gate_mix(h_m, mo, mz, o_a, az, norm_g):
    parts = []
    for hd in range(M_HEADS):
        sl = slice(hd * M_DH, (hd + 1) * M_DH)
        parts.append(_mlstm_gate_head(h_m[:, sl], mo[:, sl], mz[:, sl], norm_g[:, sl]))
    parts.append((o_a * _silu(az)).astype(BF16))
    return jnp.concatenate(parts, axis=-1)


def _out_and_norm(hp, mix, wo_ref, g, b):
    z = DN_ALPHA * hp + _dot(mix, wo_ref[...])
    return _layer_norm(z, g, b)


def _meta_kernel(meta_ref, w_ref, bias_ref, g0_ref, b0_ref,
                 ct0_ref, m0_ref, km_ref, vm_ref, vmt_ref, u_ref):
    hp = _layer_norm(meta_ref[...], g0_ref[...], b0_ref[...])
    _project(hp.astype(BF16), w_ref, bias_ref, u_ref)
    row = _iota2((CHUNK, CHUNK), 0)
    col = _iota2((CHUNK, CHUNK), 1)
    key_ok = (row <= col) & (row < N_META)
    li_rows, b_rows = _gate_rows(u_ref[:, C_G:C_G + 128])
    zero_m = jnp.zeros((1, 128), F32)
    zero_ct = jnp.zeros((CT_ROWS, M_DH), F32)
    m0_ref[...] = jnp.zeros(m0_ref.shape, F32)
    for hd in range(M_HEADS):
        q = u_ref[:, C_Q + hd * M_DH:C_Q + (hd + 1) * M_DH]
        k = u_ref[:, C_K + hd * M_DH:C_K + (hd + 1) * M_DH] * (M_DH ** -0.5)
        v = u_ref[:, C_V + hd * M_DH:C_V + (hd + 1) * M_DH]
        _, ct_new, m_new = _mlstm_scores_and_state(
            q, k, v, li_rows[hd:hd + 1], b_rows[M_HEADS + hd:M_HEADS + hd + 1],
            zero_m, zero_ct, key_ok, N_META)
        ct0_ref[hd] = ct_new
        m0_ref[hd:hd + 1, :] = jnp.broadcast_to(m_new, (1, 128))
    km_ref[...] = u_ref[0:N_META, C_AK:C_AK + KV_W]
    vm_ref[...] = u_ref[0:N_META, C_AV:C_AV + KV_W]
    vmt_ref[...] = u_ref[:, C_AV:C_AV + KV_W].T[:, 0:N_META]


def _keep_kv_half(x, kv):
    low = _iota2(x.shape, 1) < A_DH
    return jnp.where(low if kv == 0 else ~low, x, 0.0)


def _swa_scores_t(tile, q_tile, keys):
    kv = tile // (A_GROUP // 2)
    q_scaled = q_tile * (A_DH ** -0.5)
    q_swapped = pltpu.roll(q_scaled, A_DH, 1)
    out = []
    for half in range(2):
        qz = (q_scaled if half == kv else q_swapped).astype(BF16)
        out.append(tuple(_dot_nt(kk, qz) for kk in keys))
    return out


def _swa_weighted_values_t(scores, values_t, sink, own_ok, prev_ok):
    s_own = jnp.where(own_ok, scores[0], NEG_INF)
    s_prev = jnp.where(prev_ok, scores[1], NEG_INF)
    s_meta = scores[2]
    mx = jnp.maximum(jnp.maximum(jnp.max(s_own, axis=0, keepdims=True),
                                 jnp.max(s_prev, axis=0, keepdims=True)),
                     jnp.maximum(jnp.max(s_meta, axis=0, keepdims=True), sink))
    p_own, p_prev, p_meta = (jnp.exp(s - mx) for s in (s_own, s_prev, s_meta))
    den = (jnp.sum(p_own, axis=0, keepdims=True) + jnp.sum(p_prev, axis=0, keepdims=True)
           + jnp.sum(p_meta, axis=0, keepdims=True) + jnp.exp(sink - mx))
    o_t = (_dot(values_t[0], p_own.astype(BF16)) + _dot(values_t[1], p_prev.astype(BF16))
           + _dot(values_t[2], p_meta.astype(BF16)))
    return o_t, den


def _prompt_kernel(sink_ref, x_ref, w_ref, bias_ref, g0_ref, b0_ref, ct0_ref, m0_ref,
                   km_ref, vm_ref, vmt_ref, ng_ref, wo_ref, lng_ref, lnb_ref,
                   y_ref, pk_ref, pv_ref, pc_ref, pn_ref, pm_ref,
                   u_ref, hp_ref, mix_ref, ct_ref, kprev_ref, vtprev_ref, li_ref, cumf_ref, *, tb):
    j = pl.program_id(1)
    n_chunks = tb // CHUNK

    @pl.when(j == 0)
    def _():
        ct_ref[...] = ct0_ref[...]
        pm_ref[0] = m0_ref[...]
        kprev_ref[...] = jnp.zeros(kprev_ref.shape, F32)
        vtprev_ref[...] = jnp.zeros(vtprev_ref.shape, F32)
        pk_ref[0, 0:N_META, :] = km_ref[...]
        pv_ref[0, 0:N_META, :] = vm_ref[...]

    hp = _layer_norm(x_ref[0], g0_ref[...], b0_ref[...])
    hp_ref[...] = hp
    _project(hp.astype(BF16), w_ref, bias_ref, u_ref)
    li_ref[...], cumf_ref[...] = _gate_rows(u_ref[:, C_G:C_G + 128])

    def chunk_body(ci, carry):
        r0 = pl.multiple_of(ci * CHUNK, CHUNK)
        rows = pl.ds(r0, CHUNK)
        key = _iota2((CHUNK, CHUNK), 0)
        query = _iota2((CHUNK, CHUNK), 1)
        causal = key <= query

        gate_rows = pl.ds(pl.multiple_of(ci * 8, 8), 8)
        li_rows = li_ref[gate_rows, :]
        b_rows = cumf_ref[gate_rows, :]
        m_carry = []
        for hd in range(M_HEADS):
            q = u_ref[rows, C_Q + hd * M_DH:C_Q + (hd + 1) * M_DH]
            k = u_ref[rows, C_K + hd * M_DH:C_K + (hd + 1) * M_DH] * (M_DH ** -0.5)
            v = u_ref[rows, C_V + hd * M_DH:C_V + (hd + 1) * M_DH]
            carry_hd, ct_new, m_new = _mlstm_scores_and_state(
                q, k, v, li_rows[hd:hd + 1], b_rows[M_HEADS + hd:M_HEADS + hd + 1],
                pm_ref[0, hd:hd + 1, :], ct_ref[hd], causal, CHUNK)
            ct_ref[hd] = ct_new
            pm_ref[0, hd:hd + 1, :] = jnp.broadcast_to(m_new, (1, 128))
            m_carry.append(carry_hd)

        k_own = u_ref[rows, C_AK:C_AK + KV_W]
        vt_own = u_ref[rows, C_AV:C_AV + KV_W].T
        k_prev = kprev_ref[...]
        vt_prev = vtprev_ref[...]
        prev_ok = (key > query) & ((j * n_chunks + ci) > 0)
        keys_kv = [tuple(_keep_kv_half(x, kv).astype(BF16) for x in (k_own, k_prev, km_ref[...]))
                   for kv in range(A_KV_HEADS)]
        values_kv = [tuple(x[kv * A_DH:(kv + 1) * A_DH].astype(BF16) for x in (vt_own, vt_prev, vmt_ref[...]))
                     for kv in range(A_KV_HEADS)]
        a_scores = []
        for tile in range(A_HEADS // 2):
            a_scores += _swa_scores_t(tile, u_ref[rows, C_AQ + tile * 128:C_AQ + (tile + 1) * 128],
                                      keys_kv[tile // (A_GROUP // 2)])
        kprev_ref[...] = k_own
        vtprev_ref[...] = vt_own

        m_carry = [_mlstm_weighted_values(*c) for c in m_carry]
        a_out = [_swa_weighted_values_t(a_scores[hd], values_kv[hd // A_GROUP], sink_ref[hd], causal, prev_ok)
                 for hd in range(A_HEADS)]

        for hd in range(M_HEADS):
            sl = slice(hd * M_DH, (hd + 1) * M_DH)
            mix_ref[rows, sl] = _mlstm_gate_head(
                _mlstm_finish(*m_carry[hd]).T, u_ref[rows, C_O + hd * M_DH:C_O + (hd + 1) * M_DH],
                u_ref[rows, C_Z + hd * M_DH:C_Z + (hd + 1) * M_DH], ng_ref[:, sl])
        for tile in range(A_HEADS // 2):
            o_tile = jnp.concatenate([o_t * (1.0 / den) for o_t, den in a_out[2 * tile:2 * tile + 2]], axis=0).T
            az = u_ref[rows, C_AZ + tile * 128:C_AZ + (tile + 1) * 128]
            mix_ref[rows, D_MLSTM + tile * 128:D_MLSTM + (tile + 1) * 128] = (o_tile * _silu(az)).astype(BF16)
        return carry

    lax.fori_loop(0, n_chunks, chunk_body, 0)

    y_ref[0] = _out_and_norm(hp_ref[...], mix_ref[...], wo_ref, lng_ref[...], lnb_ref[...])

    @pl.when(j == pl.num_programs(1) - 1)
    def _():
        pk_ref[0, N_META:N_BUF, :] = u_ref[tb - WINDOW:tb, C_AK:C_AK + KV_W]
        pv_ref[0, N_META:N_BUF, :] = u_ref[tb - WINDOW:tb, C_AV:C_AV + KV_W]
        for hd in range(M_HEADS):
            pc_ref[0, hd] = ct_ref[hd, 0:M_DH, :].T
            pn_ref[0, hd:hd + 1, :] = ct_ref[hd, M_DH:M_DH + 1, :]
        pn_ref[0, M_HEADS:8, :] = jnp.zeros((8 - M_HEADS, M_DH), F32)


def _const_spec(shape):
    return pl.BlockSpec(shape, lambda *_: (0,) * len(shape))


def _prompt_path(x_prompt, meta_tokens, ln0_g, ln0_b, w_all, b_all, sinks, norm_g, w_o, ln_g, ln_b, tb=512):
    batch, seq, _ = x_prompt.shape
    meta_pad = jnp.pad(meta_tokens.astype(F32), ((0, CHUNK - N_META), (0, 0)))
    ct0, m0, km, vm, vmt = pl.pallas_call(
        _meta_kernel,
        out_shape=(jax.ShapeDtypeStruct((M_HEADS, CT_ROWS, M_DH), F32),
                   jax.ShapeDtypeStruct((8, 128), F32),
                   jax.ShapeDtypeStruct((N_META, KV_W), F32),
                   jax.ShapeDtypeStruct((N_META, KV_W), F32),
                   jax.ShapeDtypeStruct((KV_W, N_META), F32)),
        scratch_shapes=[pltpu.VMEM((CHUNK, N_PAD), F32)],
        compiler_params=pltpu.CompilerParams(vmem_limit_bytes=VMEM_LIMIT_BYTES),
        name="meta_tokens",
    )(meta_pad, w_all, b_all, ln0_g, ln0_b)

    nj = seq // tb
    in_specs = [
        pl.BlockSpec(memory_space=pltpu.SMEM),
        pl.BlockSpec((1, tb, D_MODEL), lambda b, j: (b, j, 0)),
        _const_spec((D_MODEL, N_PAD)), _const_spec((1, N_PAD)),
        _const_spec((1, D_MODEL)), _const_spec((1, D_MODEL)),
        _const_spec((M_HEADS, CT_ROWS, M_DH)), _const_spec((8, 128)),
        _const_spec((N_META, KV_W)), _const_spec((N_META, KV_W)), _const_spec((KV_W, N_META)),
        _const_spec((1, D_MLSTM)),
        _const_spec((D_MODEL, D_MODEL)),
        _const_spec((1, D_MODEL)), _const_spec((1, D_MODEL)),
    ]
    out_specs = [
        pl.BlockSpec((1, tb, D_MODEL), lambda b, j: (b, j, 0)),
        pl.BlockSpec((1, N_BUF, KV_W), lambda b, j: (b, 0, 0)),
        pl.BlockSpec((1, N_BUF, KV_W), lambda b, j: (b, 0, 0)),
        pl.BlockSpec((1, M_HEADS, M_DH, M_DH), lambda b, j: (b, 0, 0, 0)),
        pl.BlockSpec((1, 8, M_DH), lambda b, j: (b, 0, 0)),
        pl.BlockSpec((1, 8, 128), lambda b, j: (b, 0, 0)),
    ]
    out_shape = (
        jax.ShapeDtypeStruct((batch, seq, D_MODEL), F32),
        jax.ShapeDtypeStruct((batch, N_BUF, KV_W), F32),
        jax.ShapeDtypeStruct((batch, N_BUF, KV_W), F32),
        jax.ShapeDtypeStruct((batch, M_HEADS, M_DH, M_DH), F32),
        jax.ShapeDtypeStruct((batch, 8, M_DH), F32),
        jax.ShapeDtypeStruct((batch, 8, 128), F32),
    )
    y, pk, pv, pc, pn, pm = pl.pallas_call(
        functools.partial(_prompt_kernel, tb=tb),
        grid=(batch, nj),
        in_specs=in_specs,
        out_specs=out_specs,
        out_shape=out_shape,
        scratch_shapes=[pltpu.VMEM((tb, N_PAD), F32), pltpu.VMEM((tb, D_MODEL), F32),
                        pltpu.VMEM((tb, D_MODEL), BF16),
                        pltpu.VMEM((M_HEADS, CT_ROWS, M_DH), F32),
                        pltpu.VMEM((CHUNK, KV_W), F32), pltpu.VMEM((KV_W, CHUNK), F32),
                        pltpu.VMEM((tb // CHUNK * 8, CHUNK), F32), pltpu.VMEM((tb // CHUNK * 8, CHUNK), F32)],
        compiler_params=pltpu.CompilerParams(dimension_semantics=("arbitrary", "arbitrary"),
                                             vmem_limit_bytes=VMEM_LIMIT_BYTES),
        name="prompt_layer",
    )(sinks, x_prompt, w_all, b_all, ln0_g, ln0_b, ct0, m0, km, vm, vmt, norm_g, w_o, ln_g, ln_b)
    pk = pk.reshape(1, batch, N_BUF, A_KV_HEADS, A_DH)
    pv = pv.reshape(1, batch, N_BUF, A_KV_HEADS, A_DH)
    return y, pk, pv, pc[None], pn[:, :M_HEADS][None], pm[:, :M_HEADS, 0][None]


SEQ_PER_GROUP = 32
SWA_SEQ_PER_STEP = 8
SEQ_UNROLL = 4


def _sample_proj_kernel(x_ref, w_ref, bias_ref, g0_ref, b0_ref, hs_ref, u_ref):
    hs = _layer_norm(x_ref[...], g0_ref[...], b0_ref[...])
    hs_ref[...] = hs
    _project(hs.astype(BF16), w_ref, bias_ref, u_ref)


def _sample_mlstm_kernel(q_ref, k_ref, v_ref, g_ref, c_ref, n_ref, m_ref,
                         h_ref, cn_ref, nn_ref, mn_ref, decay_ref, kwt_ref, *, dec_seq):
    hd = pl.program_id(1)
    n = CHUNK
    nb = n // dec_seq
    q = q_ref[...]
    k = k_ref[...] * (M_DH ** -0.5)
    v = v_ref[...]
    gates = g_ref[...]
    row = _iota2((n, n), 0)
    col = _iota2((n, n), 1)
    row_seq = row // dec_seq
    col_seq = col // dec_seq
    allowed = (row_seq == col_seq) & (col <= row)
    b_all = _dot_exact(jnp.where(allowed, 1.0, 0.0), _log_sigmoid(gates))
    li_col = jnp.sum(jnp.where(col == hd, gates, 0.0), axis=1, keepdims=True)
    b_col = jnp.sum(jnp.where(col == hd + M_HEADS, b_all, 0.0), axis=1, keepdims=True)

    expand = jnp.where(_iota2((n, nb), 0) // dec_seq == _iota2((n, nb), 1), 1.0, 0.0)
    m_seq = m_ref[0]
    n_seq = n_ref[0]
    m_old = _dot_exact(expand, m_seq)[:, 0:1]
    n_rows = _dot_exact(expand, n_seq)

    num_i, den_i, mm, a_col = _mlstm_intra(q, k, v, li_col, b_col, m_old, allowed)
    sel_last = jnp.where(col == row_seq * dec_seq + (dec_seq - 1), 1.0, 0.0)
    packed = jnp.where(col == 0, mm, jnp.where(col == 1, b_col, 0.0))
    last_vals = _dot_exact(sel_last, packed)
    mm_last = last_vals[:, 0:1]
    b_last = last_vals[:, 1:2]
    m_t = b_col + mm
    w_state = jnp.exp(a_col - mm_last)
    decay_col = jnp.exp(m_old - mm_last)
    kw = k * w_state
    kwt_ref[...] = kw.T
    decay_ref[...] = jnp.broadcast_to(decay_col, (n, n))

    pick = jnp.where(_iota2((nb, n), 1) == _iota2((nb, n), 0) * dec_seq + (dec_seq - 1), 1.0, 0.0)
    seg = jnp.where(_iota2((nb, n), 1) // dec_seq == _iota2((nb, n), 0), 1.0, 0.0)
    decay_seq = _dot_exact(pick, decay_ref[...])
    nn_ref[0] = decay_seq * n_seq + _dot_exact(seg, kw)
    mn_ref[0] = _dot_exact(pick, jnp.broadcast_to(b_last + mm_last, (n, n)))

    qb = q.astype(BF16)
    vb = v.astype(BF16)

    def seq_body(s, acc):
        c_s = c_ref[s, 0]
        qc = _dot(qb, c_s.astype(BF16))
        acc = acc + jnp.where(row_seq == s, qc, 0.0)
        kwt_s = jnp.where(col_seq == s, kwt_ref[...], 0.0).astype(BF16)
        decay_s = decay_ref[pl.ds(s * dec_seq + (dec_seq - 1), 1), :]
        cn_ref[s, 0] = decay_s * c_s + _dot(kwt_s, vb)
        return acc

    qc_all = lax.fori_loop(0, nb, seq_body, jnp.zeros((n, n), F32), unroll=SEQ_UNROLL)
    s_inter = jnp.exp(m_old - mm)
    qn = jnp.sum(q * n_rows, axis=1, keepdims=True)
    num = num_i + qc_all * s_inter
    den = den_i + qn * s_inter
    h_ref[...] = num / jnp.maximum(jnp.abs(den), jnp.exp(-m_t))


def _sample_swa_kernel(qz_ref, kn_ref, vn_ref, ck_ref, cv_ref, sink_ref, o_ref, nk_ref, nv_ref, *, dec_seq):
    n_rows = qz_ref.shape[1]
    t_c = _iota2((n_rows, N_BUF), 0) % dec_seq
    i_c = _iota2((n_rows, N_BUF), 1)
    ok_c = (i_c < N_META) | (i_c > t_c + N_META)
    t_n = _iota2((n_rows, 8), 0) % dec_seq
    i_n = _iota2((n_rows, 8), 1)
    ok_n = (i_n <= t_n) & (i_n < dec_seq)
    sink = sink_ref[:, 0:1]

    def seq_body(s, carry):
        qz = (qz_ref[s] * (A_DH ** -0.5)).astype(BF16)
        kc = ck_ref[s]
        vc = cv_ref[s]
        kn = kn_ref[s]
        vn = vn_ref[s]
        s_c = jnp.where(ok_c, _dot_nt(qz, kc.astype(BF16)), NEG_INF)
        s_n = jnp.where(ok_n, _dot_nt(qz, kn.astype(BF16)), NEG_INF)
        mx = jnp.maximum(jnp.maximum(jnp.max(s_c, axis=1, keepdims=True),
                                     jnp.max(s_n, axis=1, keepdims=True)), sink)
        p_c = jnp.exp(s_c - mx)
        p_n = jnp.exp(s_n - mx)
        den = jnp.sum(p_c, axis=1, keepdims=True) + jnp.sum(p_n, axis=1, keepdims=True) + jnp.exp(sink - mx)
        o_ref[s] = (_dot(p_c.astype(BF16), vc.astype(BF16)) + _dot(p_n.astype(BF16), vn.astype(BF16))) / den
        nk_ref[s, 0:N_META, :] = kc[0:N_META]
        nk_ref[s, N_META:N_BUF - dec_seq, :] = kc[N_META + dec_seq:N_BUF]
        nk_ref[s, N_BUF - dec_seq:N_BUF, :] = kn[0:dec_seq]
        nv_ref[s, 0:N_META, :] = vc[0:N_META]
        nv_ref[s, N_META:N_BUF - dec_seq, :] = vc[N_META + dec_seq:N_BUF]
        nv_ref[s, N_BUF - dec_seq:N_BUF, :] = vn[0:dec_seq]
        return carry

    lax.fori_loop(0, qz_ref.shape[0], seq_body, 0, unroll=SEQ_UNROLL)


def _sample_out_kernel(hs_ref, u_ref, hm_ref, oa_ref, ng_ref, wo_ref, lng_ref, lnb_ref, y_ref):
    mix = _gate_mix(hm_ref[...], u_ref[:, C_O:C_O + D_MLSTM], u_ref[:, C_Z:C_Z + D_MLSTM],
                    oa_ref[...], u_ref[:, C_AZ:C_AZ + D_SWA], ng_ref[...])
    y_ref[...] = _out_and_norm(hs_ref[...], mix, wo_ref, lng_ref[...], lnb_ref[...])


def _sample_path(x_sample, cache_k, cache_v, state_c, state_n, state_m,
                 ln0_g, ln0_b, w_all, b_all, a_sinks, norm_g, w_o, ln_g, ln_b):
    db, dec_seq, _ = x_sample.shape
    rows = db * dec_seq
    params = pltpu.CompilerParams(vmem_limit_bytes=VMEM_LIMIT_BYTES)
    hs, u = pl.pallas_call(
        _sample_proj_kernel,
        out_shape=(jax.ShapeDtypeStruct((rows, D_MODEL), F32), jax.ShapeDtypeStruct((rows, N_PAD), F32)),
        compiler_params=params,
        name="sample_proj",
    )(x_sample.reshape(rows, D_MODEL), w_all, b_all, ln0_g, ln0_b)

    n_groups = db // SEQ_PER_GROUP
    n_t = jnp.transpose(state_n, (1, 0, 2))
    m_t = jnp.broadcast_to(jnp.transpose(state_m, (1, 0))[:, :, None], (M_HEADS, db, 128))

    def col_spec(col0):
        return pl.BlockSpec((CHUNK, M_DH), lambda g, h: (g, col0 // M_DH + h))

    state_spec = pl.BlockSpec((SEQ_PER_GROUP, 1, M_DH, M_DH), lambda g, h: (g, h, 0, 0))
    vec_spec = pl.BlockSpec((1, SEQ_PER_GROUP, M_DH), lambda g, h: (h, g, 0))
    h_m, c_new, n_new, m_new = pl.pallas_call(
        functools.partial(_sample_mlstm_kernel, dec_seq=dec_seq),
        grid=(n_groups, M_HEADS),
        in_specs=[col_spec(C_Q), col_spec(C_K), col_spec(C_V),
                  pl.BlockSpec((CHUNK, 128), lambda g, h: (g, C_G // 128)),
                  state_spec, vec_spec, vec_spec],
        out_specs=[pl.BlockSpec((CHUNK, M_DH), lambda g, h: (g, h)), state_spec, vec_spec, vec_spec],
        out_shape=(jax.ShapeDtypeStruct((rows, D_MLSTM), F32),
                   jax.ShapeDtypeStruct(state_c.shape, F32),
                   jax.ShapeDtypeStruct((M_HEADS, db, M_DH), F32),
                   jax.ShapeDtypeStruct((M_HEADS, db, 128), F32)),
        scratch_shapes=[pltpu.VMEM((CHUNK, CHUNK), F32), pltpu.VMEM((CHUNK, CHUNK), F32)],
        compiler_params=pltpu.CompilerParams(dimension_semantics=("arbitrary", "arbitrary"),
                                             vmem_limit_bytes=VMEM_LIMIT_BYTES),
        name="sample_mlstm",
    )(u, u, u, u, state_c, n_t, m_t)

    aq = u[:, C_AQ:C_AQ + D_SWA].reshape(db, dec_seq, A_KV_HEADS, A_GROUP, A_DH)
    aq = jnp.transpose(aq, (0, 2, 3, 1, 4)).reshape(db, A_KV_HEADS, A_GROUP * dec_seq, A_DH)
    zeros = jnp.zeros_like(aq[:, 0])
    qz = jnp.stack([jnp.concatenate([aq[:, 0], zeros], axis=-1),
                    jnp.concatenate([zeros, aq[:, 1]], axis=-1)], axis=1)
    n_qrows = A_HEADS * dec_seq
    qz = qz.reshape(db, n_qrows, KV_W)
    k_new = jnp.pad(u[:, C_AK:C_AK + KV_W].reshape(db, dec_seq, KV_W), ((0, 0), (0, 8 - dec_seq), (0, 0)))
    v_new = jnp.pad(u[:, C_AV:C_AV + KV_W].reshape(db, dec_seq, KV_W), ((0, 0), (0, 8 - dec_seq), (0, 0)))
    sink_rows = jnp.broadcast_to(jnp.repeat(a_sinks.astype(F32), dec_seq)[:, None], (n_qrows, 128))
    ck = cache_k.reshape(db, N_BUF, KV_W)
    cv = cache_v.reshape(db, N_BUF, KV_W)
    sb = SWA_SEQ_PER_STEP

    def seq_spec(r):
        return pl.BlockSpec((sb, r, KV_W), lambda i: (i, 0, 0))

    o, nk, nv = pl.pallas_call(
        functools.partial(_sample_swa_kernel, dec_seq=dec_seq),
        grid=(db // sb,),
        in_specs=[seq_spec(n_qrows), seq_spec(8), seq_spec(8), seq_spec(N_BUF), seq_spec(N_BUF),
                  pl.BlockSpec((n_qrows, 128), lambda i: (0, 0))],
        out_specs=[seq_spec(n_qrows), seq_spec(N_BUF), seq_spec(N_BUF)],
        out_shape=(jax.ShapeDtypeStruct((db, n_qrows, KV_W), F32),
                   jax.ShapeDtypeStruct((db, N_BUF, KV_W), F32),
                   jax.ShapeDtypeStruct((db, N_BUF, KV_W), F32)),
        compiler_params=pltpu.CompilerParams(dimension_semantics=("arbitrary",),
                                             vmem_limit_bytes=VMEM_LIMIT_BYTES),
        name="sample_swa",
    )(qz, k_new, v_new, ck, cv, sink_rows)
    o = o.reshape(db, A_KV_HEADS, A_GROUP, dec_seq, A_KV_HEADS, A_DH)
    o = jnp.stack([o[:, 0, :, :, 0, :], o[:, 1, :, :, 1, :]], axis=1)
    o_a = jnp.transpose(o, (0, 3, 1, 2, 4)).reshape(rows, D_SWA)

    y = pl.pallas_call(
        _sample_out_kernel,
        out_shape=jax.ShapeDtypeStruct((rows, D_MODEL), F32),
        compiler_params=params,
        name="sample_out",
    )(hs, u, h_m, o_a, norm_g, w_o, ln_g, ln_b)

    shape5 = (1, db, N_BUF, A_KV_HEADS, A_DH)
    return (y.reshape(db, dec_seq, D_MODEL), nk.reshape(shape5), nv.reshape(shape5), c_new[None],
            jnp.transpose(n_new, (1, 0, 2))[None], jnp.transpose(m_new[:, :, 0], (1, 0))[None])


def kernel(x_prompt, x_sample, cache_swa_k, cache_swa_v, state_mlstm_c, state_mlstm_n, state_mlstm_m,
           meta_tokens, ln0_g, ln0_b, w_in, b_in, a_sinks, m_norm_g, w_out, ln_g, ln_b):
    assert w_in.shape[0] == DEPTH and x_prompt.shape[-1] == D_MODEL
    w = w_in[0]
    pad_w = jnp.zeros((D_MODEL, C_AQ - N_RAW_GATE_END), w.dtype)
    w_all = jnp.concatenate([w[:, :N_RAW_GATE_END], pad_w, w[:, N_RAW_GATE_END:]], axis=1).astype(BF16)
    b = b_in[0].astype(F32)
    b_all = jnp.concatenate([b[:N_RAW_GATE_END], jnp.zeros((C_AQ - N_RAW_GATE_END,), F32),
                             b[N_RAW_GATE_END:]])[None]
    w_o = w_out[0].astype(BF16)
    g0 = ln0_g.astype(F32)[None]
    b0 = ln0_b.astype(F32)[None]
    lg = ln_g[0].astype(F32)[None]
    lb = ln_b[0].astype(F32)[None]
    norm_g = m_norm_g[0].astype(F32)[None]
    sinks = a_sinks[0].astype(F32)

    y_p, pk, pv, pc, pn, pm = _prompt_path(x_prompt, meta_tokens, g0, b0, w_all, b_all, sinks, norm_g, w_o, lg, lb)
    y_s, sk, sv, sc, sn, sm = _sample_path(x_sample, cache_swa_k[0], cache_swa_v[0], state_mlstm_c[0],
                                           state_mlstm_n[0], state_mlstm_m[0],
                                           g0, b0, w_all, b_all, sinks, norm_g, w_o, lg, lb)
    return (y_p, y_s, pk, pv, pc, pn, pm, sk, sv, sc, sn, sm)
```

```python
import functools

import jax
import jax.numpy as jnp
from jax import lax
from jax.experimental import pallas as pl
from jax.experimental.pallas import tpu as pltpu

F32 = jnp.float32
BF16 = jnp.bfloat16

D_MODEL = 1024
N_META = 16
M_HEADS = 4
M_DH = 128
D_MLSTM = M_HEADS * M_DH
A_HEADS = 8
A_KV_HEADS = 2
A_GROUP = A_HEADS // A_KV_HEADS
A_DH = 64
D_SWA = A_HEADS * A_DH
KV_W = A_KV_HEADS * A_DH
WINDOW = 128
CHUNK = 128
LN_EPS = 1e-5
DEPTH = 1
DN_ALPHA = (2.0 * DEPTH) ** 0.25
N_BUF = N_META + WINDOW

C_Q, C_K, C_V, C_O, C_Z = 0, 512, 1024, 1536, 2048
C_G = 2560
C_AQ, C_AK, C_AV, C_AZ = 2688, 3200, 3328, 3456
N_PAD = 3968
N_RAW_GATE_END = 2568

PROJ_COL_STEP = 512
VMEM_LIMIT_BYTES = 56 * 1024 * 1024
NEG_INF = float("-inf")


def _dot(a, b):
    return jnp.dot(a, b, preferred_element_type=F32)


def _dot_nt(a, b):
    return lax.dot_general(a, b, (((1,), (1,)), ((), ())), preferred_element_type=F32)


def _dot_exact(a, b):
    return jnp.dot(a, b, precision=lax.Precision.HIGHEST, preferred_element_type=F32)


def _layer_norm(x, g, b):
    mu = jnp.mean(x, axis=-1, keepdims=True)
    xc = x - mu
    var = jnp.mean(xc * xc, axis=-1, keepdims=True)
    return xc * lax.rsqrt(var + LN_EPS) * g + b


def _log_sigmoid(x):
    return jnp.minimum(x, 0.0) - jnp.log1p(jnp.exp(-jnp.abs(x)))


def _sigmoid(x):
    return 0.5 * jnp.tanh(0.5 * x) + 0.5


def _iota2(shape, dim):
    return lax.broadcasted_iota(jnp.int32, shape, dim)


def _project(hb, w_refs, b_ref, u_ref):
    w_main_ref, w_gate_ref, w_att_ref = w_refs
    slabs = [(w_gate_ref, C_G, C_G, C_AQ)]
    for ref, lo, hi in ((w_main_ref, 0, C_G), (w_att_ref, C_AQ, N_PAD)):
        slabs += [(ref, lo, n0, min(n0 + PROJ_COL_STEP, hi)) for n0 in range(lo, hi, PROJ_COL_STEP)]
    for ref, base, n0, n1 in slabs:
        u_ref[:, n0:n1] = _dot(hb, ref[:, n0 - base:n1 - base]) + b_ref[:, n0:n1]


def _mlstm_intra(q, k, v, li_col, b_col, m_old, allowed):
    n = q.shape[0]
    a_col = li_col - b_col
    eye = _iota2((n, n), 0) == _iota2((n, n), 1)
    a_row = jnp.sum(jnp.where(eye, a_col, 0.0), axis=0, keepdims=True)
    a_mat = jnp.where(allowed, a_row, NEG_INF)
    mm = jnp.maximum(m_old, jnp.max(a_mat, axis=1, keepdims=True))
    w = jnp.exp(a_mat - mm)
    qb = q.astype(BF16)
    qkw = _dot_nt(qb, k.astype(BF16)) * w
    den = jnp.sum(qkw, axis=1, keepdims=True)
    num = _dot(qkw.astype(BF16), v.astype(BF16))
    return num, den, mm, a_col


def _gate_rows(gates):
    g_t = jnp.concatenate([gates[r0:r0 + CHUNK].T[0:8, :] for r0 in range(0, gates.shape[0], CHUNK)], axis=0)
    x = _log_sigmoid(g_t)
    lane = _iota2(x.shape, 1)
    shift = 1
    while shift < x.shape[1]:
        x = x + jnp.where(lane >= shift, pltpu.roll(x, shift, 1), 0.0)
        shift *= 2
    return g_t, x


CT_ROWS = M_DH + 8


def _mlstm_scores_and_state(q, k, v, li_row, b_row, m_old, ct_aug, key_ok, n_valid):
    n = q.shape[0]
    row = _iota2((n, n), 0)
    col = _iota2((n, n), 1)
    a_row = li_row - b_row
    a_keys = jnp.broadcast_to(jnp.sum(jnp.where(row == col, a_row, 0.0), axis=1, keepdims=True), (n, n))
    a_t = jnp.where(key_ok, a_keys, NEG_INF)
    mm = jnp.maximum(m_old, jnp.max(a_t, axis=0, keepdims=True))
    w_t = jnp.exp(a_t - mm)
    lane = _iota2((1, n), 1)
    last = n_valid - 1
    mm_last = jnp.max(jnp.where(lane == last, mm, NEG_INF), axis=1, keepdims=True)
    m_new = jnp.sum(jnp.where(lane == last, b_row, 0.0), axis=1, keepdims=True) + mm_last
    w_state = jnp.exp(a_row - mm_last)
    if n_valid < n:
        w_state = jnp.where(lane < n_valid, w_state, 0.0)
    decay = jnp.exp(m_old - mm_last)
    qb = q.astype(BF16)
    kb = k.astype(BF16)
    v_t = v.T
    scores_t = _dot_nt(kb, qb)
    inter = _dot_nt(ct_aug.astype(BF16), qb)
    ones_row = jnp.where(_iota2((CT_ROWS - M_DH, n), 0) == 0, w_state, 0.0)
    vtw = jnp.concatenate([v_t * w_state, ones_row], axis=0).astype(BF16)
    ct_aug_new = decay * ct_aug + _dot(vtw, kb)
    s_inter = jnp.exp(m_old - mm)
    floor = jnp.exp(-(b_row + mm))
    return (scores_t, w_t, v_t.astype(BF16), inter, s_inter, floor), ct_aug_new, m_new


def _mlstm_weighted_values(scores_t, w_t, vtb, inter, s_inter, floor):
    qkw_t = scores_t * w_t
    den = jnp.sum(qkw_t, axis=0, keepdims=True)
    num_t = _dot(vtb, qkw_t.astype(BF16))
    return num_t, den, inter, s_inter, floor


def _mlstm_finish(num_t, den, inter, s_inter, floor):
    num_t = num_t + inter[0:M_DH] * s_inter
    den = den + inter[M_DH:M_DH + 1] * s_inter
    return num_t * (1.0 / jnp.maximum(jnp.abs(den), floor))


def _silu(x):
    return x * _sigmoid(x)


def _mlstm_gate_head(h, mo, mz, norm_g):
    hh = h * _sigmoid(mo)
    mu = jnp.mean(hh, axis=-1, keepdims=True)
    hc = hh - mu
    var = jnp.mean(hc * hc, axis=-1, keepdims=True)
    return (hc * lax.rsqrt(var + LN_EPS) * norm_g * _silu(mz)).astype(BF16)


def _gate_mix(h_m, mo, mz, o_a, az, norm_g):
    parts = []
    for hd in range(M_HEADS):
        sl = slice(hd * M_DH, (hd + 1) * M_DH)
        parts.append(_mlstm_gate_head(h_m[:, sl], mo[:, sl], mz[:, sl], norm_g[:, sl]))
    parts.append((o_a * _silu(az)).astype(BF16))
    return jnp.concatenate(parts, axis=-1)


def _out_and_norm(hp, mix, wo_ref, g, b):
    z = DN_ALPHA * hp + _dot(mix, wo_ref[...])
    return _layer_norm(z, g, b)


def _meta_kernel(meta_ref, wm_ref, wg_ref, wa_ref, bias_ref, g0_ref, b0_ref,
                 ct0_ref, m0_ref, km_ref, vm_ref, vmt_ref, u_ref):
    hp = _layer_norm(meta_ref[...], g0_ref[...], b0_ref[...])
    _project(hp.astype(BF16), (wm_ref, wg_ref, wa_ref), bias_ref, u_ref)
    row = _iota2((CHUNK, CHUNK), 0)
    col = _iota2((CHUNK, CHUNK), 1)
    key_ok = (row <= col) & (row < N_META)
    li_rows, b_rows = _gate_rows(u_ref[:, C_G:C_G + 128])
    zero_m = jnp.zeros((1, 128), F32)
    zero_ct = jnp.zeros((CT_ROWS, M_DH), F32)
    m0_ref[...] = jnp.zeros(m0_ref.shape, F32)
    for hd in range(M_HEADS):
        q = u_ref[:, C_Q + hd * M_DH:C_Q + (hd + 1) * M_DH]
        k = u_ref[:, C_K + hd * M_DH:C_K + (hd + 1) * M_DH] * (M_DH ** -0.5)
        v = u_ref[:, C_V + hd * M_DH:C_V + (hd + 1) * M_DH]
        _, ct_new, m_new = _mlstm_scores_and_state(
            q, k, v, li_rows[hd:hd + 1], b_rows[M_HEADS + hd:M_HEADS + hd + 1],
            zero_m, zero_ct, key_ok, N_META)
        ct0_ref[hd] = ct_new
        m0_ref[hd:hd + 1, :] = jnp.broadcast_to(m_new, (1, 128))
    km_ref[...] = u_ref[0:N_META, C_AK:C_AK + KV_W]
    vm_ref[...] = u_ref[0:N_META, C_AV:C_AV + KV_W]
    vmt_ref[...] = u_ref[:, C_AV:C_AV + KV_W].T[:, 0:N_META]


def _keep_kv_half(x, kv):
    low = _iota2(x.shape, 1) < A_DH
    return jnp.where(low if kv == 0 else ~low, x, 0.0)


def _swa_scores_t(tile, q_tile, keys):
    kv = tile // (A_GROUP // 2)
    q_scaled = q_tile * (A_DH ** -0.5)
    q_swapped = pltpu.roll(q_scaled, A_DH, 1)
    out = []
    for half in range(2):
        qz = (q_scaled if half == kv else q_swapped).astype(BF16)
        out.append(tuple(_dot_nt(kk, qz) for kk in keys))
    return out


def _swa_weighted_values_t(scores, values_t, sink, own_ok, prev_ok):
    s_own = jnp.where(own_ok, scores[0], NEG_INF)
    s_prev = jnp.where(prev_ok, scores[1], NEG_INF)
    s_meta = scores[2]
    mx = jnp.maximum(jnp.maximum(jnp.max(s_own, axis=0, keepdims=True),
                                 jnp.max(s_prev, axis=0, keepdims=True)),
                     jnp.maximum(jnp.max(s_meta, axis=0, keepdims=True), sink))
    p_own, p_prev, p_meta = (jnp.exp(s - mx) for s in (s_own, s_prev, s_meta))
    den = (jnp.sum(p_own, axis=0, keepdims=True) + jnp.sum(p_prev, axis=0, keepdims=True)
           + jnp.sum(p_meta, axis=0, keepdims=True) + jnp.exp(sink - mx))
    o_t = (_dot(values_t[0], p_own.astype(BF16)) + _dot(values_t[1], p_prev.astype(BF16))
           + _dot(values_t[2], p_meta.astype(BF16)))
    return o_t, den


def _prompt_kernel(sink_ref, x_ref, wm_ref, wg_ref, wa_ref, bias_ref, g0_ref, b0_ref, ct0_ref, m0_ref,
                   km_ref, vm_ref, vmt_ref, ng_ref, wo_ref, lng_ref, lnb_ref,
                   y_ref, pk_ref, pv_ref, pc_ref, pn_ref, pm_ref,
                   u_ref, hp_ref, mix_ref, ct_ref, kprev_ref, vtprev_ref, li_ref, cumf_ref, *, tb):
    j = pl.program_id(1)
    n_chunks = tb // CHUNK

    @pl.when(j == 0)
    def _():
        ct_ref[...] = ct0_ref[...]
        pm_ref[0] = m0_ref[...]
        kprev_ref[...] = jnp.zeros(kprev_ref.shape, F32)
        vtprev_ref[...] = jnp.zeros(vtprev_ref.shape, F32)
        pk_ref[0, 0:N_META, :] = km_ref[...]
        pv_ref[0, 0:N_META, :] = vm_ref[...]

    hp = _layer_norm(x_ref[0], g0_ref[...], b0_ref[...])
    hp_ref[...] = hp
    _project(hp.astype(BF16), (wm_ref, wg_ref, wa_ref), bias_ref, u_ref)
    li_ref[...], cumf_ref[...] = _gate_rows(u_ref[:, C_G:C_G + 128])

    def chunk_body(ci, carry):
        r0 = pl.multiple_of(ci * CHUNK, CHUNK)
        rows = pl.ds(r0, CHUNK)
        key = _iota2((CHUNK, CHUNK), 0)
        query = _iota2((CHUNK, CHUNK), 1)
        causal = key <= query

        gate_rows = pl.ds(pl.multiple_of(ci * 8, 8), 8)
        li_rows = li_ref[gate_rows, :]
        b_rows = cumf_ref[gate_rows, :]
        m_carry = []
        for hd in range(M_HEADS):
            q = u_ref[rows, C_Q + hd * M_DH:C_Q + (hd + 1) * M_DH]
            k = u_ref[rows, C_K + hd * M_DH:C_K + (hd + 1) * M_DH] * (M_DH ** -0.5)
            v = u_ref[rows, C_V + hd * M_DH:C_V + (hd + 1) * M_DH]
            carry_hd, ct_new, m_new = _mlstm_scores_and_state(
                q, k, v, li_rows[hd:hd + 1], b_rows[M_HEADS + hd:M_HEADS + hd + 1],
                pm_ref[0, hd:hd + 1, :], ct_ref[hd], causal, CHUNK)
            ct_ref[hd] = ct_new
            pm_ref[0, hd:hd + 1, :] = jnp.broadcast_to(m_new, (1, 128))
            m_carry.append(carry_hd)

        k_own = u_ref[rows, C_AK:C_AK + KV_W]
        vt_own = u_ref[rows, C_AV:C_AV + KV_W].T
        k_prev = kprev_ref[...]
        vt_prev = vtprev_ref[...]
        prev_ok = (key > query) & ((j * n_chunks + ci) > 0)
        keys_kv = [tuple(_keep_kv_half(x, kv).astype(BF16) for x in (k_own, k_prev, km_ref[...]))
                   for kv in range(A_KV_HEADS)]
        values_kv = [tuple(x[kv * A_DH:(kv + 1) * A_DH].astype(BF16) for x in (vt_own, vt_prev, vmt_ref[...]))
                     for kv in range(A_KV_HEADS)]
        a_scores = []
        for tile in range(A_HEADS // 2):
            a_scores += _swa_scores_t(tile, u_ref[rows, C_AQ + tile * 128:C_AQ + (tile + 1) * 128],
                                      keys_kv[tile // (A_GROUP // 2)])
        kprev_ref[...] = k_own
        vtprev_ref[...] = vt_own

        m_carry = [_mlstm_weighted_values(*c) for c in m_carry]
        a_out = [_swa_weighted_values_t(a_scores[hd], values_kv[hd // A_GROUP], sink_ref[hd], causal, prev_ok)
                 for hd in range(A_HEADS)]

        for hd in range(M_HEADS):
            sl = slice(hd * M_DH, (hd + 1) * M_DH)
            mix_ref[rows, sl] = _mlstm_gate_head(
                _mlstm_finish(*m_carry[hd]).T, u_ref[rows, C_O + hd * M_DH:C_O + (hd + 1) * M_DH],
                u_ref[rows, C_Z + hd * M_DH:C_Z + (hd + 1) * M_DH], ng_ref[:, sl])
        for tile in range(A_HEADS // 2):
            o_tile = jnp.concatenate([o_t * (1.0 / den) for o_t, den in a_out[2 * tile:2 * tile + 2]], axis=0).T
            az = u_ref[rows, C_AZ + tile * 128:C_AZ + (tile + 1) * 128]
            mix_ref[rows, D_MLSTM + tile * 128:D_MLSTM + (tile + 1) * 128] = (o_tile * _silu(az)).astype(BF16)
        return carry

    lax.fori_loop(0, n_chunks, chunk_body, 0)

    y_ref[0] = _out_and_norm(hp_ref[...], mix_ref[...], wo_ref, lng_ref[...], lnb_ref[...])

    @pl.when(j == pl.num_programs(1) - 1)
    def _():
        pk_ref[0, N_META:N_BUF, :] = u_ref[tb - WINDOW:tb, C_AK:C_AK + KV_W]
        pv_ref[0, N_META:N_BUF, :] = u_ref[tb - WINDOW:tb, C_AV:C_AV + KV_W]
        for hd in range(M_HEADS):
            pc_ref[0, hd] = ct_ref[hd, 0:M_DH, :].T
            pn_ref[0, hd:hd + 1, :] = ct_ref[hd, M_DH:M_DH + 1, :]
        pn_ref[0, M_HEADS:8, :] = jnp.zeros((8 - M_HEADS, M_DH), F32)


def _const_spec(shape):
    return pl.BlockSpec(shape, lambda *_: (0,) * len(shape))


def _prompt_path(x_prompt, meta_tokens, ln0_g, ln0_b, w_parts, b_all, sinks, norm_g, w_o, ln_g, ln_b, tb=512):
    batch, seq, _ = x_prompt.shape
    meta_pad = jnp.pad(meta_tokens.astype(F32), ((0, CHUNK - N_META), (0, 0)))
    ct0, m0, km, vm, vmt = pl.pallas_call(
        _meta_kernel,
        out_shape=(jax.ShapeDtypeStruct((M_HEADS, CT_ROWS, M_DH), F32),
                   jax.ShapeDtypeStruct((8, 128), F32),
                   jax.ShapeDtypeStruct((N_META, KV_W), F32),
                   jax.ShapeDtypeStruct((N_META, KV_W), F32),
                   jax.ShapeDtypeStruct((KV_W, N_META), F32)),
        scratch_shapes=[pltpu.VMEM((CHUNK, N_PAD), F32)],
        compiler_params=pltpu.CompilerParams(vmem_limit_bytes=VMEM_LIMIT_BYTES),
        name="meta_tokens",
    )(meta_pad, *w_parts, b_all, ln0_g, ln0_b)

    nj = seq // tb
    in_specs = [
        pl.BlockSpec(memory_space=pltpu.SMEM),
        pl.BlockSpec((1, tb, D_MODEL), lambda b, j: (b, j, 0)),
        _const_spec((D_MODEL, C_G)), _const_spec((D_MODEL, C_AQ - C_G)), _const_spec((D_MODEL, N_PAD - C_AQ)),
        _const_spec((1, N_PAD)),
        _const_spec((1, D_MODEL)), _const_spec((1, D_MODEL)),
        _const_spec((M_HEADS, CT_ROWS, M_DH)), _const_spec((8, 128)),
        _const_spec((N_META, KV_W)), _const_spec((N_META, KV_W)), _const_spec((KV_W, N_META)),
        _const_spec((1, D_MLSTM)),
        _const_spec((D_MODEL, D_MODEL)),
        _const_spec((1, D_MODEL)), _const_spec((1, D_MODEL)),
    ]
    out_specs = [
        pl.BlockSpec((1, tb, D_MODEL), lambda b, j: (b, j, 0)),
        pl.BlockSpec((1, N_BUF, KV_W), lambda b, j: (b, 0, 0)),
        pl.BlockSpec((1, N_BUF, KV_W), lambda b, j: (b, 0, 0)),
        pl.BlockSpec((1, M_HEADS, M_DH, M_DH), lambda b, j: (b, 0, 0, 0)),
        pl.BlockSpec((1, 8, M_DH), lambda b, j: (b, 0, 0)),
        pl.BlockSpec((1, 8, 128), lambda b, j: (b, 0, 0)),
    ]
    out_shape = (
        jax.ShapeDtypeStruct((batch, seq, D_MODEL), F32),
        jax.ShapeDtypeStruct((batch, N_BUF, KV_W), F32),
        jax.ShapeDtypeStruct((batch, N_BUF, KV_W), F32),
        jax.ShapeDtypeStruct((batch, M_HEADS, M_DH, M_DH), F32),
        jax.ShapeDtypeStruct((batch, 8, M_DH), F32),
        jax.ShapeDtypeStruct((batch, 8, 128), F32),
    )
    y, pk, pv, pc, pn, pm = pl.pallas_call(
        functools.partial(_prompt_kernel, tb=tb),
        grid=(batch, nj),
        in_specs=in_specs,
        out_specs=out_specs,
        out_shape=out_shape,
        scratch_shapes=[pltpu.VMEM((tb, N_PAD), F32), pltpu.VMEM((tb, D_MODEL), F32),
                        pltpu.VMEM((tb, D_MODEL), BF16),
                        pltpu.VMEM((M_HEADS, CT_ROWS, M_DH), F32),
                        pltpu.VMEM((CHUNK, KV_W), F32), pltpu.VMEM((KV_W, CHUNK), F32),
                        pltpu.VMEM((tb // CHUNK * 8, CHUNK), F32), pltpu.VMEM((tb // CHUNK * 8, CHUNK), F32)],
        compiler_params=pltpu.CompilerParams(dimension_semantics=("arbitrary", "arbitrary"),
                                             vmem_limit_bytes=VMEM_LIMIT_BYTES),
        name="prompt_layer",
    )(sinks, x_prompt, *w_parts, b_all, ln0_g, ln0_b, ct0, m0, km, vm, vmt, norm_g, w_o, ln_g, ln_b)
    pk = pk.reshape(1, batch, N_BUF, A_KV_HEADS, A_DH)
    pv = pv.reshape(1, batch, N_BUF, A_KV_HEADS, A_DH)
    return y, pk, pv, pc[None], pn[:, :M_HEADS][None], pm[:, :M_HEADS, 0][None]


SEQ_PER_GROUP = 32
SWA_SEQ_PER_STEP = 8
SEQ_UNROLL = 4


def _sample_proj_kernel(x_ref, wm_ref, wg_ref, wa_ref, bias_ref, g0_ref, b0_ref, hs_ref, u_ref):
    hs = _layer_norm(x_ref[...], g0_ref[...], b0_ref[...])
    hs_ref[...] = hs
    _project(hs.astype(BF16), (wm_ref, wg_ref, wa_ref), bias_ref, u_ref)


def _sample_mlstm_kernel(q_ref, k_ref, v_ref, g_ref, c_ref, n_ref, m_ref,
                         h_ref, cn_ref, nn_ref, mn_ref, decay_ref, kwt_ref, *, dec_seq):
    hd = pl.program_id(1)
    n = CHUNK
    nb = n // dec_seq
    q = q_ref[...]
    k = k_ref[...] * (M_DH ** -0.5)
    v = v_ref[...]
    gates = g_ref[...]
    row = _iota2((n, n), 0)
    col = _iota2((n, n), 1)
    row_seq = row // dec_seq
    col_seq = col // dec_seq
    allowed = (row_seq == col_seq) & (col <= row)
    b_all = _dot_exact(jnp.where(allowed, 1.0, 0.0), _log_sigmoid(gates))
    li_col = jnp.sum(jnp.where(col == hd, gates, 0.0), axis=1, keepdims=True)
    b_col = jnp.sum(jnp.where(col == hd + M_HEADS, b_all, 0.0), axis=1, keepdims=True)

    expand = jnp.where(_iota2((n, nb), 0) // dec_seq == _iota2((n, nb), 1), 1.0, 0.0)
    m_seq = m_ref[0]
    n_seq = n_ref[0]
    m_old = _dot_exact(expand, m_seq)[:, 0:1]
    n_rows = _dot_exact(expand, n_seq)

    num_i, den_i, mm, a_col = _mlstm_intra(q, k, v, li_col, b_col, m_old, allowed)
    sel_last = jnp.where(col == row_seq * dec_seq + (dec_seq - 1), 1.0, 0.0)
    packed = jnp.where(col == 0, mm, jnp.where(col == 1, b_col, 0.0))
    last_vals = _dot_exact(sel_last, packed)
    mm_last = last_vals[:, 0:1]
    b_last = last_vals[:, 1:2]
    m_t = b_col + mm
    w_state = jnp.exp(a_col - mm_last)
    decay_col = jnp.exp(m_old - mm_last)
    kw = k * w_state
    kwt_ref[...] = kw.T
    decay_ref[...] = jnp.broadcast_to(decay_col, (n, n))

    pick = jnp.where(_iota2((nb, n), 1) == _iota2((nb, n), 0) * dec_seq + (dec_seq - 1), 1.0, 0.0)
    seg = jnp.where(_iota2((nb, n), 1) // dec_seq == _iota2((nb, n), 0), 1.0, 0.0)
    decay_seq = _dot_exact(pick, decay_ref[...])
    nn_ref[0] = decay_seq * n_seq + _dot_exact(seg, kw)
    mn_ref[0] = _dot_exact(pick, jnp.broadcast_to(b_last + mm_last, (n, n)))

    qb = q.astype(BF16)
    vb = v.astype(BF16)

    def seq_body(s, acc):
        c_s = c_ref[s, 0]
        qc = _dot(qb, c_s.astype(BF16))
        acc = acc + jnp.where(row_seq == s, qc, 0.0)
        kwt_s = jnp.where(col_seq == s, kwt_ref[...], 0.0).astype(BF16)
        decay_s = decay_ref[pl.ds(s * dec_seq + (dec_seq - 1), 1), :]
        cn_ref[s, 0] = decay_s * c_s + _dot(kwt_s, vb)
        return acc

    qc_all = lax.fori_loop(0, nb, seq_body, jnp.zeros((n, n), F32), unroll=SEQ_UNROLL)
    s_inter = jnp.exp(m_old - mm)
    qn = jnp.sum(q * n_rows, axis=1, keepdims=True)
    num = num_i + qc_all * s_inter
    den = den_i + qn * s_inter
    h_ref[...] = num / jnp.maximum(jnp.abs(den), jnp.exp(-m_t))


def _sample_swa_kernel(qz_ref, kn_ref, vn_ref, ck_ref, cv_ref, sink_ref, o_ref, nk_ref, nv_ref, *, dec_seq):
    n_rows = qz_ref.shape[1]
    t_c = _iota2((n_rows, N_BUF), 0) % dec_seq
    i_c = _iota2((n_rows, N_BUF), 1)
    ok_c = (i_c < N_META) | (i_c > t_c + N_META)
    t_n = _iota2((n_rows, 8), 0) % dec_seq
    i_n = _iota2((n_rows, 8), 1)
    ok_n = (i_n <= t_n) & (i_n < dec_seq)
    sink = sink_ref[:, 0:1]

    seqs = range(qz_ref.shape[0])
    scores = []
    for s in seqs:
        qz = (qz_ref[s] * (A_DH ** -0.5)).astype(BF16)
        scores.append((_dot_nt(qz, ck_ref[s].astype(BF16)), _dot_nt(qz, kn_ref[s].astype(BF16))))
    probs = []
    for s_c, s_n in scores:
        s_c = jnp.where(ok_c, s_c, NEG_INF)
        s_n = jnp.where(ok_n, s_n, NEG_INF)
        mx = jnp.maximum(jnp.maximum(jnp.max(s_c, axis=1, keepdims=True),
                                     jnp.max(s_n, axis=1, keepdims=True)), sink)
        p_c = jnp.exp(s_c - mx)
        p_n = jnp.exp(s_n - mx)
        den = jnp.sum(p_c, axis=1, keepdims=True) + jnp.sum(p_n, axis=1, keepdims=True) + jnp.exp(sink - mx)
        probs.append((p_c.astype(BF16), p_n.astype(BF16), den))
    outs = [_dot(p_c, cv_ref[s].astype(BF16)) + _dot(p_n, vn_ref[s].astype(BF16))
            for s, (p_c, p_n, _) in zip(seqs, probs)]
    for s, o, (_, _, den) in zip(seqs, outs, probs):
        o_ref[s] = o / den
    for s in seqs:
        for cache_ref, new_ref, out_ref in ((ck_ref, kn_ref, nk_ref), (cv_ref, vn_ref, nv_ref)):
            out_ref[s, 0:N_META, :] = cache_ref[s, 0:N_META, :]
            out_ref[s, N_META:N_BUF - dec_seq, :] = cache_ref[s, N_META + dec_seq:N_BUF, :]
            out_ref[s, N_BUF - dec_seq:N_BUF, :] = new_ref[s, 0:dec_seq, :]


def _sample_out_kernel(hs_ref, u_ref, hm_ref, oa_ref, ng_ref, wo_ref, lng_ref, lnb_ref, y_ref):
    mix = _gate_mix(hm_ref[...], u_ref[:, C_O:C_O + D_MLSTM], u_ref[:, C_Z:C_Z + D_MLSTM],
                    oa_ref[...], u_ref[:, C_AZ:C_AZ + D_SWA], ng_ref[...])
    y_ref[...] = _out_and_norm(hs_ref[...], mix, wo_ref, lng_ref[...], lnb_ref[...])


def _sample_path(x_sample, cache_k, cache_v, state_c, state_n, state_m,
                 ln0_g, ln0_b, w_parts, b_all, a_sinks, norm_g, w_o, ln_g, ln_b):
    db, dec_seq, _ = x_sample.shape
    rows = db * dec_seq
    params = pltpu.CompilerParams(vmem_limit_bytes=VMEM_LIMIT_BYTES)
    hs, u = pl.pallas_call(
        _sample_proj_kernel,
        out_shape=(jax.ShapeDtypeStruct((rows, D_MODEL), F32), jax.ShapeDtypeStruct((rows, N_PAD), F32)),
        compiler_params=params,
        name="sample_proj",
    )(x_sample.reshape(rows, D_MODEL), *w_parts, b_all, ln0_g, ln0_b)

    n_groups = db // SEQ_PER_GROUP
    n_t = jnp.transpose(state_n, (1, 0, 2))
    m_t = jnp.broadcast_to(jnp.transpose(state_m, (1, 0))[:, :, None], (M_HEADS, db, 128))

    def col_spec(col0):
        return pl.BlockSpec((CHUNK, M_DH), lambda g, h: (g, col0 // M_DH + h))

    state_spec = pl.BlockSpec((SEQ_PER_GROUP, 1, M_DH, M_DH), lambda g, h: (g, h, 0, 0))
    vec_spec = pl.BlockSpec((1, SEQ_PER_GROUP, M_DH), lambda g, h: (h, g, 0))
    h_m, c_new, n_new, m_new = pl.pallas_call(
        functools.partial(_sample_mlstm_kernel, dec_seq=dec_seq),
        grid=(n_groups, M_HEADS),
        in_specs=[col_spec(C_Q), col_spec(C_K), col_spec(C_V),
                  pl.BlockSpec((CHUNK, 128), lambda g, h: (g, C_G // 128)),
                  state_spec, vec_spec, vec_spec],
        out_specs=[pl.BlockSpec((CHUNK, M_DH), lambda g, h: (g, h)), state_spec, vec_spec, vec_spec],
        out_shape=(jax.ShapeDtypeStruct((rows, D_MLSTM), F32),
                   jax.ShapeDtypeStruct(state_c.shape, F32),
                   jax.ShapeDtypeStruct((M_HEADS, db, M_DH), F32),
                   jax.ShapeDtypeStruct((M_HEADS, db, 128), F32)),
        scratch_shapes=[pltpu.VMEM((CHUNK, CHUNK), F32), pltpu.VMEM((CHUNK, CHUNK), F32)],
        compiler_params=pltpu.CompilerParams(dimension_semantics=("arbitrary", "arbitrary"),
                                             vmem_limit_bytes=VMEM_LIMIT_BYTES),
        name="sample_mlstm",
    )(u, u, u, u, state_c, n_t, m_t)

    aq = u[:, C_AQ:C_AQ + D_SWA].reshape(db, dec_seq, A_KV_HEADS, A_GROUP, A_DH)
    aq = jnp.transpose(aq, (0, 2, 3, 1, 4)).reshape(db, A_KV_HEADS, A_GROUP * dec_seq, A_DH)
    zeros = jnp.zeros_like(aq[:, 0])
    qz = jnp.stack([jnp.concatenate([aq[:, 0], zeros], axis=-1),
                    jnp.concatenate([zeros, aq[:, 1]], axis=-1)], axis=1)
    n_qrows = A_HEADS * dec_seq
    qz = qz.reshape(db, n_qrows, KV_W)
    k_new = jnp.pad(u[:, C_AK:C_AK + KV_W].reshape(db, dec_seq, KV_W), ((0, 0), (0, 8 - dec_seq), (0, 0)))
    v_new = jnp.pad(u[:, C_AV:C_AV + KV_W].reshape(db, dec_seq, KV_W), ((0, 0), (0, 8 - dec_seq), (0, 0)))
    sink_rows = jnp.broadcast_to(jnp.repeat(a_sinks.astype(F32), dec_seq)[:, None], (n_qrows, 128))
    ck = cache_k.reshape(db, N_BUF, KV_W)
    cv = cache_v.reshape(db, N_BUF, KV_W)
    sb = SWA_SEQ_PER_STEP

    def seq_spec(r):
        return pl.BlockSpec((sb, r, KV_W), lambda i: (i, 0, 0))

    o, nk, nv = pl.pallas_call(
        functools.partial(_sample_swa_kernel, dec_seq=dec_seq),
        grid=(db // sb,),
        in_specs=[seq_spec(n_qrows), seq_spec(8), seq_spec(8), seq_spec(N_BUF), seq_spec(N_BUF),
                  pl.BlockSpec((n_qrows, 128), lambda i: (0, 0))],
        out_specs=[seq_spec(n_qrows), seq_spec(N_BUF), seq_spec(N_BUF)],
        out_shape=(jax.ShapeDtypeStruct((db, n_qrows, KV_W), F32),
                   jax.ShapeDtypeStruct((db, N_BUF, KV_W), F32),
                   jax.ShapeDtypeStruct((db, N_BUF, KV_W), F32)),
        compiler_params=pltpu.CompilerParams(dimension_semantics=("arbitrary",),
                                             vmem_limit_bytes=VMEM_LIMIT_BYTES),
        name="sample_swa",
    )(qz, k_new, v_new, ck, cv, sink_rows)
    o = o.reshape(db, A_KV_HEADS, A_GROUP, dec_seq, A_KV_HEADS, A_DH)
    o = jnp.stack([o[:, 0, :, :, 0, :], o[:, 1, :, :, 1, :]], axis=1)
    o_a = jnp.transpose(o, (0, 3, 1, 2, 4)).reshape(rows, D_SWA)

    y = pl.pallas_call(
        _sample_out_kernel,
        out_shape=jax.ShapeDtypeStruct((rows, D_MODEL), F32),
        compiler_params=params,
        name="sample_out",
    )(hs, u, h_m, o_a, norm_g, w_o, ln_g, ln_b)

    shape5 = (1, db, N_BUF, A_KV_HEADS, A_DH)
    return (y.reshape(db, dec_seq, D_MODEL), nk.reshape(shape5), nv.reshape(shape5), c_new[None],
            jnp.transpose(n_new, (1, 0, 2))[None], jnp.transpose(m_new[:, :, 0], (1, 0))[None])


def kernel(x_prompt, x_sample, cache_swa_k, cache_swa_v, state_mlstm_c, state_mlstm_n, state_mlstm_m,
           meta_tokens, ln0_g, ln0_b, w_in, b_in, a_sinks, m_norm_g, w_out, ln_g, ln_b):
    assert w_in.shape[0] == DEPTH and x_prompt.shape[-1] == D_MODEL
    w = w_in[0]
    w_parts = (w[:, :C_G].astype(BF16),
               jnp.pad(w[:, C_G:N_RAW_GATE_END], ((0, 0), (0, C_AQ - N_RAW_GATE_END))).astype(BF16),
               w[:, N_RAW_GATE_END:].astype(BF16))
    b = b_in[0].astype(F32)
    b_all = jnp.concatenate([b[:N_RAW_GATE_END], jnp.zeros((C_AQ - N_RAW_GATE_END,), F32),
                             b[N_RAW_GATE_END:]])[None]
    w_o = w_out[0].astype(BF16)
    g0 = ln0_g.astype(F32)[None]
    b0 = ln0_b.astype(F32)[None]
    lg = ln_g[0].astype(F32)[None]
    lb = ln_b[0].astype(F32)[None]
    norm_g = m_norm_g[0].astype(F32)[None]
    sinks = a_sinks[0].astype(F32)

    y_p, pk, pv, pc, pn, pm = _prompt_path(x_prompt, meta_tokens, g0, b0, w_parts, b_all, sinks, norm_g, w_o, lg, lb)
    y_s, sk, sv, sc, sn, sm = _sample_path(x_sample, cache_swa_k[0], cache_swa_v[0], state_mlstm_c[0],
                                           state_mlstm_n[0], state_mlstm_m[0],
                                           g0, b0, w_parts, b_all, sinks, norm_g, w_o, lg, lb)
    return (y_p, y_s, pk, pv, pc, pn, pm, sk, sv, sc, sn, sm)
```

```python
import functools

import jax
import jax.numpy as jnp
from jax import lax
from jax.experimental import pallas as pl
from jax.experimental.pallas import tpu as pltpu

F32 = jnp.float32
BF16 = jnp.bfloat16

D_MODEL = 1024
N_META = 16
M_HEADS = 4
M_DH = 128
D_MLSTM = M_HEADS * M_DH
A_HEADS = 8
A_KV_HEADS = 2
A_GROUP = A_HEADS // A_KV_HEADS
A_DH = 64
D_SWA = A_HEADS * A_DH
KV_W = A_KV_HEADS * A_DH
WINDOW = 128
CHUNK = 128
LN_EPS = 1e-5
DEPTH = 1
DN_ALPHA = (2.0 * DEPTH) ** 0.25
N_BUF = N_META + WINDOW

C_Q, C_K, C_V, C_O, C_Z = 0, 512, 1024, 1536, 2048
C_G = 2560
C_AQ, C_AK, C_AV, C_AZ = 2688, 3200, 3328, 3456
N_PAD = 3968
N_RAW_GATE_END = 2568

PROJ_COL_STEP = 512
VMEM_LIMIT_BYTES = 56 * 1024 * 1024
NEG_INF = float("-inf")


def _dot(a, b):
    return jnp.dot(a, b, preferred_element_type=F32)


def _dot_nt(a, b):
    return lax.dot_general(a, b, (((1,), (1,)), ((), ())), preferred_element_type=F32)


def _dot_exact(a, b):
    return jnp.dot(a, b, precision=lax.Precision.HIGHEST, preferred_element_type=F32)


def _layer_norm(x, g, b):
    mu = jnp.mean(x, axis=-1, keepdims=True)
    xc = x - mu
    var = jnp.mean(xc * xc, axis=-1, keepdims=True)
    return xc * lax.rsqrt(var + LN_EPS) * g + b


def _log_sigmoid(x):
    return jnp.minimum(x, 0.0) - jnp.log1p(jnp.exp(-jnp.abs(x)))


def _sigmoid(x):
    return 0.5 * jnp.tanh(0.5 * x) + 0.5


def _iota2(shape, dim):
    return lax.broadcasted_iota(jnp.int32, shape, dim)


def _project(hb, w_refs, b_ref, u_ref):
    w_main_ref, w_gate_ref, w_att_ref = w_refs
    slabs = [(w_gate_ref, C_G, C_G, C_AQ)]
    for ref, lo, hi in ((w_main_ref, 0, C_G), (w_att_ref, C_AQ, N_PAD)):
        slabs += [(ref, lo, n0, min(n0 + PROJ_COL_STEP, hi)) for n0 in range(lo, hi, PROJ_COL_STEP)]
    for ref, base, n0, n1 in slabs:
        u_ref[:, n0:n1] = _dot(hb, ref[:, n0 - base:n1 - base]) + b_ref[:, n0:n1]


def _mlstm_intra(q, k, v, li_col, b_col, m_old, allowed):
    n = q.shape[0]
    a_col = li_col - b_col
    eye = _iota2((n, n), 0) == _iota2((n, n), 1)
    a_row = jnp.sum(jnp.where(eye, a_col, 0.0), axis=0, keepdims=True)
    a_mat = jnp.where(allowed, a_row, NEG_INF)
    mm = jnp.maximum(m_old, jnp.max(a_mat, axis=1, keepdims=True))
    w = jnp.exp(a_mat - mm)
    qb = q.astype(BF16)
    qkw = _dot_nt(qb, k.astype(BF16)) * w
    den = jnp.sum(qkw, axis=1, keepdims=True)
    num = _dot(qkw.astype(BF16), v.astype(BF16))
    return num, den, mm, a_col


def _gate_rows(gates):
    g_t = jnp.concatenate([gates[r0:r0 + CHUNK].T[0:8, :] for r0 in range(0, gates.shape[0], CHUNK)], axis=0)
    x = _log_sigmoid(g_t)
    lane = _iota2(x.shape, 1)
    shift = 1
    while shift < x.shape[1]:
        x = x + jnp.where(lane >= shift, pltpu.roll(x, shift, 1), 0.0)
        shift *= 2
    return g_t, x


CT_ROWS = M_DH + 8


def _mlstm_scores_and_state(q, k, v, li_row, b_row, m_old, ct_aug, key_ok, n_valid):
    n = q.shape[0]
    row = _iota2((n, n), 0)
    col = _iota2((n, n), 1)
    a_row = li_row - b_row
    a_keys = jnp.broadcast_to(jnp.sum(jnp.where(row == col, a_row, 0.0), axis=1, keepdims=True), (n, n))
    a_t = jnp.where(key_ok, a_keys, NEG_INF)
    mm = jnp.maximum(m_old, jnp.max(a_t, axis=0, keepdims=True))
    w_t = jnp.exp(a_t - mm)
    lane = _iota2((1, n), 1)
    last = n_valid - 1
    mm_last = jnp.max(jnp.where(lane == last, mm, NEG_INF), axis=1, keepdims=True)
    m_new = jnp.sum(jnp.where(lane == last, b_row, 0.0), axis=1, keepdims=True) + mm_last
    w_state = jnp.exp(a_row - mm_last)
    if n_valid < n:
        w_state = jnp.where(lane < n_valid, w_state, 0.0)
    decay = jnp.exp(m_old - mm_last)
    qb = q.astype(BF16)
    kb = k.astype(BF16)
    v_t = v.T
    scores_t = _dot_nt(kb, qb)
    inter = _dot_nt(ct_aug.astype(BF16), qb)
    ones_row = jnp.where(_iota2((CT_ROWS - M_DH, n), 0) == 0, w_state, 0.0)
    vtw = jnp.concatenate([v_t * w_state, ones_row], axis=0).astype(BF16)
    ct_aug_new = decay * ct_aug + _dot(vtw, kb)
    s_inter = jnp.exp(m_old - mm)
    floor = jnp.exp(-(b_row + mm))
    return (scores_t, w_t, v_t.astype(BF16), inter, s_inter, floor), ct_aug_new, m_new


def _mlstm_weighted_values(scores_t, w_t, vtb, inter, s_inter, floor):
    qkw_t = scores_t * w_t
    den = jnp.sum(qkw_t, axis=0, keepdims=True)
    num_t = _dot(vtb, qkw_t.astype(BF16))
    return num_t, den, inter, s_inter, floor


def _mlstm_finish(num_t, den, inter, s_inter, floor):
    num_t = num_t + inter[0:M_DH] * s_inter
    den = den + inter[M_DH:M_DH + 1] * s_inter
    return num_t * (1.0 / jnp.maximum(jnp.abs(den), floor))


def _silu(x):
    return x * _sigmoid(x)


def _mlstm_gate_head(h, mo, mz, norm_g):
    hh = h * _sigmoid(mo)
    mu = jnp.mean(hh, axis=-1, keepdims=True)
    hc = hh - mu
    var = jnp.mean(hc * hc, axis=-1, keepdims=True)
    return (hc * lax.rsqrt(var + LN_EPS) * norm_g * _silu(mz)).astype(BF16)


def _gate_mix(h_m, mo, mz, o_a, az, norm_g):
    parts = []
    for hd in range(M_HEADS):
        sl = slice(hd * M_DH, (hd + 1) * M_DH)
        parts.append(_mlstm_gate_head(h_m[:, sl], mo[:, sl], mz[:, sl], norm_g[:, sl]))
    parts.append((o_a * _silu(az)).astype(BF16))
    return jnp.concatenate(parts, axis=-1)


def _out_and_norm(hp, mix, wo_ref, g, b):
    z = DN_ALPHA * hp + _dot(mix, wo_ref[...])
    return _layer_norm(z, g, b)


def _meta_kernel(meta_ref, wm_ref, wg_ref, wa_ref, bias_ref, g0_ref, b0_ref,
                 ct0_ref, m0_ref, km_ref, vm_ref, vmt_ref, u_ref):
    hp = _layer_norm(meta_ref[...], g0_ref[...], b0_ref[...])
    _project(hp.astype(BF16), (wm_ref, wg_ref, wa_ref), bias_ref, u_ref)
    row = _iota2((CHUNK, CHUNK), 0)
    col = _iota2((CHUNK, CHUNK), 1)
    key_ok = (row <= col) & (row < N_META)
    li_rows, b_rows = _gate_rows(u_ref[:, C_G:C_G + 128])
    zero_m = jnp.zeros((1, 128), F32)
    zero_ct = jnp.zeros((CT_ROWS, M_DH), F32)
    m0_ref[...] = jnp.zeros(m0_ref.shape, F32)
    for hd in range(M_HEADS):
        q = u_ref[:, C_Q + hd * M_DH:C_Q + (hd + 1) * M_DH]
        k = u_ref[:, C_K + hd * M_DH:C_K + (hd + 1) * M_DH] * (M_DH ** -0.5)
        v = u_ref[:, C_V + hd * M_DH:C_V + (hd + 1) * M_DH]
        _, ct_new, m_new = _mlstm_scores_and_state(
            q, k, v, li_rows[hd:hd + 1], b_rows[M_HEADS + hd:M_HEADS + hd + 1],
            zero_m, zero_ct, key_ok, N_META)
        ct0_ref[hd] = ct_new
        m0_ref[hd:hd + 1, :] = jnp.broadcast_to(m_new, (1, 128))
    km_ref[...] = u_ref[0:N_META, C_AK:C_AK + KV_W]
    vm_ref[...] = u_ref[0:N_META, C_AV:C_AV + KV_W]
    vmt_ref[...] = u_ref[:, C_AV:C_AV + KV_W].T[:, 0:N_META]


def _keep_kv_half(x, kv):
    low = _iota2(x.shape, 1) < A_DH
    return jnp.where(low if kv == 0 else ~low, x, 0.0)


def _swa_scores_t(tile, q_tile, keys):
    kv = tile // (A_GROUP // 2)
    q_scaled = q_tile * (A_DH ** -0.5)
    q_swapped = pltpu.roll(q_scaled, A_DH, 1)
    out = []
    for half in range(2):
        qz = (q_scaled if half == kv else q_swapped).astype(BF16)
        out.append(tuple(_dot_nt(kk, qz) for kk in keys))
    return out


def _swa_weighted_values_t(scores, values_t, sink, own_ok, prev_ok):
    s_own = jnp.where(own_ok, scores[0], NEG_INF)
    s_prev = jnp.where(prev_ok, scores[1], NEG_INF)
    s_meta = scores[2]
    mx = jnp.maximum(jnp.maximum(jnp.max(s_own, axis=0, keepdims=True),
                                 jnp.max(s_prev, axis=0, keepdims=True)),
                     jnp.maximum(jnp.max(s_meta, axis=0, keepdims=True), sink))
    p_own, p_prev, p_meta = (jnp.exp(s - mx) for s in (s_own, s_prev, s_meta))
    den = (jnp.sum(p_own, axis=0, keepdims=True) + jnp.sum(p_prev, axis=0, keepdims=True)
           + jnp.sum(p_meta, axis=0, keepdims=True) + jnp.exp(sink - mx))
    o_t = (_dot(values_t[0], p_own.astype(BF16)) + _dot(values_t[1], p_prev.astype(BF16))
           + _dot(values_t[2], p_meta.astype(BF16)))
    return o_t, den


def _prompt_kernel(sink_ref, x_ref, wm_ref, wg_ref, wa_ref, bias_ref, g0_ref, b0_ref, ct0_ref, m0_ref,
                   km_ref, vm_ref, vmt_ref, ng_ref, wo_ref, lng_ref, lnb_ref,
                   y_ref, pk_ref, pv_ref, pc_ref, pn_ref, pm_ref,
                   u_ref, hp_ref, mix_ref, ct_ref, kprev_ref, vtprev_ref, li_ref, cumf_ref, *, tb):
    j = pl.program_id(1)
    n_chunks = tb // CHUNK

    @pl.when(j == 0)
    def _():
        ct_ref[...] = ct0_ref[...]
        pm_ref[0] = m0_ref[...]
        kprev_ref[...] = jnp.zeros(kprev_ref.shape, F32)
        vtprev_ref[...] = jnp.zeros(vtprev_ref.shape, F32)
        pk_ref[0, 0:N_META, :] = km_ref[...]
        pv_ref[0, 0:N_META, :] = vm_ref[...]

    hp = _layer_norm(x_ref[0], g0_ref[...], b0_ref[...])
    hp_ref[...] = hp
    _project(hp.astype(BF16), (wm_ref, wg_ref, wa_ref), bias_ref, u_ref)
    li_ref[...], cumf_ref[...] = _gate_rows(u_ref[:, C_G:C_G + 128])

    def chunk_body(ci, carry):
        r0 = pl.multiple_of(ci * CHUNK, CHUNK)
        rows = pl.ds(r0, CHUNK)
        key = _iota2((CHUNK, CHUNK), 0)
        query = _iota2((CHUNK, CHUNK), 1)
        causal = key <= query

        gate_rows = pl.ds(pl.multiple_of(ci * 8, 8), 8)
        li_rows = li_ref[gate_rows, :]
        b_rows = cumf_ref[gate_rows, :]
        m_carry = []
        for hd in range(M_HEADS):
            q = u_ref[rows, C_Q + hd * M_DH:C_Q + (hd + 1) * M_DH]
            k = u_ref[rows, C_K + hd * M_DH:C_K + (hd + 1) * M_DH] * (M_DH ** -0.5)
            v = u_ref[rows, C_V + hd * M_DH:C_V + (hd + 1) * M_DH]
            carry_hd, ct_new, m_new = _mlstm_scores_and_state(
                q, k, v, li_rows[hd:hd + 1], b_rows[M_HEADS + hd:M_HEADS + hd + 1],
                pm_ref[0, hd:hd + 1, :], ct_ref[hd], causal, CHUNK)
            ct_ref[hd] = ct_new
            pm_ref[0, hd:hd + 1, :] = jnp.broadcast_to(m_new, (1, 128))
            m_carry.append(carry_hd)

        k_own = u_ref[rows, C_AK:C_AK + KV_W]
        vt_own = u_ref[rows, C_AV:C_AV + KV_W].T
        k_prev = kprev_ref[...]
        vt_prev = vtprev_ref[...]
        prev_ok = (key > query) & ((j * n_chunks + ci) > 0)
        keys_kv = [tuple(_keep_kv_half(x, kv).astype(BF16) for x in (k_own, k_prev, km_ref[...]))
                   for kv in range(A_KV_HEADS)]
        values_kv = [tuple(x[kv * A_DH:(kv + 1) * A_DH].astype(BF16) for x in (vt_own, vt_prev, vmt_ref[...]))
                     for kv in range(A_KV_HEADS)]
        a_scores = []
        for tile in range(A_HEADS // 2):
            a_scores += _swa_scores_t(tile, u_ref[rows, C_AQ + tile * 128:C_AQ + (tile + 1) * 128],
                                      keys_kv[tile // (A_GROUP // 2)])
        kprev_ref[...] = k_own
        vtprev_ref[...] = vt_own

        m_carry = [_mlstm_weighted_values(*c) for c in m_carry]
        a_out = [_swa_weighted_values_t(a_scores[hd], values_kv[hd // A_GROUP], sink_ref[hd], causal, prev_ok)
                 for hd in range(A_HEADS)]

        for hd in range(M_HEADS):
            sl = slice(hd * M_DH, (hd + 1) * M_DH)
            mix_ref[rows, sl] = _mlstm_gate_head(
                _mlstm_finish(*m_carry[hd]).T, u_ref[rows, C_O + hd * M_DH:C_O + (hd + 1) * M_DH],
                u_ref[rows, C_Z + hd * M_DH:C_Z + (hd + 1) * M_DH], ng_ref[:, sl])
        for tile in range(A_HEADS // 2):
            o_tile = jnp.concatenate([o_t * (1.0 / den) for o_t, den in a_out[2 * tile:2 * tile + 2]], axis=0).T
            az = u_ref[rows, C_AZ + tile * 128:C_AZ + (tile + 1) * 128]
            mix_ref[rows, D_MLSTM + tile * 128:D_MLSTM + (tile + 1) * 128] = (o_tile * _silu(az)).astype(BF16)
        return carry

    lax.fori_loop(0, n_chunks, chunk_body, 0)

    y_ref[0] = _out_and_norm(hp_ref[...], mix_ref[...], wo_ref, lng_ref[...], lnb_ref[...])

    @pl.when(j == pl.num_programs(1) - 1)
    def _():
        pk_ref[0, N_META:N_BUF, :] = u_ref[tb - WINDOW:tb, C_AK:C_AK + KV_W]
        pv_ref[0, N_META:N_BUF, :] = u_ref[tb - WINDOW:tb, C_AV:C_AV + KV_W]
        for hd in range(M_HEADS):
            pc_ref[0, hd] = ct_ref[hd, 0:M_DH, :].T
            pn_ref[0, hd:hd + 1, :] = ct_ref[hd, M_DH:M_DH + 1, :]
        pn_ref[0, M_HEADS:8, :] = jnp.zeros((8 - M_HEADS, M_DH), F32)


def _const_spec(shape):
    return pl.BlockSpec(shape, lambda *_: (0,) * len(shape))


def _prompt_path(x_prompt, meta_tokens, ln0_g, ln0_b, w_parts, b_all, sinks, norm_g, w_o, ln_g, ln_b, tb=512):
    batch, seq, _ = x_prompt.shape
    meta_pad = jnp.pad(meta_tokens.astype(F32), ((0, CHUNK - N_META), (0, 0)))
    ct0, m0, km, vm, vmt = pl.pallas_call(
        _meta_kernel,
        out_shape=(jax.ShapeDtypeStruct((M_HEADS, CT_ROWS, M_DH), F32),
                   jax.ShapeDtypeStruct((8, 128), F32),
                   jax.ShapeDtypeStruct((N_META, KV_W), F32),
                   jax.ShapeDtypeStruct((N_META, KV_W), F32),
                   jax.ShapeDtypeStruct((KV_W, N_META), F32)),
        scratch_shapes=[pltpu.VMEM((CHUNK, N_PAD), F32)],
        compiler_params=pltpu.CompilerParams(vmem_limit_bytes=VMEM_LIMIT_BYTES),
        name="meta_tokens",
    )(meta_pad, *w_parts, b_all, ln0_g, ln0_b)

    nj = seq // tb
    in_specs = [
        pl.BlockSpec(memory_space=pltpu.SMEM),
        pl.BlockSpec((1, tb, D_MODEL), lambda b, j: (b, j, 0)),
        _const_spec((D_MODEL, C_G)), _const_spec((D_MODEL, C_AQ - C_G)), _const_spec((D_MODEL, N_PAD - C_AQ)),
        _const_spec((1, N_PAD)),
        _const_spec((1, D_MODEL)), _const_spec((1, D_MODEL)),
        _const_spec((M_HEADS, CT_ROWS, M_DH)), _const_spec((8, 128)),
        _const_spec((N_META, KV_W)), _const_spec((N_META, KV_W)), _const_spec((KV_W, N_META)),
        _const_spec((1, D_MLSTM)),
        _const_spec((D_MODEL, D_MODEL)),
        _const_spec((1, D_MODEL)), _const_spec((1, D_MODEL)),
    ]
    out_specs = [
        pl.BlockSpec((1, tb, D_MODEL), lambda b, j: (b, j, 0)),
        pl.BlockSpec((1, N_BUF, KV_W), lambda b, j: (b, 0, 0)),
        pl.BlockSpec((1, N_BUF, KV_W), lambda b, j: (b, 0, 0)),
        pl.BlockSpec((1, M_HEADS, M_DH, M_DH), lambda b, j: (b, 0, 0, 0)),
        pl.BlockSpec((1, 8, M_DH), lambda b, j: (b, 0, 0)),
        pl.BlockSpec((1, 8, 128), lambda b, j: (b, 0, 0)),
    ]
    out_shape = (
        jax.ShapeDtypeStruct((batch, seq, D_MODEL), F32),
        jax.ShapeDtypeStruct((batch, N_BUF, KV_W), F32),
        jax.ShapeDtypeStruct((batch, N_BUF, KV_W), F32),
        jax.ShapeDtypeStruct((batch, M_HEADS, M_DH, M_DH), F32),
        jax.ShapeDtypeStruct((batch, 8, M_DH), F32),
        jax.ShapeDtypeStruct((batch, 8, 128), F32),
    )
    y, pk, pv, pc, pn, pm = pl.pallas_call(
        functools.partial(_prompt_kernel, tb=tb),
        grid=(batch, nj),
        in_specs=in_specs,
        out_specs=out_specs,
        out_shape=out_shape,
        scratch_shapes=[pltpu.VMEM((tb, N_PAD), F32), pltpu.VMEM((tb, D_MODEL), F32),
                        pltpu.VMEM((tb, D_MODEL), BF16),
                        pltpu.VMEM((M_HEADS, CT_ROWS, M_DH), F32),
                        pltpu.VMEM((CHUNK, KV_W), F32), pltpu.VMEM((KV_W, CHUNK), F32),
                        pltpu.VMEM((tb // CHUNK * 8, CHUNK), F32), pltpu.VMEM((tb // CHUNK * 8, CHUNK), F32)],
        compiler_params=pltpu.CompilerParams(dimension_semantics=("arbitrary", "arbitrary"),
                                             vmem_limit_bytes=VMEM_LIMIT_BYTES),
        name="prompt_layer",
    )(sinks, x_prompt, *w_parts, b_all, ln0_g, ln0_b, ct0, m0, km, vm, vmt, norm_g, w_o, ln_g, ln_b)
    pk = pk.reshape(1, batch, N_BUF, A_KV_HEADS, A_DH)
    pv = pv.reshape(1, batch, N_BUF, A_KV_HEADS, A_DH)
    return y, pk, pv, pc[None], pn[:, :M_HEADS][None], pm[:, :M_HEADS, 0][None]


SEQ_PER_GROUP = 32
SWA_SEQ_PER_STEP = 8
SEQ_BATCH = 8


def _sample_proj_kernel(x_ref, wm_ref, wg_ref, wa_ref, bias_ref, g0_ref, b0_ref, hs_ref, u_ref):
    hs = _layer_norm(x_ref[...], g0_ref[...], b0_ref[...])
    hs_ref[...] = hs
    _project(hs.astype(BF16), (wm_ref, wg_ref, wa_ref), bias_ref, u_ref)


def _segment_last(x, pos, seg_len):
    n = x.shape[1]
    step = 1
    while step < seg_len:
        x = jnp.where((pos // step) % 2 == 0, pltpu.roll(x, n - step, 1), x)
        step *= 2
    return x


def _sample_mlstm_kernel(q_ref, k_ref, v_ref, g_ref, c_ref, n_ref, m_ref,
                         h_ref, cn_ref, nn_ref, mn_ref, inter_ref, *, dec_seq):
    hd = pl.program_id(1)
    n = CHUNK
    nb = n // dec_seq
    q = q_ref[...]
    k = k_ref[...] * (M_DH ** -0.5)
    v = v_ref[...]
    qb = q.astype(BF16)
    kb = k.astype(BF16)
    vb = v.astype(BF16)

    g_t = g_ref[...].T[0:8, :]
    row8 = _iota2((8, n), 0)
    li_row = jnp.sum(jnp.where(row8 == hd, g_t, 0.0), axis=0, keepdims=True)
    lf_row = _log_sigmoid(jnp.sum(jnp.where(row8 == hd + M_HEADS, g_t, 0.0), axis=0, keepdims=True))
    lane = _iota2((1, n), 1)
    pos = lane % dec_seq
    b_row = lf_row
    shift = 1
    while shift < dec_seq:
        b_row = b_row + jnp.where(pos >= shift, pltpu.roll(b_row, shift, 1), 0.0)
        shift *= 2

    key = _iota2((n, n), 0)
    query = _iota2((n, n), 1)
    key_ok = (key <= query) & (key // dec_seq == query // dec_seq)
    a_row = li_row - b_row
    a_keys = jnp.broadcast_to(jnp.sum(jnp.where(key == query, a_row, 0.0), axis=1, keepdims=True), (n, n))
    a_t = jnp.where(key_ok, a_keys, NEG_INF)
    m_old = m_ref[0, 0, 0:1, :]
    mm = jnp.maximum(m_old, jnp.max(a_t, axis=0, keepdims=True))
    w_t = jnp.exp(a_t - mm)
    mm_last = _segment_last(mm, pos, dec_seq)
    m_new = _segment_last(b_row, pos, dec_seq) + mm_last
    w_state = jnp.exp(a_row - mm_last)
    decay = jnp.exp(m_old - mm_last)
    mn_ref[0, 0] = jnp.broadcast_to(m_new, (8, n))

    qkw_t = _dot_nt(kb, qb) * w_t
    den = jnp.sum(qkw_t, axis=0, keepdims=True)
    num_t = _dot(v.T.astype(BF16), qkw_t.astype(BF16))

    n_seq = n_ref[0]
    seq_of_lane = _iota2((nb, n), 1) // dec_seq == _iota2((nb, n), 0)
    decay_seq = jnp.sum(jnp.where(_iota2((nb, n), 1) == _iota2((nb, n), 0) * dec_seq, decay, 0.0),
                        axis=1, keepdims=True)
    nn_ref[0] = decay_seq * n_seq + _dot(jnp.where(seq_of_lane, w_state, 0.0).astype(BF16), kb)
    expand = jnp.where(_iota2((n, nb), 0) // dec_seq == _iota2((n, nb), 1), 1.0, 0.0).astype(BF16)
    n_rows = _dot(expand, n_seq.astype(BF16))
    qn_col = jnp.sum(q * n_rows, axis=1, keepdims=True)
    qn = jnp.sum(jnp.where(key == query, qn_col, 0.0), axis=0, keepdims=True)

    kwt = k.T * w_state
    decay_rows = jnp.broadcast_to(jnp.sum(jnp.where(key == query, decay, 0.0), axis=1, keepdims=True), (n, n))
    lane_seq = query // dec_seq
    low_rows = _iota2((8, M_DH), 0) < dec_seq
    assert 8 % dec_seq == 0 and 8 // dec_seq == 2
    for pair in range(nb // 2):
        q8 = q[8 * pair:8 * pair + 8].astype(BF16)
        inter_ref[8 * pair:8 * pair + 8, :] = jnp.where(low_rows, _dot(q8, c_ref[2 * pair, 0].astype(BF16)),
                                                        _dot(q8, c_ref[2 * pair + 1, 0].astype(BF16)))
    for s0 in range(0, nb, SEQ_BATCH):
        seqs = range(s0, s0 + SEQ_BATCH)
        updates = [_dot(jnp.where(lane_seq == s, kwt, 0.0).astype(BF16), vb) for s in seqs]
        for s, upd in zip(seqs, updates):
            cn_ref[s, 0] = decay_rows[s * dec_seq:s * dec_seq + 1, :] * c_ref[s, 0] + upd

    s_inter = jnp.exp(m_old - mm)
    num_t = num_t + inter_ref[...].T * s_inter
    den = den + qn * s_inter
    h_ref[...] = (num_t * (1.0 / jnp.maximum(jnp.abs(den), jnp.exp(-(b_row + mm))))).T


def _sample_swa_kernel(qz_ref, kn_ref, vn_ref, ck_ref, cv_ref, sink_ref, o_ref, nk_ref, nv_ref, *, dec_seq):
    n_rows = qz_ref.shape[1]
    t_c = _iota2((n_rows, N_BUF), 0) % dec_seq
    i_c = _iota2((n_rows, N_BUF), 1)
    ok_c = (i_c < N_META) | (i_c > t_c + N_META)
    t_n = _iota2((n_rows, 8), 0) % dec_seq
    i_n = _iota2((n_rows, 8), 1)
    ok_n = (i_n <= t_n) & (i_n < dec_seq)
    sink = sink_ref[:, 0:1]

    seqs = range(qz_ref.shape[0])
    scores = []
    for s in seqs:
        qz = (qz_ref[s] * (A_DH ** -0.5)).astype(BF16)
        scores.append((_dot_nt(qz, ck_ref[s].astype(BF16)), _dot_nt(qz, kn_ref[s].astype(BF16))))
    probs = []
    for s_c, s_n in scores:
        s_c = jnp.where(ok_c, s_c, NEG_INF)
        s_n = jnp.where(ok_n, s_n, NEG_INF)
        mx = jnp.maximum(jnp.maximum(jnp.max(s_c, axis=1, keepdims=True),
                                     jnp.max(s_n, axis=1, keepdims=True)), sink)
        p_c = jnp.exp(s_c - mx)
        p_n = jnp.exp(s_n - mx)
        den = jnp.sum(p_c, axis=1, keepdims=True) + jnp.sum(p_n, axis=1, keepdims=True) + jnp.exp(sink - mx)
        probs.append((p_c.astype(BF16), p_n.astype(BF16), den))
    outs = [_dot(p_c, cv_ref[s].astype(BF16)) + _dot(p_n, vn_ref[s].astype(BF16))
            for s, (p_c, p_n, _) in zip(seqs, probs)]
    for s, o, (_, _, den) in zip(seqs, outs, probs):
        o_ref[s] = o / den
    for s in seqs:
        for cache_ref, new_ref, out_ref in ((ck_ref, kn_ref, nk_ref), (cv_ref, vn_ref, nv_ref)):
            out_ref[s, 0:N_META, :] = cache_ref[s, 0:N_META, :]
            out_ref[s, N_META:N_BUF - dec_seq, :] = cache_ref[s, N_META + dec_seq:N_BUF, :]
            out_ref[s, N_BUF - dec_seq:N_BUF, :] = new_ref[s, 0:dec_seq, :]


def _sample_out_kernel(hs_ref, u_ref, hm_ref, oa_ref, ng_ref, wo_ref, lng_ref, lnb_ref, y_ref):
    mix = _gate_mix(hm_ref[...], u_ref[:, C_O:C_O + D_MLSTM], u_ref[:, C_Z:C_Z + D_MLSTM],
                    oa_ref[...], u_ref[:, C_AZ:C_AZ + D_SWA], ng_ref[...])
    y_ref[...] = _out_and_norm(hs_ref[...], mix, wo_ref, lng_ref[...], lnb_ref[...])


def _sample_path(x_sample, cache_k, cache_v, state_c, state_n, state_m,
                 ln0_g, ln0_b, w_parts, b_all, a_sinks, norm_g, w_o, ln_g, ln_b):
    db, dec_seq, _ = x_sample.shape
    rows = db * dec_seq
    params = pltpu.CompilerParams(vmem_limit_bytes=VMEM_LIMIT_BYTES)
    hs, u = pl.pallas_call(
        _sample_proj_kernel,
        out_shape=(jax.ShapeDtypeStruct((rows, D_MODEL), F32), jax.ShapeDtypeStruct((rows, N_PAD), F32)),
        compiler_params=params,
        name="sample_proj",
    )(x_sample.reshape(rows, D_MODEL), *w_parts, b_all, ln0_g, ln0_b)

    n_groups = db // SEQ_PER_GROUP
    n_t = jnp.transpose(state_n, (1, 0, 2))
    m_t = jnp.repeat(jnp.transpose(state_m, (1, 0)), dec_seq, axis=1).reshape(M_HEADS, n_groups, 1, CHUNK)
    m_t = jnp.broadcast_to(m_t, (M_HEADS, n_groups, 8, CHUNK))

    def col_spec(col0):
        return pl.BlockSpec((CHUNK, M_DH), lambda g, h: (g, col0 // M_DH + h))

    state_spec = pl.BlockSpec((SEQ_PER_GROUP, 1, M_DH, M_DH), lambda g, h: (g, h, 0, 0))
    vec_spec = pl.BlockSpec((1, SEQ_PER_GROUP, M_DH), lambda g, h: (h, g, 0))
    m_spec = pl.BlockSpec((1, 1, 8, CHUNK), lambda g, h: (h, g, 0, 0))
    h_m, c_new, n_new, m_new = pl.pallas_call(
        functools.partial(_sample_mlstm_kernel, dec_seq=dec_seq),
        grid=(n_groups, M_HEADS),
        in_specs=[col_spec(C_Q), col_spec(C_K), col_spec(C_V),
                  pl.BlockSpec((CHUNK, 128), lambda g, h: (g, C_G // 128)),
                  state_spec, vec_spec, m_spec],
        out_specs=[pl.BlockSpec((CHUNK, M_DH), lambda g, h: (g, h)), state_spec, vec_spec, m_spec],
        out_shape=(jax.ShapeDtypeStruct((rows, D_MLSTM), F32),
                   jax.ShapeDtypeStruct(state_c.shape, F32),
                   jax.ShapeDtypeStruct((M_HEADS, db, M_DH), F32),
                   jax.ShapeDtypeStruct((M_HEADS, n_groups, 8, CHUNK), F32)),
        scratch_shapes=[pltpu.VMEM((CHUNK, M_DH), F32)],
        compiler_params=pltpu.CompilerParams(dimension_semantics=("arbitrary", "arbitrary"),
                                             vmem_limit_bytes=VMEM_LIMIT_BYTES),
        name="sample_mlstm",
    )(u, u, u, u, state_c, n_t, m_t)

    aq = u[:, C_AQ:C_AQ + D_SWA].reshape(db, dec_seq, A_KV_HEADS, A_GROUP, A_DH)
    aq = jnp.transpose(aq, (0, 2, 3, 1, 4)).reshape(db, A_KV_HEADS, A_GROUP * dec_seq, A_DH)
    zeros = jnp.zeros_like(aq[:, 0])
    qz = jnp.stack([jnp.concatenate([aq[:, 0], zeros], axis=-1),
                    jnp.concatenate([zeros, aq[:, 1]], axis=-1)], axis=1)
    n_qrows = A_HEADS * dec_seq
    qz = qz.reshape(db, n_qrows, KV_W)
    k_new = jnp.pad(u[:, C_AK:C_AK + KV_W].reshape(db, dec_seq, KV_W), ((0, 0), (0, 8 - dec_seq), (0, 0)))
    v_new = jnp.pad(u[:, C_AV:C_AV + KV_W].reshape(db, dec_seq, KV_W), ((0, 0), (0, 8 - dec_seq), (0, 0)))
    sink_rows = jnp.broadcast_to(jnp.repeat(a_sinks.astype(F32), dec_seq)[:, None], (n_qrows, 128))
    ck = cache_k.reshape(db, N_BUF, KV_W)
    cv = cache_v.reshape(db, N_BUF, KV_W)
    sb = SWA_SEQ_PER_STEP

    def seq_spec(r):
        return pl.BlockSpec((sb, r, KV_W), lambda i: (i, 0, 0))

    o, nk, nv = pl.pallas_call(
        functools.partial(_sample_swa_kernel, dec_seq=dec_seq),
        grid=(db // sb,),
        in_specs=[seq_spec(n_qrows), seq_spec(8), seq_spec(8), seq_spec(N_BUF), seq_spec(N_BUF),
                  pl.BlockSpec((n_qrows, 128), lambda i: (0, 0))],
        out_specs=[seq_spec(n_qrows), seq_spec(N_BUF), seq_spec(N_BUF)],
        out_shape=(jax.ShapeDtypeStruct((db, n_qrows, KV_W), F32),
                   jax.ShapeDtypeStruct((db, N_BUF, KV_W), F32),
                   jax.ShapeDtypeStruct((db, N_BUF, KV_W), F32)),
        compiler_params=pltpu.CompilerParams(dimension_semantics=("arbitrary",),
                                             vmem_limit_bytes=VMEM_LIMIT_BYTES),
        name="sample_swa",
    )(qz, k_new, v_new, ck, cv, sink_rows)
    o = o.reshape(db, A_KV_HEADS, A_GROUP, dec_seq, A_KV_HEADS, A_DH)
    o = jnp.stack([o[:, 0, :, :, 0, :], o[:, 1, :, :, 1, :]], axis=1)
    o_a = jnp.transpose(o, (0, 3, 1, 2, 4)).reshape(rows, D_SWA)

    y = pl.pallas_call(
        _sample_out_kernel,
        out_shape=jax.ShapeDtypeStruct((rows, D_MODEL), F32),
        compiler_params=params,
        name="sample_out",
    )(hs, u, h_m, o_a, norm_g, w_o, ln_g, ln_b)

    shape5 = (1, db, N_BUF, A_KV_HEADS, A_DH)
    return (y.reshape(db, dec_seq, D_MODEL), nk.reshape(shape5), nv.reshape(shape5), c_new[None],
            jnp.transpose(n_new, (1, 0, 2))[None], jnp.transpose(m_new[:, :, 0, ::dec_seq].reshape(M_HEADS, db), (1, 0))[None])


def kernel(x_prompt, x_sample, cache_swa_k, cache_swa_v, state_mlstm_c, state_mlstm_n, state_mlstm_m,
           meta_tokens, ln0_g, ln0_b, w_in, b_in, a_sinks, m_norm_g, w_out, ln_g, ln_b):
    assert w_in.shape[0] == DEPTH and x_prompt.shape[-1] == D_MODEL
    w = w_in[0]
    w_parts = (w[:, :C_G].astype(BF16),
               jnp.pad(w[:, C_G:N_RAW_GATE_END], ((0, 0), (0, C_AQ - N_RAW_GATE_END))).astype(BF16),
               w[:, N_RAW_GATE_END:].astype(BF16))
    b = b_in[0].astype(F32)
    b_all = jnp.concatenate([b[:N_RAW_GATE_END], jnp.zeros((C_AQ - N_RAW_GATE_END,), F32),
                             b[N_RAW_GATE_END:]])[None]
    w_o = w_out[0].astype(BF16)
    g0 = ln0_g.astype(F32)[None]
    b0 = ln0_b.astype(F32)[None]
    lg = ln_g[0].astype(F32)[None]
    lb = ln_b[0].astype(F32)[None]
    norm_g = m_norm_g[0].astype(F32)[None]
    sinks = a_sinks[0].astype(F32)

    y_p, pk, pv, pc, pn, pm = _prompt_path(x_prompt, meta_tokens, g0, b0, w_parts, b_all, sinks, norm_g, w_o, lg, lb)
    y_s, sk, sv, sc, sn, sm = _sample_path(x_sample, cache_swa_k[0], cache_swa_v[0], state_mlstm_c[0],
                                           state_mlstm_n[0], state_mlstm_m[0],
                                           g0, b0, w_parts, b_all, sinks, norm_g, w_o, lg, lb)
    return (y_p, y_s, pk, pv, pc, pn, pm, sk, sv, sc, sn, sm)
```

```python
import functools

import jax
import jax.numpy as jnp
from jax import lax
from jax.experimental import pallas as pl
from jax.experimental.pallas import tpu as pltpu

F32 = jnp.float32
BF16 = jnp.bfloat16

D_MODEL = 1024
N_META = 16
M_HEADS = 4
M_DH = 128
D_MLSTM = M_HEADS * M_DH
A_HEADS = 8
A_KV_HEADS = 2
A_GROUP = A_HEADS // A_KV_HEADS
A_DH = 64
D_SWA = A_HEADS * A_DH
KV_W = A_KV_HEADS * A_DH
WINDOW = 128
CHUNK = 128
LN_EPS = 1e-5
DEPTH = 1
DN_ALPHA = (2.0 * DEPTH) ** 0.25
N_BUF = N_META + WINDOW

C_Q, C_AQ = 0, 512
N_T_BF16 = 1024
C_AV, C_V, C_O, C_Z, C_AZ, C_G = 1024, 1152, 1664, 2176, 2688, 3200
GATE_ROWS = 16
N_T = C_G + GATE_ROWS
C_K, C_AK = 3328, 3840
N_PAD = 3968
N_KEYS = N_PAD - C_K
RAW_Q, RAW_K, RAW_V, RAW_O, RAW_Z, RAW_G = (0, 512), (512, 512), (1024, 512), (1536, 512), (2048, 512), (2560, 8)
RAW_AQ, RAW_AK, RAW_AV, RAW_AZ = (2568, 512), (3080, 128), (3208, 128), (3336, 512)

PROJ_STEP = 512
CHUNK_UNROLL = 4
VMEM_LIMIT_BYTES = 56 * 1024 * 1024
NEG_INF = float("-inf")


def _dot(a, b):
    return jnp.dot(a, b, preferred_element_type=F32)


def _dot_nt(a, b):
    return lax.dot_general(a, b, (((1,), (1,)), ((), ())), preferred_element_type=F32)


def _layer_norm(x, g, b):
    mu = jnp.mean(x, axis=-1, keepdims=True)
    xc = x - mu
    var = jnp.mean(xc * xc, axis=-1, keepdims=True)
    return xc * lax.rsqrt(var + LN_EPS) * g + b


def _log_sigmoid(x):
    return jnp.minimum(x, 0.0) - jnp.log1p(jnp.exp(-jnp.abs(x)))


def _times_sigmoid(h, half_x):
    return 0.5 * (h * jnp.tanh(half_x) + h)


def _silu_of_twice(half_x):
    return half_x * jnp.tanh(half_x) + half_x


def _iota2(shape, dim):
    return lax.broadcasted_iota(jnp.int32, shape, dim)


def _project(hb, wt_ref, b_ref, u_ref):
    for n0 in range(0, N_PAD, PROJ_STEP):
        n1 = min(n0 + PROJ_STEP, N_PAD)
        u_ref[:, n0:n1] = _dot_nt(hb, wt_ref[n0:n1, :]) + b_ref[:, n0:n1]


def _project_both(hb, wt_ref, bcol_ref, brow_ref, utb_ref, utf_ref, ukey_ref):
    reps = hb.shape[0] // 128
    starts = list(range(0, N_T, PROJ_STEP))
    for n0 in starts[-1:] + starts[:-1]:
        n1 = min(n0 + PROJ_STEP, N_T)
        res = _dot_nt(wt_ref[n0:n1, :], hb) + jnp.concatenate([bcol_ref[n0:n1, :]] * reps, axis=1)
        if n1 <= N_T_BF16:
            utb_ref[n0:n1, :] = res.astype(BF16)
        else:
            utf_ref[n0 - N_T_BF16:n1 - N_T_BF16, :] = res
    for n0 in range(C_K, N_PAD, PROJ_STEP):
        n1 = min(n0 + PROJ_STEP, N_PAD)
        ukey_ref[:, n0 - C_K:n1 - C_K] = _dot_nt(hb, wt_ref[n0:n1, :]) + brow_ref[:, n0:n1]


def _gate_rows(gates):
    return _gate_scan(jnp.concatenate([gates[r0:r0 + CHUNK].T[0:8, :] for r0 in range(0, gates.shape[0], CHUNK)],
                                      axis=0))


def _gate_rows_t(gates_t):
    return _gate_scan(jnp.concatenate([gates_t[:, c0:c0 + CHUNK] for c0 in range(0, gates_t.shape[1], CHUNK)],
                                      axis=0))


def _gate_scan(g_t):
    x = _log_sigmoid(g_t)
    lane = _iota2(x.shape, 1)
    shift = 1
    while shift < x.shape[1]:
        x = x + jnp.where(lane >= shift, pltpu.roll(x, shift, 1), 0.0)
        shift *= 2
    return g_t, x


CT_ROWS = M_DH + 8


def _mlstm_scores_and_state(q_tb, kb, v_t, li_row, b_row, m_old, ct_aug, key_ok, n_valid):
    n = kb.shape[0]
    row = _iota2((n, n), 0)
    col = _iota2((n, n), 1)
    a_row = li_row - b_row
    a_keys = jnp.broadcast_to(jnp.sum(jnp.where(row == col, a_row, 0.0), axis=1, keepdims=True), (n, n))
    a_t = jnp.where(key_ok, a_keys, NEG_INF)
    mm = jnp.maximum(m_old, jnp.max(a_t, axis=0, keepdims=True))
    w_t = jnp.exp(a_t - mm)
    lane = _iota2((1, n), 1)
    last = n_valid - 1
    mm_last = jnp.max(jnp.where(lane == last, mm, NEG_INF), axis=1, keepdims=True)
    m_new = jnp.sum(jnp.where(lane == last, b_row, 0.0), axis=1, keepdims=True) + mm_last
    w_state = jnp.exp(a_row - mm_last)
    if n_valid < n:
        w_state = jnp.where(lane < n_valid, w_state, 0.0)
    decay = jnp.exp(m_old - mm_last)
    scores_t = _dot(kb, q_tb)
    inter = _dot(ct_aug.astype(BF16), q_tb)
    ones_row = jnp.where(_iota2((CT_ROWS - M_DH, n), 0) == 0, w_state, 0.0)
    vtw = jnp.concatenate([v_t * w_state, ones_row], axis=0).astype(BF16)
    ct_aug_new = decay * ct_aug + _dot(vtw, kb)
    s_inter = jnp.exp(m_old - mm)
    floor = jnp.exp(-(b_row + mm))
    return (scores_t, w_t, v_t.astype(BF16), inter, s_inter, floor), ct_aug_new, m_new


def _mlstm_weighted_values(scores_t, w_t, vtb, inter, s_inter, floor):
    qkw_t = scores_t * w_t
    den = jnp.sum(qkw_t, axis=0, keepdims=True)
    num_t = _dot(vtb, qkw_t.astype(BF16))
    return num_t, den, inter, s_inter, floor


def _mlstm_finish(num_t, den, inter, s_inter, floor):
    num_t = num_t + inter[0:M_DH] * s_inter
    den = den + inter[M_DH:M_DH + 1] * s_inter
    return num_t * (1.0 / jnp.maximum(jnp.abs(den), floor))


def _mlstm_gate_head(h, half_o, half_z, norm_g, axis):
    hh = _times_sigmoid(h, half_o)
    mu = jnp.mean(hh, axis=axis, keepdims=True)
    hc = hh - mu
    var = jnp.mean(hc * hc, axis=axis, keepdims=True)
    return (hc * lax.rsqrt(var + LN_EPS) * norm_g * _silu_of_twice(half_z)).astype(BF16)


def _gate_mix(h_m, half_o, half_z, o_a, half_az, norm_g):
    parts = []
    for hd in range(M_HEADS):
        sl = slice(hd * M_DH, (hd + 1) * M_DH)
        parts.append(_mlstm_gate_head(h_m[:, sl], half_o[:, sl], half_z[:, sl], norm_g[:, sl], axis=-1))
    parts.append((o_a * _silu_of_twice(half_az)).astype(BF16))
    return jnp.concatenate(parts, axis=-1)


def _out_and_norm(hp, mix, wo_ref, g, b):
    z = DN_ALPHA * hp + _dot(mix, wo_ref[...])
    return _layer_norm(z, g, b)


def _meta_kernel(meta_ref, wt_ref, bias_ref, g0_ref, b0_ref,
                 ct0_ref, m0_ref, km_ref, vm_ref, vmt_ref, u_ref):
    hp = _layer_norm(meta_ref[...], g0_ref[...], b0_ref[...])
    _project(hp.astype(BF16), wt_ref, bias_ref, u_ref)
    row = _iota2((CHUNK, CHUNK), 0)
    col = _iota2((CHUNK, CHUNK), 1)
    key_ok = (row <= col) & (row < N_META)
    li_rows, b_rows = _gate_rows(u_ref[:, C_G:C_G + 128])
    zero_m = jnp.zeros((1, 128), F32)
    zero_ct = jnp.zeros((CT_ROWS, M_DH), F32)
    m0_ref[...] = jnp.zeros(m0_ref.shape, F32)
    for hd in range(M_HEADS):
        q = u_ref[:, C_Q + hd * M_DH:C_Q + (hd + 1) * M_DH]
        k = u_ref[:, C_K + hd * M_DH:C_K + (hd + 1) * M_DH] * (M_DH ** -0.5)
        v = u_ref[:, C_V + hd * M_DH:C_V + (hd + 1) * M_DH]
        _, ct_new, m_new = _mlstm_scores_and_state(
            q.T.astype(BF16), k.astype(BF16), v.T, li_rows[hd:hd + 1], b_rows[M_HEADS + hd:M_HEADS + hd + 1],
            zero_m, zero_ct, key_ok, N_META)
        ct0_ref[hd] = ct_new
        m0_ref[hd:hd + 1, :] = jnp.broadcast_to(m_new, (1, 128))
    km_ref[...] = u_ref[0:N_META, C_AK:C_AK + KV_W]
    vm_ref[...] = u_ref[0:N_META, C_AV:C_AV + KV_W]
    vmt_ref[...] = u_ref[:, C_AV:C_AV + KV_W].T[:, 0:N_META]


def _keep_kv_half(x, kv):
    low = _iota2(x.shape, 1) < A_DH
    return jnp.where(low if kv == 0 else ~low, x, 0.0)


def _to_token_rows(x_t, eye):
    return _dot_nt(eye, x_t).astype(BF16)


def _swa_weighted_values_t(scores, values_t, sink, own_ok, prev_ok):
    s_own = jnp.where(own_ok, scores[0], NEG_INF)
    s_prev = jnp.where(prev_ok, scores[1], NEG_INF)
    s_meta = scores[2]
    mx = jnp.maximum(jnp.maximum(jnp.max(s_own, axis=0, keepdims=True),
                                 jnp.max(s_prev, axis=0, keepdims=True)),
                     jnp.maximum(jnp.max(s_meta, axis=0, keepdims=True), sink))
    p_own, p_prev, p_meta = (jnp.exp(s - mx) for s in (s_own, s_prev, s_meta))
    den = (jnp.sum(p_own, axis=0, keepdims=True) + jnp.sum(p_prev, axis=0, keepdims=True)
           + jnp.sum(p_meta, axis=0, keepdims=True) + jnp.exp(sink - mx))
    o_t = (_dot(values_t[0], p_own.astype(BF16)) + _dot(values_t[1], p_prev.astype(BF16))
           + _dot(values_t[2], p_meta.astype(BF16)))
    return o_t, den


def _prompt_kernel(sink_ref, x_ref, wt_ref, bcol_ref, brow_ref, g0_ref, b0_ref,
                   ct0_ref, m0_ref, km_ref, vm_ref, vmt_ref, ng_ref, wo_ref, lng_ref, lnb_ref,
                   y_ref, pk_ref, pv_ref, pc_ref, pn_ref, pm_ref,
                   utb_ref, utf_ref, ukey_ref, hp_ref, mix_ref, ct_ref, kprev_ref, vtprev_ref, li_ref, cumf_ref,
                   *, tb):
    j = pl.program_id(1)
    n_chunks = tb // CHUNK

    @pl.when(j == 0)
    def _():
        ct_ref[...] = ct0_ref[...]
        pm_ref[0] = m0_ref[...]
        kprev_ref[...] = jnp.zeros(kprev_ref.shape, F32)
        vtprev_ref[...] = jnp.zeros(vtprev_ref.shape, F32)
        pk_ref[0, 0:N_META, :] = km_ref[...]
        pv_ref[0, 0:N_META, :] = vm_ref[...]

    hp = _layer_norm(x_ref[0], g0_ref[...], b0_ref[...])
    hp_ref[...] = hp
    _project_both(hp.astype(BF16), wt_ref, bcol_ref, brow_ref, utb_ref, utf_ref, ukey_ref)

    def feat(c0, n=M_DH):
        return slice(c0 - N_T_BF16, c0 - N_T_BF16 + n)

    li_ref[...], cumf_ref[...] = _gate_rows_t(utf_ref[feat(C_G, 8), :])

    def chunk_body(ci, carry):
        r0 = pl.multiple_of(ci * CHUNK, CHUNK)
        rows = pl.ds(r0, CHUNK)
        key = _iota2((CHUNK, CHUNK), 0)
        query = _iota2((CHUNK, CHUNK), 1)
        causal = key <= query

        gate_rows = pl.ds(pl.multiple_of(ci * 8, 8), 8)
        li_rows = li_ref[gate_rows, :]
        b_rows = cumf_ref[gate_rows, :]
        m_carry = []
        for hd in range(M_HEADS):
            q_tb = utb_ref[C_Q + hd * M_DH:C_Q + (hd + 1) * M_DH, rows]
            kb = (ukey_ref[rows, hd * M_DH:(hd + 1) * M_DH] * (M_DH ** -0.5)).astype(BF16)
            v_t = utf_ref[feat(C_V + hd * M_DH), rows]
            carry_hd, ct_new, m_new = _mlstm_scores_and_state(
                q_tb, kb, v_t, li_rows[hd:hd + 1], b_rows[M_HEADS + hd:M_HEADS + hd + 1],
                pm_ref[0, hd:hd + 1, :], ct_ref[hd], causal, CHUNK)
            ct_ref[hd] = ct_new
            pm_ref[0, hd:hd + 1, :] = jnp.broadcast_to(m_new, (1, 128))
            m_carry.append(carry_hd)

        k_own = ukey_ref[rows, C_AK - C_K:C_AK - C_K + KV_W]
        vt_own = utf_ref[feat(C_AV, KV_W), rows]
        k_prev = kprev_ref[...]
        vt_prev = vtprev_ref[...]
        prev_ok = (key > query) & ((j * n_chunks + ci) > 0)
        keys_kv = [tuple((_keep_kv_half(x, kv) * (A_DH ** -0.5)).astype(BF16) for x in (k_own, k_prev, km_ref[...]))
                   for kv in range(A_KV_HEADS)]
        values_kv = [tuple(x[kv * A_DH:(kv + 1) * A_DH].astype(BF16) for x in (vt_own, vt_prev, vmt_ref[...]))
                     for kv in range(A_KV_HEADS)]
        a_scores = []
        for hd in range(A_HEADS):
            kv = hd // A_GROUP
            q0 = C_AQ + (hd - kv) * A_DH
            q_win = utb_ref[q0:q0 + 2 * A_DH, rows]
            a_scores.append(tuple(_dot(kk, q_win) for kk in keys_kv[kv]))
        kprev_ref[...] = k_own
        vtprev_ref[...] = vt_own

        m_carry = [_mlstm_weighted_values(*c) for c in m_carry]
        a_out = [_swa_weighted_values_t(a_scores[hd], values_kv[hd // A_GROUP], sink_ref[hd], causal, prev_ok)
                 for hd in range(A_HEADS)]

        mix_t = []
        for hd in range(M_HEADS):
            mix_t.append(_mlstm_gate_head(_mlstm_finish(*m_carry[hd]), utf_ref[feat(C_O + hd * M_DH), rows],
                                          utf_ref[feat(C_Z + hd * M_DH), rows],
                                          ng_ref[hd * M_DH:(hd + 1) * M_DH, :], axis=0))
        for tile in range(A_HEADS // 2):
            o_t = jnp.concatenate([o * (1.0 / den) for o, den in a_out[2 * tile:2 * tile + 2]], axis=0)
            mix_t.append((o_t * _silu_of_twice(utf_ref[feat(C_AZ + tile * 128), rows])).astype(BF16))
        eye = jnp.where(key == query, 1.0, 0.0).astype(BF16)
        mix_rows = [_to_token_rows(x_t, eye) for x_t in mix_t]
        for i, x in enumerate(mix_rows):
            mix_ref[rows, i * 128:(i + 1) * 128] = x
        return carry

    lax.fori_loop(0, n_chunks, chunk_body, 0, unroll=CHUNK_UNROLL)

    y_ref[0] = _out_and_norm(hp_ref[...], mix_ref[...], wo_ref, lng_ref[...], lnb_ref[...])

    @pl.when(j == pl.num_programs(1) - 1)
    def _():
        pk_ref[0, N_META:N_BUF, :] = ukey_ref[tb - WINDOW:tb, C_AK - C_K:C_AK - C_K + KV_W]
        pv_ref[0, N_META:N_BUF, :] = utf_ref[feat(C_AV, KV_W), tb - WINDOW:tb].T
        for hd in range(M_HEADS):
            pc_ref[0, hd] = ct_ref[hd, 0:M_DH, :].T
            pn_ref[0, hd:hd + 1, :] = ct_ref[hd, M_DH:M_DH + 1, :]
        pn_ref[0, M_HEADS:8, :] = jnp.zeros((8 - M_HEADS, M_DH), F32)


def _const_spec(shape):
    return pl.BlockSpec(shape, lambda *_: (0,) * len(shape))


def _prompt_path(x_prompt, meta_tokens, ln0_g, ln0_b, w_t, b_all, sinks, norm_g, w_o, ln_g, ln_b, tb=512):
    batch, seq, _ = x_prompt.shape
    meta_pad = jnp.pad(meta_tokens.astype(F32), ((0, CHUNK - N_META), (0, 0)))
    ct0, m0, km, vm, vmt = pl.pallas_call(
        _meta_kernel,
        out_shape=(jax.ShapeDtypeStruct((M_HEADS, CT_ROWS, M_DH), F32),
                   jax.ShapeDtypeStruct((8, 128), F32),
                   jax.ShapeDtypeStruct((N_META, KV_W), F32),
                   jax.ShapeDtypeStruct((N_META, KV_W), F32),
                   jax.ShapeDtypeStruct((KV_W, N_META), F32)),
        scratch_shapes=[pltpu.VMEM((CHUNK, N_PAD), F32)],
        compiler_params=pltpu.CompilerParams(vmem_limit_bytes=VMEM_LIMIT_BYTES),
        name="meta_tokens",
    )(meta_pad, w_t, b_all, ln0_g, ln0_b)

    nj = seq // tb
    in_specs = [
        pl.BlockSpec(memory_space=pltpu.SMEM),
        pl.BlockSpec((1, tb, D_MODEL), lambda b, j: (b, j, 0)),
        _const_spec((N_PAD, D_MODEL)),
        _const_spec((N_PAD, 128)), _const_spec((1, N_PAD)),
        _const_spec((1, D_MODEL)), _const_spec((1, D_MODEL)),
        _const_spec((M_HEADS, CT_ROWS, M_DH)), _const_spec((8, 128)),
        _const_spec((N_META, KV_W)), _const_spec((N_META, KV_W)), _const_spec((KV_W, N_META)),
        _const_spec((D_MLSTM, 128)),
        _const_spec((D_MODEL, D_MODEL)),
        _const_spec((1, D_MODEL)), _const_spec((1, D_MODEL)),
    ]
    out_specs = [
        pl.BlockSpec((1, tb, D_MODEL), lambda b, j: (b, j, 0)),
        pl.BlockSpec((1, N_BUF, KV_W), lambda b, j: (b, 0, 0)),
        pl.BlockSpec((1, N_BUF, KV_W), lambda b, j: (b, 0, 0)),
        pl.BlockSpec((1, M_HEADS, M_DH, M_DH), lambda b, j: (b, 0, 0, 0)),
        pl.BlockSpec((1, 8, M_DH), lambda b, j: (b, 0, 0)),
        pl.BlockSpec((1, 8, 128), lambda b, j: (b, 0, 0)),
    ]
    out_shape = (
        jax.ShapeDtypeStruct((batch, seq, D_MODEL), F32),
        jax.ShapeDtypeStruct((batch, N_BUF, KV_W), F32),
        jax.ShapeDtypeStruct((batch, N_BUF, KV_W), F32),
        jax.ShapeDtypeStruct((batch, M_HEADS, M_DH, M_DH), F32),
        jax.ShapeDtypeStruct((batch, 8, M_DH), F32),
        jax.ShapeDtypeStruct((batch, 8, 128), F32),
    )
    y, pk, pv, pc, pn, pm = pl.pallas_call(
        functools.partial(_prompt_kernel, tb=tb),
        grid=(batch, nj),
        in_specs=in_specs,
        out_specs=out_specs,
        out_shape=out_shape,
        scratch_shapes=[pltpu.VMEM((N_T_BF16, tb), BF16), pltpu.VMEM((N_T - N_T_BF16, tb), F32),
                        pltpu.VMEM((tb, N_KEYS), F32),
                        pltpu.VMEM((tb, D_MODEL), F32), pltpu.VMEM((tb, D_MODEL), BF16),
                        pltpu.VMEM((M_HEADS, CT_ROWS, M_DH), F32),
                        pltpu.VMEM((CHUNK, KV_W), F32), pltpu.VMEM((KV_W, CHUNK), F32),
                        pltpu.VMEM((tb // CHUNK * 8, CHUNK), F32), pltpu.VMEM((tb // CHUNK * 8, CHUNK), F32)],
        compiler_params=pltpu.CompilerParams(dimension_semantics=("arbitrary", "arbitrary"),
                                             vmem_limit_bytes=VMEM_LIMIT_BYTES),
        name="prompt_layer",
    )(sinks, x_prompt, w_t, jnp.broadcast_to(b_all.reshape(N_PAD, 1), (N_PAD, 128)), b_all, ln0_g, ln0_b,
      ct0, m0, km, vm, vmt, jnp.broadcast_to(norm_g.reshape(D_MLSTM, 1), (D_MLSTM, 128)), w_o, ln_g, ln_b)
    pk = pk.reshape(1, batch, N_BUF, A_KV_HEADS, A_DH)
    pv = pv.reshape(1, batch, N_BUF, A_KV_HEADS, A_DH)
    return y, pk, pv, pc[None], pn[:, :M_HEADS][None], pm[:, :M_HEADS, 0][None]


SEQ_PER_GROUP = 32
SWA_SEQ_PER_STEP = 8
SEQ_BATCH = 8


def _sample_proj_kernel(x_ref, wt_ref, bias_ref, g0_ref, b0_ref, hs_ref, u_ref):
    hs = _layer_norm(x_ref[...], g0_ref[...], b0_ref[...])
    hs_ref[...] = hs
    _project(hs.astype(BF16), wt_ref, bias_ref, u_ref)


def _segment_last(x, pos, seg_len):
    n = x.shape[1]
    step = 1
    while step < seg_len:
        x = jnp.where((pos // step) % 2 == 0, pltpu.roll(x, n - step, 1), x)
        step *= 2
    return x


def _sample_mlstm_kernel(q_ref, k_ref, v_ref, g_ref, c_ref, n_ref, m_ref,
                         h_ref, cn_ref, nn_ref, mn_ref, inter_ref, *, dec_seq):
    hd = pl.program_id(1)
    n = CHUNK
    nb = n // dec_seq
    q = q_ref[...]
    k = k_ref[...] * (M_DH ** -0.5)
    v = v_ref[...]
    qb = q.astype(BF16)
    kb = k.astype(BF16)
    vb = v.astype(BF16)

    g_t = g_ref[...].T[0:8, :]
    row8 = _iota2((8, n), 0)
    li_row = jnp.sum(jnp.where(row8 == hd, g_t, 0.0), axis=0, keepdims=True)
    lf_row = _log_sigmoid(jnp.sum(jnp.where(row8 == hd + M_HEADS, g_t, 0.0), axis=0, keepdims=True))
    lane = _iota2((1, n), 1)
    pos = lane % dec_seq
    b_row = lf_row
    shift = 1
    while shift < dec_seq:
        b_row = b_row + jnp.where(pos >= shift, pltpu.roll(b_row, shift, 1), 0.0)
        shift *= 2

    key = _iota2((n, n), 0)
    query = _iota2((n, n), 1)
    key_ok = (key <= query) & (key // dec_seq == query // dec_seq)
    a_row = li_row - b_row
    a_keys = jnp.broadcast_to(jnp.sum(jnp.where(key == query, a_row, 0.0), axis=1, keepdims=True), (n, n))
    a_t = jnp.where(key_ok, a_keys, NEG_INF)
    m_old = m_ref[0, 0, 0:1, :]
    mm = jnp.maximum(m_old, jnp.max(a_t, axis=0, keepdims=True))
    w_t = jnp.exp(a_t - mm)
    mm_last = _segment_last(mm, pos, dec_seq)
    m_new = _segment_last(b_row, pos, dec_seq) + mm_last
    w_state = jnp.exp(a_row - mm_last)
    decay = jnp.exp(m_old - mm_last)
    mn_ref[0, 0] = jnp.broadcast_to(m_new, (8, n))

    qkw_t = _dot_nt(kb, qb) * w_t
    den = jnp.sum(qkw_t, axis=0, keepdims=True)
    num_t = _dot(v.T.astype(BF16), qkw_t.astype(BF16))

    n_seq = n_ref[0]
    seq_of_lane = _iota2((nb, n), 1) // dec_seq == _iota2((nb, n), 0)
    decay_seq = jnp.sum(jnp.where(_iota2((nb, n), 1) == _iota2((nb, n), 0) * dec_seq, decay, 0.0),
                        axis=1, keepdims=True)
    nn_ref[0] = decay_seq * n_seq + _dot(jnp.where(seq_of_lane, w_state, 0.0).astype(BF16), kb)
    expand = jnp.where(_iota2((n, nb), 0) // dec_seq == _iota2((n, nb), 1), 1.0, 0.0).astype(BF16)
    n_rows = _dot(expand, n_seq.astype(BF16))
    qn_col = jnp.sum(q * n_rows, axis=1, keepdims=True)
    qn = jnp.sum(jnp.where(key == query, qn_col, 0.0), axis=0, keepdims=True)

    kwt = k.T * w_state
    decay_rows = jnp.broadcast_to(jnp.sum(jnp.where(key == query, decay, 0.0), axis=1, keepdims=True), (n, n))
    lane_seq = query // dec_seq
    low_rows = _iota2((8, M_DH), 0) < dec_seq
    assert 8 % dec_seq == 0 and 8 // dec_seq == 2
    for pair in range(nb // 2):
        q8 = q[8 * pair:8 * pair + 8].astype(BF16)
        inter_ref[8 * pair:8 * pair + 8, :] = jnp.where(low_rows, _dot(q8, c_ref[2 * pair, 0].astype(BF16)),
                                                        _dot(q8, c_ref[2 * pair + 1, 0].astype(BF16)))
    for s0 in range(0, nb, SEQ_BATCH):
        seqs = range(s0, s0 + SEQ_BATCH)
        updates = [_dot(jnp.where(lane_seq == s, kwt, 0.0).astype(BF16), vb) for s in seqs]
        for s, upd in zip(seqs, updates):
            cn_ref[s, 0] = decay_rows[s * dec_seq:s * dec_seq + 1, :] * c_ref[s, 0] + upd

    s_inter = jnp.exp(m_old - mm)
    num_t = num_t + inter_ref[...].T * s_inter
    den = den + qn * s_inter
    h_ref[...] = (num_t * (1.0 / jnp.maximum(jnp.abs(den), jnp.exp(-(b_row + mm))))).T


def _sample_swa_kernel(qz_ref, kn_ref, vn_ref, ck_ref, cv_ref, sink_ref, o_ref, nk_ref, nv_ref, *, dec_seq):
    n_rows = qz_ref.shape[1]
    t_c = _iota2((n_rows, N_BUF), 0) % dec_seq
    i_c = _iota2((n_rows, N_BUF), 1)
    ok_c = (i_c < N_META) | (i_c > t_c + N_META)
    t_n = _iota2((n_rows, 8), 0) % dec_seq
    i_n = _iota2((n_rows, 8), 1)
    ok_n = (i_n <= t_n) & (i_n < dec_seq)
    sink = sink_ref[:, 0:1]

    seqs = range(qz_ref.shape[0])
    scores = []
    for s in seqs:
        qz = (qz_ref[s] * (A_DH ** -0.5)).astype(BF16)
        scores.append((_dot_nt(qz, ck_ref[s].astype(BF16)), _dot_nt(qz, kn_ref[s].astype(BF16))))
    probs = []
    for s_c, s_n in scores:
        s_c = jnp.where(ok_c, s_c, NEG_INF)
        s_n = jnp.where(ok_n, s_n, NEG_INF)
        mx = jnp.maximum(jnp.maximum(jnp.max(s_c, axis=1, keepdims=True),
                                     jnp.max(s_n, axis=1, keepdims=True)), sink)
        p_c = jnp.exp(s_c - mx)
        p_n = jnp.exp(s_n - mx)
        den = jnp.sum(p_c, axis=1, keepdims=True) + jnp.sum(p_n, axis=1, keepdims=True) + jnp.exp(sink - mx)
        probs.append((p_c.astype(BF16), p_n.astype(BF16), den))
    outs = [_dot(p_c, cv_ref[s].astype(BF16)) + _dot(p_n, vn_ref[s].astype(BF16))
            for s, (p_c, p_n, _) in zip(seqs, probs)]
    for s, o, (_, _, den) in zip(seqs, outs, probs):
        o_ref[s] = o / den
    for s in seqs:
        for cache_ref, new_ref, out_ref in ((ck_ref, kn_ref, nk_ref), (cv_ref, vn_ref, nv_ref)):
            out_ref[s, 0:N_META, :] = cache_ref[s, 0:N_META, :]
            out_ref[s, N_META:N_BUF - dec_seq, :] = cache_ref[s, N_META + dec_seq:N_BUF, :]
            out_ref[s, N_BUF - dec_seq:N_BUF, :] = new_ref[s, 0:dec_seq, :]


def _sample_out_kernel(hs_ref, u_ref, hm_ref, oa_ref, ng_ref, wo_ref, lng_ref, lnb_ref, y_ref):
    mix = _gate_mix(hm_ref[...], u_ref[:, C_O:C_O + D_MLSTM], u_ref[:, C_Z:C_Z + D_MLSTM],
                    oa_ref[...], u_ref[:, C_AZ:C_AZ + D_SWA], ng_ref[...])
    y_ref[...] = _out_and_norm(hs_ref[...], mix, wo_ref, lng_ref[...], lnb_ref[...])


def _sample_path(x_sample, cache_k, cache_v, state_c, state_n, state_m,
                 ln0_g, ln0_b, w_t, b_all, a_sinks, norm_g, w_o, ln_g, ln_b):
    db, dec_seq, _ = x_sample.shape
    rows = db * dec_seq
    params = pltpu.CompilerParams(vmem_limit_bytes=VMEM_LIMIT_BYTES)
    hs, u = pl.pallas_call(
        _sample_proj_kernel,
        out_shape=(jax.ShapeDtypeStruct((rows, D_MODEL), F32), jax.ShapeDtypeStruct((rows, N_PAD), F32)),
        compiler_params=params,
        name="sample_proj",
    )(x_sample.reshape(rows, D_MODEL), w_t, b_all, ln0_g, ln0_b)

    n_groups = db // SEQ_PER_GROUP
    n_t = jnp.transpose(state_n, (1, 0, 2))
    m_t = jnp.repeat(jnp.transpose(state_m, (1, 0)), dec_seq, axis=1).reshape(M_HEADS, n_groups, 1, CHUNK)
    m_t = jnp.broadcast_to(m_t, (M_HEADS, n_groups, 8, CHUNK))

    def col_spec(col0):
        return pl.BlockSpec((CHUNK, M_DH), lambda g, h: (g, col0 // M_DH + h))

    state_spec = pl.BlockSpec((SEQ_PER_GROUP, 1, M_DH, M_DH), lambda g, h: (g, h, 0, 0))
    vec_spec = pl.BlockSpec((1, SEQ_PER_GROUP, M_DH), lambda g, h: (h, g, 0))
    m_spec = pl.BlockSpec((1, 1, 8, CHUNK), lambda g, h: (h, g, 0, 0))
    h_m, c_new, n_new, m_new = pl.pallas_call(
        functools.partial(_sample_mlstm_kernel, dec_seq=dec_seq),
        grid=(n_groups, M_HEADS),
        in_specs=[col_spec(C_Q), col_spec(C_K), col_spec(C_V),
                  pl.BlockSpec((CHUNK, 128), lambda g, h: (g, C_G // 128)),
                  state_spec, vec_spec, m_spec],
        out_specs=[pl.BlockSpec((CHUNK, M_DH), lambda g, h: (g, h)), state_spec, vec_spec, m_spec],
        out_shape=(jax.ShapeDtypeStruct((rows, D_MLSTM), F32),
                   jax.ShapeDtypeStruct(state_c.shape, F32),
                   jax.ShapeDtypeStruct((M_HEADS, db, M_DH), F32),
                   jax.ShapeDtypeStruct((M_HEADS, n_groups, 8, CHUNK), F32)),
        scratch_shapes=[pltpu.VMEM((CHUNK, M_DH), F32)],
        compiler_params=pltpu.CompilerParams(dimension_semantics=("arbitrary", "arbitrary"),
                                             vmem_limit_bytes=VMEM_LIMIT_BYTES),
        name="sample_mlstm",
    )(u, u, u, u, state_c, n_t, m_t)

    aq = u[:, C_AQ:C_AQ + D_SWA].reshape(db, dec_seq, A_KV_HEADS, A_GROUP, A_DH)
    aq = jnp.transpose(aq, (0, 2, 3, 1, 4)).reshape(db, A_KV_HEADS, A_GROUP * dec_seq, A_DH)
    zeros = jnp.zeros_like(aq[:, 0])
    qz = jnp.stack([jnp.concatenate([aq[:, 0], zeros], axis=-1),
                    jnp.concatenate([zeros, aq[:, 1]], axis=-1)], axis=1)
    n_qrows = A_HEADS * dec_seq
    qz = qz.reshape(db, n_qrows, KV_W)
    k_new = jnp.pad(u[:, C_AK:C_AK + KV_W].reshape(db, dec_seq, KV_W), ((0, 0), (0, 8 - dec_seq), (0, 0)))
    v_new = jnp.pad(u[:, C_AV:C_AV + KV_W].reshape(db, dec_seq, KV_W), ((0, 0), (0, 8 - dec_seq), (0, 0)))
    sink_rows = jnp.broadcast_to(jnp.repeat(a_sinks.astype(F32), dec_seq)[:, None], (n_qrows, 128))
    ck = cache_k.reshape(db, N_BUF, KV_W)
    cv = cache_v.reshape(db, N_BUF, KV_W)
    sb = SWA_SEQ_PER_STEP

    def seq_spec(r):
        return pl.BlockSpec((sb, r, KV_W), lambda i: (i, 0, 0))

    o, nk, nv = pl.pallas_call(
        functools.partial(_sample_swa_kernel, dec_seq=dec_seq),
        grid=(db // sb,),
        in_specs=[seq_spec(n_qrows), seq_spec(8), seq_spec(8), seq_spec(N_BUF), seq_spec(N_BUF),
                  pl.BlockSpec((n_qrows, 128), lambda i: (0, 0))],
        out_specs=[seq_spec(n_qrows), seq_spec(N_BUF), seq_spec(N_BUF)],
        out_shape=(jax.ShapeDtypeStruct((db, n_qrows, KV_W), F32),
                   jax.ShapeDtypeStruct((db, N_BUF, KV_W), F32),
                   jax.ShapeDtypeStruct((db, N_BUF, KV_W), F32)),
        compiler_params=pltpu.CompilerParams(dimension_semantics=("arbitrary",),
                                             vmem_limit_bytes=VMEM_LIMIT_BYTES),
        name="sample_swa",
    )(qz, k_new, v_new, ck, cv, sink_rows)
    o = o.reshape(db, A_KV_HEADS, A_GROUP, dec_seq, A_KV_HEADS, A_DH)
    o = jnp.stack([o[:, 0, :, :, 0, :], o[:, 1, :, :, 1, :]], axis=1)
    o_a = jnp.transpose(o, (0, 3, 1, 2, 4)).reshape(rows, D_SWA)

    y = pl.pallas_call(
        _sample_out_kernel,
        out_shape=jax.ShapeDtypeStruct((rows, D_MODEL), F32),
        compiler_params=params,
        name="sample_out",
    )(hs, u, h_m, o_a, norm_g, w_o, ln_g, ln_b)

    shape5 = (1, db, N_BUF, A_KV_HEADS, A_DH)
    return (y.reshape(db, dec_seq, D_MODEL), nk.reshape(shape5), nv.reshape(shape5), c_new[None],
            jnp.transpose(n_new, (1, 0, 2))[None], jnp.transpose(m_new[:, :, 0, ::dec_seq].reshape(M_HEADS, db), (1, 0))[None])


def kernel(x_prompt, x_sample, cache_swa_k, cache_swa_v, state_mlstm_c, state_mlstm_n, state_mlstm_m,
           meta_tokens, ln0_g, ln0_b, w_in, b_in, a_sinks, m_norm_g, w_out, ln_g, ln_b):
    assert w_in.shape[0] == DEPTH and x_prompt.shape[-1] == D_MODEL
    w_raw = jnp.transpose(w_in[0].astype(F32))
    b_raw = b_in[0].astype(F32)[:, None]

    def rearranged(p):
        def take(raw, scale=1.0):
            return p[raw[0]:raw[0] + raw[1]] * scale
        zeros = lambda n: jnp.zeros((n, p.shape[1]), F32)
        return jnp.concatenate([take(RAW_Q), take(RAW_AQ), take(RAW_AV), take(RAW_V), take(RAW_O, 0.5),
                                take(RAW_Z, 0.5), take(RAW_AZ, 0.5), take(RAW_G), zeros(C_K - C_G - RAW_G[1]),
                                take(RAW_K), take(RAW_AK)], axis=0)

    w_t = rearranged(w_raw).astype(BF16)
    b_all = rearranged(b_raw).reshape(1, N_PAD)
    w_o = w_out[0].astype(BF16)
    g0 = ln0_g.astype(F32)[None]
    b0 = ln0_b.astype(F32)[None]
    lg = ln_g[0].astype(F32)[None]
    lb = ln_b[0].astype(F32)[None]
    norm_g = m_norm_g[0].astype(F32)[None]
    sinks = a_sinks[0].astype(F32)

    y_p, pk, pv, pc, pn, pm = _prompt_path(x_prompt, meta_tokens, g0, b0, w_t, b_all, sinks, norm_g, w_o, lg, lb)
    y_s, sk, sv, sc, sn, sm = _sample_path(x_sample, cache_swa_k[0], cache_swa_v[0], state_mlstm_c[0],
                                           state_mlstm_n[0], state_mlstm_m[0],
                                           g0, b0, w_t, b_all, sinks, norm_g, w_o, lg, lb)
    return (y_p, y_s, pk, pv, pc, pn, pm, sk, sv, sc, sn, sm)
```

```python
import functools
import itertools

import jax
import jax.numpy as jnp
from jax import lax
from jax.experimental import pallas as pl
from jax.experimental.pallas import tpu as pltpu

F32 = jnp.float32
BF16 = jnp.bfloat16

D_MODEL = 1024
N_META = 16
M_HEADS = 4
M_DH = 128
D_MLSTM = M_HEADS * M_DH
A_HEADS = 8
A_KV_HEADS = 2
A_GROUP = A_HEADS // A_KV_HEADS
A_DH = 64
D_SWA = A_HEADS * A_DH
KV_W = A_KV_HEADS * A_DH
WINDOW = 128
CHUNK = 128
LN_EPS = 1e-5
DEPTH = 1
DN_ALPHA = (2.0 * DEPTH) ** 0.25
N_BUF = N_META + WINDOW

C_Q, C_AQ = 0, 512
N_T_BF16 = 1024
C_AV, C_V, C_O, C_Z, C_AZ, C_G = 1024, 1152, 1664, 2176, 2688, 3200
GATE_ROWS = 16
N_T = C_G + GATE_ROWS
C_K, C_AK = 3328, 3840
N_PAD = 3968
N_KEYS = N_PAD - C_K
RAW_Q, RAW_K, RAW_V, RAW_O, RAW_Z, RAW_G = (0, 512), (512, 512), (1024, 512), (1536, 512), (2048, 512), (2560, 8)
RAW_AQ, RAW_AK, RAW_AV, RAW_AZ = (2568, 512), (3080, 128), (3208, 128), (3336, 512)

PROJ_STEP = 512
CHUNK_UNROLL = 4
VMEM_LIMIT_BYTES = 62 * 1024 * 1024
NEG_INF = float("-inf")


def _dot(a, b):
    return jnp.dot(a, b, preferred_element_type=F32)


def _dot_nt(a, b):
    return lax.dot_general(a, b, (((1,), (1,)), ((), ())), preferred_element_type=F32)


def _layer_norm(x, g, b):
    mu = jnp.mean(x, axis=-1, keepdims=True)
    xc = x - mu
    var = jnp.mean(xc * xc, axis=-1, keepdims=True)
    return xc * lax.rsqrt(var + LN_EPS) * g + b


def _log_sigmoid(x):
    return jnp.minimum(x, 0.0) - jnp.log1p(jnp.exp(-jnp.abs(x)))


def _times_sigmoid(h, half_x):
    return 0.5 * (h * jnp.tanh(half_x) + h)


def _silu_of_twice(half_x):
    return half_x * jnp.tanh(half_x) + half_x


def _iota2(shape, dim):
    return lax.broadcasted_iota(jnp.int32, shape, dim)


def _project(hb, wt_ref, b_ref, u_ref):
    for n0 in range(0, N_PAD, PROJ_STEP):
        n1 = min(n0 + PROJ_STEP, N_PAD)
        u_ref[:, n0:n1] = _dot_nt(hb, wt_ref[n0:n1, :]) + b_ref[:, n0:n1]


def _project_both_steps(hb, wt_ref, bcol_ref, brow_ref, utb_ref, utf_ref, ukey_ref):
    reps = hb.shape[0] // 128
    starts = list(range(0, N_T, PROJ_STEP))
    for n0 in starts[-1:] + starts[:-1]:
        n1 = min(n0 + PROJ_STEP, N_T)
        res = _dot_nt(wt_ref[n0:n1, :], hb) + jnp.concatenate([bcol_ref[n0:n1, :]] * reps, axis=1)
        if n1 <= N_T_BF16:
            utb_ref[n0:n1, :] = res.astype(BF16)
        else:
            utf_ref[n0 - N_T_BF16:n1 - N_T_BF16, :] = res
        yield
    for n0 in range(C_K, N_PAD, PROJ_STEP):
        n1 = min(n0 + PROJ_STEP, N_PAD)
        ukey_ref[:, n0 - C_K:n1 - C_K] = _dot_nt(hb, wt_ref[n0:n1, :]) + brow_ref[:, n0:n1]
        yield


def _interleave(*step_generators):
    for _ in itertools.zip_longest(*step_generators):
        pass


def _gate_rows(gates):
    return _gate_scan(jnp.concatenate([gates[r0:r0 + CHUNK].T[0:8, :] for r0 in range(0, gates.shape[0], CHUNK)],
                                      axis=0))


def _gate_rows_t(gates_t):
    return _gate_scan(jnp.concatenate([gates_t[:, c0:c0 + CHUNK] for c0 in range(0, gates_t.shape[1], CHUNK)],
                                      axis=0))


def _gate_scan(g_t):
    x = _log_sigmoid(g_t)
    lane = _iota2(x.shape, 1)
    shift = 1
    while shift < x.shape[1]:
        x = x + jnp.where(lane >= shift, pltpu.roll(x, shift, 1), 0.0)
        shift *= 2
    return g_t, x


CT_ROWS = M_DH + 8


def _mlstm_scores_and_state(q_tb, kb, v_t, li_row, b_row, m_old, ct_aug, key_ok, n_valid):
    n = kb.shape[0]
    row = _iota2((n, n), 0)
    col = _iota2((n, n), 1)
    a_row = li_row - b_row
    a_keys = jnp.broadcast_to(jnp.sum(jnp.where(row == col, a_row, 0.0), axis=1, keepdims=True), (n, n))
    a_t = jnp.where(key_ok, a_keys, NEG_INF)
    mm = jnp.maximum(m_old, jnp.max(a_t, axis=0, keepdims=True))
    w_t = jnp.exp(a_t - mm)
    lane = _iota2((1, n), 1)
    last = n_valid - 1
    mm_last = jnp.max(jnp.where(lane == last, mm, NEG_INF), axis=1, keepdims=True)
    m_new = jnp.sum(jnp.where(lane == last, b_row, 0.0), axis=1, keepdims=True) + mm_last
    w_state = jnp.exp(a_row - mm_last)
    if n_valid < n:
        w_state = jnp.where(lane < n_valid, w_state, 0.0)
    decay = jnp.exp(m_old - mm_last)
    scores_t = _dot(kb, q_tb)
    inter = _dot(ct_aug.astype(BF16), q_tb)
    ones_row = jnp.where(_iota2((CT_ROWS - M_DH, n), 0) == 0, w_state, 0.0)
    vtw = jnp.concatenate([v_t * w_state, ones_row], axis=0).astype(BF16)
    ct_aug_new = decay * ct_aug + _dot(vtw, kb)
    s_inter = jnp.exp(m_old - mm)
    floor = jnp.exp(-(b_row + mm))
    return (scores_t, w_t, v_t.astype(BF16), inter, s_inter, floor), ct_aug_new, m_new


def _mlstm_weighted_values(scores_t, w_t, vtb, inter, s_inter, floor):
    qkw_t = scores_t * w_t
    den = jnp.sum(qkw_t, axis=0, keepdims=True)
    num_t = _dot(vtb, qkw_t.astype(BF16))
    return num_t, den, inter, s_inter, floor


def _mlstm_finish(num_t, den, inter, s_inter, floor):
    num_t = num_t + inter[0:M_DH] * s_inter
    den = den + inter[M_DH:M_DH + 1] * s_inter
    return num_t * (1.0 / jnp.maximum(jnp.abs(den), floor))


def _mlstm_gate_head(h, half_o, half_z, norm_g, axis):
    hh = _times_sigmoid(h, half_o)
    mu = jnp.mean(hh, axis=axis, keepdims=True)
    hc = hh - mu
    var = jnp.mean(hc * hc, axis=axis, keepdims=True)
    return hc * lax.rsqrt(var + LN_EPS) * norm_g * _silu_of_twice(half_z)


def _gate_mix(h_m, half_o, half_z, o_a, half_az, norm_g):
    parts = []
    for hd in range(M_HEADS):
        sl = slice(hd * M_DH, (hd + 1) * M_DH)
        parts.append(_mlstm_gate_head(h_m[:, sl], half_o[:, sl], half_z[:, sl], norm_g[:, sl], axis=-1).astype(BF16))
    parts.append((o_a * _silu_of_twice(half_az)).astype(BF16))
    return jnp.concatenate(parts, axis=-1)


def _out_and_norm(hp, mix, wo_ref, g, b):
    z = DN_ALPHA * hp + _dot(mix, wo_ref[...])
    return _layer_norm(z, g, b)


def _meta_kernel(meta_ref, wt_ref, bias_ref, g0_ref, b0_ref,
                 ct0_ref, m0_ref, km_ref, vm_ref, vmt_ref, u_ref):
    hp = _layer_norm(meta_ref[...], g0_ref[...], b0_ref[...])
    _project(hp.astype(BF16), wt_ref, bias_ref, u_ref)
    row = _iota2((CHUNK, CHUNK), 0)
    col = _iota2((CHUNK, CHUNK), 1)
    key_ok = (row <= col) & (row < N_META)
    li_rows, b_rows = _gate_rows(u_ref[:, C_G:C_G + 128])
    zero_m = jnp.zeros((1, 128), F32)
    zero_ct = jnp.zeros((CT_ROWS, M_DH), F32)
    m0_ref[...] = jnp.zeros(m0_ref.shape, F32)
    for hd in range(M_HEADS):
        q = u_ref[:, C_Q + hd * M_DH:C_Q + (hd + 1) * M_DH]
        k = u_ref[:, C_K + hd * M_DH:C_K + (hd + 1) * M_DH] * (M_DH ** -0.5)
        v = u_ref[:, C_V + hd * M_DH:C_V + (hd + 1) * M_DH]
        _, ct_new, m_new = _mlstm_scores_and_state(
            q.T.astype(BF16), k.astype(BF16), v.T, li_rows[hd:hd + 1], b_rows[M_HEADS + hd:M_HEADS + hd + 1],
            zero_m, zero_ct, key_ok, N_META)
        ct0_ref[hd] = ct_new
        m0_ref[hd:hd + 1, :] = jnp.broadcast_to(m_new, (1, 128))
    km_ref[...] = u_ref[0:N_META, C_AK:C_AK + KV_W]
    vm_ref[...] = u_ref[0:N_META, C_AV:C_AV + KV_W]
    vmt_ref[...] = u_ref[:, C_AV:C_AV + KV_W].T[:, 0:N_META]


def _keep_kv_half(x, kv):
    low = _iota2(x.shape, 1) < A_DH
    return jnp.where(low if kv == 0 else ~low, x, 0.0)


def _swa_probs_t(s_own, s_prev, s_meta, sink, own_ok, prev_ok):
    s_own = jnp.where(own_ok, s_own, NEG_INF)
    s_prev = jnp.where(prev_ok, s_prev, NEG_INF)
    mx = jnp.maximum(jnp.maximum(jnp.max(s_own, axis=0, keepdims=True),
                                 jnp.max(s_prev, axis=0, keepdims=True)),
                     jnp.maximum(jnp.max(s_meta, axis=0, keepdims=True), sink))
    p_own, p_prev, p_meta = (jnp.exp(s - mx) for s in (s_own, s_prev, s_meta))
    den = (jnp.sum(p_own, axis=0, keepdims=True) + jnp.sum(p_prev, axis=0, keepdims=True)
           + jnp.sum(p_meta, axis=0, keepdims=True) + jnp.exp(sink - mx))
    return p_own.astype(BF16), p_prev.astype(BF16), p_meta.astype(BF16), den


N_BLOCK_BUFFERS = 6


def _prompt_kernel(sink_ref, x0_ref, x1_ref, x2_ref, wt_ref, bcol_ref, brow_ref, g0_ref, b0_ref,
                   ct0_ref, m0_ref, km_ref, vm_ref, vmt_ref, ng_ref, wo_ref, lng_ref, lnb_ref,
                   y_ref, pk_ref, pv_ref, pc_ref, pn_ref, pm_ref, *scratch, tb):
    buf_a = scratch[0:N_BLOCK_BUFFERS]
    buf_b = scratch[N_BLOCK_BUFFERS:2 * N_BLOCK_BUFFERS]
    mix_ref, ct_ref, kprev_ref, vtprev_ref = scratch[2 * N_BLOCK_BUFFERS:]
    j = pl.program_id(1)
    n_chunks = tb // CHUNK

    def feat(c0, n=M_DH):
        return slice(c0 - N_T_BF16, c0 - N_T_BF16 + n)

    def project_block(x_ref, buf):
        utb_ref, utf_ref, ukey_ref, hp_ref, li_ref, cumf_ref = buf
        hp = _layer_norm(x_ref[0], g0_ref[...], b0_ref[...])
        hp_ref[...] = hp
        yield
        for slab, _ in enumerate(_project_both_steps(hp.astype(BF16), wt_ref, bcol_ref, brow_ref,
                                                     utb_ref, utf_ref, ukey_ref)):
            if slab == 0:
                li_ref[...], cumf_ref[...] = _gate_rows_t(utf_ref[feat(C_G, 8), :])
            yield

    @pl.when(j == 0)
    def _():
        ct_ref[...] = ct0_ref[...]
        pm_ref[0] = m0_ref[...]
        kprev_ref[...] = jnp.zeros(kprev_ref.shape, F32)
        vtprev_ref[...] = jnp.zeros(vtprev_ref.shape, F32)
        pk_ref[0, 0:N_META, :] = km_ref[...]
        pv_ref[0, 0:N_META, :] = vm_ref[...]
        _interleave(project_block(x0_ref, buf_a))

    def chunk(buf, mix_ref, ci, first_chunk):
        utb_ref, utf_ref, ukey_ref, _, li_ref, cumf_ref = buf
        rows = slice(ci * CHUNK, (ci + 1) * CHUNK)
        key = _iota2((CHUNK, CHUNK), 0)
        query = _iota2((CHUNK, CHUNK), 1)
        causal = key <= query

        li_rows = li_ref[ci * 8:(ci + 1) * 8, :]
        b_rows = cumf_ref[ci * 8:(ci + 1) * 8, :]
        m_carry = []
        for hd in range(M_HEADS):
            q_tb = utb_ref[C_Q + hd * M_DH:C_Q + (hd + 1) * M_DH, rows]
            kb = (ukey_ref[rows, hd * M_DH:(hd + 1) * M_DH] * (M_DH ** -0.5)).astype(BF16)
            v_t = utf_ref[feat(C_V + hd * M_DH), rows]
            carry_hd, ct_new, m_new = _mlstm_scores_and_state(
                q_tb, kb, v_t, li_rows[hd:hd + 1], b_rows[M_HEADS + hd:M_HEADS + hd + 1],
                pm_ref[0, hd:hd + 1, :], ct_ref[hd], causal, CHUNK)
            ct_ref[hd] = ct_new
            pm_ref[0, hd:hd + 1, :] = jnp.broadcast_to(m_new, (1, 128))
            m_carry.append(carry_hd)

        k_own = ukey_ref[rows, C_AK - C_K:C_AK - C_K + KV_W]
        vt_own = utf_ref[feat(C_AV, KV_W), rows]
        k_prev = kprev_ref[...]
        vt_prev = vtprev_ref[...]
        prev_ok = (key > query) & ((first_chunk + ci) > 0)
        keys_kv = [tuple((_keep_kv_half(x, kv) * (A_DH ** -0.5)).astype(BF16) for x in (k_own, k_prev, km_ref[...]))
                   for kv in range(A_KV_HEADS)]
        a_scores = []
        for kv in range(A_KV_HEADS):
            q_cat = jnp.concatenate([utb_ref[C_AQ + (hd - kv) * A_DH:C_AQ + (hd - kv + 2) * A_DH, rows]
                                     for hd in range(kv * A_GROUP, (kv + 1) * A_GROUP)], axis=1)
            a_scores.append(tuple(_dot(kk, q_cat) for kk in keys_kv[kv]))
        kprev_ref[...] = k_own
        vtprev_ref[...] = vt_own
        yield

        m_carry = [_mlstm_weighted_values(*c) for c in m_carry]
        a_out = []
        for kv in range(A_KV_HEADS):
            probs = [_swa_probs_t(*(s[:, g * CHUNK:(g + 1) * CHUNK] for s in a_scores[kv]),
                                  sink_ref[kv * A_GROUP + g], causal, prev_ok) for g in range(A_GROUP)]
            vsl = slice(kv * A_DH, (kv + 1) * A_DH)
            p_win = jnp.concatenate([jnp.concatenate([p[1] for p in probs], axis=1),
                                     jnp.concatenate([p[0] for p in probs], axis=1)], axis=0)
            vt_win = jnp.concatenate([vt_prev[vsl], vt_own[vsl]], axis=1).astype(BF16)
            o_cat = (_dot(vt_win, p_win)
                     + _dot(vmt_ref[vsl, :].astype(BF16), jnp.concatenate([p[2] for p in probs], axis=1)))
            a_out += [(o_cat[:, g * CHUNK:(g + 1) * CHUNK], probs[g][3]) for g in range(A_GROUP)]
        yield

        mix_t = []
        for hd in range(M_HEADS):
            mix_t.append(_mlstm_gate_head(_mlstm_finish(*m_carry[hd]), utf_ref[feat(C_O + hd * M_DH), rows],
                                          utf_ref[feat(C_Z + hd * M_DH), rows],
                                          ng_ref[hd * M_DH:(hd + 1) * M_DH, :], axis=0))
        for tile in range(A_HEADS // 2):
            o_t = jnp.concatenate([o * (1.0 / den) for o, den in a_out[2 * tile:2 * tile + 2]], axis=0)
            mix_t.append(o_t * _silu_of_twice(utf_ref[feat(C_AZ + tile * 128), rows]))
        for i, x_t in enumerate(mix_t):
            mix_ref[rows, i * 128:(i + 1) * 128] = x_t.T.astype(BF16)
        yield

    def finish_block(buf, mix_ref, first_chunk, y_rows):
        for ci in range(n_chunks):
            yield from chunk(buf, mix_ref, ci, first_chunk)
        y_ref[0, y_rows, :] = _out_and_norm(buf[3][...], mix_ref[...], wo_ref, lng_ref[...], lnb_ref[...])
        yield

    _interleave(project_block(x1_ref, buf_b), finish_block(buf_a, mix_ref, 2 * j * n_chunks, slice(0, tb)))
    _interleave(project_block(x2_ref, buf_a), finish_block(buf_b, mix_ref, (2 * j + 1) * n_chunks, slice(tb, 2 * tb)))

    @pl.when(j == pl.num_programs(1) - 1)
    def _():
        ukey_ref, utf_ref = buf_b[2], buf_b[1]
        pk_ref[0, N_META:N_BUF, :] = ukey_ref[tb - WINDOW:tb, C_AK - C_K:C_AK - C_K + KV_W]
        pv_ref[0, N_META:N_BUF, :] = utf_ref[feat(C_AV, KV_W), tb - WINDOW:tb].T
        for hd in range(M_HEADS):
            pc_ref[0, hd] = ct_ref[hd, 0:M_DH, :].T
            pn_ref[0, hd:hd + 1, :] = ct_ref[hd, M_DH:M_DH + 1, :]
        pn_ref[0, M_HEADS:8, :] = jnp.zeros((8 - M_HEADS, M_DH), F32)


def _const_spec(shape):
    return pl.BlockSpec(shape, lambda *_: (0,) * len(shape))


def _prompt_path(x_prompt, meta_tokens, ln0_g, ln0_b, w_t, b_all, sinks, norm_g, w_o, ln_g, ln_b, tb=512):
    batch, seq, _ = x_prompt.shape
    meta_pad = jnp.pad(meta_tokens.astype(F32), ((0, CHUNK - N_META), (0, 0)))
    ct0, m0, km, vm, vmt = pl.pallas_call(
        _meta_kernel,
        out_shape=(jax.ShapeDtypeStruct((M_HEADS, CT_ROWS, M_DH), F32),
                   jax.ShapeDtypeStruct((8, 128), F32),
                   jax.ShapeDtypeStruct((N_META, KV_W), F32),
                   jax.ShapeDtypeStruct((N_META, KV_W), F32),
                   jax.ShapeDtypeStruct((KV_W, N_META), F32)),
        scratch_shapes=[pltpu.VMEM((CHUNK, N_PAD), F32)],
        compiler_params=pltpu.CompilerParams(vmem_limit_bytes=VMEM_LIMIT_BYTES),
        name="meta_tokens",
    )(meta_pad, w_t, b_all, ln0_g, ln0_b)

    nj = seq // tb
    assert nj % 2 == 0
    resident = pl.Buffered(1)

    def big_const_spec(shape):
        return pl.BlockSpec(shape, lambda *_: (0,) * len(shape), pipeline_mode=resident)

    in_specs = [
        pl.BlockSpec(memory_space=pltpu.SMEM),
        pl.BlockSpec((1, tb, D_MODEL), lambda b, j: (b, 0, 0), pipeline_mode=resident),
        pl.BlockSpec((1, tb, D_MODEL), lambda b, j: (b, 2 * j + 1, 0)),
        pl.BlockSpec((1, tb, D_MODEL), lambda b, j: (b, jnp.minimum(2 * j + 2, nj - 1), 0)),
        big_const_spec((N_PAD, D_MODEL)),
        big_const_spec((N_PAD, 128)), _const_spec((1, N_PAD)),
        _const_spec((1, D_MODEL)), _const_spec((1, D_MODEL)),
        _const_spec((M_HEADS, CT_ROWS, M_DH)), _const_spec((8, 128)),
        _const_spec((N_META, KV_W)), _const_spec((N_META, KV_W)), _const_spec((KV_W, N_META)),
        _const_spec((D_MLSTM, 128)),
        big_const_spec((D_MODEL, D_MODEL)),
        _const_spec((1, D_MODEL)), _const_spec((1, D_MODEL)),
    ]
    out_specs = [
        pl.BlockSpec((1, 2 * tb, D_MODEL), lambda b, j: (b, j, 0)),
        pl.BlockSpec((1, N_BUF, KV_W), lambda b, j: (b, 0, 0)),
        pl.BlockSpec((1, N_BUF, KV_W), lambda b, j: (b, 0, 0)),
        pl.BlockSpec((1, M_HEADS, M_DH, M_DH), lambda b, j: (b, 0, 0, 0)),
        pl.BlockSpec((1, 8, M_DH), lambda b, j: (b, 0, 0)),
        pl.BlockSpec((1, 8, 128), lambda b, j: (b, 0, 0)),
    ]
    out_shape = (
        jax.ShapeDtypeStruct((batch, seq, D_MODEL), F32),
        jax.ShapeDtypeStruct((batch, N_BUF, KV_W), F32),
        jax.ShapeDtypeStruct((batch, N_BUF, KV_W), F32),
        jax.ShapeDtypeStruct((batch, M_HEADS, M_DH, M_DH), F32),
        jax.ShapeDtypeStruct((batch, 8, M_DH), F32),
        jax.ShapeDtypeStruct((batch, 8, 128), F32),
    )
    block_buffers = [pltpu.VMEM((N_T_BF16, tb), BF16), pltpu.VMEM((N_T - N_T_BF16, tb), F32),
                     pltpu.VMEM((tb, N_KEYS), F32), pltpu.VMEM((tb, D_MODEL), F32),
                     pltpu.VMEM((tb // CHUNK * 8, CHUNK), F32), pltpu.VMEM((tb // CHUNK * 8, CHUNK), F32)]
    assert len(block_buffers) == N_BLOCK_BUFFERS
    y, pk, pv, pc, pn, pm = pl.pallas_call(
        functools.partial(_prompt_kernel, tb=tb),
        grid=(batch, nj // 2),
        in_specs=in_specs,
        out_specs=out_specs,
        out_shape=out_shape,
        scratch_shapes=2 * block_buffers + [
            pltpu.VMEM((tb, D_MODEL), BF16),
            pltpu.VMEM((M_HEADS, CT_ROWS, M_DH), F32),
            pltpu.VMEM((CHUNK, KV_W), F32), pltpu.VMEM((KV_W, CHUNK), F32)],
        compiler_params=pltpu.CompilerParams(dimension_semantics=("arbitrary", "arbitrary"),
                                             vmem_limit_bytes=VMEM_LIMIT_BYTES),
        name="prompt_layer",
    )(sinks, x_prompt, x_prompt, x_prompt, w_t, jnp.broadcast_to(b_all.reshape(N_PAD, 1), (N_PAD, 128)), b_all, ln0_g, ln0_b,
      ct0, m0, km, vm, vmt, jnp.broadcast_to(norm_g.reshape(D_MLSTM, 1), (D_MLSTM, 128)), w_o, ln_g, ln_b)
    pk = pk.reshape(1, batch, N_BUF, A_KV_HEADS, A_DH)
    pv = pv.reshape(1, batch, N_BUF, A_KV_HEADS, A_DH)
    return y, pk, pv, pc[None], pn[:, :M_HEADS][None], pm[:, :M_HEADS, 0][None]


SEQ_PER_GROUP = 32
SWA_SEQ_PER_STEP = 8
SEQ_BATCH = 8


def _sample_proj_kernel(x_ref, wt_ref, bias_ref, g0_ref, b0_ref, hs_ref, u_ref):
    hs = _layer_norm(x_ref[...], g0_ref[...], b0_ref[...])
    hs_ref[...] = hs
    _project(hs.astype(BF16), wt_ref, bias_ref, u_ref)


def _segment_last(x, pos, seg_len):
    n = x.shape[1]
    step = 1
    while step < seg_len:
        x = jnp.where((pos // step) % 2 == 0, pltpu.roll(x, n - step, 1), x)
        step *= 2
    return x


def _sample_mlstm_kernel(q_ref, k_ref, v_ref, g_ref, c_ref, n_ref, m_ref,
                         h_ref, cn_ref, nn_ref, mn_ref, inter_ref, *, dec_seq):
    hd = pl.program_id(1)
    n = CHUNK
    nb = n // dec_seq
    q = q_ref[...]
    k = k_ref[...] * (M_DH ** -0.5)
    v = v_ref[...]
    qb = q.astype(BF16)
    kb = k.astype(BF16)
    vb = v.astype(BF16)

    g_t = g_ref[...].T[0:8, :]
    row8 = _iota2((8, n), 0)
    li_row = jnp.sum(jnp.where(row8 == hd, g_t, 0.0), axis=0, keepdims=True)
    lf_row = _log_sigmoid(jnp.sum(jnp.where(row8 == hd + M_HEADS, g_t, 0.0), axis=0, keepdims=True))
    lane = _iota2((1, n), 1)
    pos = lane % dec_seq
    b_row = lf_row
    shift = 1
    while shift < dec_seq:
        b_row = b_row + jnp.where(pos >= shift, pltpu.roll(b_row, shift, 1), 0.0)
        shift *= 2

    key = _iota2((n, n), 0)
    query = _iota2((n, n), 1)
    key_ok = (key <= query) & (key // dec_seq == query // dec_seq)
    a_row = li_row - b_row
    a_keys = jnp.broadcast_to(jnp.sum(jnp.where(key == query, a_row, 0.0), axis=1, keepdims=True), (n, n))
    a_t = jnp.where(key_ok, a_keys, NEG_INF)
    m_old = m_ref[0, 0, 0:1, :]
    mm = jnp.maximum(m_old, jnp.max(a_t, axis=0, keepdims=True))
    w_t = jnp.exp(a_t - mm)
    mm_last = _segment_last(mm, pos, dec_seq)
    m_new = _segment_last(b_row, pos, dec_seq) + mm_last
    w_state = jnp.exp(a_row - mm_last)
    decay = jnp.exp(m_old - mm_last)
    mn_ref[0, 0] = jnp.broadcast_to(m_new, (8, n))

    qkw_t = _dot_nt(kb, qb) * w_t
    den = jnp.sum(qkw_t, axis=0, keepdims=True)
    num_t = _dot(v.T.astype(BF16), qkw_t.astype(BF16))

    n_seq = n_ref[0]
    seq_of_lane = _iota2((nb, n), 1) // dec_seq == _iota2((nb, n), 0)
    decay_seq = jnp.sum(jnp.where(_iota2((nb, n), 1) == _iota2((nb, n), 0) * dec_seq, decay, 0.0),
                        axis=1, keepdims=True)
    nn_ref[0] = decay_seq * n_seq + _dot(jnp.where(seq_of_lane, w_state, 0.0).astype(BF16), kb)
    expand = jnp.where(_iota2((n, nb), 0) // dec_seq == _iota2((n, nb), 1), 1.0, 0.0).astype(BF16)
    n_rows = _dot(expand, n_seq.astype(BF16))
    qn_col = jnp.sum(q * n_rows, axis=1, keepdims=True)
    qn = jnp.sum(jnp.where(key == query, qn_col, 0.0), axis=0, keepdims=True)

    kwt = k.T * w_state
    decay_rows = jnp.broadcast_to(jnp.sum(jnp.where(key == query, decay, 0.0), axis=1, keepdims=True), (n, n))
    lane_seq = query // dec_seq
    low_rows = _iota2((8, M_DH), 0) < dec_seq
    assert 8 % dec_seq == 0 and 8 // dec_seq == 2
    for pair in range(nb // 2):
        q8 = q[8 * pair:8 * pair + 8].astype(BF16)
        inter_ref[8 * pair:8 * pair + 8, :] = jnp.where(low_rows, _dot(q8, c_ref[2 * pair, 0].astype(BF16)),
                                                        _dot(q8, c_ref[2 * pair + 1, 0].astype(BF16)))
    for s0 in range(0, nb, SEQ_BATCH):
        seqs = range(s0, s0 + SEQ_BATCH)
        updates = [_dot(jnp.where(lane_seq == s, kwt, 0.0).astype(BF16), vb) for s in seqs]
        for s, upd in zip(seqs, updates):
            cn_ref[s, 0] = decay_rows[s * dec_seq:s * dec_seq + 1, :] * c_ref[s, 0] + upd

    s_inter = jnp.exp(m_old - mm)
    num_t = num_t + inter_ref[...].T * s_inter
    den = den + qn * s_inter
    h_ref[...] = (num_t * (1.0 / jnp.maximum(jnp.abs(den), jnp.exp(-(b_row + mm))))).T


def _sample_swa_kernel(qz_ref, kn_ref, vn_ref, ck_ref, cv_ref, sink_ref, o_ref, nk_ref, nv_ref, *, dec_seq):
    n_rows = qz_ref.shape[1]
    t_c = _iota2((n_rows, N_BUF), 0) % dec_seq
    i_c = _iota2((n_rows, N_BUF), 1)
    ok_c = (i_c < N_META) | (i_c > t_c + N_META)
    t_n = _iota2((n_rows, 8), 0) % dec_seq
    i_n = _iota2((n_rows, 8), 1)
    ok_n = (i_n <= t_n) & (i_n < dec_seq)
    sink = sink_ref[:, 0:1]

    seqs = range(qz_ref.shape[0])
    scores = []
    for s in seqs:
        qz = (qz_ref[s] * (A_DH ** -0.5)).astype(BF16)
        scores.append((_dot_nt(qz, ck_ref[s].astype(BF16)), _dot_nt(qz, kn_ref[s].astype(BF16))))
    probs = []
    for s_c, s_n in scores:
        s_c = jnp.where(ok_c, s_c, NEG_INF)
        s_n = jnp.where(ok_n, s_n, NEG_INF)
        mx = jnp.maximum(jnp.maximum(jnp.max(s_c, axis=1, keepdims=True),
                                     jnp.max(s_n, axis=1, keepdims=True)), sink)
        p_c = jnp.exp(s_c - mx)
        p_n = jnp.exp(s_n - mx)
        den = jnp.sum(p_c, axis=1, keepdims=True) + jnp.sum(p_n, axis=1, keepdims=True) + jnp.exp(sink - mx)
        probs.append((p_c.astype(BF16), p_n.astype(BF16), den))
    outs = [_dot(p_c, cv_ref[s].astype(BF16)) + _dot(p_n, vn_ref[s].astype(BF16))
            for s, (p_c, p_n, _) in zip(seqs, probs)]
    for s, o, (_, _, den) in zip(seqs, outs, probs):
        o_ref[s] = o / den
    for s in seqs:
        for cache_ref, new_ref, out_ref in ((ck_ref, kn_ref, nk_ref), (cv_ref, vn_ref, nv_ref)):
            out_ref[s, 0:N_META, :] = cache_ref[s, 0:N_META, :]
            out_ref[s, N_META:N_BUF - dec_seq, :] = cache_ref[s, N_META + dec_seq:N_BUF, :]
            out_ref[s, N_BUF - dec_seq:N_BUF, :] = new_ref[s, 0:dec_seq, :]


def _sample_out_kernel(hs_ref, u_ref, hm_ref, oa_ref, ng_ref, wo_ref, lng_ref, lnb_ref, y_ref):
    mix = _gate_mix(hm_ref[...], u_ref[:, C_O:C_O + D_MLSTM], u_ref[:, C_Z:C_Z + D_MLSTM],
                    oa_ref[...], u_ref[:, C_AZ:C_AZ + D_SWA], ng_ref[...])
    y_ref[...] = _out_and_norm(hs_ref[...], mix, wo_ref, lng_ref[...], lnb_ref[...])


def _sample_path(x_sample, cache_k, cache_v, state_c, state_n, state_m,
                 ln0_g, ln0_b, w_t, b_all, a_sinks, norm_g, w_o, ln_g, ln_b):
    db, dec_seq, _ = x_sample.shape
    rows = db * dec_seq
    params = pltpu.CompilerParams(vmem_limit_bytes=VMEM_LIMIT_BYTES)
    hs, u = pl.pallas_call(
        _sample_proj_kernel,
        out_shape=(jax.ShapeDtypeStruct((rows, D_MODEL), F32), jax.ShapeDtypeStruct((rows, N_PAD), F32)),
        compiler_params=params,
        name="sample_proj",
    )(x_sample.reshape(rows, D_MODEL), w_t, b_all, ln0_g, ln0_b)

    n_groups = db // SEQ_PER_GROUP
    n_t = jnp.transpose(state_n, (1, 0, 2))
    m_t = jnp.repeat(jnp.transpose(state_m, (1, 0)), dec_seq, axis=1).reshape(M_HEADS, n_groups, 1, CHUNK)
    m_t = jnp.broadcast_to(m_t, (M_HEADS, n_groups, 8, CHUNK))

    def col_spec(col0):
        return pl.BlockSpec((CHUNK, M_DH), lambda g, h: (g, col0 // M_DH + h))

    state_spec = pl.BlockSpec((SEQ_PER_GROUP, 1, M_DH, M_DH), lambda g, h: (g, h, 0, 0))
    vec_spec = pl.BlockSpec((1, SEQ_PER_GROUP, M_DH), lambda g, h: (h, g, 0))
    m_spec = pl.BlockSpec((1, 1, 8, CHUNK), lambda g, h: (h, g, 0, 0))
    h_m, c_new, n_new, m_new = pl.pallas_call(
        functools.partial(_sample_mlstm_kernel, dec_seq=dec_seq),
        grid=(n_groups, M_HEADS),
        in_specs=[col_spec(C_Q), col_spec(C_K), col_spec(C_V),
                  pl.BlockSpec((CHUNK, 128), lambda g, h: (g, C_G // 128)),
                  state_spec, vec_spec, m_spec],
        out_specs=[pl.BlockSpec((CHUNK, M_DH), lambda g, h: (g, h)), state_spec, vec_spec, m_spec],
        out_shape=(jax.ShapeDtypeStruct((rows, D_MLSTM), F32),
                   jax.ShapeDtypeStruct(state_c.shape, F32),
                   jax.ShapeDtypeStruct((M_HEADS, db, M_DH), F32),
                   jax.ShapeDtypeStruct((M_HEADS, n_groups, 8, CHUNK), F32)),
        scratch_shapes=[pltpu.VMEM((CHUNK, M_DH), F32)],
        compiler_params=pltpu.CompilerParams(dimension_semantics=("arbitrary", "arbitrary"),
                                             vmem_limit_bytes=VMEM_LIMIT_BYTES),
        name="sample_mlstm",
    )(u, u, u, u, state_c, n_t, m_t)

    aq = u[:, C_AQ:C_AQ + D_SWA].reshape(db, dec_seq, A_KV_HEADS, A_GROUP, A_DH)
    aq = jnp.transpose(aq, (0, 2, 3, 1, 4)).reshape(db, A_KV_HEADS, A_GROUP * dec_seq, A_DH)
    zeros = jnp.zeros_like(aq[:, 0])
    qz = jnp.stack([jnp.concatenate([aq[:, 0], zeros], axis=-1),
                    jnp.concatenate([zeros, aq[:, 1]], axis=-1)], axis=1)
    n_qrows = A_HEADS * dec_seq
    qz = qz.reshape(db, n_qrows, KV_W)
    k_new = jnp.pad(u[:, C_AK:C_AK + KV_W].reshape(db, dec_seq, KV_W), ((0, 0), (0, 8 - dec_seq), (0, 0)))
    v_new = jnp.pad(u[:, C_AV:C_AV + KV_W].reshape(db, dec_seq, KV_W), ((0, 0), (0, 8 - dec_seq), (0, 0)))
    sink_rows = jnp.broadcast_to(jnp.repeat(a_sinks.astype(F32), dec_seq)[:, None], (n_qrows, 128))
    ck = cache_k.reshape(db, N_BUF, KV_W)
    cv = cache_v.reshape(db, N_BUF, KV_W)
    sb = SWA_SEQ_PER_STEP

    def seq_spec(r):
        return pl.BlockSpec((sb, r, KV_W), lambda i: (i, 0, 0))

    o, nk, nv = pl.pallas_call(
        functools.partial(_sample_swa_kernel, dec_seq=dec_seq),
        grid=(db // sb,),
        in_specs=[seq_spec(n_qrows), seq_spec(8), seq_spec(8), seq_spec(N_BUF), seq_spec(N_BUF),
                  pl.BlockSpec((n_qrows, 128), lambda i: (0, 0))],
        out_specs=[seq_spec(n_qrows), seq_spec(N_BUF), seq_spec(N_BUF)],
        out_shape=(jax.ShapeDtypeStruct((db, n_qrows, KV_W), F32),
                   jax.ShapeDtypeStruct((db, N_BUF, KV_W), F32),
                   jax.ShapeDtypeStruct((db, N_BUF, KV_W), F32)),
        compiler_params=pltpu.CompilerParams(dimension_semantics=("arbitrary",),
                                             vmem_limit_bytes=VMEM_LIMIT_BYTES),
        name="sample_swa",
    )(qz, k_new, v_new, ck, cv, sink_rows)
    o = o.reshape(db, A_KV_HEADS, A_GROUP, dec_seq, A_KV_HEADS, A_DH)
    o = jnp.stack([o[:, 0, :, :, 0, :], o[:, 1, :, :, 1, :]], axis=1)
    o_a = jnp.transpose(o, (0, 3, 1, 2, 4)).reshape(rows, D_SWA)

    y = pl.pallas_call(
        _sample_out_kernel,
        out_shape=jax.ShapeDtypeStruct((rows, D_MODEL), F32),
        compiler_params=params,
        name="sample_out",
    )(hs, u, h_m, o_a, norm_g, w_o, ln_g, ln_b)

    shape5 = (1, db, N_BUF, A_KV_HEADS, A_DH)
    return (y.reshape(db, dec_seq, D_MODEL), nk.reshape(shape5), nv.reshape(shape5), c_new[None],
            jnp.transpose(n_new, (1, 0, 2))[None], jnp.transpose(m_new[:, :, 0, ::dec_seq].reshape(M_HEADS, db), (1, 0))[None])


def kernel(x_prompt, x_sample, cache_swa_k, cache_swa_v, state_mlstm_c, state_mlstm_n, state_mlstm_m,
           meta_tokens, ln0_g, ln0_b, w_in, b_in, a_sinks, m_norm_g, w_out, ln_g, ln_b):
    assert w_in.shape[0] == DEPTH and x_prompt.shape[-1] == D_MODEL
    w_raw = jnp.transpose(w_in[0].astype(F32))
    b_raw = b_in[0].astype(F32)[:, None]

    def rearranged(p):
        def take(raw, scale=1.0):
            return p[raw[0]:raw[0] + raw[1]] * scale
        zeros = lambda n: jnp.zeros((n, p.shape[1]), F32)
        return jnp.concatenate([take(RAW_Q), take(RAW_AQ), take(RAW_AV), take(RAW_V), take(RAW_O, 0.5),
                                take(RAW_Z, 0.5), take(RAW_AZ, 0.5), take(RAW_G), zeros(C_K - C_G - RAW_G[1]),
                                take(RAW_K), take(RAW_AK)], axis=0)

    w_t = rearranged(w_raw).astype(BF16)
    b_all = rearranged(b_raw).reshape(1, N_PAD)
    w_o = w_out[0].astype(BF16)
    g0 = ln0_g.astype(F32)[None]
    b0 = ln0_b.astype(F32)[None]
    lg = ln_g[0].astype(F32)[None]
    lb = ln_b[0].astype(F32)[None]
    norm_g = m_norm_g[0].astype(F32)[None]
    sinks = a_sinks[0].astype(F32)

    y_p, pk, pv, pc, pn, pm = _prompt_path(x_prompt, meta_tokens, g0, b0, w_t, b_all, sinks, norm_g, w_o, lg, lb)
    y_s, sk, sv, sc, sn, sm = _sample_path(x_sample, cache_swa_k[0], cache_swa_v[0], state_mlstm_c[0],
                                           state_mlstm_n[0], state_mlstm_m[0],
                                           g0, b0, w_t, b_all, sinks, norm_g, w_o, lg, lb)
    return (y_p, y_s, pk, pv, pc, pn, pm, sk, sv, sc, sn, sm)
```

```python
import functools
import itertools

import jax
import jax.numpy as jnp
from jax import lax
from jax.experimental import pallas as pl
from jax.experimental.pallas import tpu as pltpu

F32 = jnp.float32
BF16 = jnp.bfloat16

D_MODEL = 1024
N_META = 16
M_HEADS = 4
M_DH = 128
D_MLSTM = M_HEADS * M_DH
A_HEADS = 8
A_KV_HEADS = 2
A_GROUP = A_HEADS // A_KV_HEADS
A_DH = 64
D_SWA = A_HEADS * A_DH
KV_W = A_KV_HEADS * A_DH
WINDOW = 128
CHUNK = 128
LN_EPS = 1e-5
DEPTH = 1
DN_ALPHA = (2.0 * DEPTH) ** 0.25
N_BUF = N_META + WINDOW

C_Q, C_AQ = 0, 512
N_T_BF16 = 1024
C_AV, C_V, C_O, C_Z, C_AZ, C_G = 1024, 1152, 1664, 2176, 2688, 3200
GATE_ROWS = 16
N_T = C_G + GATE_ROWS
C_K, C_AK = 3328, 3840
N_PAD = 3968
N_KEYS = N_PAD - C_K
RAW_Q, RAW_K, RAW_V, RAW_O, RAW_Z, RAW_G = (0, 512), (512, 512), (1024, 512), (1536, 512), (2048, 512), (2560, 8)
RAW_AQ, RAW_AK, RAW_AV, RAW_AZ = (2568, 512), (3080, 128), (3208, 128), (3336, 512)

PROJ_STEP = 512
CHUNK_UNROLL = 4
VMEM_LIMIT_BYTES = 62 * 1024 * 1024
NEG_INF = float("-inf")


def _dot(a, b):
    return jnp.dot(a, b, preferred_element_type=F32)


def _dot_nt(a, b):
    return lax.dot_general(a, b, (((1,), (1,)), ((), ())), preferred_element_type=F32)


def _layer_norm(x, g, b):
    mu = jnp.mean(x, axis=-1, keepdims=True)
    xc = x - mu
    var = jnp.mean(xc * xc, axis=-1, keepdims=True)
    return xc * lax.rsqrt(var + LN_EPS) * g + b


def _log_sigmoid(x):
    return jnp.minimum(x, 0.0) - jnp.log1p(jnp.exp(-jnp.abs(x)))


def _times_sigmoid(h, half_x):
    return 0.5 * (h * jnp.tanh(half_x) + h)


def _silu_of_twice(half_x):
    return half_x * jnp.tanh(half_x) + half_x


def _iota2(shape, dim):
    return lax.broadcasted_iota(jnp.int32, shape, dim)


def _project(hb, wt_ref, b_ref, u_ref):
    for n0 in range(0, N_PAD, PROJ_STEP):
        n1 = min(n0 + PROJ_STEP, N_PAD)
        u_ref[:, n0:n1] = _dot_nt(hb, wt_ref[n0:n1, :]) + b_ref[:, n0:n1]


def _project_both_steps(hb, wt_ref, bcol_ref, brow_ref, utb_ref, utf_ref, ukey_ref):
    reps = hb.shape[0] // 128
    starts = list(range(0, N_T, PROJ_STEP))
    for n0 in starts[-1:] + starts[:-1]:
        n1 = min(n0 + PROJ_STEP, N_T)
        res = _dot_nt(wt_ref[n0:n1, :], hb) + jnp.concatenate([bcol_ref[n0:n1, :]] * reps, axis=1)
        if n1 <= N_T_BF16:
            utb_ref[n0:n1, :] = res.astype(BF16)
        else:
            utf_ref[n0 - N_T_BF16:n1 - N_T_BF16, :] = res
        yield
    for n0 in range(C_K, N_PAD, PROJ_STEP):
        n1 = min(n0 + PROJ_STEP, N_PAD)
        ukey_ref[:, n0 - C_K:n1 - C_K] = _dot_nt(hb, wt_ref[n0:n1, :]) + brow_ref[:, n0:n1]
        yield


def _interleave(*step_generators):
    for gen in step_generators:
        for _ in gen:
            pass


def _gate_rows(gates):
    return _gate_scan(jnp.concatenate([gates[r0:r0 + CHUNK].T[0:8, :] for r0 in range(0, gates.shape[0], CHUNK)],
                                      axis=0))


def _gate_rows_t(gates_t):
    return _gate_scan(jnp.concatenate([gates_t[:, c0:c0 + CHUNK] for c0 in range(0, gates_t.shape[1], CHUNK)],
                                      axis=0))


def _gate_scan(g_t):
    x = _log_sigmoid(g_t)
    lane = _iota2(x.shape, 1)
    shift = 1
    while shift < x.shape[1]:
        x = x + jnp.where(lane >= shift, pltpu.roll(x, shift, 1), 0.0)
        shift *= 2
    return g_t, x


CT_ROWS = M_DH + 8


def _mlstm_scores_and_state(q_tb, kb, v_t, li_row, b_row, m_old, ct_aug, key_ok, n_valid):
    n = kb.shape[0]
    row = _iota2((n, n), 0)
    col = _iota2((n, n), 1)
    a_row = li_row - b_row
    a_keys = jnp.broadcast_to(jnp.sum(jnp.where(row == col, a_row, 0.0), axis=1, keepdims=True), (n, n))
    a_t = jnp.where(key_ok, a_keys, NEG_INF)
    mm = jnp.maximum(m_old, jnp.max(a_t, axis=0, keepdims=True))
    w_t = jnp.exp(a_t - mm)
    lane = _iota2((1, n), 1)
    last = n_valid - 1
    mm_last = jnp.max(jnp.where(lane == last, mm, NEG_INF), axis=1, keepdims=True)
    m_new = jnp.sum(jnp.where(lane == last, b_row, 0.0), axis=1, keepdims=True) + mm_last
    w_state = jnp.exp(a_row - mm_last)
    if n_valid < n:
        w_state = jnp.where(lane < n_valid, w_state, 0.0)
    decay = jnp.exp(m_old - mm_last)
    scores_t = _dot(kb, q_tb)
    inter = _dot(ct_aug.astype(BF16), q_tb)
    ones_row = jnp.where(_iota2((CT_ROWS - M_DH, n), 0) == 0, w_state, 0.0)
    vtw = jnp.concatenate([v_t * w_state, ones_row], axis=0).astype(BF16)
    ct_aug_new = decay * ct_aug + _dot(vtw, kb)
    s_inter = jnp.exp(m_old - mm)
    floor = jnp.exp(-(b_row + mm))
    return (scores_t, w_t, v_t.astype(BF16), inter, s_inter, floor), ct_aug_new, m_new


def _mlstm_weighted_values(scores_t, w_t, vtb, inter, s_inter, floor):
    qkw_t = scores_t * w_t
    den = jnp.sum(qkw_t, axis=0, keepdims=True)
    num_t = _dot(vtb, qkw_t.astype(BF16))
    return num_t, den, inter, s_inter, floor


def _mlstm_finish(num_t, den, inter, s_inter, floor):
    num_t = num_t + inter[0:M_DH] * s_inter
    den = den + inter[M_DH:M_DH + 1] * s_inter
    return num_t * (1.0 / jnp.maximum(jnp.abs(den), floor))


def _mlstm_gate_head(h, half_o, half_z, norm_g, axis):
    hh = _times_sigmoid(h, half_o)
    mu = jnp.mean(hh, axis=axis, keepdims=True)
    hc = hh - mu
    var = jnp.mean(hc * hc, axis=axis, keepdims=True)
    return hc * lax.rsqrt(var + LN_EPS) * norm_g * _silu_of_twice(half_z)


def _gate_mix(h_m, half_o, half_z, o_a, half_az, norm_g):
    parts = []
    for hd in range(M_HEADS):
        sl = slice(hd * M_DH, (hd + 1) * M_DH)
        parts.append(_mlstm_gate_head(h_m[:, sl], half_o[:, sl], half_z[:, sl], norm_g[:, sl], axis=-1).astype(BF16))
    parts.append((o_a * _silu_of_twice(half_az)).astype(BF16))
    return jnp.concatenate(parts, axis=-1)


def _out_and_norm(hp, mix, wo_ref, g, b):
    z = DN_ALPHA * hp + _dot(mix, wo_ref[...])
    return _layer_norm(z, g, b)


def _meta_kernel(meta_ref, wt_ref, bias_ref, g0_ref, b0_ref,
                 ct0_ref, m0_ref, km_ref, vm_ref, vmt_ref, u_ref):
    hp = _layer_norm(meta_ref[...], g0_ref[...], b0_ref[...])
    _project(hp.astype(BF16), wt_ref, bias_ref, u_ref)
    row = _iota2((CHUNK, CHUNK), 0)
    col = _iota2((CHUNK, CHUNK), 1)
    key_ok = (row <= col) & (row < N_META)
    li_rows, b_rows = _gate_rows(u_ref[:, C_G:C_G + 128])
    zero_m = jnp.zeros((1, 128), F32)
    zero_ct = jnp.zeros((CT_ROWS, M_DH), F32)
    m0_ref[...] = jnp.zeros(m0_ref.shape, F32)
    for hd in range(M_HEADS):
        q = u_ref[:, C_Q + hd * M_DH:C_Q + (hd + 1) * M_DH]
        k = u_ref[:, C_K + hd * M_DH:C_K + (hd + 1) * M_DH] * (M_DH ** -0.5)
        v = u_ref[:, C_V + hd * M_DH:C_V + (hd + 1) * M_DH]
        _, ct_new, m_new = _mlstm_scores_and_state(
            q.T.astype(BF16), k.astype(BF16), v.T, li_rows[hd:hd + 1], b_rows[M_HEADS + hd:M_HEADS + hd + 1],
            zero_m, zero_ct, key_ok, N_META)
        ct0_ref[hd] = ct_new
        m0_ref[hd:hd + 1, :] = jnp.broadcast_to(m_new, (1, 128))
    km_ref[...] = u_ref[0:N_META, C_AK:C_AK + KV_W]
    vm_ref[...] = u_ref[0:N_META, C_AV:C_AV + KV_W]
    vmt_ref[...] = u_ref[:, C_AV:C_AV + KV_W].T[:, 0:N_META]


def _keep_kv_half(x, kv):
    low = _iota2(x.shape, 1) < A_DH
    return jnp.where(low if kv == 0 else ~low, x, 0.0)


def _swa_probs_t(s_own, s_prev, s_meta, sink, own_ok, prev_ok):
    s_own = jnp.where(own_ok, s_own, NEG_INF)
    s_prev = jnp.where(prev_ok, s_prev, NEG_INF)
    mx = jnp.maximum(jnp.maximum(jnp.max(s_own, axis=0, keepdims=True),
                                 jnp.max(s_prev, axis=0, keepdims=True)),
                     jnp.maximum(jnp.max(s_meta, axis=0, keepdims=True), sink))
    p_own, p_prev, p_meta = (jnp.exp(s - mx) for s in (s_own, s_prev, s_meta))
    den = (jnp.sum(p_own, axis=0, keepdims=True) + jnp.sum(p_prev, axis=0, keepdims=True)
           + jnp.sum(p_meta, axis=0, keepdims=True) + jnp.exp(sink - mx))
    return p_own.astype(BF16), p_prev.astype(BF16), p_meta.astype(BF16), den


N_BLOCK_BUFFERS = 6


def _prompt_kernel(sink_ref, x0_ref, x1_ref, x2_ref, wt_ref, bcol_ref, brow_ref, g0_ref, b0_ref,
                   ct0_ref, m0_ref, km_ref, vm_ref, vmt_ref, ng_ref, wo_ref, lng_ref, lnb_ref,
                   y_ref, pk_ref, pv_ref, pc_ref, pn_ref, pm_ref, *scratch, tb):
    buf_a = scratch[0:N_BLOCK_BUFFERS]
    buf_b = scratch[N_BLOCK_BUFFERS:2 * N_BLOCK_BUFFERS]
    mix_ref, ct_ref, kprev_ref, vtprev_ref = scratch[2 * N_BLOCK_BUFFERS:]
    j = pl.program_id(1)
    n_chunks = tb // CHUNK

    def feat(c0, n=M_DH):
        return slice(c0 - N_T_BF16, c0 - N_T_BF16 + n)

    def project_block(x_ref, buf):
        utb_ref, utf_ref, ukey_ref, hp_ref, li_ref, cumf_ref = buf
        hp = _layer_norm(x_ref[0], g0_ref[...], b0_ref[...])
        hp_ref[...] = hp
        yield
        for slab, _ in enumerate(_project_both_steps(hp.astype(BF16), wt_ref, bcol_ref, brow_ref,
                                                     utb_ref, utf_ref, ukey_ref)):
            if slab == 0:
                li_ref[...], cumf_ref[...] = _gate_rows_t(utf_ref[feat(C_G, 8), :])
            yield

    @pl.when(j == 0)
    def _():
        ct_ref[...] = ct0_ref[...]
        pm_ref[0] = m0_ref[...]
        kprev_ref[...] = jnp.zeros(kprev_ref.shape, F32)
        vtprev_ref[...] = jnp.zeros(vtprev_ref.shape, F32)
        pk_ref[0, 0:N_META, :] = km_ref[...]
        pv_ref[0, 0:N_META, :] = vm_ref[...]
        _interleave(project_block(x0_ref, buf_a))

    def chunk(buf, mix_ref, ci, first_chunk):
        utb_ref, utf_ref, ukey_ref, _, li_ref, cumf_ref = buf
        rows = slice(ci * CHUNK, (ci + 1) * CHUNK)
        key = _iota2((CHUNK, CHUNK), 0)
        query = _iota2((CHUNK, CHUNK), 1)
        causal = key <= query

        li_rows = li_ref[ci * 8:(ci + 1) * 8, :]
        b_rows = cumf_ref[ci * 8:(ci + 1) * 8, :]
        m_carry = []
        for hd in range(M_HEADS):
            q_tb = utb_ref[C_Q + hd * M_DH:C_Q + (hd + 1) * M_DH, rows]
            kb = (ukey_ref[rows, hd * M_DH:(hd + 1) * M_DH] * (M_DH ** -0.5)).astype(BF16)
            v_t = utf_ref[feat(C_V + hd * M_DH), rows]
            carry_hd, ct_new, m_new = _mlstm_scores_and_state(
                q_tb, kb, v_t, li_rows[hd:hd + 1], b_rows[M_HEADS + hd:M_HEADS + hd + 1],
                pm_ref[0, hd:hd + 1, :], ct_ref[hd], causal, CHUNK)
            ct_ref[hd] = ct_new
            pm_ref[0, hd:hd + 1, :] = jnp.broadcast_to(m_new, (1, 128))
            m_carry.append(carry_hd)

        k_own = ukey_ref[rows, C_AK - C_K:C_AK - C_K + KV_W]
        vt_own = utf_ref[feat(C_AV, KV_W), rows]
        k_prev = kprev_ref[...]
        vt_prev = vtprev_ref[...]
        prev_ok = (key > query) & ((first_chunk + ci) > 0)
        keys_kv = [tuple((_keep_kv_half(x, kv) * (A_DH ** -0.5)).astype(BF16) for x in (k_own, k_prev, km_ref[...]))
                   for kv in range(A_KV_HEADS)]
        a_scores = []
        for kv in range(A_KV_HEADS):
            q_cat = jnp.concatenate([utb_ref[C_AQ + (hd - kv) * A_DH:C_AQ + (hd - kv + 2) * A_DH, rows]
                                     for hd in range(kv * A_GROUP, (kv + 1) * A_GROUP)], axis=1)
            a_scores.append(tuple(_dot(kk, q_cat) for kk in keys_kv[kv]))
        kprev_ref[...] = k_own
        vtprev_ref[...] = vt_own
        yield

        m_carry = [_mlstm_weighted_values(*c) for c in m_carry]
        a_out = []
        for kv in range(A_KV_HEADS):
            probs = [_swa_probs_t(*(s[:, g * CHUNK:(g + 1) * CHUNK] for s in a_scores[kv]),
                                  sink_ref[kv * A_GROUP + g], causal, prev_ok) for g in range(A_GROUP)]
            vsl = slice(kv * A_DH, (kv + 1) * A_DH)
            p_win = jnp.concatenate([jnp.concatenate([p[1] for p in probs], axis=1),
                                     jnp.concatenate([p[0] for p in probs], axis=1)], axis=0)
            vt_win = jnp.concatenate([vt_prev[vsl], vt_own[vsl]], axis=1).astype(BF16)
            o_cat = (_dot(vt_win, p_win)
                     + _dot(vmt_ref[vsl, :].astype(BF16), jnp.concatenate([p[2] for p in probs], axis=1)))
            a_out += [(o_cat[:, g * CHUNK:(g + 1) * CHUNK], probs[g][3]) for g in range(A_GROUP)]
        yield

        mix_t = []
        for hd in range(M_HEADS):
            mix_t.append(_mlstm_gate_head(_mlstm_finish(*m_carry[hd]), utf_ref[feat(C_O + hd * M_DH), rows],
                                          utf_ref[feat(C_Z + hd * M_DH), rows],
                                          ng_ref[hd * M_DH:(hd + 1) * M_DH, :], axis=0))
        for tile in range(A_HEADS // 2):
            o_t = jnp.concatenate([o * (1.0 / den) for o, den in a_out[2 * tile:2 * tile + 2]], axis=0)
            mix_t.append(o_t * _silu_of_twice(utf_ref[feat(C_AZ + tile * 128), rows]))
        for i, x_t in enumerate(mix_t):
            mix_ref[rows, i * 128:(i + 1) * 128] = x_t.T.astype(BF16)
        yield

    def finish_block(buf, mix_ref, first_chunk, y_rows):
        for ci in range(n_chunks):
            yield from chunk(buf, mix_ref, ci, first_chunk)
        y_ref[0, y_rows, :] = _out_and_norm(buf[3][...], mix_ref[...], wo_ref, lng_ref[...], lnb_ref[...])
        yield

    _interleave(project_block(x1_ref, buf_b), finish_block(buf_a, mix_ref, 2 * j * n_chunks, slice(0, tb)))
    _interleave(project_block(x2_ref, buf_a), finish_block(buf_b, mix_ref, (2 * j + 1) * n_chunks, slice(tb, 2 * tb)))

    @pl.when(j == pl.num_programs(1) - 1)
    def _():
        ukey_ref, utf_ref = buf_b[2], buf_b[1]
        pk_ref[0, N_META:N_BUF, :] = ukey_ref[tb - WINDOW:tb, C_AK - C_K:C_AK - C_K + KV_W]
        pv_ref[0, N_META:N_BUF, :] = utf_ref[feat(C_AV, KV_W), tb - WINDOW:tb].T
        for hd in range(M_HEADS):
            pc_ref[0, hd] = ct_ref[hd, 0:M_DH, :].T
            pn_ref[0, hd:hd + 1, :] = ct_ref[hd, M_DH:M_DH + 1, :]
        pn_ref[0, M_HEADS:8, :] = jnp.zeros((8 - M_HEADS, M_DH), F32)


def _const_spec(shape):
    return pl.BlockSpec(shape, lambda *_: (0,) * len(shape))


def _prompt_path(x_prompt, meta_tokens, ln0_g, ln0_b, w_t, b_all, sinks, norm_g, w_o, ln_g, ln_b, tb=512):
    batch, seq, _ = x_prompt.shape
    meta_pad = jnp.pad(meta_tokens.astype(F32), ((0, CHUNK - N_META), (0, 0)))
    ct0, m0, km, vm, vmt = pl.pallas_call(
        _meta_kernel,
        out_shape=(jax.ShapeDtypeStruct((M_HEADS, CT_ROWS, M_DH), F32),
                   jax.ShapeDtypeStruct((8, 128), F32),
                   jax.ShapeDtypeStruct((N_META, KV_W), F32),
                   jax.ShapeDtypeStruct((N_META, KV_W), F32),
                   jax.ShapeDtypeStruct((KV_W, N_META), F32)),
        scratch_shapes=[pltpu.VMEM((CHUNK, N_PAD), F32)],
        compiler_params=pltpu.CompilerParams(vmem_limit_bytes=VMEM_LIMIT_BYTES),
        name="meta_tokens",
    )(meta_pad, w_t, b_all, ln0_g, ln0_b)

    nj = seq // tb
    assert nj % 2 == 0
    resident = pl.Buffered(1)

    def big_const_spec(shape):
        return pl.BlockSpec(shape, lambda *_: (0,) * len(shape), pipeline_mode=resident)

    in_specs = [
        pl.BlockSpec(memory_space=pltpu.SMEM),
        pl.BlockSpec((1, tb, D_MODEL), lambda b, j: (b, 0, 0), pipeline_mode=resident),
        pl.BlockSpec((1, tb, D_MODEL), lambda b, j: (b, 2 * j + 1, 0)),
        pl.BlockSpec((1, tb, D_MODEL), lambda b, j: (b, jnp.minimum(2 * j + 2, nj - 1), 0)),
        big_const_spec((N_PAD, D_MODEL)),
        big_const_spec((N_PAD, 128)), _const_spec((1, N_PAD)),
        _const_spec((1, D_MODEL)), _const_spec((1, D_MODEL)),
        _const_spec((M_HEADS, CT_ROWS, M_DH)), _const_spec((8, 128)),
        _const_spec((N_META, KV_W)), _const_spec((N_META, KV_W)), _const_spec((KV_W, N_META)),
        _const_spec((D_MLSTM, 128)),
        big_const_spec((D_MODEL, D_MODEL)),
        _const_spec((1, D_MODEL)), _const_spec((1, D_MODEL)),
    ]
    out_specs = [
        pl.BlockSpec((1, 2 * tb, D_MODEL), lambda b, j: (b, j, 0)),
        pl.BlockSpec((1, N_BUF, KV_W), lambda b, j: (b, 0, 0)),
        pl.BlockSpec((1, N_BUF, KV_W), lambda b, j: (b, 0, 0)),
        pl.BlockSpec((1, M_HEADS, M_DH, M_DH), lambda b, j: (b, 0, 0, 0)),
        pl.BlockSpec((1, 8, M_DH), lambda b, j: (b, 0, 0)),
        pl.BlockSpec((1, 8, 128), lambda b, j: (b, 0, 0)),
    ]
    out_shape = (
        jax.ShapeDtypeStruct((batch, seq, D_MODEL), F32),
        jax.ShapeDtypeStruct((batch, N_BUF, KV_W), F32),
        jax.ShapeDtypeStruct((batch, N_BUF, KV_W), F32),
        jax.ShapeDtypeStruct((batch, M_HEADS, M_DH, M_DH), F32),
        jax.ShapeDtypeStruct((batch, 8, M_DH), F32),
        jax.ShapeDtypeStruct((batch, 8, 128), F32),
    )
    block_buffers = [pltpu.VMEM((N_T_BF16, tb), BF16), pltpu.VMEM((N_T - N_T_BF16, tb), F32),
                     pltpu.VMEM((tb, N_KEYS), F32), pltpu.VMEM((tb, D_MODEL), F32),
                     pltpu.VMEM((tb // CHUNK * 8, CHUNK), F32), pltpu.VMEM((tb // CHUNK * 8, CHUNK), F32)]
    assert len(block_buffers) == N_BLOCK_BUFFERS
    y, pk, pv, pc, pn, pm = pl.pallas_call(
        functools.partial(_prompt_kernel, tb=tb),
        grid=(batch, nj // 2),
        in_specs=in_specs,
        out_specs=out_specs,
        out_shape=out_shape,
        scratch_shapes=2 * block_buffers + [
            pltpu.VMEM((tb, D_MODEL), BF16),
            pltpu.VMEM((M_HEADS, CT_ROWS, M_DH), F32),
            pltpu.VMEM((CHUNK, KV_W), F32), pltpu.VMEM((KV_W, CHUNK), F32)],
        compiler_params=pltpu.CompilerParams(dimension_semantics=("arbitrary", "arbitrary"),
                                             vmem_limit_bytes=VMEM_LIMIT_BYTES),
        name="prompt_layer",
    )(sinks, x_prompt, x_prompt, x_prompt, w_t, jnp.broadcast_to(b_all.reshape(N_PAD, 1), (N_PAD, 128)), b_all, ln0_g, ln0_b,
      ct0, m0, km, vm, vmt, jnp.broadcast_to(norm_g.reshape(D_MLSTM, 1), (D_MLSTM, 128)), w_o, ln_g, ln_b)
    pk = pk.reshape(1, batch, N_BUF, A_KV_HEADS, A_DH)
    pv = pv.reshape(1, batch, N_BUF, A_KV_HEADS, A_DH)
    return y, pk, pv, pc[None], pn[:, :M_HEADS][None], pm[:, :M_HEADS, 0][None]


SEQ_PER_GROUP = 32
SWA_SEQ_PER_STEP = 8
SEQ_BATCH = 8


def _sample_proj_kernel(x_ref, wt_ref, bias_ref, g0_ref, b0_ref, hs_ref, u_ref):
    hs = _layer_norm(x_ref[...], g0_ref[...], b0_ref[...])
    hs_ref[...] = hs
    _project(hs.astype(BF16), wt_ref, bias_ref, u_ref)


def _segment_last(x, pos, seg_len):
    n = x.shape[1]
    step = 1
    while step < seg_len:
        x = jnp.where((pos // step) % 2 == 0, pltpu.roll(x, n - step, 1), x)
        step *= 2
    return x


def _sample_mlstm_kernel(q_ref, k_ref, v_ref, g_ref, c_ref, n_ref, m_ref,
                         h_ref, cn_ref, nn_ref, mn_ref, inter_ref, *, dec_seq):
    hd = pl.program_id(1)
    n = CHUNK
    nb = n // dec_seq
    q = q_ref[...]
    k = k_ref[...] * (M_DH ** -0.5)
    v = v_ref[...]
    qb = q.astype(BF16)
    kb = k.astype(BF16)
    vb = v.astype(BF16)

    g_t = g_ref[...].T[0:8, :]
    row8 = _iota2((8, n), 0)
    li_row = jnp.sum(jnp.where(row8 == hd, g_t, 0.0), axis=0, keepdims=True)
    lf_row = _log_sigmoid(jnp.sum(jnp.where(row8 == hd + M_HEADS, g_t, 0.0), axis=0, keepdims=True))
    lane = _iota2((1, n), 1)
    pos = lane % dec_seq
    b_row = lf_row
    shift = 1
    while shift < dec_seq:
        b_row = b_row + jnp.where(pos >= shift, pltpu.roll(b_row, shift, 1), 0.0)
        shift *= 2

    key = _iota2((n, n), 0)
    query = _iota2((n, n), 1)
    key_ok = (key <= query) & (key // dec_seq == query // dec_seq)
    a_row = li_row - b_row
    a_keys = jnp.broadcast_to(jnp.sum(jnp.where(key == query, a_row, 0.0), axis=1, keepdims=True), (n, n))
    a_t = jnp.where(key_ok, a_keys, NEG_INF)
    m_old = m_ref[0, 0, 0:1, :]
    mm = jnp.maximum(m_old, jnp.max(a_t, axis=0, keepdims=True))
    w_t = jnp.exp(a_t - mm)
    mm_last = _segment_last(mm, pos, dec_seq)
    m_new = _segment_last(b_row, pos, dec_seq) + mm_last
    w_state = jnp.exp(a_row - mm_last)
    decay = jnp.exp(m_old - mm_last)
    mn_ref[0, 0] = jnp.broadcast_to(m_new, (8, n))

    qkw_t = _dot_nt(kb, qb) * w_t
    den = jnp.sum(qkw_t, axis=0, keepdims=True)
    num_t = _dot(v.T.astype(BF16), qkw_t.astype(BF16))

    n_seq = n_ref[0]
    seq_of_lane = _iota2((nb, n), 1) // dec_seq == _iota2((nb, n), 0)
    decay_seq = jnp.sum(jnp.where(_iota2((nb, n), 1) == _iota2((nb, n), 0) * dec_seq, decay, 0.0),
                        axis=1, keepdims=True)
    nn_ref[0] = decay_seq * n_seq + _dot(jnp.where(seq_of_lane, w_state, 0.0).astype(BF16), kb)
    expand = jnp.where(_iota2((n, nb), 0) // dec_seq == _iota2((n, nb), 1), 1.0, 0.0).astype(BF16)
    n_rows = _dot(expand, n_seq.astype(BF16))
    qn_col = jnp.sum(q * n_rows, axis=1, keepdims=True)
    qn = jnp.sum(jnp.where(key == query, qn_col, 0.0), axis=0, keepdims=True)

    kwt = k.T * w_state
    decay_rows = jnp.broadcast_to(jnp.sum(jnp.where(key == query, decay, 0.0), axis=1, keepdims=True), (n, n))
    lane_seq = query // dec_seq
    low_rows = _iota2((8, M_DH), 0) < dec_seq
    assert 8 % dec_seq == 0 and 8 // dec_seq == 2
    for pair in range(nb // 2):
        q8 = q[8 * pair:8 * pair + 8].astype(BF16)
        inter_ref[8 * pair:8 * pair + 8, :] = jnp.where(low_rows, _dot(q8, c_ref[2 * pair, 0].astype(BF16)),
                                                        _dot(q8, c_ref[2 * pair + 1, 0].astype(BF16)))
    for s0 in range(0, nb, SEQ_BATCH):
        seqs = range(s0, s0 + SEQ_BATCH)
        updates = [_dot(jnp.where(lane_seq == s, kwt, 0.0).astype(BF16), vb) for s in seqs]
        for s, upd in zip(seqs, updates):
            cn_ref[s, 0] = decay_rows[s * dec_seq:s * dec_seq + 1, :] * c_ref[s, 0] + upd

    s_inter = jnp.exp(m_old - mm)
    num_t = num_t + inter_ref[...].T * s_inter
    den = den + qn * s_inter
    h_ref[...] = (num_t * (1.0 / jnp.maximum(jnp.abs(den), jnp.exp(-(b_row + mm))))).T


def _sample_swa_kernel(qz_ref, kn_ref, vn_ref, ck_ref, cv_ref, sink_ref, o_ref, nk_ref, nv_ref, *, dec_seq):
    n_rows = qz_ref.shape[1]
    t_c = _iota2((n_rows, N_BUF), 0) % dec_seq
    i_c = _iota2((n_rows, N_BUF), 1)
    ok_c = (i_c < N_META) | (i_c > t_c + N_META)
    t_n = _iota2((n_rows, 8), 0) % dec_seq
    i_n = _iota2((n_rows, 8), 1)
    ok_n = (i_n <= t_n) & (i_n < dec_seq)
    sink = sink_ref[:, 0:1]

    seqs = range(qz_ref.shape[0])
    scores = []
    for s in seqs:
        qz = (qz_ref[s] * (A_DH ** -0.5)).astype(BF16)
        scores.append((_dot_nt(qz, ck_ref[s].astype(BF16)), _dot_nt(qz, kn_ref[s].astype(BF16))))
    probs = []
    for s_c, s_n in scores:
        s_c = jnp.where(ok_c, s_c, NEG_INF)
        s_n = jnp.where(ok_n, s_n, NEG_INF)
        mx = jnp.maximum(jnp.maximum(jnp.max(s_c, axis=1, keepdims=True),
                                     jnp.max(s_n, axis=1, keepdims=True)), sink)
        p_c = jnp.exp(s_c - mx)
        p_n = jnp.exp(s_n - mx)
        den = jnp.sum(p_c, axis=1, keepdims=True) + jnp.sum(p_n, axis=1, keepdims=True) + jnp.exp(sink - mx)
        probs.append((p_c.astype(BF16), p_n.astype(BF16), den))
    outs = [_dot(p_c, cv_ref[s].astype(BF16)) + _dot(p_n, vn_ref[s].astype(BF16))
            for s, (p_c, p_n, _) in zip(seqs, probs)]
    for s, o, (_, _, den) in zip(seqs, outs, probs):
        o_ref[s] = o / den
    for s in seqs:
        for cache_ref, new_ref, out_ref in ((ck_ref, kn_ref, nk_ref), (cv_ref, vn_ref, nv_ref)):
            out_ref[s, 0:N_META, :] = cache_ref[s, 0:N_META, :]
            out_ref[s, N_META:N_BUF - dec_seq, :] = cache_ref[s, N_META + dec_seq:N_BUF, :]
            out_ref[s, N_BUF - dec_seq:N_BUF, :] = new_ref[s, 0:dec_seq, :]


def _sample_out_kernel(hs_ref, u_ref, hm_ref, oa_ref, ng_ref, wo_ref, lng_ref, lnb_ref, y_ref):
    mix = _gate_mix(hm_ref[...], u_ref[:, C_O:C_O + D_MLSTM], u_ref[:, C_Z:C_Z + D_MLSTM],
                    oa_ref[...], u_ref[:, C_AZ:C_AZ + D_SWA], ng_ref[...])
    y_ref[...] = _out_and_norm(hs_ref[...], mix, wo_ref, lng_ref[...], lnb_ref[...])


def _sample_path(x_sample, cache_k, cache_v, state_c, state_n, state_m,
                 ln0_g, ln0_b, w_t, b_all, a_sinks, norm_g, w_o, ln_g, ln_b):
    db, dec_seq, _ = x_sample.shape
    rows = db * dec_seq
    params = pltpu.CompilerParams(vmem_limit_bytes=VMEM_LIMIT_BYTES)
    hs, u = pl.pallas_call(
        _sample_proj_kernel,
        out_shape=(jax.ShapeDtypeStruct((rows, D_MODEL), F32), jax.ShapeDtypeStruct((rows, N_PAD), F32)),
        compiler_params=params,
        name="sample_proj",
    )(x_sample.reshape(rows, D_MODEL), w_t, b_all, ln0_g, ln0_b)

    n_groups = db // SEQ_PER_GROUP
    n_t = jnp.transpose(state_n, (1, 0, 2))
    m_t = jnp.repeat(jnp.transpose(state_m, (1, 0)), dec_seq, axis=1).reshape(M_HEADS, n_groups, 1, CHUNK)
    m_t = jnp.broadcast_to(m_t, (M_HEADS, n_groups, 8, CHUNK))

    def col_spec(col0):
        return pl.BlockSpec((CHUNK, M_DH), lambda g, h: (g, col0 // M_DH + h))

    state_spec = pl.BlockSpec((SEQ_PER_GROUP, 1, M_DH, M_DH), lambda g, h: (g, h, 0, 0))
    vec_spec = pl.BlockSpec((1, SEQ_PER_GROUP, M_DH), lambda g, h: (h, g, 0))
    m_spec = pl.BlockSpec((1, 1, 8, CHUNK), lambda g, h: (h, g, 0, 0))
    h_m, c_new, n_new, m_new = pl.pallas_call(
        functools.partial(_sample_mlstm_kernel, dec_seq=dec_seq),
        grid=(n_groups, M_HEADS),
        in_specs=[col_spec(C_Q), col_spec(C_K), col_spec(C_V),
                  pl.BlockSpec((CHUNK, 128), lambda g, h: (g, C_G // 128)),
                  state_spec, vec_spec, m_spec],
        out_specs=[pl.BlockSpec((CHUNK, M_DH), lambda g, h: (g, h)), state_spec, vec_spec, m_spec],
        out_shape=(jax.ShapeDtypeStruct((rows, D_MLSTM), F32),
                   jax.ShapeDtypeStruct(state_c.shape, F32),
                   jax.ShapeDtypeStruct((M_HEADS, db, M_DH), F32),
                   jax.ShapeDtypeStruct((M_HEADS, n_groups, 8, CHUNK), F32)),
        scratch_shapes=[pltpu.VMEM((CHUNK, M_DH), F32)],
        compiler_params=pltpu.CompilerParams(dimension_semantics=("arbitrary", "arbitrary"),
                                             vmem_limit_bytes=VMEM_LIMIT_BYTES),
        name="sample_mlstm",
    )(u, u, u, u, state_c, n_t, m_t)

    aq = u[:, C_AQ:C_AQ + D_SWA].reshape(db, dec_seq, A_KV_HEADS, A_GROUP, A_DH)
    aq = jnp.transpose(aq, (0, 2, 3, 1, 4)).reshape(db, A_KV_HEADS, A_GROUP * dec_seq, A_DH)
    zeros = jnp.zeros_like(aq[:, 0])
    qz = jnp.stack([jnp.concatenate([aq[:, 0], zeros], axis=-1),
                    jnp.concatenate([zeros, aq[:, 1]], axis=-1)], axis=1)
    n_qrows = A_HEADS * dec_seq
    qz = qz.reshape(db, n_qrows, KV_W)
    k_new = jnp.pad(u[:, C_AK:C_AK + KV_W].reshape(db, dec_seq, KV_W), ((0, 0), (0, 8 - dec_seq), (0, 0)))
    v_new = jnp.pad(u[:, C_AV:C_AV + KV_W].reshape(db, dec_seq, KV_W), ((0, 0), (0, 8 - dec_seq), (0, 0)))
    sink_rows = jnp.broadcast_to(jnp.repeat(a_sinks.astype(F32), dec_seq)[:, None], (n_qrows, 128))
    ck = cache_k.reshape(db, N_BUF, KV_W)
    cv = cache_v.reshape(db, N_BUF, KV_W)
    sb = SWA_SEQ_PER_STEP

    def seq_spec(r):
        return pl.BlockSpec((sb, r, KV_W), lambda i: (i, 0, 0))

    o, nk, nv = pl.pallas_call(
        functools.partial(_sample_swa_kernel, dec_seq=dec_seq),
        grid=(db // sb,),
        in_specs=[seq_spec(n_qrows), seq_spec(8), seq_spec(8), seq_spec(N_BUF), seq_spec(N_BUF),
                  pl.BlockSpec((n_qrows, 128), lambda i: (0, 0))],
        out_specs=[seq_spec(n_qrows), seq_spec(N_BUF), seq_spec(N_BUF)],
        out_shape=(jax.ShapeDtypeStruct((db, n_qrows, KV_W), F32),
                   jax.ShapeDtypeStruct((db, N_BUF, KV_W), F32),
                   jax.ShapeDtypeStruct((db, N_BUF, KV_W), F32)),
        compiler_params=pltpu.CompilerParams(dimension_semantics=("arbitrary",),
                                             vmem_limit_bytes=VMEM_LIMIT_BYTES),
        name="sample_swa",
    )(qz, k_new, v_new, ck, cv, sink_rows)
    o = o.reshape(db, A_KV_HEADS, A_GROUP, dec_seq, A_KV_HEADS, A_DH)
    o = jnp.stack([o[:, 0, :, :, 0, :], o[:, 1, :, :, 1, :]], axis=1)
    o_a = jnp.transpose(o, (0, 3, 1, 2, 4)).reshape(rows, D_SWA)

    y = pl.pallas_call(
        _sample_out_kernel,
        out_shape=jax.ShapeDtypeStruct((rows, D_MODEL), F32),
        compiler_params=params,
        name="sample_out",
    )(hs, u, h_m, o_a, norm_g, w_o, ln_g, ln_b)

    shape5 = (1, db, N_BUF, A_KV_HEADS, A_DH)
    return (y.reshape(db, dec_seq, D_MODEL), nk.reshape(shape5), nv.reshape(shape5), c_new[None],
            jnp.transpose(n_new, (1, 0, 2))[None], jnp.transpose(m_new[:, :, 0, ::dec_seq].reshape(M_HEADS, db), (1, 0))[None])


def kernel(x_prompt, x_sample, cache_swa_k, cache_swa_v, state_mlstm_c, state_mlstm_n, state_mlstm_m,
           meta_tokens, ln0_g, ln0_b, w_in, b_in, a_sinks, m_norm_g, w_out, ln_g, ln_b):
    assert w_in.shape[0] == DEPTH and x_prompt.shape[-1] == D_MODEL
    w_raw = jnp.transpose(w_in[0].astype(F32))
    b_raw = b_in[0].astype(F32)[:, None]

    def rearranged(p):
        def take(raw, scale=1.0):
            return p[raw[0]:raw[0] + raw[1]] * scale
        zeros = lambda n: jnp.zeros((n, p.shape[1]), F32)
        return jnp.concatenate([take(RAW_Q), take(RAW_AQ), take(RAW_AV), take(RAW_V), take(RAW_O, 0.5),
                                take(RAW_Z, 0.5), take(RAW_AZ, 0.5), take(RAW_G), zeros(C_K - C_G - RAW_G[1]),
                                take(RAW_K), take(RAW_AK)], axis=0)

    w_t = rearranged(w_raw).astype(BF16)
    b_all = rearranged(b_raw).reshape(1, N_PAD)
    w_o = w_out[0].astype(BF16)
    g0 = ln0_g.astype(F32)[None]
    b0 = ln0_b.astype(F32)[None]
    lg = ln_g[0].astype(F32)[None]
    lb = ln_b[0].astype(F32)[None]
    norm_g = m_norm_g[0].astype(F32)[None]
    sinks = a_sinks[0].astype(F32)

    y_p, pk, pv, pc, pn, pm = _prompt_path(x_prompt, meta_tokens, g0, b0, w_t, b_all, sinks, norm_g, w_o, lg, lb)
    y_s, sk, sv, sc, sn, sm = _sample_path(x_sample, cache_swa_k[0], cache_swa_v[0], state_mlstm_c[0],
                                           state_mlstm_n[0], state_mlstm_m[0],
                                           g0, b0, w_t, b_all, sinks, norm_g, w_o, lg, lb)
    return (y_p, y_s, pk, pv, pc, pn, pm, sk, sv, sc, sn, sm)
```

```python
import functools

import jax
import jax.numpy as jnp
from jax import lax
from jax.experimental import pallas as pl
from jax.experimental.pallas import tpu as pltpu

F32 = jnp.float32
BF16 = jnp.bfloat16

D_MODEL = 1024
N_META = 16
M_HEADS = 4
M_DH = 128
D_MLSTM = M_HEADS * M_DH
A_HEADS = 8
A_KV_HEADS = 2
A_GROUP = A_HEADS // A_KV_HEADS
A_DH = 64
D_SWA = A_HEADS * A_DH
KV_W = A_KV_HEADS * A_DH
WINDOW = 128
CHUNK = 128
LN_EPS = 1e-5
DEPTH = 1
DN_ALPHA = (2.0 * DEPTH) ** 0.25
N_BUF = N_META + WINDOW

C_Q, C_AQ = 0, 512
N_T_BF16 = 1024
C_AV, C_V, C_O, C_Z, C_AZ, C_G = 1024, 1152, 1664, 2176, 2688, 3200
GATE_ROWS = 16
N_T = C_G + GATE_ROWS
C_K, C_AK = 3328, 3840
N_PAD = 3968
N_KEYS = N_PAD - C_K
RAW_Q, RAW_K, RAW_V, RAW_O, RAW_Z, RAW_G = (0, 512), (512, 512), (1024, 512), (1536, 512), (2048, 512), (2560, 8)
RAW_AQ, RAW_AK, RAW_AV, RAW_AZ = (2568, 512), (3080, 128), (3208, 128), (3336, 512)

PROJ_STEP = 512
TOKEN_SPLITS = 2
CHUNK_UNROLL = 4
VMEM_LIMIT_BYTES = 56 * 1024 * 1024
NEG_INF = float("-inf")


def _dot(a, b):
    return jnp.dot(a, b, preferred_element_type=F32)


def _dot_nt(a, b):
    return lax.dot_general(a, b, (((1,), (1,)), ((), ())), preferred_element_type=F32)


def _layer_norm(x, g, b):
    mu = jnp.mean(x, axis=-1, keepdims=True)
    xc = x - mu
    var = jnp.mean(xc * xc, axis=-1, keepdims=True)
    return xc * lax.rsqrt(var + LN_EPS) * g + b


def _log_sigmoid(x):
    return jnp.minimum(x, 0.0) - jnp.log1p(jnp.exp(-jnp.abs(x)))


def _times_sigmoid(h, half_x):
    return 0.5 * (h * jnp.tanh(half_x) + h)


def _silu_of_twice(half_x):
    return half_x * jnp.tanh(half_x) + half_x


def _iota2(shape, dim):
    return lax.broadcasted_iota(jnp.int32, shape, dim)


def _project(hb, wt_ref, b_ref, u_ref):
    for n0 in range(0, N_PAD, PROJ_STEP):
        n1 = min(n0 + PROJ_STEP, N_PAD)
        u_ref[:, n0:n1] = _dot_nt(hb, wt_ref[n0:n1, :]) + b_ref[:, n0:n1]


def _project_both(hb, tok, wt_ref, bcol_ref, utb_ref, utf_ref):
    reps = hb.shape[0] // 128
    starts = list(range(0, N_T, PROJ_STEP))
    for n0 in starts[-1:] + starts[:-1]:
        n1 = min(n0 + PROJ_STEP, N_T)
        res = _dot_nt(wt_ref[n0:n1, :], hb) + jnp.concatenate([bcol_ref[n0:n1, :]] * reps, axis=1)
        if n1 <= N_T_BF16:
            utb_ref[n0:n1, tok] = res.astype(BF16)
        else:
            utf_ref[n0 - N_T_BF16:n1 - N_T_BF16, tok] = res


def _project_keys(hb, wt_ref, brow_ref, ukey_ref):
    for n0 in range(C_K, N_PAD, PROJ_STEP):
        n1 = min(n0 + PROJ_STEP, N_PAD)
        ukey_ref[:, n0 - C_K:n1 - C_K] = _dot_nt(hb, wt_ref[n0:n1, :]) + brow_ref[:, n0:n1]


def _gate_rows(gates):
    return _gate_scan(jnp.concatenate([gates[r0:r0 + CHUNK].T[0:8, :] for r0 in range(0, gates.shape[0], CHUNK)],
                                      axis=0))


def _gate_rows_t(gates_t):
    return _gate_scan(jnp.concatenate([gates_t[:, c0:c0 + CHUNK] for c0 in range(0, gates_t.shape[1], CHUNK)],
                                      axis=0))


def _gate_scan(g_t):
    x = _log_sigmoid(g_t)
    lane = _iota2(x.shape, 1)
    shift = 1
    while shift < x.shape[1]:
        x = x + jnp.where(lane >= shift, pltpu.roll(x, shift, 1), 0.0)
        shift *= 2
    return g_t, x


CT_ROWS = M_DH + 8


def _mlstm_scores_and_state(q_tb, kb, v_t, li_row, b_row, m_old, ct_aug, key_ok, n_valid):
    n = kb.shape[0]
    row = _iota2((n, n), 0)
    col = _iota2((n, n), 1)
    a_row = li_row - b_row
    a_keys = jnp.broadcast_to(jnp.sum(jnp.where(row == col, a_row, 0.0), axis=1, keepdims=True), (n, n))
    a_t = jnp.where(key_ok, a_keys, NEG_INF)
    mm = jnp.maximum(m_old, jnp.max(a_t, axis=0, keepdims=True))
    w_t = jnp.exp(a_t - mm)
    lane = _iota2((1, n), 1)
    last = n_valid - 1
    mm_last = jnp.max(jnp.where(lane == last, mm, NEG_INF), axis=1, keepdims=True)
    m_new = jnp.sum(jnp.where(lane == last, b_row, 0.0), axis=1, keepdims=True) + mm_last
    w_state = jnp.exp(a_row - mm_last)
    if n_valid < n:
        w_state = jnp.where(lane < n_valid, w_state, 0.0)
    decay = jnp.exp(m_old - mm_last)
    scores_t = _dot(kb, q_tb)
    inter = _dot(ct_aug.astype(BF16), q_tb)
    ones_row = jnp.where(_iota2((CT_ROWS - M_DH, n), 0) == 0, w_state, 0.0)
    vtw = jnp.concatenate([v_t * w_state, ones_row], axis=0).astype(BF16)
    ct_aug_new = decay * ct_aug + _dot(vtw, kb)
    s_inter = jnp.exp(m_old - mm)
    floor = jnp.exp(-(b_row + mm))
    return (scores_t, w_t, v_t.astype(BF16), inter, s_inter, floor), ct_aug_new, m_new


def _mlstm_weighted_values(scores_t, w_t, vtb, inter, s_inter, floor):
    qkw_t = scores_t * w_t
    den = jnp.sum(qkw_t, axis=0, keepdims=True)
    num_t = _dot(vtb, qkw_t.astype(BF16))
    return num_t, den, inter, s_inter, floor


def _mlstm_finish(num_t, den, inter, s_inter, floor):
    num_t = num_t + inter[0:M_DH] * s_inter
    den = den + inter[M_DH:M_DH + 1] * s_inter
    return num_t * (1.0 / jnp.maximum(jnp.abs(den), floor))


def _mlstm_gate_head(h, half_o, half_z, norm_g, axis):
    hh = _times_sigmoid(h, half_o)
    mu = jnp.mean(hh, axis=axis, keepdims=True)
    hc = hh - mu
    var = jnp.mean(hc * hc, axis=axis, keepdims=True)
    return (hc * lax.rsqrt(var + LN_EPS) * norm_g * _silu_of_twice(half_z)).astype(BF16)


def _gate_mix(h_m, half_o, half_z, o_a, half_az, norm_g):
    parts = []
    for hd in range(M_HEADS):
        sl = slice(hd * M_DH, (hd + 1) * M_DH)
        parts.append(_mlstm_gate_head(h_m[:, sl], half_o[:, sl], half_z[:, sl], norm_g[:, sl], axis=-1))
    parts.append((o_a * _silu_of_twice(half_az)).astype(BF16))
    return jnp.concatenate(parts, axis=-1)


def _out_and_norm(hp, mix, wo_ref, g, b):
    z = DN_ALPHA * hp + _dot(mix, wo_ref[...])
    return _layer_norm(z, g, b)


def _meta_kernel(meta_ref, wt_ref, bias_ref, g0_ref, b0_ref,
                 ct0_ref, m0_ref, km_ref, vm_ref, vmt_ref, u_ref):
    hp = _layer_norm(meta_ref[...], g0_ref[...], b0_ref[...])
    _project(hp.astype(BF16), wt_ref, bias_ref, u_ref)
    row = _iota2((CHUNK, CHUNK), 0)
    col = _iota2((CHUNK, CHUNK), 1)
    key_ok = (row <= col) & (row < N_META)
    li_rows, b_rows = _gate_rows(u_ref[:, C_G:C_G + 128])
    zero_m = jnp.zeros((1, 128), F32)
    zero_ct = jnp.zeros((CT_ROWS, M_DH), F32)
    m0_ref[...] = jnp.zeros(m0_ref.shape, F32)
    for hd in range(M_HEADS):
        q = u_ref[:, C_Q + hd * M_DH:C_Q + (hd + 1) * M_DH]
        k = u_ref[:, C_K + hd * M_DH:C_K + (hd + 1) * M_DH] * (M_DH ** -0.5)
        v = u_ref[:, C_V + hd * M_DH:C_V + (hd + 1) * M_DH]
        _, ct_new, m_new = _mlstm_scores_and_state(
            q.T.astype(BF16), k.astype(BF16), v.T, li_rows[hd:hd + 1], b_rows[M_HEADS + hd:M_HEADS + hd + 1],
            zero_m, zero_ct, key_ok, N_META)
        ct0_ref[hd] = ct_new
        m0_ref[hd:hd + 1, :] = jnp.broadcast_to(m_new, (1, 128))
    km_ref[...] = u_ref[0:N_META, C_AK:C_AK + KV_W]
    vm_ref[...] = u_ref[0:N_META, C_AV:C_AV + KV_W]
    vmt_ref[...] = u_ref[:, C_AV:C_AV + KV_W].T[:, 0:N_META]


def _keep_kv_half(x, kv):
    low = _iota2(x.shape, 1) < A_DH
    return jnp.where(low if kv == 0 else ~low, x, 0.0)


def _to_token_rows(x_t, eye):
    return _dot_nt(eye, x_t).astype(BF16)


def _swa_weighted_values_t(scores, values_t, sink, own_ok, prev_ok):
    s_own = jnp.where(own_ok, scores[0], NEG_INF)
    s_prev = jnp.where(prev_ok, scores[1], NEG_INF)
    s_meta = scores[2]
    mx = jnp.maximum(jnp.maximum(jnp.max(s_own, axis=0, keepdims=True),
                                 jnp.max(s_prev, axis=0, keepdims=True)),
                     jnp.maximum(jnp.max(s_meta, axis=0, keepdims=True), sink))
    p_own, p_prev, p_meta = (jnp.exp(s - mx) for s in (s_own, s_prev, s_meta))
    den = (jnp.sum(p_own, axis=0, keepdims=True) + jnp.sum(p_prev, axis=0, keepdims=True)
           + jnp.sum(p_meta, axis=0, keepdims=True) + jnp.exp(sink - mx))
    o_t = (_dot(values_t[0], p_own.astype(BF16)) + _dot(values_t[1], p_prev.astype(BF16))
           + _dot(values_t[2], p_meta.astype(BF16)))
    return o_t, den


def _prompt_kernel(sink_ref, x_ref, wt_ref, bcol_ref, brow_ref, g0_ref, b0_ref,
                   ct0_ref, m0_ref, km_ref, vm_ref, vmt_ref, ng_ref, wo_ref, lng_ref, lnb_ref,
                   y_ref, pk_ref, pv_ref, pc_ref, pn_ref, pm_ref,
                   utb_ref, utf_ref, ukey_ref, hp_ref, mix_ref, ct_ref, kprev_ref, vtprev_ref, li_ref, cumf_ref,
                   *, tb):
    j = pl.program_id(1)
    n_chunks = tb // CHUNK

    @pl.when(j == 0)
    def _():
        ct_ref[...] = ct0_ref[...]
        pm_ref[0] = m0_ref[...]
        kprev_ref[...] = jnp.zeros(kprev_ref.shape, F32)
        vtprev_ref[...] = jnp.zeros(vtprev_ref.shape, F32)
        pk_ref[0, 0:N_META, :] = km_ref[...]
        pv_ref[0, 0:N_META, :] = vm_ref[...]

    hp = _layer_norm(x_ref[0], g0_ref[...], b0_ref[...])
    hp_ref[...] = hp
    hb = hp.astype(BF16)
    _project_keys(hb, wt_ref, brow_ref, ukey_ref)
    for t0 in range(0, tb, tb // TOKEN_SPLITS):
        tok = slice(t0, t0 + tb // TOKEN_SPLITS)
        _project_both(hb[tok], tok, wt_ref, bcol_ref, utb_ref, utf_ref)

    def feat(c0, n=M_DH):
        return slice(c0 - N_T_BF16, c0 - N_T_BF16 + n)

    li_ref[...], cumf_ref[...] = _gate_rows_t(utf_ref[feat(C_G, 8), :])

    def chunk_body(ci, carry):
        r0 = pl.multiple_of(ci * CHUNK, CHUNK)
        rows = pl.ds(r0, CHUNK)
        key = _iota2((CHUNK, CHUNK), 0)
        query = _iota2((CHUNK, CHUNK), 1)
        causal = key <= query

        gate_rows = pl.ds(pl.multiple_of(ci * 8, 8), 8)
        li_rows = li_ref[gate_rows, :]
        b_rows = cumf_ref[gate_rows, :]
        m_carry = []
        for hd in range(M_HEADS):
            q_tb = utb_ref[C_Q + hd * M_DH:C_Q + (hd + 1) * M_DH, rows]
            kb = (ukey_ref[rows, hd * M_DH:(hd + 1) * M_DH] * (M_DH ** -0.5)).astype(BF16)
            v_t = utf_ref[feat(C_V + hd * M_DH), rows]
            carry_hd, ct_new, m_new = _mlstm_scores_and_state(
                q_tb, kb, v_t, li_rows[hd:hd + 1], b_rows[M_HEADS + hd:M_HEADS + hd + 1],
                pm_ref[0, hd:hd + 1, :], ct_ref[hd], causal, CHUNK)
            ct_ref[hd] = ct_new
            pm_ref[0, hd:hd + 1, :] = jnp.broadcast_to(m_new, (1, 128))
            m_carry.append(carry_hd)

        k_own = ukey_ref[rows, C_AK - C_K:C_AK - C_K + KV_W]
        vt_own = utf_ref[feat(C_AV, KV_W), rows]
        k_prev = kprev_ref[...]
        vt_prev = vtprev_ref[...]
        prev_ok = (key > query) & ((j * n_chunks + ci) > 0)
        keys_kv = [tuple((_keep_kv_half(x, kv) * (A_DH ** -0.5)).astype(BF16) for x in (k_own, k_prev, km_ref[...]))
                   for kv in range(A_KV_HEADS)]
        values_kv = [tuple(x[kv * A_DH:(kv + 1) * A_DH].astype(BF16) for x in (vt_own, vt_prev, vmt_ref[...]))
                     for kv in range(A_KV_HEADS)]
        a_scores = []
        for hd in range(A_HEADS):
            kv = hd // A_GROUP
            q0 = C_AQ + (hd - kv) * A_DH
            q_win = utb_ref[q0:q0 + 2 * A_DH, rows]
            a_scores.append(tuple(_dot(kk, q_win) for kk in keys_kv[kv]))
        kprev_ref[...] = k_own
        vtprev_ref[...] = vt_own

        m_carry = [_mlstm_weighted_values(*c) for c in m_carry]
        a_out = [_swa_weighted_values_t(a_scores[hd], values_kv[hd // A_GROUP], sink_ref[hd], causal, prev_ok)
                 for hd in range(A_HEADS)]

        mix_t = []
        for hd in range(M_HEADS):
            mix_t.append(_mlstm_gate_head(_mlstm_finish(*m_carry[hd]), utf_ref[feat(C_O + hd * M_DH), rows],
                                          utf_ref[feat(C_Z + hd * M_DH), rows],
                                          ng_ref[hd * M_DH:(hd + 1) * M_DH, :], axis=0))
        for tile in range(A_HEADS // 2):
            o_t = jnp.concatenate([o * (1.0 / den) for o, den in a_out[2 * tile:2 * tile + 2]], axis=0)
            mix_t.append((o_t * _silu_of_twice(utf_ref[feat(C_AZ + tile * 128), rows])).astype(BF16))
        eye = jnp.where(key == query, 1.0, 0.0).astype(BF16)
        mix_rows = [_to_token_rows(x_t, eye) for x_t in mix_t]
        for i, x in enumerate(mix_rows):
            mix_ref[rows, i * 128:(i + 1) * 128] = x
        return carry

    lax.fori_loop(0, n_chunks, chunk_body, 0, unroll=CHUNK_UNROLL)

    for t0 in range(0, tb, tb // TOKEN_SPLITS):
        tok = slice(t0, t0 + tb // TOKEN_SPLITS)
        y_ref[0, tok, :] = _out_and_norm(hp_ref[tok, :], mix_ref[tok, :], wo_ref, lng_ref[...], lnb_ref[...])

    @pl.when(j == pl.num_programs(1) - 1)
    def _():
        pk_ref[0, N_META:N_BUF, :] = ukey_ref[tb - WINDOW:tb, C_AK - C_K:C_AK - C_K + KV_W]
        pv_ref[0, N_META:N_BUF, :] = utf_ref[feat(C_AV, KV_W), tb - WINDOW:tb].T
        for hd in range(M_HEADS):
            pc_ref[0, hd] = ct_ref[hd, 0:M_DH, :].T
            pn_ref[0, hd:hd + 1, :] = ct_ref[hd, M_DH:M_DH + 1, :]
        pn_ref[0, M_HEADS:8, :] = jnp.zeros((8 - M_HEADS, M_DH), F32)


def _const_spec(shape):
    return pl.BlockSpec(shape, lambda *_: (0,) * len(shape))


def _prompt_path(x_prompt, meta_tokens, ln0_g, ln0_b, w_t, b_all, sinks, norm_g, w_o, ln_g, ln_b, tb=512):
    batch, seq, _ = x_prompt.shape
    meta_pad = jnp.pad(meta_tokens.astype(F32), ((0, CHUNK - N_META), (0, 0)))
    ct0, m0, km, vm, vmt = pl.pallas_call(
        _meta_kernel,
        out_shape=(jax.ShapeDtypeStruct((M_HEADS, CT_ROWS, M_DH), F32),
                   jax.ShapeDtypeStruct((8, 128), F32),
                   jax.ShapeDtypeStruct((N_META, KV_W), F32),
                   jax.ShapeDtypeStruct((N_META, KV_W), F32),
                   jax.ShapeDtypeStruct((KV_W, N_META), F32)),
        scratch_shapes=[pltpu.VMEM((CHUNK, N_PAD), F32)],
        compiler_params=pltpu.CompilerParams(vmem_limit_bytes=VMEM_LIMIT_BYTES),
        name="meta_tokens",
    )(meta_pad, w_t, b_all, ln0_g, ln0_b)

    nj = seq // tb
    in_specs = [
        pl.BlockSpec(memory_space=pltpu.SMEM),
        pl.BlockSpec((1, tb, D_MODEL), lambda b, j: (b, j, 0)),
        _const_spec((N_PAD, D_MODEL)),
        _const_spec((N_PAD, 128)), _const_spec((1, N_PAD)),
        _const_spec((1, D_MODEL)), _const_spec((1, D_MODEL)),
        _const_spec((M_HEADS, CT_ROWS, M_DH)), _const_spec((8, 128)),
        _const_spec((N_META, KV_W)), _const_spec((N_META, KV_W)), _const_spec((KV_W, N_META)),
        _const_spec((D_MLSTM, 128)),
        _const_spec((D_MODEL, D_MODEL)),
        _const_spec((1, D_MODEL)), _const_spec((1, D_MODEL)),
    ]
    out_specs = [
        pl.BlockSpec((1, tb, D_MODEL), lambda b, j: (b, j, 0)),
        pl.BlockSpec((1, N_BUF, KV_W), lambda b, j: (b, 0, 0)),
        pl.BlockSpec((1, N_BUF, KV_W), lambda b, j: (b, 0, 0)),
        pl.BlockSpec((1, M_HEADS, M_DH, M_DH), lambda b, j: (b, 0, 0, 0)),
        pl.BlockSpec((1, 8, M_DH), lambda b, j: (b, 0, 0)),
        pl.BlockSpec((1, 8, 128), lambda b, j: (b, 0, 0)),
    ]
    out_shape = (
        jax.ShapeDtypeStruct((batch, seq, D_MODEL), F32),
        jax.ShapeDtypeStruct((batch, N_BUF, KV_W), F32),
        jax.ShapeDtypeStruct((batch, N_BUF, KV_W), F32),
        jax.ShapeDtypeStruct((batch, M_HEADS, M_DH, M_DH), F32),
        jax.ShapeDtypeStruct((batch, 8, M_DH), F32),
        jax.ShapeDtypeStruct((batch, 8, 128), F32),
    )
    y, pk, pv, pc, pn, pm = pl.pallas_call(
        functools.partial(_prompt_kernel, tb=tb),
        grid=(batch, nj),
        in_specs=in_specs,
        out_specs=out_specs,
        out_shape=out_shape,
        scratch_shapes=[pltpu.VMEM((N_T_BF16, tb), BF16), pltpu.VMEM((N_T - N_T_BF16, tb), F32),
                        pltpu.VMEM((tb, N_KEYS), F32),
                        pltpu.VMEM((tb, D_MODEL), F32), pltpu.VMEM((tb, D_MODEL), BF16),
                        pltpu.VMEM((M_HEADS, CT_ROWS, M_DH), F32),
                        pltpu.VMEM((CHUNK, KV_W), F32), pltpu.VMEM((KV_W, CHUNK), F32),
                        pltpu.VMEM((tb // CHUNK * 8, CHUNK), F32), pltpu.VMEM((tb // CHUNK * 8, CHUNK), F32)],
        compiler_params=pltpu.CompilerParams(dimension_semantics=("arbitrary", "arbitrary"),
                                             vmem_limit_bytes=VMEM_LIMIT_BYTES),
        name="prompt_layer",
    )(sinks, x_prompt, w_t, jnp.broadcast_to(b_all.reshape(N_PAD, 1), (N_PAD, 128)), b_all, ln0_g, ln0_b,
      ct0, m0, km, vm, vmt, jnp.broadcast_to(norm_g.reshape(D_MLSTM, 1), (D_MLSTM, 128)), w_o, ln_g, ln_b)
    pk = pk.reshape(1, batch, N_BUF, A_KV_HEADS, A_DH)
    pv = pv.reshape(1, batch, N_BUF, A_KV_HEADS, A_DH)
    return y, pk, pv, pc[None], pn[:, :M_HEADS][None], pm[:, :M_HEADS, 0][None]


SEQ_PER_GROUP = 32
SWA_SEQ_PER_STEP = 8
SEQ_BATCH = 8


def _sample_proj_kernel(x_ref, wt_ref, bias_ref, g0_ref, b0_ref, hs_ref, u_ref):
    hs = _layer_norm(x_ref[...], g0_ref[...], b0_ref[...])
    hs_ref[...] = hs
    _project(hs.astype(BF16), wt_ref, bias_ref, u_ref)


def _segment_last(x, pos, seg_len):
    n = x.shape[1]
    step = 1
    while step < seg_len:
        x = jnp.where((pos // step) % 2 == 0, pltpu.roll(x, n - step, 1), x)
        step *= 2
    return x


def _sample_mlstm_kernel(q_ref, k_ref, v_ref, g_ref, c_ref, n_ref, m_ref,
                         h_ref, cn_ref, nn_ref, mn_ref, inter_ref, *, dec_seq):
    hd = pl.program_id(1)
    n = CHUNK
    nb = n // dec_seq
    q = q_ref[...]
    k = k_ref[...] * (M_DH ** -0.5)
    v = v_ref[...]
    qb = q.astype(BF16)
    kb = k.astype(BF16)
    vb = v.astype(BF16)

    g_t = g_ref[...].T[0:8, :]
    row8 = _iota2((8, n), 0)
    li_row = jnp.sum(jnp.where(row8 == hd, g_t, 0.0), axis=0, keepdims=True)
    lf_row = _log_sigmoid(jnp.sum(jnp.where(row8 == hd + M_HEADS, g_t, 0.0), axis=0, keepdims=True))
    lane = _iota2((1, n), 1)
    pos = lane % dec_seq
    b_row = lf_row
    shift = 1
    while shift < dec_seq:
        b_row = b_row + jnp.where(pos >= shift, pltpu.roll(b_row, shift, 1), 0.0)
        shift *= 2

    key = _iota2((n, n), 0)
    query = _iota2((n, n), 1)
    key_ok = (key <= query) & (key // dec_seq == query // dec_seq)
    a_row = li_row - b_row
    a_keys = jnp.broadcast_to(jnp.sum(jnp.where(key == query, a_row, 0.0), axis=1, keepdims=True), (n, n))
    a_t = jnp.where(key_ok, a_keys, NEG_INF)
    m_old = m_ref[0, 0, 0:1, :]
    mm = jnp.maximum(m_old, jnp.max(a_t, axis=0, keepdims=True))
    w_t = jnp.exp(a_t - mm)
    mm_last = _segment_last(mm, pos, dec_seq)
    m_new = _segment_last(b_row, pos, dec_seq) + mm_last
    w_state = jnp.exp(a_row - mm_last)
    decay = jnp.exp(m_old - mm_last)
    mn_ref[0, 0] = jnp.broadcast_to(m_new, (8, n))

    qkw_t = _dot_nt(kb, qb) * w_t
    den = jnp.sum(qkw_t, axis=0, keepdims=True)
    num_t = _dot(v.T.astype(BF16), qkw_t.astype(BF16))

    n_seq = n_ref[0]
    seq_of_lane = _iota2((nb, n), 1) // dec_seq == _iota2((nb, n), 0)
    decay_seq = jnp.sum(jnp.where(_iota2((nb, n), 1) == _iota2((nb, n), 0) * dec_seq, decay, 0.0),
                        axis=1, keepdims=True)
    nn_ref[0] = decay_seq * n_seq + _dot(jnp.where(seq_of_lane, w_state, 0.0).astype(BF16), kb)
    expand = jnp.where(_iota2((n, nb), 0) // dec_seq == _iota2((n, nb), 1), 1.0, 0.0).astype(BF16)
    n_rows = _dot(expand, n_seq.astype(BF16))
    qn_col = jnp.sum(q * n_rows, axis=1, keepdims=True)
    qn = jnp.sum(jnp.where(key == query, qn_col, 0.0), axis=0, keepdims=True)

    kwt = k.T * w_state
    decay_rows = jnp.broadcast_to(jnp.sum(jnp.where(key == query, decay, 0.0), axis=1, keepdims=True), (n, n))
    lane_seq = query // dec_seq
    low_rows = _iota2((8, M_DH), 0) < dec_seq
    assert 8 % dec_seq == 0 and 8 // dec_seq == 2
    for pair in range(nb // 2):
        q8 = q[8 * pair:8 * pair + 8].astype(BF16)
        inter_ref[8 * pair:8 * pair + 8, :] = jnp.where(low_rows, _dot(q8, c_ref[2 * pair, 0].astype(BF16)),
                                                        _dot(q8, c_ref[2 * pair + 1, 0].astype(BF16)))
    for s0 in range(0, nb, SEQ_BATCH):
        seqs = range(s0, s0 + SEQ_BATCH)
        updates = [_dot(jnp.where(lane_seq == s, kwt, 0.0).astype(BF16), vb) for s in seqs]
        for s, upd in zip(seqs, updates):
            cn_ref[s, 0] = decay_rows[s * dec_seq:s * dec_seq + 1, :] * c_ref[s, 0] + upd

    s_inter = jnp.exp(m_old - mm)
    num_t = num_t + inter_ref[...].T * s_inter
    den = den + qn * s_inter
    h_ref[...] = (num_t * (1.0 / jnp.maximum(jnp.abs(den), jnp.exp(-(b_row + mm))))).T


def _sample_swa_kernel(qz_ref, kn_ref, vn_ref, ck_ref, cv_ref, sink_ref, o_ref, nk_ref, nv_ref, *, dec_seq):
    n_rows = qz_ref.shape[1]
    t_c = _iota2((n_rows, N_BUF), 0) % dec_seq
    i_c = _iota2((n_rows, N_BUF), 1)
    ok_c = (i_c < N_META) | (i_c > t_c + N_META)
    t_n = _iota2((n_rows, 8), 0) % dec_seq
    i_n = _iota2((n_rows, 8), 1)
    ok_n = (i_n <= t_n) & (i_n < dec_seq)
    sink = sink_ref[:, 0:1]

    seqs = range(qz_ref.shape[0])
    scores = []
    for s in seqs:
        qz = (qz_ref[s] * (A_DH ** -0.5)).astype(BF16)
        scores.append((_dot_nt(qz, ck_ref[s].astype(BF16)), _dot_nt(qz, kn_ref[s].astype(BF16))))
    probs = []
    for s_c, s_n in scores:
        s_c = jnp.where(ok_c, s_c, NEG_INF)
        s_n = jnp.where(ok_n, s_n, NEG_INF)
        mx = jnp.maximum(jnp.maximum(jnp.max(s_c, axis=1, keepdims=True),
                                     jnp.max(s_n, axis=1, keepdims=True)), sink)
        p_c = jnp.exp(s_c - mx)
        p_n = jnp.exp(s_n - mx)
        den = jnp.sum(p_c, axis=1, keepdims=True) + jnp.sum(p_n, axis=1, keepdims=True) + jnp.exp(sink - mx)
        probs.append((p_c.astype(BF16), p_n.astype(BF16), den))
    outs = [_dot(p_c, cv_ref[s].astype(BF16)) + _dot(p_n, vn_ref[s].astype(BF16))
            for s, (p_c, p_n, _) in zip(seqs, probs)]
    for s, o, (_, _, den) in zip(seqs, outs, probs):
        o_ref[s] = o / den
    for s in seqs:
        for cache_ref, new_ref, out_ref in ((ck_ref, kn_ref, nk_ref), (cv_ref, vn_ref, nv_ref)):
            out_ref[s, 0:N_META, :] = cache_ref[s, 0:N_META, :]
            out_ref[s, N_META:N_BUF - dec_seq, :] = cache_ref[s, N_META + dec_seq:N_BUF, :]
            out_ref[s, N_BUF - dec_seq:N_BUF, :] = new_ref[s, 0:dec_seq, :]


def _sample_out_kernel(hs_ref, u_ref, hm_ref, oa_ref, ng_ref, wo_ref, lng_ref, lnb_ref, y_ref):
    mix = _gate_mix(hm_ref[...], u_ref[:, C_O:C_O + D_MLSTM], u_ref[:, C_Z:C_Z + D_MLSTM],
                    oa_ref[...], u_ref[:, C_AZ:C_AZ + D_SWA], ng_ref[...])
    y_ref[...] = _out_and_norm(hs_ref[...], mix, wo_ref, lng_ref[...], lnb_ref[...])


def _sample_path(x_sample, cache_k, cache_v, state_c, state_n, state_m,
                 ln0_g, ln0_b, w_t, b_all, a_sinks, norm_g, w_o, ln_g, ln_b):
    db, dec_seq, _ = x_sample.shape
    rows = db * dec_seq
    params = pltpu.CompilerParams(vmem_limit_bytes=VMEM_LIMIT_BYTES)
    hs, u = pl.pallas_call(
        _sample_proj_kernel,
        out_shape=(jax.ShapeDtypeStruct((rows, D_MODEL), F32), jax.ShapeDtypeStruct((rows, N_PAD), F32)),
        compiler_params=params,
        name="sample_proj",
    )(x_sample.reshape(rows, D_MODEL), w_t, b_all, ln0_g, ln0_b)

    n_groups = db // SEQ_PER_GROUP
    n_t = jnp.transpose(state_n, (1, 0, 2))
    m_t = jnp.repeat(jnp.transpose(state_m, (1, 0)), dec_seq, axis=1).reshape(M_HEADS, n_groups, 1, CHUNK)
    m_t = jnp.broadcast_to(m_t, (M_HEADS, n_groups, 8, CHUNK))

    def col_spec(col0):
        return pl.BlockSpec((CHUNK, M_DH), lambda g, h: (g, col0 // M_DH + h))

    state_spec = pl.BlockSpec((SEQ_PER_GROUP, 1, M_DH, M_DH), lambda g, h: (g, h, 0, 0))
    vec_spec = pl.BlockSpec((1, SEQ_PER_GROUP, M_DH), lambda g, h: (h, g, 0))
    m_spec = pl.BlockSpec((1, 1, 8, CHUNK), lambda g, h: (h, g, 0, 0))
    h_m, c_new, n_new, m_new = pl.pallas_call(
        functools.partial(_sample_mlstm_kernel, dec_seq=dec_seq),
        grid=(n_groups, M_HEADS),
        in_specs=[col_spec(C_Q), col_spec(C_K), col_spec(C_V),
                  pl.BlockSpec((CHUNK, 128), lambda g, h: (g, C_G // 128)),
                  state_spec, vec_spec, m_spec],
        out_specs=[pl.BlockSpec((CHUNK, M_DH), lambda g, h: (g, h)), state_spec, vec_spec, m_spec],
        out_shape=(jax.ShapeDtypeStruct((rows, D_MLSTM), F32),
                   jax.ShapeDtypeStruct(state_c.shape, F32),
                   jax.ShapeDtypeStruct((M_HEADS, db, M_DH), F32),
                   jax.ShapeDtypeStruct((M_HEADS, n_groups, 8, CHUNK), F32)),
        scratch_shapes=[pltpu.VMEM((CHUNK, M_DH), F32)],
        compiler_params=pltpu.CompilerParams(dimension_semantics=("arbitrary", "arbitrary"),
                                             vmem_limit_bytes=VMEM_LIMIT_BYTES),
        name="sample_mlstm",
    )(u, u, u, u, state_c, n_t, m_t)

    aq = u[:, C_AQ:C_AQ + D_SWA].reshape(db, dec_seq, A_KV_HEADS, A_GROUP, A_DH)
    aq = jnp.transpose(aq, (0, 2, 3, 1, 4)).reshape(db, A_KV_HEADS, A_GROUP * dec_seq, A_DH)
    zeros = jnp.zeros_like(aq[:, 0])
    qz = jnp.stack([jnp.concatenate([aq[:, 0], zeros], axis=-1),
                    jnp.concatenate([zeros, aq[:, 1]], axis=-1)], axis=1)
    n_qrows = A_HEADS * dec_seq
    qz = qz.reshape(db, n_qrows, KV_W)
    k_new = jnp.pad(u[:, C_AK:C_AK + KV_W].reshape(db, dec_seq, KV_W), ((0, 0), (0, 8 - dec_seq), (0, 0)))
    v_new = jnp.pad(u[:, C_AV:C_AV + KV_W].reshape(db, dec_seq, KV_W), ((0, 0), (0, 8 - dec_seq), (0, 0)))
    sink_rows = jnp.broadcast_to(jnp.repeat(a_sinks.astype(F32), dec_seq)[:, None], (n_qrows, 128))
    ck = cache_k.reshape(db, N_BUF, KV_W)
    cv = cache_v.reshape(db, N_BUF, KV_W)
    sb = SWA_SEQ_PER_STEP

    def seq_spec(r):
        return pl.BlockSpec((sb, r, KV_W), lambda i: (i, 0, 0))

    o, nk, nv = pl.pallas_call(
        functools.partial(_sample_swa_kernel, dec_seq=dec_seq),
        grid=(db // sb,),
        in_specs=[seq_spec(n_qrows), seq_spec(8), seq_spec(8), seq_spec(N_BUF), seq_spec(N_BUF),
                  pl.BlockSpec((n_qrows, 128), lambda i: (0, 0))],
        out_specs=[seq_spec(n_qrows), seq_spec(N_BUF), seq_spec(N_BUF)],
        out_shape=(jax.ShapeDtypeStruct((db, n_qrows, KV_W), F32),
                   jax.ShapeDtypeStruct((db, N_BUF, KV_W), F32),
                   jax.ShapeDtypeStruct((db, N_BUF, KV_W), F32)),
        compiler_params=pltpu.CompilerParams(dimension_semantics=("arbitrary",),
                                             vmem_limit_bytes=VMEM_LIMIT_BYTES),
        name="sample_swa",
    )(qz, k_new, v_new, ck, cv, sink_rows)
    o = o.reshape(db, A_KV_HEADS, A_GROUP, dec_seq, A_KV_HEADS, A_DH)
    o = jnp.stack([o[:, 0, :, :, 0, :], o[:, 1, :, :, 1, :]], axis=1)
    o_a = jnp.transpose(o, (0, 3, 1, 2, 4)).reshape(rows, D_SWA)

    y = pl.pallas_call(
        _sample_out_kernel,
        out_shape=jax.ShapeDtypeStruct((rows, D_MODEL), F32),
        compiler_params=params,
        name="sample_out",
    )(hs, u, h_m, o_a, norm_g, w_o, ln_g, ln_b)

    shape5 = (1, db, N_BUF, A_KV_HEADS, A_DH)
    return (y.reshape(db, dec_seq, D_MODEL), nk.reshape(shape5), nv.reshape(shape5), c_new[None],
            jnp.transpose(n_new, (1, 0, 2))[None], jnp.transpose(m_new[:, :, 0, ::dec_seq].reshape(M_HEADS, db), (1, 0))[None])


def kernel(x_prompt, x_sample, cache_swa_k, cache_swa_v, state_mlstm_c, state_mlstm_n, state_mlstm_m,
           meta_tokens, ln0_g, ln0_b, w_in, b_in, a_sinks, m_norm_g, w_out, ln_g, ln_b):
    assert w_in.shape[0] == DEPTH and x_prompt.shape[-1] == D_MODEL
    w_raw = jnp.transpose(w_in[0].astype(F32))
    b_raw = b_in[0].astype(F32)[:, None]

    def rearranged(p):
        def take(raw, scale=1.0):
            return p[raw[0]:raw[0] + raw[1]] * scale
        zeros = lambda n: jnp.zeros((n, p.shape[1]), F32)
        return jnp.concatenate([take(RAW_Q), take(RAW_AQ), take(RAW_AV), take(RAW_V), take(RAW_O, 0.5),
                                take(RAW_Z, 0.5), take(RAW_AZ, 0.5), take(RAW_G), zeros(C_K - C_G - RAW_G[1]),
                                take(RAW_K), take(RAW_AK)], axis=0)

    w_t = rearranged(w_raw).astype(BF16)
    b_all = rearranged(b_raw).reshape(1, N_PAD)
    w_o = w_out[0].astype(BF16)
    g0 = ln0_g.astype(F32)[None]
    b0 = ln0_b.astype(F32)[None]
    lg = ln_g[0].astype(F32)[None]
    lb = ln_b[0].astype(F32)[None]
    norm_g = m_norm_g[0].astype(F32)[None]
    sinks = a_sinks[0].astype(F32)

    y_p, pk, pv, pc, pn, pm = _prompt_path(x_prompt, meta_tokens, g0, b0, w_t, b_all, sinks, norm_g, w_o, lg, lb)
    y_s, sk, sv, sc, sn, sm = _sample_path(x_sample, cache_swa_k[0], cache_swa_v[0], state_mlstm_c[0],
                                           state_mlstm_n[0], state_mlstm_m[0],
                                           g0, b0, w_t, b_all, sinks, norm_g, w_o, lg, lb)
    return (y_p, y_s, pk, pv, pc, pn, pm, sk, sv, sc, sn, sm)
```

```python
import functools

import jax
import jax.numpy as jnp
from jax import lax
from jax.experimental import pallas as pl
from jax.experimental.pallas import tpu as pltpu

F32 = jnp.float32
BF16 = jnp.bfloat16

D_MODEL = 1024
N_META = 16
M_HEADS = 4
M_DH = 128
D_MLSTM = M_HEADS * M_DH
A_HEADS = 8
A_KV_HEADS = 2
A_GROUP = A_HEADS // A_KV_HEADS
A_DH = 64
D_SWA = A_HEADS * A_DH
KV_W = A_KV_HEADS * A_DH
WINDOW = 128
CHUNK = 128
LN_EPS = 1e-5
DEPTH = 1
DN_ALPHA = (2.0 * DEPTH) ** 0.25
N_BUF = N_META + WINDOW

C_Q, C_AQ = 0, 512
N_T_BF16 = 1024
C_V, C_O, C_Z, C_AZ, C_AV, C_G = 1024, 1536, 2048, 2560, 3072, 3200
GATE_ROWS = 16
N_T = C_G + GATE_ROWS
C_K, C_AK = 3328, 3840
N_PAD = 3968
N_KEYS = N_PAD - C_K
RAW_Q, RAW_K, RAW_V, RAW_O, RAW_Z, RAW_G = (0, 512), (512, 512), (1024, 512), (1536, 512), (2048, 512), (2560, 8)
RAW_AQ, RAW_AK, RAW_AV, RAW_AZ = (2568, 512), (3080, 128), (3208, 128), (3336, 512)
N_RAW = 3848

PROJ_STEP = 512
TOKEN_SPLITS = 2
CHUNK_UNROLL = 4
VMEM_LIMIT_BYTES = 56 * 1024 * 1024
NEG_INF = float("-inf")


def _dot(a, b):
    return jnp.dot(a, b, preferred_element_type=F32)


def _dot_nt(a, b):
    return lax.dot_general(a, b, (((1,), (1,)), ((), ())), preferred_element_type=F32)


def _layer_norm(x, g, b):
    mu = jnp.mean(x, axis=-1, keepdims=True)
    xc = x - mu
    var = jnp.mean(xc * xc, axis=-1, keepdims=True)
    return xc * lax.rsqrt(var + LN_EPS) * g + b


def _log_sigmoid(x):
    return jnp.minimum(x, 0.0) - jnp.log1p(jnp.exp(-jnp.abs(x)))


def _times_sigmoid(h, half_x):
    return 0.5 * (h * jnp.tanh(half_x) + h)


def _silu_of_twice(half_x):
    return half_x * jnp.tanh(half_x) + half_x


def _iota2(shape, dim):
    return lax.broadcasted_iota(jnp.int32, shape, dim)


T_FEATURES = ((RAW_G, C_G, 1.0), (RAW_Q, C_Q, 1.0), (RAW_AQ, C_AQ, 1.0), (RAW_AV, C_AV, 1.0), (RAW_V, C_V, 1.0),
              (RAW_O, C_O, 0.5), (RAW_Z, C_Z, 0.5), (RAW_AZ, C_AZ, 0.5))
KEY_FEATURES = ((RAW_K, C_K, 1.0), (RAW_AK, C_AK, 1.0))


def _weights(wt_ref, raw):
    return wt_ref[raw[0]:raw[0] + raw[1], :].astype(BF16)


def _project(token_sets, wt_ref, b_ref):
    for _, u_ref in token_sets:
        u_ref[:, C_G:C_K] = jnp.zeros((u_ref.shape[0], C_K - C_G), F32)
    for raw, dst, scale in T_FEATURES + KEY_FEATURES:
        w = _weights(wt_ref, raw)
        for hb, u_ref in token_sets:
            res = _dot_nt(hb, w) + b_ref[:, raw[0]:raw[0] + raw[1]]
            u_ref[:, dst:dst + raw[1]] = res if scale == 1.0 else res * scale


def _project_both(hb, tok, wt_ref, bcol_ref, utb_ref, utf_ref):
    reps = hb.shape[0] // 128
    assert RAW_Z[0] + RAW_Z[1] == RAW_G[0]
    for raw, dst, scale in ((RAW_Z[0], RAW_Z[1] + RAW_G[1]), C_Z, 0.5), *T_FEATURES:
        if raw in (RAW_Z, RAW_G):
            continue
        res = _dot_nt(_weights(wt_ref, raw), hb) + jnp.concatenate([bcol_ref[raw[0]:raw[0] + raw[1], :]] * reps, axis=1)
        if raw[1] > RAW_Z[1]:
            utf_ref[C_G - N_T_BF16:C_G - N_T_BF16 + RAW_G[1], tok] = res[RAW_Z[1]:]
            res = res[0:RAW_Z[1]]
        if scale != 1.0:
            res = res * scale
        if dst < N_T_BF16:
            utb_ref[dst:dst + res.shape[0], tok] = res.astype(BF16)
        else:
            utf_ref[dst - N_T_BF16:dst - N_T_BF16 + res.shape[0], tok] = res


def _project_keys(hb, wt_ref, brow_ref, ukey_ref):
    for raw, dst, _ in KEY_FEATURES:
        ukey_ref[:, dst - C_K:dst - C_K + raw[1]] = (_dot_nt(hb, _weights(wt_ref, raw))
                                                     + brow_ref[:, raw[0]:raw[0] + raw[1]])


def _gate_rows(gates):
    return _gate_scan(jnp.concatenate([gates[r0:r0 + CHUNK].T[0:8, :] for r0 in range(0, gates.shape[0], CHUNK)],
                                      axis=0))


def _gate_rows_t(gates_t):
    return _gate_scan(jnp.concatenate([gates_t[:, c0:c0 + CHUNK] for c0 in range(0, gates_t.shape[1], CHUNK)],
                                      axis=0))


def _gate_scan(g_t):
    x = _log_sigmoid(g_t)
    lane = _iota2(x.shape, 1)
    shift = 1
    while shift < x.shape[1]:
        x = x + jnp.where(lane >= shift, pltpu.roll(x, shift, 1), 0.0)
        shift *= 2
    return g_t, x


CT_ROWS = M_DH + 8


def _mlstm_scores_and_state(q_tb, kb, v_t, li_row, b_row, m_old, ct_aug, key_ok, n_valid):
    n = kb.shape[0]
    row = _iota2((n, n), 0)
    col = _iota2((n, n), 1)
    a_row = li_row - b_row
    a_keys = jnp.broadcast_to(jnp.sum(jnp.where(row == col, a_row, 0.0), axis=1, keepdims=True), (n, n))
    a_t = jnp.where(key_ok, a_keys, NEG_INF)
    mm = jnp.maximum(m_old, jnp.max(a_t, axis=0, keepdims=True))
    w_t = jnp.exp(a_t - mm)
    lane = _iota2((1, n), 1)
    last = n_valid - 1
    mm_last = jnp.max(jnp.where(lane == last, mm, NEG_INF), axis=1, keepdims=True)
    m_new = jnp.sum(jnp.where(lane == last, b_row, 0.0), axis=1, keepdims=True) + mm_last
    w_state = jnp.exp(a_row - mm_last)
    if n_valid < n:
        w_state = jnp.where(lane < n_valid, w_state, 0.0)
    decay = jnp.exp(m_old - mm_last)
    scores_t = _dot(kb, q_tb)
    inter = _dot(ct_aug.astype(BF16), q_tb)
    ones_row = jnp.where(_iota2((CT_ROWS - M_DH, n), 0) == 0, w_state, 0.0)
    vtw = jnp.concatenate([v_t * w_state, ones_row], axis=0).astype(BF16)
    ct_aug_new = decay * ct_aug + _dot(vtw, kb)
    s_inter = jnp.exp(m_old - mm)
    floor = jnp.exp(-(b_row + mm))
    return (scores_t, w_t, v_t.astype(BF16), inter, s_inter, floor), ct_aug_new, m_new


def _mlstm_weighted_values(scores_t, w_t, vtb, inter, s_inter, floor):
    qkw_t = scores_t * w_t
    den = jnp.sum(qkw_t, axis=0, keepdims=True)
    num_t = _dot(vtb, qkw_t.astype(BF16))
    return num_t, den, inter, s_inter, floor


def _mlstm_finish(num_t, den, inter, s_inter, floor):
    num_t = num_t + inter[0:M_DH] * s_inter
    den = den + inter[M_DH:M_DH + 1] * s_inter
    return num_t * (1.0 / jnp.maximum(jnp.abs(den), floor))


def _mlstm_gate_head(h, half_o, half_z, norm_g, axis):
    hh = _times_sigmoid(h, half_o)
    mu = jnp.mean(hh, axis=axis, keepdims=True)
    hc = hh - mu
    var = jnp.mean(hc * hc, axis=axis, keepdims=True)
    return (hc * lax.rsqrt(var + LN_EPS) * norm_g * _silu_of_twice(half_z)).astype(BF16)


def _gate_mix(h_m, half_o, half_z, o_a, half_az, norm_g):
    parts = []
    for hd in range(M_HEADS):
        sl = slice(hd * M_DH, (hd + 1) * M_DH)
        parts.append(_mlstm_gate_head(h_m[:, sl], half_o[:, sl], half_z[:, sl], norm_g[:, sl], axis=-1))
    parts.append((o_a * _silu_of_twice(half_az)).astype(BF16))
    return jnp.concatenate(parts, axis=-1)


def _out_and_norm(hp, mix, wo_ref, g, b):
    z = DN_ALPHA * hp + _dot(mix, wo_ref[...])
    return _layer_norm(z, g, b)


def _small_projections_kernel(meta_ref, xs_ref, wt_ref, bias_ref, g0_ref, b0_ref,
                              ct0_ref, m0_ref, km_ref, vm_ref, vmt_ref, hs_ref, us_ref, u_ref):
    hs = _layer_norm(xs_ref[...], g0_ref[...], b0_ref[...])
    hs_ref[...] = hs
    meta = jnp.concatenate([meta_ref[...], jnp.zeros((CHUNK - N_META, D_MODEL), F32)], axis=0)
    hp = _layer_norm(meta, g0_ref[...], b0_ref[...])
    _project(((hs.astype(BF16), us_ref), (hp.astype(BF16), u_ref)), wt_ref, bias_ref)
    row = _iota2((CHUNK, CHUNK), 0)
    col = _iota2((CHUNK, CHUNK), 1)
    key_ok = (row <= col) & (row < N_META)
    li_rows, b_rows = _gate_rows(u_ref[:, C_G:C_G + 128])
    zero_m = jnp.zeros((1, 128), F32)
    zero_ct = jnp.zeros((CT_ROWS, M_DH), F32)
    m0_ref[...] = jnp.zeros(m0_ref.shape, F32)
    for hd in range(M_HEADS):
        q = u_ref[:, C_Q + hd * M_DH:C_Q + (hd + 1) * M_DH]
        k = u_ref[:, C_K + hd * M_DH:C_K + (hd + 1) * M_DH] * (M_DH ** -0.5)
        v = u_ref[:, C_V + hd * M_DH:C_V + (hd + 1) * M_DH]
        _, ct_new, m_new = _mlstm_scores_and_state(
            q.T.astype(BF16), k.astype(BF16), v.T, li_rows[hd:hd + 1], b_rows[M_HEADS + hd:M_HEADS + hd + 1],
            zero_m, zero_ct, key_ok, N_META)
        ct0_ref[hd] = ct_new
        m0_ref[hd:hd + 1, :] = jnp.broadcast_to(m_new, (1, 128))
    km_ref[...] = u_ref[0:N_META, C_AK:C_AK + KV_W]
    vm_ref[...] = u_ref[0:N_META, C_AV:C_AV + KV_W]
    vmt_ref[...] = u_ref[:, C_AV:C_AV + KV_W].T[:, 0:N_META]


def _keep_kv_half(x, kv):
    low = _iota2(x.shape, 1) < A_DH
    return jnp.where(low if kv == 0 else ~low, x, 0.0)


def _to_token_rows(x_t, eye):
    return _dot_nt(eye, x_t).astype(BF16)


def _swa_weighted_values_t(scores, values_t, sink, own_ok, prev_ok):
    s_own = jnp.where(own_ok, scores[0], NEG_INF)
    s_prev = jnp.where(prev_ok, scores[1], NEG_INF)
    s_meta = scores[2]
    mx = jnp.maximum(jnp.maximum(jnp.max(s_own, axis=0, keepdims=True),
                                 jnp.max(s_prev, axis=0, keepdims=True)),
                     jnp.maximum(jnp.max(s_meta, axis=0, keepdims=True), sink))
    p_own, p_prev, p_meta = (jnp.exp(s - mx) for s in (s_own, s_prev, s_meta))
    den = (jnp.sum(p_own, axis=0, keepdims=True) + jnp.sum(p_prev, axis=0, keepdims=True)
           + jnp.sum(p_meta, axis=0, keepdims=True) + jnp.exp(sink - mx))
    o_t = (_dot(values_t[0], p_own.astype(BF16)) + _dot(values_t[1], p_prev.astype(BF16))
           + _dot(values_t[2], p_meta.astype(BF16)))
    return o_t, den


def _prompt_kernel(sink_ref, x_ref, wt_ref, bcol_ref, brow_ref, g0_ref, b0_ref,
                   ct0_ref, m0_ref, km_ref, vm_ref, vmt_ref, ng_ref, wo_ref, lng_ref, lnb_ref,
                   y_ref, pk_ref, pv_ref, pc_ref, pn_ref, pm_ref,
                   utb_ref, utf_ref, ukey_ref, hp_ref, mix_ref, ct_ref, kprev_ref, vtprev_ref, li_ref, cumf_ref,
                   *, tb):
    j = pl.program_id(1)
    n_chunks = tb // CHUNK

    @pl.when(j == 0)
    def _():
        ct_ref[...] = ct0_ref[...]
        pm_ref[0] = m0_ref[...]
        kprev_ref[...] = jnp.zeros(kprev_ref.shape, F32)
        vtprev_ref[...] = jnp.zeros(vtprev_ref.shape, F32)
        pk_ref[0, 0:N_META, :] = km_ref[...]
        pv_ref[0, 0:N_META, :] = vm_ref[...]

    hp = _layer_norm(x_ref[0], g0_ref[...], b0_ref[...])
    hp_ref[...] = hp
    hb = hp.astype(BF16)
    _project_keys(hb, wt_ref, brow_ref, ukey_ref)
    for t0 in range(0, tb, tb // TOKEN_SPLITS):
        tok = slice(t0, t0 + tb // TOKEN_SPLITS)
        _project_both(hb[tok], tok, wt_ref, bcol_ref, utb_ref, utf_ref)

    def feat(c0, n=M_DH):
        return slice(c0 - N_T_BF16, c0 - N_T_BF16 + n)

    li_ref[...], cumf_ref[...] = _gate_rows_t(utf_ref[feat(C_G, 8), :])

    def chunk_body(ci, carry):
        r0 = pl.multiple_of(ci * CHUNK, CHUNK)
        rows = pl.ds(r0, CHUNK)
        key = _iota2((CHUNK, CHUNK), 0)
        query = _iota2((CHUNK, CHUNK), 1)
        causal = key <= query

        gate_rows = pl.ds(pl.multiple_of(ci * 8, 8), 8)
        li_rows = li_ref[gate_rows, :]
        b_rows = cumf_ref[gate_rows, :]
        m_carry = []
        for hd in range(M_HEADS):
            q_tb = utb_ref[C_Q + hd * M_DH:C_Q + (hd + 1) * M_DH, rows]
            kb = (ukey_ref[rows, hd * M_DH:(hd + 1) * M_DH] * (M_DH ** -0.5)).astype(BF16)
            v_t = utf_ref[feat(C_V + hd * M_DH), rows]
            carry_hd, ct_new, m_new = _mlstm_scores_and_state(
                q_tb, kb, v_t, li_rows[hd:hd + 1], b_rows[M_HEADS + hd:M_HEADS + hd + 1],
                pm_ref[0, hd:hd + 1, :], ct_ref[hd], causal, CHUNK)
            ct_ref[hd] = ct_new
            pm_ref[0, hd:hd + 1, :] = jnp.broadcast_to(m_new, (1, 128))
            m_carry.append(carry_hd)

        k_own = ukey_ref[rows, C_AK - C_K:C_AK - C_K + KV_W]
        vt_own = utf_ref[feat(C_AV, KV_W), rows]
        k_prev = kprev_ref[...]
        vt_prev = vtprev_ref[...]
        prev_ok = (key > query) & ((j * n_chunks + ci) > 0)
        keys_kv = [tuple((_keep_kv_half(x, kv) * (A_DH ** -0.5)).astype(BF16) for x in (k_own, k_prev, km_ref[...]))
                   for kv in range(A_KV_HEADS)]
        values_kv = [tuple(x[kv * A_DH:(kv + 1) * A_DH].astype(BF16) for x in (vt_own, vt_prev, vmt_ref[...]))
                     for kv in range(A_KV_HEADS)]
        a_scores = []
        for hd in range(A_HEADS):
            kv = hd // A_GROUP
            q0 = C_AQ + (hd - kv) * A_DH
            q_win = utb_ref[q0:q0 + 2 * A_DH, rows]
            a_scores.append(tuple(_dot(kk, q_win) for kk in keys_kv[kv]))
        kprev_ref[...] = k_own
        vtprev_ref[...] = vt_own

        m_carry = [_mlstm_weighted_values(*c) for c in m_carry]
        a_out = [_swa_weighted_values_t(a_scores[hd], values_kv[hd // A_GROUP], sink_ref[hd], causal, prev_ok)
                 for hd in range(A_HEADS)]

        mix_t = []
        for hd in range(M_HEADS):
            mix_t.append(_mlstm_gate_head(_mlstm_finish(*m_carry[hd]), utf_ref[feat(C_O + hd * M_DH), rows],
                                          utf_ref[feat(C_Z + hd * M_DH), rows],
                                          ng_ref[hd * M_DH:(hd + 1) * M_DH, :], axis=0))
        for tile in range(A_HEADS // 2):
            o_t = jnp.concatenate([o * (1.0 / den) for o, den in a_out[2 * tile:2 * tile + 2]], axis=0)
            mix_t.append((o_t * _silu_of_twice(utf_ref[feat(C_AZ + tile * 128), rows])).astype(BF16))
        eye = jnp.where(key == query, 1.0, 0.0).astype(BF16)
        mix_rows = [_to_token_rows(x_t, eye) for x_t in mix_t]
        for i, x in enumerate(mix_rows):
            mix_ref[rows, i * 128:(i + 1) * 128] = x
        return carry

    lax.fori_loop(0, n_chunks, chunk_body, 0, unroll=CHUNK_UNROLL)

    for t0 in range(0, tb, tb // TOKEN_SPLITS):
        tok = slice(t0, t0 + tb // TOKEN_SPLITS)
        y_ref[0, tok, :] = _out_and_norm(hp_ref[tok, :], mix_ref[tok, :], wo_ref, lng_ref[...], lnb_ref[...])

    @pl.when(j == pl.num_programs(1) - 1)
    def _():
        pk_ref[0, N_META:N_BUF, :] = ukey_ref[tb - WINDOW:tb, C_AK - C_K:C_AK - C_K + KV_W]
        pv_ref[0, N_META:N_BUF, :] = utf_ref[feat(C_AV, KV_W), tb - WINDOW:tb].T
        for hd in range(M_HEADS):
            pc_ref[0, hd] = ct_ref[hd, 0:M_DH, :].T
            pn_ref[0, hd:hd + 1, :] = ct_ref[hd, M_DH:M_DH + 1, :]
        pn_ref[0, M_HEADS:8, :] = jnp.zeros((8 - M_HEADS, M_DH), F32)


def _const_spec(shape):
    return pl.BlockSpec(shape, lambda *_: (0,) * len(shape))


def _small_projections(meta_tokens, x_sample, ln0_g, ln0_b, w_t, b_all):
    rows = x_sample.shape[0] * x_sample.shape[1]
    return pl.pallas_call(
        _small_projections_kernel,
        out_shape=(jax.ShapeDtypeStruct((M_HEADS, CT_ROWS, M_DH), F32),
                   jax.ShapeDtypeStruct((8, 128), F32),
                   jax.ShapeDtypeStruct((N_META, KV_W), F32),
                   jax.ShapeDtypeStruct((N_META, KV_W), F32),
                   jax.ShapeDtypeStruct((KV_W, N_META), F32),
                   jax.ShapeDtypeStruct((rows, D_MODEL), F32),
                   jax.ShapeDtypeStruct((rows, N_PAD), F32)),
        scratch_shapes=[pltpu.VMEM((CHUNK, N_PAD), F32)],
        compiler_params=pltpu.CompilerParams(vmem_limit_bytes=VMEM_LIMIT_BYTES),
        name="small_projections",
    )(meta_tokens.astype(F32), x_sample.reshape(rows, D_MODEL), w_t, b_all, ln0_g, ln0_b)


def _prompt_path(x_prompt, meta_state, ln0_g, ln0_b, w_t, b_all, sinks, norm_g, w_o, ln_g, ln_b, tb=512):
    batch, seq, _ = x_prompt.shape
    ct0, m0, km, vm, vmt = meta_state

    nj = seq // tb
    in_specs = [
        pl.BlockSpec(memory_space=pltpu.SMEM),
        pl.BlockSpec((1, tb, D_MODEL), lambda b, j: (b, j, 0)),
        pl.BlockSpec((N_RAW, D_MODEL), lambda b, j: (0, 0), pipeline_mode=pl.Buffered(1)),
        _const_spec((N_RAW, 128)), _const_spec((1, N_RAW)),
        _const_spec((1, D_MODEL)), _const_spec((1, D_MODEL)),
        _const_spec((M_HEADS, CT_ROWS, M_DH)), _const_spec((8, 128)),
        _const_spec((N_META, KV_W)), _const_spec((N_META, KV_W)), _const_spec((KV_W, N_META)),
        _const_spec((D_MLSTM, 128)),
        _const_spec((D_MODEL, D_MODEL)),
        _const_spec((1, D_MODEL)), _const_spec((1, D_MODEL)),
    ]
    out_specs = [
        pl.BlockSpec((1, tb, D_MODEL), lambda b, j: (b, j, 0)),
        pl.BlockSpec((1, N_BUF, KV_W), lambda b, j: (b, 0, 0)),
        pl.BlockSpec((1, N_BUF, KV_W), lambda b, j: (b, 0, 0)),
        pl.BlockSpec((1, M_HEADS, M_DH, M_DH), lambda b, j: (b, 0, 0, 0)),
        pl.BlockSpec((1, 8, M_DH), lambda b, j: (b, 0, 0)),
        pl.BlockSpec((1, 8, 128), lambda b, j: (b, 0, 0)),
    ]
    out_shape = (
        jax.ShapeDtypeStruct((batch, seq, D_MODEL), F32),
        jax.ShapeDtypeStruct((batch, N_BUF, KV_W), F32),
        jax.ShapeDtypeStruct((batch, N_BUF, KV_W), F32),
        jax.ShapeDtypeStruct((batch, M_HEADS, M_DH, M_DH), F32),
        jax.ShapeDtypeStruct((batch, 8, M_DH), F32),
        jax.ShapeDtypeStruct((batch, 8, 128), F32),
    )
    y, pk, pv, pc, pn, pm = pl.pallas_call(
        functools.partial(_prompt_kernel, tb=tb),
        grid=(batch, nj),
        in_specs=in_specs,
        out_specs=out_specs,
        out_shape=out_shape,
        scratch_shapes=[pltpu.VMEM((N_T_BF16, tb), BF16), pltpu.VMEM((N_T - N_T_BF16, tb), F32),
                        pltpu.VMEM((tb, N_KEYS), F32),
                        pltpu.VMEM((tb, D_MODEL), F32), pltpu.VMEM((tb, D_MODEL), BF16),
                        pltpu.VMEM((M_HEADS, CT_ROWS, M_DH), F32),
                        pltpu.VMEM((CHUNK, KV_W), F32), pltpu.VMEM((KV_W, CHUNK), F32),
                        pltpu.VMEM((tb // CHUNK * 8, CHUNK), F32), pltpu.VMEM((tb // CHUNK * 8, CHUNK), F32)],
        compiler_params=pltpu.CompilerParams(dimension_semantics=("arbitrary", "arbitrary"),
                                             vmem_limit_bytes=VMEM_LIMIT_BYTES),
        name="prompt_layer",
    )(sinks, x_prompt, w_t, jnp.broadcast_to(b_all.reshape(N_RAW, 1), (N_RAW, 128)), b_all, ln0_g, ln0_b,
      ct0, m0, km, vm, vmt, jnp.broadcast_to(norm_g.reshape(D_MLSTM, 1), (D_MLSTM, 128)), w_o, ln_g, ln_b)
    pk = pk.reshape(1, batch, N_BUF, A_KV_HEADS, A_DH)
    pv = pv.reshape(1, batch, N_BUF, A_KV_HEADS, A_DH)
    return y, pk, pv, pc[None], pn[:, :M_HEADS][None], pm[:, :M_HEADS, 0][None]


SEQ_PER_GROUP = 32
SWA_SEQ_PER_STEP = 8
SEQ_BATCH = 8


def _segment_last(x, pos, seg_len):
    n = x.shape[1]
    step = 1
    while step < seg_len:
        x = jnp.where((pos // step) % 2 == 0, pltpu.roll(x, n - step, 1), x)
        step *= 2
    return x


def _sample_mlstm_kernel(q_ref, k_ref, v_ref, g_ref, c_ref, n_ref, m_ref,
                         h_ref, cn_ref, nn_ref, mn_ref, inter_ref, *, dec_seq):
    hd = pl.program_id(1)
    n = CHUNK
    nb = n // dec_seq
    q = q_ref[...]
    k = k_ref[...] * (M_DH ** -0.5)
    v = v_ref[...]
    qb = q.astype(BF16)
    kb = k.astype(BF16)
    vb = v.astype(BF16)

    g_t = g_ref[...].T[0:8, :]
    row8 = _iota2((8, n), 0)
    li_row = jnp.sum(jnp.where(row8 == hd, g_t, 0.0), axis=0, keepdims=True)
    lf_row = _log_sigmoid(jnp.sum(jnp.where(row8 == hd + M_HEADS, g_t, 0.0), axis=0, keepdims=True))
    lane = _iota2((1, n), 1)
    pos = lane % dec_seq
    b_row = lf_row
    shift = 1
    while shift < dec_seq:
        b_row = b_row + jnp.where(pos >= shift, pltpu.roll(b_row, shift, 1), 0.0)
        shift *= 2

    key = _iota2((n, n), 0)
    query = _iota2((n, n), 1)
    key_ok = (key <= query) & (key // dec_seq == query // dec_seq)
    a_row = li_row - b_row
    a_keys = jnp.broadcast_to(jnp.sum(jnp.where(key == query, a_row, 0.0), axis=1, keepdims=True), (n, n))
    a_t = jnp.where(key_ok, a_keys, NEG_INF)
    m_old = m_ref[0, 0, 0:1, :]
    mm = jnp.maximum(m_old, jnp.max(a_t, axis=0, keepdims=True))
    w_t = jnp.exp(a_t - mm)
    mm_last = _segment_last(mm, pos, dec_seq)
    m_new = _segment_last(b_row, pos, dec_seq) + mm_last
    w_state = jnp.exp(a_row - mm_last)
    decay = jnp.exp(m_old - mm_last)
    mn_ref[0, 0] = jnp.broadcast_to(m_new, (8, n))

    qkw_t = _dot_nt(kb, qb) * w_t
    den = jnp.sum(qkw_t, axis=0, keepdims=True)
    num_t = _dot(v.T.astype(BF16), qkw_t.astype(BF16))

    n_seq = n_ref[0]
    seq_of_lane = _iota2((nb, n), 1) // dec_seq == _iota2((nb, n), 0)
    decay_seq = jnp.sum(jnp.where(_iota2((nb, n), 1) == _iota2((nb, n), 0) * dec_seq, decay, 0.0),
                        axis=1, keepdims=True)
    nn_ref[0] = decay_seq * n_seq + _dot(jnp.where(seq_of_lane, w_state, 0.0).astype(BF16), kb)
    expand = jnp.where(_iota2((n, nb), 0) // dec_seq == _iota2((n, nb), 1), 1.0, 0.0).astype(BF16)
    n_rows = _dot(expand, n_seq.astype(BF16))
    qn_col = jnp.sum(q * n_rows, axis=1, keepdims=True)
    qn = jnp.sum(jnp.where(key == query, qn_col, 0.0), axis=0, keepdims=True)

    kwt = k.T * w_state
    decay_rows = jnp.broadcast_to(jnp.sum(jnp.where(key == query, decay, 0.0), axis=1, keepdims=True), (n, n))
    lane_seq = query // dec_seq
    low_rows = _iota2((8, M_DH), 0) < dec_seq
    assert 8 % dec_seq == 0 and 8 // dec_seq == 2
    for pair in range(nb // 2):
        q8 = q[8 * pair:8 * pair + 8].astype(BF16)
        inter_ref[8 * pair:8 * pair + 8, :] = jnp.where(low_rows, _dot(q8, c_ref[2 * pair, 0].astype(BF16)),
                                                        _dot(q8, c_ref[2 * pair + 1, 0].astype(BF16)))
    for s0 in range(0, nb, SEQ_BATCH):
        seqs = range(s0, s0 + SEQ_BATCH)
        updates = [_dot(jnp.where(lane_seq == s, kwt, 0.0).astype(BF16), vb) for s in seqs]
        for s, upd in zip(seqs, updates):
            cn_ref[s, 0] = decay_rows[s * dec_seq:s * dec_seq + 1, :] * c_ref[s, 0] + upd

    s_inter = jnp.exp(m_old - mm)
    num_t = num_t + inter_ref[...].T * s_inter
    den = den + qn * s_inter
    h_ref[...] = (num_t * (1.0 / jnp.maximum(jnp.abs(den), jnp.exp(-(b_row + mm))))).T


def _sample_swa_kernel(qz_ref, kn_ref, vn_ref, ck_ref, cv_ref, sink_ref, o_ref, nk_ref, nv_ref, *, dec_seq):
    n_rows = qz_ref.shape[1]
    t_c = _iota2((n_rows, N_BUF), 0) % dec_seq
    i_c = _iota2((n_rows, N_BUF), 1)
    ok_c = (i_c < N_META) | (i_c > t_c + N_META)
    t_n = _iota2((n_rows, 8), 0) % dec_seq
    i_n = _iota2((n_rows, 8), 1)
    ok_n = (i_n <= t_n) & (i_n < dec_seq)
    sink = sink_ref[:, 0:1]

    seqs = range(qz_ref.shape[0])
    scores = []
    for s in seqs:
        qz = (qz_ref[s] * (A_DH ** -0.5)).astype(BF16)
        scores.append((_dot_nt(qz, ck_ref[s].astype(BF16)), _dot_nt(qz, kn_ref[s].astype(BF16))))
    probs = []
    for s_c, s_n in scores:
        s_c = jnp.where(ok_c, s_c, NEG_INF)
        s_n = jnp.where(ok_n, s_n, NEG_INF)
        mx = jnp.maximum(jnp.maximum(jnp.max(s_c, axis=1, keepdims=True),
                                     jnp.max(s_n, axis=1, keepdims=True)), sink)
        p_c = jnp.exp(s_c - mx)
        p_n = jnp.exp(s_n - mx)
        den = jnp.sum(p_c, axis=1, keepdims=True) + jnp.sum(p_n, axis=1, keepdims=True) + jnp.exp(sink - mx)
        probs.append((p_c.astype(BF16), p_n.astype(BF16), den))
    outs = [_dot(p_c, cv_ref[s].astype(BF16)) + _dot(p_n, vn_ref[s].astype(BF16))
            for s, (p_c, p_n, _) in zip(seqs, probs)]
    for s, o, (_, _, den) in zip(seqs, outs, probs):
        o_ref[s] = o / den
    for s in seqs:
        for cache_ref, new_ref, out_ref in ((ck_ref, kn_ref, nk_ref), (cv_ref, vn_ref, nv_ref)):
            out_ref[s, 0:N_META, :] = cache_ref[s, 0:N_META, :]
            out_ref[s, N_META:N_BUF - dec_seq, :] = cache_ref[s, N_META + dec_seq:N_BUF, :]
            out_ref[s, N_BUF - dec_seq:N_BUF, :] = new_ref[s, 0:dec_seq, :]


def _sample_out_kernel(hs_ref, half_o_ref, half_z_ref, half_az_ref, hm_ref, oa_ref, ng_ref, wo_ref, lng_ref, lnb_ref,
                       y_ref):
    mix = _gate_mix(hm_ref[...], half_o_ref[...], half_z_ref[...], oa_ref[...], half_az_ref[...], ng_ref[...])
    y_ref[...] = _out_and_norm(hs_ref[...], mix, wo_ref, lng_ref[...], lnb_ref[...])


def _sample_path(hs, u, db, dec_seq, cache_k, cache_v, state_c, state_n, state_m, a_sinks, norm_g, w_o, ln_g, ln_b):
    rows = db * dec_seq
    params = pltpu.CompilerParams(vmem_limit_bytes=VMEM_LIMIT_BYTES)

    n_groups = db // SEQ_PER_GROUP
    n_t = jnp.transpose(state_n, (1, 0, 2))
    m_t = jnp.repeat(jnp.transpose(state_m, (1, 0)), dec_seq, axis=1).reshape(M_HEADS, n_groups, 1, CHUNK)
    m_t = jnp.broadcast_to(m_t, (M_HEADS, n_groups, 8, CHUNK))

    def col_spec(col0):
        return pl.BlockSpec((CHUNK, M_DH), lambda g, h: (g, col0 // M_DH + h))

    state_spec = pl.BlockSpec((SEQ_PER_GROUP, 1, M_DH, M_DH), lambda g, h: (g, h, 0, 0))
    vec_spec = pl.BlockSpec((1, SEQ_PER_GROUP, M_DH), lambda g, h: (h, g, 0))
    m_spec = pl.BlockSpec((1, 1, 8, CHUNK), lambda g, h: (h, g, 0, 0))
    h_m, c_new, n_new, m_new = pl.pallas_call(
        functools.partial(_sample_mlstm_kernel, dec_seq=dec_seq),
        grid=(n_groups, M_HEADS),
        in_specs=[col_spec(C_Q), col_spec(C_K), col_spec(C_V),
                  pl.BlockSpec((CHUNK, 128), lambda g, h: (g, C_G // 128)),
                  state_spec, vec_spec, m_spec],
        out_specs=[pl.BlockSpec((CHUNK, M_DH), lambda g, h: (g, h)), state_spec, vec_spec, m_spec],
        out_shape=(jax.ShapeDtypeStruct((rows, D_MLSTM), F32),
                   jax.ShapeDtypeStruct(state_c.shape, F32),
                   jax.ShapeDtypeStruct((M_HEADS, db, M_DH), F32),
                   jax.ShapeDtypeStruct((M_HEADS, n_groups, 8, CHUNK), F32)),
        scratch_shapes=[pltpu.VMEM((CHUNK, M_DH), F32)],
        compiler_params=pltpu.CompilerParams(dimension_semantics=("arbitrary", "arbitrary"),
                                             vmem_limit_bytes=VMEM_LIMIT_BYTES),
        name="sample_mlstm",
    )(u, u, u, u, state_c, n_t, m_t)

    aq = u[:, C_AQ:C_AQ + D_SWA].reshape(db, dec_seq, A_KV_HEADS, A_GROUP, A_DH)
    aq = jnp.transpose(aq, (0, 2, 3, 1, 4)).reshape(db, A_KV_HEADS, A_GROUP * dec_seq, A_DH)
    zeros = jnp.zeros_like(aq[:, 0])
    qz = jnp.stack([jnp.concatenate([aq[:, 0], zeros], axis=-1),
                    jnp.concatenate([zeros, aq[:, 1]], axis=-1)], axis=1)
    n_qrows = A_HEADS * dec_seq
    qz = qz.reshape(db, n_qrows, KV_W)
    k_new = jnp.pad(u[:, C_AK:C_AK + KV_W].reshape(db, dec_seq, KV_W), ((0, 0), (0, 8 - dec_seq), (0, 0)))
    v_new = jnp.pad(u[:, C_AV:C_AV + KV_W].reshape(db, dec_seq, KV_W), ((0, 0), (0, 8 - dec_seq), (0, 0)))
    sink_rows = jnp.broadcast_to(jnp.repeat(a_sinks.astype(F32), dec_seq)[:, None], (n_qrows, 128))
    ck = cache_k.reshape(db, N_BUF, KV_W)
    cv = cache_v.reshape(db, N_BUF, KV_W)
    sb = SWA_SEQ_PER_STEP

    def seq_spec(r):
        return pl.BlockSpec((sb, r, KV_W), lambda i: (i, 0, 0))

    o, nk, nv = pl.pallas_call(
        functools.partial(_sample_swa_kernel, dec_seq=dec_seq),
        grid=(db // sb,),
        in_specs=[seq_spec(n_qrows), seq_spec(8), seq_spec(8), seq_spec(N_BUF), seq_spec(N_BUF),
                  pl.BlockSpec((n_qrows, 128), lambda i: (0, 0))],
        out_specs=[seq_spec(n_qrows), seq_spec(N_BUF), seq_spec(N_BUF)],
        out_shape=(jax.ShapeDtypeStruct((db, n_qrows, KV_W), F32),
                   jax.ShapeDtypeStruct((db, N_BUF, KV_W), F32),
                   jax.ShapeDtypeStruct((db, N_BUF, KV_W), F32)),
        compiler_params=pltpu.CompilerParams(dimension_semantics=("arbitrary",),
                                             vmem_limit_bytes=VMEM_LIMIT_BYTES),
        name="sample_swa",
    )(qz, k_new, v_new, ck, cv, sink_rows)
    o = o.reshape(db, A_KV_HEADS, A_GROUP, dec_seq, A_KV_HEADS, A_DH)
    o = jnp.stack([o[:, 0, :, :, 0, :], o[:, 1, :, :, 1, :]], axis=1)
    o_a = jnp.transpose(o, (0, 3, 1, 2, 4)).reshape(rows, D_SWA)

    def gate_spec(col0):
        assert col0 % D_MLSTM == 0
        return pl.BlockSpec((rows, D_MLSTM), lambda i: (0, col0 // D_MLSTM))

    def whole(shape):
        return pl.BlockSpec(shape, lambda i: (0,) * len(shape))

    y = pl.pallas_call(
        _sample_out_kernel,
        grid=(1,),
        in_specs=[whole((rows, D_MODEL)), gate_spec(C_O), gate_spec(C_Z), gate_spec(C_AZ),
                  whole((rows, D_MLSTM)), whole((rows, D_SWA)), whole((1, D_MLSTM)), whole((D_MODEL, D_MODEL)),
                  whole((1, D_MODEL)), whole((1, D_MODEL))],
        out_specs=whole((rows, D_MODEL)),
        out_shape=jax.ShapeDtypeStruct((rows, D_MODEL), F32),
        compiler_params=params,
        name="sample_out",
    )(hs, u, u, u, h_m, o_a, norm_g, w_o, ln_g, ln_b)

    shape5 = (1, db, N_BUF, A_KV_HEADS, A_DH)
    return (y.reshape(db, dec_seq, D_MODEL), nk.reshape(shape5), nv.reshape(shape5), c_new[None],
            jnp.transpose(n_new, (1, 0, 2))[None], jnp.transpose(m_new[:, :, 0, ::dec_seq].reshape(M_HEADS, db), (1, 0))[None])


def kernel(x_prompt, x_sample, cache_swa_k, cache_swa_v, state_mlstm_c, state_mlstm_n, state_mlstm_m,
           meta_tokens, ln0_g, ln0_b, w_in, b_in, a_sinks, m_norm_g, w_out, ln_g, ln_b):
    assert w_in.shape[0] == DEPTH and x_prompt.shape[-1] == D_MODEL
    w_t = jnp.transpose(w_in[0].astype(F32))
    b_all = b_in[0].astype(F32)[None]
    w_o = w_out[0].astype(BF16)
    g0 = ln0_g.astype(F32)[None]
    b0 = ln0_b.astype(F32)[None]
    lg = ln_g[0].astype(F32)[None]
    lb = ln_b[0].astype(F32)[None]
    norm_g = m_norm_g[0].astype(F32)[None]
    sinks = a_sinks[0].astype(F32)

    *meta_state, hs, u = _small_projections(meta_tokens, x_sample, g0, b0, w_t, b_all)
    y_p, pk, pv, pc, pn, pm = _prompt_path(x_prompt, meta_state, g0, b0, w_t, b_all, sinks, norm_g, w_o, lg, lb)
    y_s, sk, sv, sc, sn, sm = _sample_path(hs, u, x_sample.shape[0], x_sample.shape[1], cache_swa_k[0], cache_swa_v[0],
                                           state_mlstm_c[0], state_mlstm_n[0], state_mlstm_m[0],
                                           sinks, norm_g, w_o, lg, lb)
    return (y_p, y_s, pk, pv, pc, pn, pm, sk, sv, sc, sn, sm)
```

```python
import functools

import jax
import jax.numpy as jnp
from jax import lax
from jax.experimental import pallas as pl
from jax.experimental.pallas import tpu as pltpu

F32 = jnp.float32
BF16 = jnp.bfloat16

D_MODEL = 1024
N_META = 16
M_HEADS = 4
M_DH = 128
D_MLSTM = M_HEADS * M_DH
A_HEADS = 8
A_KV_HEADS = 2
A_GROUP = A_HEADS // A_KV_HEADS
A_DH = 64
D_SWA = A_HEADS * A_DH
KV_W = A_KV_HEADS * A_DH
WINDOW = 128
CHUNK = 128
LN_EPS = 1e-5
DEPTH = 1
DN_ALPHA = (2.0 * DEPTH) ** 0.25
N_BUF = N_META + WINDOW

C_Q, C_AQ = 0, 512
N_T_BF16 = 1024
C_V, C_O, C_Z, C_AZ, C_AV, C_G = 1024, 1536, 2048, 2560, 3072, 3200
GATE_ROWS = 16
N_T = C_G + GATE_ROWS
C_K, C_AK = 3584, 4096
N_PAD = 4224
N_KEYS = N_PAD - C_K
RAW_Q, RAW_K, RAW_V, RAW_O, RAW_Z, RAW_G = (0, 512), (512, 512), (1024, 512), (1536, 512), (2048, 512), (2560, 8)
RAW_AQ, RAW_AK, RAW_AV, RAW_AZ = (2568, 512), (3080, 128), (3208, 128), (3336, 512)
N_RAW = 3848

PROJ_STEP = 512
TOKEN_SPLITS = 2
CHUNK_UNROLL = 4
VMEM_LIMIT_BYTES = 56 * 1024 * 1024
NEG_INF = float("-inf")


def _dot(a, b):
    return jnp.dot(a, b, preferred_element_type=F32)


def _dot_nt(a, b):
    return lax.dot_general(a, b, (((1,), (1,)), ((), ())), preferred_element_type=F32)


def _layer_norm(x, g, b):
    mu = jnp.mean(x, axis=-1, keepdims=True)
    xc = x - mu
    var = jnp.mean(xc * xc, axis=-1, keepdims=True)
    return xc * lax.rsqrt(var + LN_EPS) * g + b


def _log_sigmoid(x):
    return jnp.minimum(x, 0.0) - jnp.log1p(jnp.exp(-jnp.abs(x)))


def _times_sigmoid(h, half_x):
    return 0.5 * (h * jnp.tanh(half_x) + h)


def _silu_of_twice(half_x):
    return half_x * jnp.tanh(half_x) + half_x


def _iota2(shape, dim):
    return lax.broadcasted_iota(jnp.int32, shape, dim)


T_FEATURES = ((RAW_G, C_G, 1.0), (RAW_Q, C_Q, 1.0), (RAW_AQ, C_AQ, 1.0), (RAW_AV, C_AV, 1.0), (RAW_V, C_V, 1.0),
              (RAW_O, C_O, 0.5), (RAW_Z, C_Z, 0.5), (RAW_AZ, C_AZ, 0.5))
KEY_FEATURES = ((RAW_K, C_K, 1.0), (RAW_AK, C_AK, 1.0))


def _weights(wt_ref, raw):
    return wt_ref[raw[0]:raw[0] + raw[1], :].astype(BF16)


def _project(token_sets, wt_ref, b_ref):
    for _, u_ref in token_sets:
        u_ref[:, C_G:C_K] = jnp.zeros((u_ref.shape[0], C_K - C_G), F32)
    for raw, dst, scale in T_FEATURES + KEY_FEATURES:
        w = _weights(wt_ref, raw)
        for hb, u_ref in token_sets:
            res = _dot_nt(hb, w) + b_ref[:, raw[0]:raw[0] + raw[1]]
            u_ref[:, dst:dst + raw[1]] = res if scale == 1.0 else res * scale


def _project_both(hb, tok, wt_ref, bcol_ref, utb_ref, utf_ref):
    reps = hb.shape[0] // 128
    assert RAW_Z[0] + RAW_Z[1] == RAW_G[0]
    for raw, dst, scale in ((RAW_Z[0], RAW_Z[1] + RAW_G[1]), C_Z, 0.5), *T_FEATURES:
        if raw in (RAW_Z, RAW_G):
            continue
        res = _dot_nt(_weights(wt_ref, raw), hb) + jnp.concatenate([bcol_ref[raw[0]:raw[0] + raw[1], :]] * reps, axis=1)
        if raw[1] > RAW_Z[1]:
            utf_ref[C_G - N_T_BF16:C_G - N_T_BF16 + RAW_G[1], tok] = res[RAW_Z[1]:]
            res = res[0:RAW_Z[1]]
        if scale != 1.0:
            res = res * scale
        if dst < N_T_BF16:
            utb_ref[dst:dst + res.shape[0], tok] = res.astype(BF16)
        else:
            utf_ref[dst - N_T_BF16:dst - N_T_BF16 + res.shape[0], tok] = res


def _project_keys(hb, wt_ref, brow_ref, ukey_ref):
    for raw, dst, _ in KEY_FEATURES:
        ukey_ref[:, dst - C_K:dst - C_K + raw[1]] = (_dot_nt(hb, _weights(wt_ref, raw))
                                                     + brow_ref[:, raw[0]:raw[0] + raw[1]])


def _gate_rows(gates):
    return _gate_scan(jnp.concatenate([gates[r0:r0 + CHUNK].T[0:8, :] for r0 in range(0, gates.shape[0], CHUNK)],
                                      axis=0))


def _gate_rows_t(gates_t):
    return _gate_scan(jnp.concatenate([gates_t[:, c0:c0 + CHUNK] for c0 in range(0, gates_t.shape[1], CHUNK)],
                                      axis=0))


def _gate_scan(g_t):
    x = _log_sigmoid(g_t)
    lane = _iota2(x.shape, 1)
    shift = 1
    while shift < x.shape[1]:
        x = x + jnp.where(lane >= shift, pltpu.roll(x, shift, 1), 0.0)
        shift *= 2
    return g_t, x


CT_ROWS = M_DH + 8


def _mlstm_scores_and_state(q_tb, kb, v_t, li_row, b_row, m_old, ct_aug, key_ok, n_valid):
    n = kb.shape[0]
    row = _iota2((n, n), 0)
    col = _iota2((n, n), 1)
    a_row = li_row - b_row
    a_keys = jnp.broadcast_to(jnp.sum(jnp.where(row == col, a_row, 0.0), axis=1, keepdims=True), (n, n))
    a_t = jnp.where(key_ok, a_keys, NEG_INF)
    mm = jnp.maximum(m_old, jnp.max(a_t, axis=0, keepdims=True))
    w_t = jnp.exp(a_t - mm)
    lane = _iota2((1, n), 1)
    last = n_valid - 1
    mm_last = jnp.max(jnp.where(lane == last, mm, NEG_INF), axis=1, keepdims=True)
    m_new = jnp.sum(jnp.where(lane == last, b_row, 0.0), axis=1, keepdims=True) + mm_last
    w_state = jnp.exp(a_row - mm_last)
    if n_valid < n:
        w_state = jnp.where(lane < n_valid, w_state, 0.0)
    decay = jnp.exp(m_old - mm_last)
    scores_t = _dot(kb, q_tb)
    inter = _dot(ct_aug.astype(BF16), q_tb)
    ones_row = jnp.where(_iota2((CT_ROWS - M_DH, n), 0) == 0, w_state, 0.0)
    vtw = jnp.concatenate([v_t * w_state, ones_row], axis=0).astype(BF16)
    ct_aug_new = decay * ct_aug + _dot(vtw, kb)
    s_inter = jnp.exp(m_old - mm)
    floor = jnp.exp(-(b_row + mm))
    return (scores_t, w_t, v_t.astype(BF16), inter, s_inter, floor), ct_aug_new, m_new


def _mlstm_weighted_values(scores_t, w_t, vtb, inter, s_inter, floor):
    qkw_t = scores_t * w_t
    den = jnp.sum(qkw_t, axis=0, keepdims=True)
    num_t = _dot(vtb, qkw_t.astype(BF16))
    return num_t, den, inter, s_inter, floor


def _mlstm_finish(num_t, den, inter, s_inter, floor):
    num_t = num_t + inter[0:M_DH] * s_inter
    den = den + inter[M_DH:M_DH + 1] * s_inter
    return num_t * (1.0 / jnp.maximum(jnp.abs(den), floor))


def _mlstm_gate_head(h, half_o, half_z, norm_g, axis):
    hh = _times_sigmoid(h, half_o)
    mu = jnp.mean(hh, axis=axis, keepdims=True)
    hc = hh - mu
    var = jnp.mean(hc * hc, axis=axis, keepdims=True)
    return (hc * lax.rsqrt(var + LN_EPS) * norm_g * _silu_of_twice(half_z)).astype(BF16)


def _gate_mix(h_m, half_o, half_z, o_a, half_az, norm_g):
    parts = []
    for hd in range(M_HEADS):
        sl = slice(hd * M_DH, (hd + 1) * M_DH)
        parts.append(_mlstm_gate_head(h_m[:, sl], half_o[:, sl], half_z[:, sl], norm_g[:, sl], axis=-1))
    parts.append((o_a * _silu_of_twice(half_az)).astype(BF16))
    return jnp.concatenate(parts, axis=-1)


def _out_and_norm(hp, mix, wo_ref, g, b):
    z = DN_ALPHA * hp + _dot(mix, wo_ref[...])
    return _layer_norm(z, g, b)


def _small_projections_kernel(meta_ref, xs_ref, wt_ref, bias_ref, g0_ref, b0_ref,
                              ct0_ref, m0_ref, km_ref, vm_ref, vmt_ref, hs_ref, us_ref, u_ref):
    hs = _layer_norm(xs_ref[...], g0_ref[...], b0_ref[...])
    hs_ref[...] = hs
    meta = jnp.concatenate([meta_ref[...], jnp.zeros((CHUNK - N_META, D_MODEL), F32)], axis=0)
    hp = _layer_norm(meta, g0_ref[...], b0_ref[...])
    _project(((hs.astype(BF16), us_ref), (hp.astype(BF16), u_ref)), wt_ref, bias_ref)
    row = _iota2((CHUNK, CHUNK), 0)
    col = _iota2((CHUNK, CHUNK), 1)
    key_ok = (row <= col) & (row < N_META)
    li_rows, b_rows = _gate_rows(u_ref[:, C_G:C_G + 128])
    zero_m = jnp.zeros((1, 128), F32)
    zero_ct = jnp.zeros((CT_ROWS, M_DH), F32)
    m0_ref[...] = jnp.zeros(m0_ref.shape, F32)
    for hd in range(M_HEADS):
        q = u_ref[:, C_Q + hd * M_DH:C_Q + (hd + 1) * M_DH]
        k = u_ref[:, C_K + hd * M_DH:C_K + (hd + 1) * M_DH] * (M_DH ** -0.5)
        v = u_ref[:, C_V + hd * M_DH:C_V + (hd + 1) * M_DH]
        _, ct_new, m_new = _mlstm_scores_and_state(
            q.T.astype(BF16), k.astype(BF16), v.T, li_rows[hd:hd + 1], b_rows[M_HEADS + hd:M_HEADS + hd + 1],
            zero_m, zero_ct, key_ok, N_META)
        ct0_ref[hd] = ct_new
        m0_ref[hd:hd + 1, :] = jnp.broadcast_to(m_new, (1, 128))
    km_ref[...] = u_ref[0:N_META, C_AK:C_AK + KV_W]
    vm_ref[...] = u_ref[0:N_META, C_AV:C_AV + KV_W]
    vmt_ref[...] = u_ref[:, C_AV:C_AV + KV_W].T[:, 0:N_META]


def _keep_kv_half(x, kv):
    low = _iota2(x.shape, 1) < A_DH
    return jnp.where(low if kv == 0 else ~low, x, 0.0)


def _to_token_rows(x_t, eye):
    return _dot_nt(eye, x_t).astype(BF16)


def _swa_weighted_values_t(scores, values_t, sink, own_ok, prev_ok):
    s_own = jnp.where(own_ok, scores[0], NEG_INF)
    s_prev = jnp.where(prev_ok, scores[1], NEG_INF)
    s_meta = scores[2]
    mx = jnp.maximum(jnp.maximum(jnp.max(s_own, axis=0, keepdims=True),
                                 jnp.max(s_prev, axis=0, keepdims=True)),
                     jnp.maximum(jnp.max(s_meta, axis=0, keepdims=True), sink))
    p_own, p_prev, p_meta = (jnp.exp(s - mx) for s in (s_own, s_prev, s_meta))
    den = (jnp.sum(p_own, axis=0, keepdims=True) + jnp.sum(p_prev, axis=0, keepdims=True)
           + jnp.sum(p_meta, axis=0, keepdims=True) + jnp.exp(sink - mx))
    o_t = (_dot(values_t[0], p_own.astype(BF16)) + _dot(values_t[1], p_prev.astype(BF16))
           + _dot(values_t[2], p_meta.astype(BF16)))
    return o_t, den


def _prompt_kernel(sink_ref, x_ref, wt_ref, bcol_ref, brow_ref, g0_ref, b0_ref,
                   ct0_ref, m0_ref, km_ref, vm_ref, vmt_ref, ng_ref, wo_ref, lng_ref, lnb_ref,
                   y_ref, pk_ref, pv_ref, pc_ref, pn_ref, pm_ref,
                   utb_ref, utf_ref, ukey_ref, hp_ref, mix_ref, ct_ref, kprev_ref, vtprev_ref, li_ref, cumf_ref,
                   *, tb):
    j = pl.program_id(1)
    n_chunks = tb // CHUNK

    @pl.when(j == 0)
    def _():
        ct_ref[...] = ct0_ref[...]
        pm_ref[0] = m0_ref[...]
        kprev_ref[...] = jnp.zeros(kprev_ref.shape, F32)
        vtprev_ref[...] = jnp.zeros(vtprev_ref.shape, F32)
        pk_ref[0, 0:N_META, :] = km_ref[...]
        pv_ref[0, 0:N_META, :] = vm_ref[...]

    hp = _layer_norm(x_ref[0], g0_ref[...], b0_ref[...])
    hp_ref[...] = hp
    hb = hp.astype(BF16)
    _project_keys(hb, wt_ref, brow_ref, ukey_ref)
    for t0 in range(0, tb, tb // TOKEN_SPLITS):
        tok = slice(t0, t0 + tb // TOKEN_SPLITS)
        _project_both(hb[tok], tok, wt_ref, bcol_ref, utb_ref, utf_ref)

    def feat(c0, n=M_DH):
        return slice(c0 - N_T_BF16, c0 - N_T_BF16 + n)

    li_ref[...], cumf_ref[...] = _gate_rows_t(utf_ref[feat(C_G, 8), :])

    def chunk_body(ci, carry):
        r0 = pl.multiple_of(ci * CHUNK, CHUNK)
        rows = pl.ds(r0, CHUNK)
        key = _iota2((CHUNK, CHUNK), 0)
        query = _iota2((CHUNK, CHUNK), 1)
        causal = key <= query

        gate_rows = pl.ds(pl.multiple_of(ci * 8, 8), 8)
        li_rows = li_ref[gate_rows, :]
        b_rows = cumf_ref[gate_rows, :]
        m_carry = []
        for hd in range(M_HEADS):
            q_tb = utb_ref[C_Q + hd * M_DH:C_Q + (hd + 1) * M_DH, rows]
            kb = (ukey_ref[rows, hd * M_DH:(hd + 1) * M_DH] * (M_DH ** -0.5)).astype(BF16)
            v_t = utf_ref[feat(C_V + hd * M_DH), rows]
            carry_hd, ct_new, m_new = _mlstm_scores_and_state(
                q_tb, kb, v_t, li_rows[hd:hd + 1], b_rows[M_HEADS + hd:M_HEADS + hd + 1],
                pm_ref[0, hd:hd + 1, :], ct_ref[hd], causal, CHUNK)
            ct_ref[hd] = ct_new
            pm_ref[0, hd:hd + 1, :] = jnp.broadcast_to(m_new, (1, 128))
            m_carry.append(carry_hd)

        k_own = ukey_ref[rows, C_AK - C_K:C_AK - C_K + KV_W]
        vt_own = utf_ref[feat(C_AV, KV_W), rows]
        k_prev = kprev_ref[...]
        vt_prev = vtprev_ref[...]
        prev_ok = (key > query) & ((j * n_chunks + ci) > 0)
        keys_kv = [tuple((_keep_kv_half(x, kv) * (A_DH ** -0.5)).astype(BF16) for x in (k_own, k_prev, km_ref[...]))
                   for kv in range(A_KV_HEADS)]
        values_kv = [tuple(x[kv * A_DH:(kv + 1) * A_DH].astype(BF16) for x in (vt_own, vt_prev, vmt_ref[...]))
                     for kv in range(A_KV_HEADS)]
        a_scores = []
        for hd in range(A_HEADS):
            kv = hd // A_GROUP
            q0 = C_AQ + (hd - kv) * A_DH
            q_win = utb_ref[q0:q0 + 2 * A_DH, rows]
            a_scores.append(tuple(_dot(kk, q_win) for kk in keys_kv[kv]))
        kprev_ref[...] = k_own
        vtprev_ref[...] = vt_own

        m_carry = [_mlstm_weighted_values(*c) for c in m_carry]
        a_out = [_swa_weighted_values_t(a_scores[hd], values_kv[hd // A_GROUP], sink_ref[hd], causal, prev_ok)
                 for hd in range(A_HEADS)]

        mix_t = []
        for hd in range(M_HEADS):
            mix_t.append(_mlstm_gate_head(_mlstm_finish(*m_carry[hd]), utf_ref[feat(C_O + hd * M_DH), rows],
                                          utf_ref[feat(C_Z + hd * M_DH), rows],
                                          ng_ref[hd * M_DH:(hd + 1) * M_DH, :], axis=0))
        for tile in range(A_HEADS // 2):
            o_t = jnp.concatenate([o * (1.0 / den) for o, den in a_out[2 * tile:2 * tile + 2]], axis=0)
            mix_t.append((o_t * _silu_of_twice(utf_ref[feat(C_AZ + tile * 128), rows])).astype(BF16))
        eye = jnp.where(key == query, 1.0, 0.0).astype(BF16)
        mix_rows = [_to_token_rows(x_t, eye) for x_t in mix_t]
        for i, x in enumerate(mix_rows):
            mix_ref[rows, i * 128:(i + 1) * 128] = x
        return carry

    lax.fori_loop(0, n_chunks, chunk_body, 0, unroll=CHUNK_UNROLL)

    for t0 in range(0, tb, tb // TOKEN_SPLITS):
        tok = slice(t0, t0 + tb // TOKEN_SPLITS)
        y_ref[0, tok, :] = _out_and_norm(hp_ref[tok, :], mix_ref[tok, :], wo_ref, lng_ref[...], lnb_ref[...])

    @pl.when(j == pl.num_programs(1) - 1)
    def _():
        pk_ref[0, N_META:N_BUF, :] = ukey_ref[tb - WINDOW:tb, C_AK - C_K:C_AK - C_K + KV_W]
        pv_ref[0, N_META:N_BUF, :] = utf_ref[feat(C_AV, KV_W), tb - WINDOW:tb].T
        for hd in range(M_HEADS):
            pc_ref[0, hd] = ct_ref[hd, 0:M_DH, :].T
            pn_ref[0, hd:hd + 1, :] = ct_ref[hd, M_DH:M_DH + 1, :]
        pn_ref[0, M_HEADS:8, :] = jnp.zeros((8 - M_HEADS, M_DH), F32)


def _const_spec(shape):
    return pl.BlockSpec(shape, lambda *_: (0,) * len(shape))


def _small_projections(meta_tokens, x_sample, ln0_g, ln0_b, w_t, b_all):
    rows = x_sample.shape[0] * x_sample.shape[1]
    return pl.pallas_call(
        _small_projections_kernel,
        out_shape=(jax.ShapeDtypeStruct((M_HEADS, CT_ROWS, M_DH), F32),
                   jax.ShapeDtypeStruct((8, 128), F32),
                   jax.ShapeDtypeStruct((N_META, KV_W), F32),
                   jax.ShapeDtypeStruct((N_META, KV_W), F32),
                   jax.ShapeDtypeStruct((KV_W, N_META), F32),
                   jax.ShapeDtypeStruct((rows, D_MODEL), F32),
                   jax.ShapeDtypeStruct((rows, N_PAD), F32)),
        scratch_shapes=[pltpu.VMEM((CHUNK, N_PAD), F32)],
        compiler_params=pltpu.CompilerParams(vmem_limit_bytes=VMEM_LIMIT_BYTES),
        name="small_projections",
    )(meta_tokens.astype(F32), x_sample.reshape(rows, D_MODEL), w_t, b_all, ln0_g, ln0_b)


def _prompt_path(x_prompt, meta_state, ln0_g, ln0_b, w_t, b_all, sinks, norm_g, w_o, ln_g, ln_b, tb=512):
    batch, seq, _ = x_prompt.shape
    ct0, m0, km, vm, vmt = meta_state

    nj = seq // tb
    in_specs = [
        pl.BlockSpec(memory_space=pltpu.SMEM),
        pl.BlockSpec((1, tb, D_MODEL), lambda b, j: (b, j, 0)),
        pl.BlockSpec((N_RAW, D_MODEL), lambda b, j: (0, 0), pipeline_mode=pl.Buffered(1)),
        _const_spec((N_RAW, 128)), _const_spec((1, N_RAW)),
        _const_spec((1, D_MODEL)), _const_spec((1, D_MODEL)),
        _const_spec((M_HEADS, CT_ROWS, M_DH)), _const_spec((8, 128)),
        _const_spec((N_META, KV_W)), _const_spec((N_META, KV_W)), _const_spec((KV_W, N_META)),
        _const_spec((D_MLSTM, 128)),
        _const_spec((D_MODEL, D_MODEL)),
        _const_spec((1, D_MODEL)), _const_spec((1, D_MODEL)),
    ]
    out_specs = [
        pl.BlockSpec((1, tb, D_MODEL), lambda b, j: (b, j, 0)),
        pl.BlockSpec((1, N_BUF, KV_W), lambda b, j: (b, 0, 0)),
        pl.BlockSpec((1, N_BUF, KV_W), lambda b, j: (b, 0, 0)),
        pl.BlockSpec((1, M_HEADS, M_DH, M_DH), lambda b, j: (b, 0, 0, 0)),
        pl.BlockSpec((1, 8, M_DH), lambda b, j: (b, 0, 0)),
        pl.BlockSpec((1, 8, 128), lambda b, j: (b, 0, 0)),
    ]
    out_shape = (
        jax.ShapeDtypeStruct((batch, seq, D_MODEL), F32),
        jax.ShapeDtypeStruct((batch, N_BUF, KV_W), F32),
        jax.ShapeDtypeStruct((batch, N_BUF, KV_W), F32),
        jax.ShapeDtypeStruct((batch, M_HEADS, M_DH, M_DH), F32),
        jax.ShapeDtypeStruct((batch, 8, M_DH), F32),
        jax.ShapeDtypeStruct((batch, 8, 128), F32),
    )
    y, pk, pv, pc, pn, pm = pl.pallas_call(
        functools.partial(_prompt_kernel, tb=tb),
        grid=(batch, nj),
        in_specs=in_specs,
        out_specs=out_specs,
        out_shape=out_shape,
        scratch_shapes=[pltpu.VMEM((N_T_BF16, tb), BF16), pltpu.VMEM((N_T - N_T_BF16, tb), F32),
                        pltpu.VMEM((tb, N_KEYS), F32),
                        pltpu.VMEM((tb, D_MODEL), F32), pltpu.VMEM((tb, D_MODEL), BF16),
                        pltpu.VMEM((M_HEADS, CT_ROWS, M_DH), F32),
                        pltpu.VMEM((CHUNK, KV_W), F32), pltpu.VMEM((KV_W, CHUNK), F32),
                        pltpu.VMEM((tb // CHUNK * 8, CHUNK), F32), pltpu.VMEM((tb // CHUNK * 8, CHUNK), F32)],
        compiler_params=pltpu.CompilerParams(dimension_semantics=("arbitrary", "arbitrary"),
                                             vmem_limit_bytes=VMEM_LIMIT_BYTES),
        name="prompt_layer",
    )(sinks, x_prompt, w_t, jnp.broadcast_to(b_all.reshape(N_RAW, 1), (N_RAW, 128)), b_all, ln0_g, ln0_b,
      ct0, m0, km, vm, vmt, jnp.broadcast_to(norm_g.reshape(D_MLSTM, 1), (D_MLSTM, 128)), w_o, ln_g, ln_b)
    pk = pk.reshape(1, batch, N_BUF, A_KV_HEADS, A_DH)
    pv = pv.reshape(1, batch, N_BUF, A_KV_HEADS, A_DH)
    return y, pk, pv, pc[None], pn[:, :M_HEADS][None], pm[:, :M_HEADS, 0][None]


SEQ_PER_GROUP = 32
SWA_SEQ_PER_STEP = 8
SEQ_BATCH = 8


def _segment_last(x, pos, seg_len):
    n = x.shape[1]
    step = 1
    while step < seg_len:
        x = jnp.where((pos // step) % 2 == 0, pltpu.roll(x, n - step, 1), x)
        step *= 2
    return x


def _sample_mlstm_kernel(q_ref, k_ref, v_ref, g_ref, c_ref, n_ref, m_ref,
                         h_ref, cn_ref, nn_ref, mn_ref, inter_ref, *, dec_seq):
    g_t = g_ref[...].T[0:8, :]
    for hd in range(M_HEADS):
        _sample_mlstm_head(hd, g_t, q_ref, k_ref, v_ref, c_ref, n_ref, m_ref,
                           h_ref, cn_ref, nn_ref, mn_ref, inter_ref, dec_seq)


def _sample_mlstm_head(hd, g_t, q_ref, k_ref, v_ref, c_ref, n_ref, m_ref,
                       h_ref, cn_ref, nn_ref, mn_ref, inter_ref, dec_seq):
    sl = slice(hd * M_DH, (hd + 1) * M_DH)
    n = CHUNK
    nb = n // dec_seq
    q = q_ref[:, sl]
    k = k_ref[:, sl] * (M_DH ** -0.5)
    v = v_ref[:, sl]
    qb = q.astype(BF16)
    kb = k.astype(BF16)
    vb = v.astype(BF16)

    li_row = g_t[hd:hd + 1, :]
    lf_row = _log_sigmoid(g_t[M_HEADS + hd:M_HEADS + hd + 1, :])
    lane = _iota2((1, n), 1)
    pos = lane % dec_seq
    b_row = lf_row
    shift = 1
    while shift < dec_seq:
        b_row = b_row + jnp.where(pos >= shift, pltpu.roll(b_row, shift, 1), 0.0)
        shift *= 2

    key = _iota2((n, n), 0)
    query = _iota2((n, n), 1)
    key_ok = (key <= query) & (key // dec_seq == query // dec_seq)
    a_row = li_row - b_row
    a_keys = jnp.broadcast_to(jnp.sum(jnp.where(key == query, a_row, 0.0), axis=1, keepdims=True), (n, n))
    a_t = jnp.where(key_ok, a_keys, NEG_INF)
    m_old = m_ref[hd, 0, 0:1, :]
    mm = jnp.maximum(m_old, jnp.max(a_t, axis=0, keepdims=True))
    w_t = jnp.exp(a_t - mm)
    mm_last = _segment_last(mm, pos, dec_seq)
    m_new = _segment_last(b_row, pos, dec_seq) + mm_last
    w_state = jnp.exp(a_row - mm_last)
    decay = jnp.exp(m_old - mm_last)
    mn_ref[hd, 0] = jnp.broadcast_to(m_new, (8, n))

    qkw_t = _dot_nt(kb, qb) * w_t
    den = jnp.sum(qkw_t, axis=0, keepdims=True)
    num_t = _dot(v.T.astype(BF16), qkw_t.astype(BF16))

    n_seq = n_ref[hd]
    seq_of_lane = _iota2((nb, n), 1) // dec_seq == _iota2((nb, n), 0)
    decay_seq = jnp.sum(jnp.where(_iota2((nb, n), 1) == _iota2((nb, n), 0) * dec_seq, decay, 0.0),
                        axis=1, keepdims=True)
    nn_ref[hd] = decay_seq * n_seq + _dot(jnp.where(seq_of_lane, w_state, 0.0).astype(BF16), kb)
    expand = jnp.where(_iota2((n, nb), 0) // dec_seq == _iota2((n, nb), 1), 1.0, 0.0).astype(BF16)
    n_rows = _dot(expand, n_seq.astype(BF16))
    qn_col = jnp.sum(q * n_rows, axis=1, keepdims=True)
    qn = jnp.sum(jnp.where(key == query, qn_col, 0.0), axis=0, keepdims=True)

    kwt = k.T * w_state
    decay_rows = jnp.broadcast_to(jnp.sum(jnp.where(key == query, decay, 0.0), axis=1, keepdims=True), (n, n))
    lane_seq = query // dec_seq
    low_rows = _iota2((8, M_DH), 0) < dec_seq
    assert 8 % dec_seq == 0 and 8 // dec_seq == 2
    for pair in range(nb // 2):
        q8 = q[8 * pair:8 * pair + 8].astype(BF16)
        inter_ref[8 * pair:8 * pair + 8, :] = jnp.where(low_rows, _dot(q8, c_ref[2 * pair, hd].astype(BF16)),
                                                        _dot(q8, c_ref[2 * pair + 1, hd].astype(BF16)))
    for s0 in range(0, nb, SEQ_BATCH):
        seqs = range(s0, s0 + SEQ_BATCH)
        updates = [_dot(jnp.where(lane_seq == s, kwt, 0.0).astype(BF16), vb) for s in seqs]
        for s, upd in zip(seqs, updates):
            cn_ref[s, hd] = decay_rows[s * dec_seq:s * dec_seq + 1, :] * c_ref[s, hd] + upd

    s_inter = jnp.exp(m_old - mm)
    num_t = num_t + inter_ref[...].T * s_inter
    den = den + qn * s_inter
    h_ref[:, sl] = (num_t * (1.0 / jnp.maximum(jnp.abs(den), jnp.exp(-(b_row + mm))))).T


def _sample_swa_kernel(qz_ref, kn_ref, vn_ref, ck_ref, cv_ref, sink_ref, o_ref, nk_ref, nv_ref, *, dec_seq):
    n_rows = qz_ref.shape[1]
    t_c = _iota2((n_rows, N_BUF), 0) % dec_seq
    i_c = _iota2((n_rows, N_BUF), 1)
    ok_c = (i_c < N_META) | (i_c > t_c + N_META)
    t_n = _iota2((n_rows, 8), 0) % dec_seq
    i_n = _iota2((n_rows, 8), 1)
    ok_n = (i_n <= t_n) & (i_n < dec_seq)
    sink = sink_ref[:, 0:1]

    seqs = range(qz_ref.shape[0])
    scores = []
    for s in seqs:
        qz = (qz_ref[s] * (A_DH ** -0.5)).astype(BF16)
        scores.append((_dot_nt(qz, ck_ref[s].astype(BF16)), _dot_nt(qz, kn_ref[s].astype(BF16))))
    probs = []
    for s_c, s_n in scores:
        s_c = jnp.where(ok_c, s_c, NEG_INF)
        s_n = jnp.where(ok_n, s_n, NEG_INF)
        mx = jnp.maximum(jnp.maximum(jnp.max(s_c, axis=1, keepdims=True),
                                     jnp.max(s_n, axis=1, keepdims=True)), sink)
        p_c = jnp.exp(s_c - mx)
        p_n = jnp.exp(s_n - mx)
        den = jnp.sum(p_c, axis=1, keepdims=True) + jnp.sum(p_n, axis=1, keepdims=True) + jnp.exp(sink - mx)
        probs.append((p_c.astype(BF16), p_n.astype(BF16), den))
    outs = [_dot(p_c, cv_ref[s].astype(BF16)) + _dot(p_n, vn_ref[s].astype(BF16))
            for s, (p_c, p_n, _) in zip(seqs, probs)]
    for s, o, (_, _, den) in zip(seqs, outs, probs):
        o_ref[s] = o / den
    for s in seqs:
        for cache_ref, new_ref, out_ref in ((ck_ref, kn_ref, nk_ref), (cv_ref, vn_ref, nv_ref)):
            out_ref[s, 0:N_META, :] = cache_ref[s, 0:N_META, :]
            out_ref[s, N_META:N_BUF - dec_seq, :] = cache_ref[s, N_META + dec_seq:N_BUF, :]
            out_ref[s, N_BUF - dec_seq:N_BUF, :] = new_ref[s, 0:dec_seq, :]


def _sample_out_kernel(hs_ref, half_o_ref, half_z_ref, half_az_ref, hm_ref, oa_ref, ng_ref, wo_ref, lng_ref, lnb_ref,
                       y_ref):
    mix = _gate_mix(hm_ref[...], half_o_ref[...], half_z_ref[...], oa_ref[...], half_az_ref[...], ng_ref[...])
    y_ref[...] = _out_and_norm(hs_ref[...], mix, wo_ref, lng_ref[...], lnb_ref[...])


def _sample_path(hs, u, db, dec_seq, cache_k, cache_v, state_c, state_n, state_m, a_sinks, norm_g, w_o, ln_g, ln_b):
    rows = db * dec_seq
    params = pltpu.CompilerParams(vmem_limit_bytes=VMEM_LIMIT_BYTES)

    n_groups = db // SEQ_PER_GROUP
    n_t = jnp.transpose(state_n, (1, 0, 2))
    m_t = jnp.repeat(jnp.transpose(state_m, (1, 0)), dec_seq, axis=1).reshape(M_HEADS, n_groups, 1, CHUNK)
    m_t = jnp.broadcast_to(m_t, (M_HEADS, n_groups, 8, CHUNK))

    def col_spec(col0):
        assert col0 % D_MLSTM == 0
        return pl.BlockSpec((CHUNK, D_MLSTM), lambda g: (g, col0 // D_MLSTM))

    state_spec = pl.BlockSpec((SEQ_PER_GROUP, M_HEADS, M_DH, M_DH), lambda g: (g, 0, 0, 0))
    vec_spec = pl.BlockSpec((M_HEADS, SEQ_PER_GROUP, M_DH), lambda g: (0, g, 0))
    m_spec = pl.BlockSpec((M_HEADS, 1, 8, CHUNK), lambda g: (0, g, 0, 0))
    h_m, c_new, n_new, m_new = pl.pallas_call(
        functools.partial(_sample_mlstm_kernel, dec_seq=dec_seq),
        grid=(n_groups,),
        in_specs=[col_spec(C_Q), col_spec(C_K), col_spec(C_V),
                  pl.BlockSpec((CHUNK, 128), lambda g: (g, C_G // 128)),
                  state_spec, vec_spec, m_spec],
        out_specs=[pl.BlockSpec((CHUNK, D_MLSTM), lambda g: (g, 0)), state_spec, vec_spec, m_spec],
        out_shape=(jax.ShapeDtypeStruct((rows, D_MLSTM), F32),
                   jax.ShapeDtypeStruct(state_c.shape, F32),
                   jax.ShapeDtypeStruct((M_HEADS, db, M_DH), F32),
                   jax.ShapeDtypeStruct((M_HEADS, n_groups, 8, CHUNK), F32)),
        scratch_shapes=[pltpu.VMEM((CHUNK, M_DH), F32)],
        compiler_params=pltpu.CompilerParams(dimension_semantics=("arbitrary",),
                                             vmem_limit_bytes=VMEM_LIMIT_BYTES),
        name="sample_mlstm",
    )(u, u, u, u, state_c, n_t, m_t)

    aq = u[:, C_AQ:C_AQ + D_SWA].reshape(db, dec_seq, A_KV_HEADS, A_GROUP, A_DH)
    aq = jnp.transpose(aq, (0, 2, 3, 1, 4)).reshape(db, A_KV_HEADS, A_GROUP * dec_seq, A_DH)
    zeros = jnp.zeros_like(aq[:, 0])
    qz = jnp.stack([jnp.concatenate([aq[:, 0], zeros], axis=-1),
                    jnp.concatenate([zeros, aq[:, 1]], axis=-1)], axis=1)
    n_qrows = A_HEADS * dec_seq
    qz = qz.reshape(db, n_qrows, KV_W)
    k_new = jnp.pad(u[:, C_AK:C_AK + KV_W].reshape(db, dec_seq, KV_W), ((0, 0), (0, 8 - dec_seq), (0, 0)))
    v_new = jnp.pad(u[:, C_AV:C_AV + KV_W].reshape(db, dec_seq, KV_W), ((0, 0), (0, 8 - dec_seq), (0, 0)))
    sink_rows = jnp.broadcast_to(jnp.repeat(a_sinks.astype(F32), dec_seq)[:, None], (n_qrows, 128))
    ck = cache_k.reshape(db, N_BUF, KV_W)
    cv = cache_v.reshape(db, N_BUF, KV_W)
    sb = SWA_SEQ_PER_STEP

    def seq_spec(r):
        return pl.BlockSpec((sb, r, KV_W), lambda i: (i, 0, 0))

    o, nk, nv = pl.pallas_call(
        functools.partial(_sample_swa_kernel, dec_seq=dec_seq),
        grid=(db // sb,),
        in_specs=[seq_spec(n_qrows), seq_spec(8), seq_spec(8), seq_spec(N_BUF), seq_spec(N_BUF),
                  pl.BlockSpec((n_qrows, 128), lambda i: (0, 0))],
        out_specs=[seq_spec(n_qrows), seq_spec(N_BUF), seq_spec(N_BUF)],
        out_shape=(jax.ShapeDtypeStruct((db, n_qrows, KV_W), F32),
                   jax.ShapeDtypeStruct((db, N_BUF, KV_W), F32),
                   jax.ShapeDtypeStruct((db, N_BUF, KV_W), F32)),
        compiler_params=pltpu.CompilerParams(dimension_semantics=("arbitrary",),
                                             vmem_limit_bytes=VMEM_LIMIT_BYTES),
        name="sample_swa",
    )(qz, k_new, v_new, ck, cv, sink_rows)
    o = o.reshape(db, A_KV_HEADS, A_GROUP, dec_seq, A_KV_HEADS, A_DH)
    o = jnp.stack([o[:, 0, :, :, 0, :], o[:, 1, :, :, 1, :]], axis=1)
    o_a = jnp.transpose(o, (0, 3, 1, 2, 4)).reshape(rows, D_SWA)

    def gate_spec(col0):
        assert col0 % D_MLSTM == 0
        return pl.BlockSpec((rows, D_MLSTM), lambda i: (0, col0 // D_MLSTM))

    def whole(shape):
        return pl.BlockSpec(shape, lambda i: (0,) * len(shape))

    y = pl.pallas_call(
        _sample_out_kernel,
        grid=(1,),
        in_specs=[whole((rows, D_MODEL)), gate_spec(C_O), gate_spec(C_Z), gate_spec(C_AZ),
                  whole((rows, D_MLSTM)), whole((rows, D_SWA)), whole((1, D_MLSTM)), whole((D_MODEL, D_MODEL)),
                  whole((1, D_MODEL)), whole((1, D_MODEL))],
        out_specs=whole((rows, D_MODEL)),
        out_shape=jax.ShapeDtypeStruct((rows, D_MODEL), F32),
        compiler_params=params,
        name="sample_out",
    )(hs, u, u, u, h_m, o_a, norm_g, w_o, ln_g, ln_b)

    shape5 = (1, db, N_BUF, A_KV_HEADS, A_DH)
    return (y.reshape(db, dec_seq, D_MODEL), nk.reshape(shape5), nv.reshape(shape5), c_new[None],
            jnp.transpose(n_new, (1, 0, 2))[None], jnp.transpose(m_new[:, :, 0, ::dec_seq].reshape(M_HEADS, db), (1, 0))[None])


def kernel(x_prompt, x_sample, cache_swa_k, cache_swa_v, state_mlstm_c, state_mlstm_n, state_mlstm_m,
           meta_tokens, ln0_g, ln0_b, w_in, b_in, a_sinks, m_norm_g, w_out, ln_g, ln_b):
    assert w_in.shape[0] == DEPTH and x_prompt.shape[-1] == D_MODEL
    w_t = jnp.transpose(w_in[0].astype(F32))
    b_all = b_in[0].astype(F32)[None]
    w_o = w_out[0].astype(BF16)
    g0 = ln0_g.astype(F32)[None]
    b0 = ln0_b.astype(F32)[None]
    lg = ln_g[0].astype(F32)[None]
    lb = ln_b[0].astype(F32)[None]
    norm_g = m_norm_g[0].astype(F32)[None]
    sinks = a_sinks[0].astype(F32)

    *meta_state, hs, u = _small_projections(meta_tokens, x_sample, g0, b0, w_t, b_all)
    y_p, pk, pv, pc, pn, pm = _prompt_path(x_prompt, meta_state, g0, b0, w_t, b_all, sinks, norm_g, w_o, lg, lb)
    y_s, sk, sv, sc, sn, sm = _sample_path(hs, u, x_sample.shape[0], x_sample.shape[1], cache_swa_k[0], cache_swa_v[0],
                                           state_mlstm_c[0], state_mlstm_n[0], state_mlstm_m[0],
                                           sinks, norm_g, w_o, lg, lb)
    return (y_p, y_s, pk, pv, pc, pn, pm, sk, sv, sc, sn, sm)
```

```python
import functools

import jax
import jax.numpy as jnp
from jax import lax
from jax.experimental import pallas as pl
from jax.experimental.pallas import tpu as pltpu

F32 = jnp.float32
BF16 = jnp.bfloat16

D_MODEL = 1024
N_META = 16
M_HEADS = 4
M_DH = 128
D_MLSTM = M_HEADS * M_DH
A_HEADS = 8
A_KV_HEADS = 2
A_GROUP = A_HEADS // A_KV_HEADS
A_DH = 64
D_SWA = A_HEADS * A_DH
KV_W = A_KV_HEADS * A_DH
WINDOW = 128
CHUNK = 128
LN_EPS = 1e-5
DEPTH = 1
DN_ALPHA = (2.0 * DEPTH) ** 0.25
N_BUF = N_META + WINDOW

C_Q, C_AQ = 0, 512
N_T_BF16 = 1024
C_V, C_O, C_Z, C_AZ, C_AV, C_G = 1024, 1536, 2048, 2560, 3072, 3200
GATE_ROWS = 16
N_T = C_G + GATE_ROWS
C_K, C_AK = 3584, 4096
N_PAD = 4224
N_KEYS = N_PAD - C_K
RAW_Q, RAW_K, RAW_V, RAW_O, RAW_Z, RAW_G = (0, 512), (512, 512), (1024, 512), (1536, 512), (2048, 512), (2560, 8)
RAW_AQ, RAW_AK, RAW_AV, RAW_AZ = (2568, 512), (3080, 128), (3208, 128), (3336, 512)
N_RAW = 3848

PROJ_STEP = 512
TOKEN_SPLITS = 2
CHUNK_UNROLL = 4
VMEM_LIMIT_BYTES = 56 * 1024 * 1024
NEG_INF = float("-inf")


def _dot(a, b):
    return jnp.dot(a, b, preferred_element_type=F32)


def _dot_nt(a, b):
    return lax.dot_general(a, b, (((1,), (1,)), ((), ())), preferred_element_type=F32)


def _layer_norm(x, g, b):
    mu = jnp.mean(x, axis=-1, keepdims=True)
    xc = x - mu
    var = jnp.mean(xc * xc, axis=-1, keepdims=True)
    return xc * lax.rsqrt(var + LN_EPS) * g + b


def _log_sigmoid(x):
    return jnp.minimum(x, 0.0) - jnp.log1p(jnp.exp(-jnp.abs(x)))


def _times_sigmoid(h, half_x):
    return 0.5 * (h * jnp.tanh(half_x) + h)


def _silu_of_twice(half_x):
    return half_x * jnp.tanh(half_x) + half_x


def _iota2(shape, dim):
    return lax.broadcasted_iota(jnp.int32, shape, dim)


T_FEATURES = ((RAW_G, C_G, 1.0), (RAW_Q, C_Q, 1.0), (RAW_AQ, C_AQ, 1.0), (RAW_AV, C_AV, 1.0), (RAW_V, C_V, 1.0),
              (RAW_O, C_O, 0.5), (RAW_Z, C_Z, 0.5), (RAW_AZ, C_AZ, 0.5))
KEY_FEATURES = ((RAW_K, C_K, 1.0), (RAW_AK, C_AK, 1.0))


def _weights(wt_ref, raw):
    return wt_ref[raw[0]:raw[0] + raw[1], :].astype(BF16)


def _project(token_sets, wt_ref, b_ref):
    for _, u_ref in token_sets:
        u_ref[:, C_G:C_K] = jnp.zeros((u_ref.shape[0], C_K - C_G), F32)
    for raw, dst, scale in T_FEATURES + KEY_FEATURES:
        w = _weights(wt_ref, raw)
        for hb, u_ref in token_sets:
            res = _dot_nt(hb, w) + b_ref[:, raw[0]:raw[0] + raw[1]]
            u_ref[:, dst:dst + raw[1]] = res if scale == 1.0 else res * scale


def _project_both(hb, tok, wt_ref, bcol_ref, utb_ref, utf_ref):
    reps = hb.shape[0] // 128
    assert RAW_Z[0] + RAW_Z[1] == RAW_G[0]
    for raw, dst, scale in ((RAW_Z[0], RAW_Z[1] + RAW_G[1]), C_Z, 0.5), *T_FEATURES:
        if raw in (RAW_Z, RAW_G):
            continue
        res = _dot_nt(_weights(wt_ref, raw), hb) + jnp.concatenate([bcol_ref[raw[0]:raw[0] + raw[1], :]] * reps, axis=1)
        if raw[1] > RAW_Z[1]:
            utf_ref[C_G - N_T_BF16:C_G - N_T_BF16 + RAW_G[1], tok] = res[RAW_Z[1]:]
            res = res[0:RAW_Z[1]]
        if scale != 1.0:
            res = res * scale
        if dst < N_T_BF16:
            utb_ref[dst:dst + res.shape[0], tok] = res.astype(BF16)
        else:
            utf_ref[dst - N_T_BF16:dst - N_T_BF16 + res.shape[0], tok] = res


def _project_keys(hb, wt_ref, brow_ref, ukey_ref):
    for raw, dst, _ in KEY_FEATURES:
        ukey_ref[:, dst - C_K:dst - C_K + raw[1]] = (_dot_nt(hb, _weights(wt_ref, raw))
                                                     + brow_ref[:, raw[0]:raw[0] + raw[1]])


def _gate_rows(gates):
    return _gate_scan(jnp.concatenate([gates[r0:r0 + CHUNK].T[0:8, :] for r0 in range(0, gates.shape[0], CHUNK)],
                                      axis=0))


def _gate_rows_t(gates_t):
    return _gate_scan(jnp.concatenate([gates_t[:, c0:c0 + CHUNK] for c0 in range(0, gates_t.shape[1], CHUNK)],
                                      axis=0))


def _gate_scan(g_t):
    x = _log_sigmoid(g_t)
    lane = _iota2(x.shape, 1)
    shift = 1
    while shift < x.shape[1]:
        x = x + jnp.where(lane >= shift, pltpu.roll(x, shift, 1), 0.0)
        shift *= 2
    return g_t, x


CT_ROWS = M_DH + 8


def _mlstm_scores_and_state(q_tb, kb, v_t, li_row, b_row, m_old, ct_aug, key_ok, n_valid):
    n = kb.shape[0]
    row = _iota2((n, n), 0)
    col = _iota2((n, n), 1)
    a_row = li_row - b_row
    a_keys = jnp.broadcast_to(jnp.sum(jnp.where(row == col, a_row, 0.0), axis=1, keepdims=True), (n, n))
    a_t = jnp.where(key_ok, a_keys, NEG_INF)
    mm = jnp.maximum(m_old, jnp.max(a_t, axis=0, keepdims=True))
    w_t = jnp.exp(a_t - mm)
    lane = _iota2((1, n), 1)
    last = n_valid - 1
    mm_last = jnp.max(jnp.where(lane == last, mm, NEG_INF), axis=1, keepdims=True)
    m_new = jnp.sum(jnp.where(lane == last, b_row, 0.0), axis=1, keepdims=True) + mm_last
    w_state = jnp.exp(a_row - mm_last)
    if n_valid < n:
        w_state = jnp.where(lane < n_valid, w_state, 0.0)
    decay = jnp.exp(m_old - mm_last)
    scores_t = _dot(kb, q_tb)
    inter = _dot(ct_aug.astype(BF16), q_tb)
    ones_row = jnp.where(_iota2((CT_ROWS - M_DH, n), 0) == 0, w_state, 0.0)
    vtw = jnp.concatenate([v_t * w_state, ones_row], axis=0).astype(BF16)
    ct_aug_new = decay * ct_aug + _dot(vtw, kb)
    s_inter = jnp.exp(m_old - mm)
    floor = jnp.exp(-(b_row + mm))
    return (scores_t, w_t, v_t.astype(BF16), inter, s_inter, floor), ct_aug_new, m_new


def _mlstm_weighted_values(scores_t, w_t, vtb, inter, s_inter, floor):
    qkw_t = scores_t * w_t
    den = jnp.sum(qkw_t, axis=0, keepdims=True)
    num_t = _dot(vtb, qkw_t.astype(BF16))
    return num_t, den, inter, s_inter, floor


def _mlstm_finish(num_t, den, inter, s_inter, floor):
    num_t = num_t + inter[0:M_DH] * s_inter
    den = den + inter[M_DH:M_DH + 1] * s_inter
    return num_t * (1.0 / jnp.maximum(jnp.abs(den), floor))


def _mlstm_gate_head(h, half_o, half_z, norm_g, axis):
    hh = _times_sigmoid(h, half_o)
    mu = jnp.mean(hh, axis=axis, keepdims=True)
    hc = hh - mu
    var = jnp.mean(hc * hc, axis=axis, keepdims=True)
    return (hc * lax.rsqrt(var + LN_EPS) * norm_g * _silu_of_twice(half_z)).astype(BF16)


def _gate_mix(h_m, half_o, half_z, o_a, half_az, norm_g):
    parts = []
    for hd in range(M_HEADS):
        sl = slice(hd * M_DH, (hd + 1) * M_DH)
        parts.append(_mlstm_gate_head(h_m[:, sl], half_o[:, sl], half_z[:, sl], norm_g[:, sl], axis=-1))
    parts.append((o_a * _silu_of_twice(half_az)).astype(BF16))
    return jnp.concatenate(parts, axis=-1)


def _out_and_norm(hp, mix, wo_ref, g, b):
    z = DN_ALPHA * hp + _dot(mix, wo_ref[...])
    return _layer_norm(z, g, b)


def _small_projections_kernel(meta_ref, xs_ref, wt_ref, bias_ref, g0_ref, b0_ref,
                              ct0_ref, m0_ref, km_ref, vm_ref, vmt_ref, hs_ref, us_ref, u_ref):
    hs = _layer_norm(xs_ref[...], g0_ref[...], b0_ref[...])
    hs_ref[...] = hs
    meta = jnp.concatenate([meta_ref[...], jnp.zeros((CHUNK - N_META, D_MODEL), F32)], axis=0)
    hp = _layer_norm(meta, g0_ref[...], b0_ref[...])
    _project(((hs.astype(BF16), us_ref), (hp.astype(BF16), u_ref)), wt_ref, bias_ref)
    row = _iota2((CHUNK, CHUNK), 0)
    col = _iota2((CHUNK, CHUNK), 1)
    key_ok = (row <= col) & (row < N_META)
    li_rows, b_rows = _gate_rows(u_ref[:, C_G:C_G + 128])
    zero_m = jnp.zeros((1, 128), F32)
    zero_ct = jnp.zeros((CT_ROWS, M_DH), F32)
    m0_ref[...] = jnp.zeros(m0_ref.shape, F32)
    for hd in range(M_HEADS):
        q = u_ref[:, C_Q + hd * M_DH:C_Q + (hd + 1) * M_DH]
        k = u_ref[:, C_K + hd * M_DH:C_K + (hd + 1) * M_DH] * (M_DH ** -0.5)
        v = u_ref[:, C_V + hd * M_DH:C_V + (hd + 1) * M_DH]
        _, ct_new, m_new = _mlstm_scores_and_state(
            q.T.astype(BF16), k.astype(BF16), v.T, li_rows[hd:hd + 1], b_rows[M_HEADS + hd:M_HEADS + hd + 1],
            zero_m, zero_ct, key_ok, N_META)
        ct0_ref[hd] = ct_new
        m0_ref[hd:hd + 1, :] = jnp.broadcast_to(m_new, (1, 128))
    km_ref[...] = u_ref[0:N_META, C_AK:C_AK + KV_W]
    vm_ref[...] = u_ref[0:N_META, C_AV:C_AV + KV_W]
    vmt_ref[...] = u_ref[:, C_AV:C_AV + KV_W].T[:, 0:N_META]


def _keep_kv_half(x, kv):
    low = _iota2(x.shape, 1) < A_DH
    return jnp.where(low if kv == 0 else ~low, x, 0.0)


def _to_token_rows(x_t, eye):
    return _dot_nt(eye, x_t).astype(BF16)


def _swa_weighted_values_t(scores, values_t, sink, own_ok, prev_ok):
    s_own = jnp.where(own_ok, scores[0], NEG_INF)
    s_prev = jnp.where(prev_ok, scores[1], NEG_INF)
    s_meta = scores[2]
    mx = jnp.maximum(jnp.maximum(jnp.max(s_own, axis=0, keepdims=True),
                                 jnp.max(s_prev, axis=0, keepdims=True)),
                     jnp.maximum(jnp.max(s_meta, axis=0, keepdims=True), sink))
    p_own, p_prev, p_meta = (jnp.exp(s - mx) for s in (s_own, s_prev, s_meta))
    den = (jnp.sum(p_own, axis=0, keepdims=True) + jnp.sum(p_prev, axis=0, keepdims=True)
           + jnp.sum(p_meta, axis=0, keepdims=True) + jnp.exp(sink - mx))
    o_t = (_dot(values_t[0], p_own.astype(BF16)) + _dot(values_t[1], p_prev.astype(BF16))
           + _dot(values_t[2], p_meta.astype(BF16)))
    return o_t, den


def _prompt_kernel(sink_ref, x_ref, wt_ref, bcol_ref, brow_ref, g0_ref, b0_ref,
                   ct0_ref, m0_ref, km_ref, vm_ref, vmt_ref, ng_ref, wo_ref, lng_ref, lnb_ref,
                   y_ref, pk_ref, pv_ref, pc_ref, pn_ref, pm_ref,
                   utb_ref, utf_ref, ukey_ref, hp_ref, mix_ref, ct_ref, kprev_ref, vtprev_ref, li_ref, cumf_ref,
                   *, tb):
    j = pl.program_id(1)
    n_chunks = tb // CHUNK

    @pl.when(j == 0)
    def _():
        ct_ref[...] = ct0_ref[...]
        pm_ref[0] = m0_ref[...]
        kprev_ref[...] = jnp.zeros(kprev_ref.shape, F32)
        vtprev_ref[...] = jnp.zeros(vtprev_ref.shape, F32)
        pk_ref[0, 0:N_META, :] = km_ref[...]
        pv_ref[0, 0:N_META, :] = vm_ref[...]

    hp = _layer_norm(x_ref[0], g0_ref[...], b0_ref[...])
    hp_ref[...] = hp
    hb = hp.astype(BF16)
    _project_keys(hb, wt_ref, brow_ref, ukey_ref)
    for t0 in range(0, tb, tb // TOKEN_SPLITS):
        tok = slice(t0, t0 + tb // TOKEN_SPLITS)
        _project_both(hb[tok], tok, wt_ref, bcol_ref, utb_ref, utf_ref)

    def feat(c0, n=M_DH):
        return slice(c0 - N_T_BF16, c0 - N_T_BF16 + n)

    li_ref[...], cumf_ref[...] = _gate_rows_t(utf_ref[feat(C_G, 8), :])

    def chunk_body(ci, carry):
        r0 = pl.multiple_of(ci * CHUNK, CHUNK)
        rows = pl.ds(r0, CHUNK)
        key = _iota2((CHUNK, CHUNK), 0)
        query = _iota2((CHUNK, CHUNK), 1)
        causal = key <= query

        gate_rows = pl.ds(pl.multiple_of(ci * 8, 8), 8)
        li_rows = li_ref[gate_rows, :]
        b_rows = cumf_ref[gate_rows, :]
        m_carry = []
        for hd in range(M_HEADS):
            q_tb = utb_ref[C_Q + hd * M_DH:C_Q + (hd + 1) * M_DH, rows]
            kb = (ukey_ref[rows, hd * M_DH:(hd + 1) * M_DH] * (M_DH ** -0.5)).astype(BF16)
            v_t = utf_ref[feat(C_V + hd * M_DH), rows]
            carry_hd, ct_new, m_new = _mlstm_scores_and_state(
                q_tb, kb, v_t, li_rows[hd:hd + 1], b_rows[M_HEADS + hd:M_HEADS + hd + 1],
                pm_ref[0, hd:hd + 1, :], ct_ref[hd], causal, CHUNK)
            ct_ref[hd] = ct_new
            pm_ref[0, hd:hd + 1, :] = jnp.broadcast_to(m_new, (1, 128))
            m_carry.append(carry_hd)

        k_own = ukey_ref[rows, C_AK - C_K:C_AK - C_K + KV_W]
        vt_own = utf_ref[feat(C_AV, KV_W), rows]
        k_prev = kprev_ref[...]
        vt_prev = vtprev_ref[...]
        prev_ok = (key > query) & ((j * n_chunks + ci) > 0)
        keys_kv = [tuple((_keep_kv_half(x, kv) * (A_DH ** -0.5)).astype(BF16) for x in (k_own, k_prev, km_ref[...]))
                   for kv in range(A_KV_HEADS)]
        values_kv = [tuple(x[kv * A_DH:(kv + 1) * A_DH].astype(BF16) for x in (vt_own, vt_prev, vmt_ref[...]))
                     for kv in range(A_KV_HEADS)]
        a_scores = []
        for hd in range(A_HEADS):
            kv = hd // A_GROUP
            q0 = C_AQ + (hd - kv) * A_DH
            q_win = utb_ref[q0:q0 + 2 * A_DH, rows]
            a_scores.append(tuple(_dot(kk, q_win) for kk in keys_kv[kv]))
        kprev_ref[...] = k_own
        vtprev_ref[...] = vt_own

        m_carry = [_mlstm_weighted_values(*c) for c in m_carry]
        a_out = [_swa_weighted_values_t(a_scores[hd], values_kv[hd // A_GROUP], sink_ref[hd], causal, prev_ok)
                 for hd in range(A_HEADS)]

        mix_t = []
        for hd in range(M_HEADS):
            mix_t.append(_mlstm_gate_head(_mlstm_finish(*m_carry[hd]), utf_ref[feat(C_O + hd * M_DH), rows],
                                          utf_ref[feat(C_Z + hd * M_DH), rows],
                                          ng_ref[hd * M_DH:(hd + 1) * M_DH, :], axis=0))
        for tile in range(A_HEADS // 2):
            o_t = jnp.concatenate([o * (1.0 / den) for o, den in a_out[2 * tile:2 * tile + 2]], axis=0)
            mix_t.append((o_t * _silu_of_twice(utf_ref[feat(C_AZ + tile * 128), rows])).astype(BF16))
        eye = jnp.where(key == query, 1.0, 0.0).astype(BF16)
        mix_rows = [_to_token_rows(x_t, eye) for x_t in mix_t]
        for i, x in enumerate(mix_rows):
            mix_ref[rows, i * 128:(i + 1) * 128] = x
        return carry

    lax.fori_loop(0, n_chunks, chunk_body, 0, unroll=CHUNK_UNROLL)

    for t0 in range(0, tb, tb // TOKEN_SPLITS):
        tok = slice(t0, t0 + tb // TOKEN_SPLITS)
        y_ref[0, tok, :] = _out_and_norm(hp_ref[tok, :], mix_ref[tok, :], wo_ref, lng_ref[...], lnb_ref[...])

    @pl.when(j == pl.num_programs(1) - 1)
    def _():
        pk_ref[0, N_META:N_BUF, :] = ukey_ref[tb - WINDOW:tb, C_AK - C_K:C_AK - C_K + KV_W]
        pv_ref[0, N_META:N_BUF, :] = utf_ref[feat(C_AV, KV_W), tb - WINDOW:tb].T
        for hd in range(M_HEADS):
            pc_ref[0, hd] = ct_ref[hd, 0:M_DH, :].T
            pn_ref[0, hd:hd + 1, :] = ct_ref[hd, M_DH:M_DH + 1, :]
        pn_ref[0, M_HEADS:8, :] = jnp.zeros((8 - M_HEADS, M_DH), F32)


def _const_spec(shape):
    return pl.BlockSpec(shape, lambda *_: (0,) * len(shape))


def _small_projections(meta_tokens, x_sample, ln0_g, ln0_b, w_t, b_all):
    rows = x_sample.shape[0] * x_sample.shape[1]
    return pl.pallas_call(
        _small_projections_kernel,
        out_shape=(jax.ShapeDtypeStruct((M_HEADS, CT_ROWS, M_DH), F32),
                   jax.ShapeDtypeStruct((8, 128), F32),
                   jax.ShapeDtypeStruct((N_META, KV_W), F32),
                   jax.ShapeDtypeStruct((N_META, KV_W), F32),
                   jax.ShapeDtypeStruct((KV_W, N_META), F32),
                   jax.ShapeDtypeStruct((rows, D_MODEL), F32),
                   jax.ShapeDtypeStruct((rows, N_PAD), F32)),
        scratch_shapes=[pltpu.VMEM((CHUNK, N_PAD), F32)],
        compiler_params=pltpu.CompilerParams(vmem_limit_bytes=VMEM_LIMIT_BYTES),
        name="small_projections",
    )(meta_tokens.astype(F32), x_sample.reshape(rows, D_MODEL), w_t, b_all, ln0_g, ln0_b)


def _prompt_path(x_prompt, meta_state, ln0_g, ln0_b, w_t, b_all, sinks, norm_g, w_o, ln_g, ln_b, tb=512):
    batch, seq, _ = x_prompt.shape
    ct0, m0, km, vm, vmt = meta_state

    nj = seq // tb
    in_specs = [
        pl.BlockSpec(memory_space=pltpu.SMEM),
        pl.BlockSpec((1, tb, D_MODEL), lambda b, j: (b, j, 0)),
        pl.BlockSpec((N_RAW, D_MODEL), lambda b, j: (0, 0), pipeline_mode=pl.Buffered(1)),
        _const_spec((N_RAW, 128)), _const_spec((1, N_RAW)),
        _const_spec((1, D_MODEL)), _const_spec((1, D_MODEL)),
        _const_spec((M_HEADS, CT_ROWS, M_DH)), _const_spec((8, 128)),
        _const_spec((N_META, KV_W)), _const_spec((N_META, KV_W)), _const_spec((KV_W, N_META)),
        _const_spec((D_MLSTM, 128)),
        _const_spec((D_MODEL, D_MODEL)),
        _const_spec((1, D_MODEL)), _const_spec((1, D_MODEL)),
    ]
    out_specs = [
        pl.BlockSpec((1, tb, D_MODEL), lambda b, j: (b, j, 0)),
        pl.BlockSpec((1, N_BUF, KV_W), lambda b, j: (b, 0, 0)),
        pl.BlockSpec((1, N_BUF, KV_W), lambda b, j: (b, 0, 0)),
        pl.BlockSpec((1, M_HEADS, M_DH, M_DH), lambda b, j: (b, 0, 0, 0)),
        pl.BlockSpec((1, 8, M_DH), lambda b, j: (b, 0, 0)),
        pl.BlockSpec((1, 8, 128), lambda b, j: (b, 0, 0)),
    ]
    out_shape = (
        jax.ShapeDtypeStruct((batch, seq, D_MODEL), F32),
        jax.ShapeDtypeStruct((batch, N_BUF, KV_W), F32),
        jax.ShapeDtypeStruct((batch, N_BUF, KV_W), F32),
        jax.ShapeDtypeStruct((batch, M_HEADS, M_DH, M_DH), F32),
        jax.ShapeDtypeStruct((batch, 8, M_DH), F32),
        jax.ShapeDtypeStruct((batch, 8, 128), F32),
    )
    y, pk, pv, pc, pn, pm = pl.pallas_call(
        functools.partial(_prompt_kernel, tb=tb),
        grid=(batch, nj),
        in_specs=in_specs,
        out_specs=out_specs,
        out_shape=out_shape,
        scratch_shapes=[pltpu.VMEM((N_T_BF16, tb), BF16), pltpu.VMEM((N_T - N_T_BF16, tb), F32),
                        pltpu.VMEM((tb, N_KEYS), F32),
                        pltpu.VMEM((tb, D_MODEL), F32), pltpu.VMEM((tb, D_MODEL), BF16),
                        pltpu.VMEM((M_HEADS, CT_ROWS, M_DH), F32),
                        pltpu.VMEM((CHUNK, KV_W), F32), pltpu.VMEM((KV_W, CHUNK), F32),
                        pltpu.VMEM((tb // CHUNK * 8, CHUNK), F32), pltpu.VMEM((tb // CHUNK * 8, CHUNK), F32)],
        compiler_params=pltpu.CompilerParams(dimension_semantics=("arbitrary", "arbitrary"),
                                             vmem_limit_bytes=VMEM_LIMIT_BYTES),
        name="prompt_layer",
    )(sinks, x_prompt, w_t, jnp.broadcast_to(b_all.reshape(N_RAW, 1), (N_RAW, 128)), b_all, ln0_g, ln0_b,
      ct0, m0, km, vm, vmt, jnp.broadcast_to(norm_g.reshape(D_MLSTM, 1), (D_MLSTM, 128)), w_o, ln_g, ln_b)
    pk = pk.reshape(1, batch, N_BUF, A_KV_HEADS, A_DH)
    pv = pv.reshape(1, batch, N_BUF, A_KV_HEADS, A_DH)
    return y, pk, pv, pc[None], pn[:, :M_HEADS][None], pm[:, :M_HEADS, 0][None]


SEQ_PER_GROUP = 32
SWA_SEQ_PER_STEP = 8
SEQ_BATCH = 8


def _segment_last(x, pos, seg_len):
    n = x.shape[1]
    step = 1
    while step < seg_len:
        x = jnp.where((pos // step) % 2 == 0, pltpu.roll(x, n - step, 1), x)
        step *= 2
    return x


def _sample_mlstm_kernel(q_ref, k_ref, v_ref, g_ref, c_ref, n_ref, m_ref,
                         h_ref, cn_ref, nn_ref, mn_ref, inter_ref, *, dec_seq):
    g_t = g_ref[...].T[0:8, :]
    for hd in range(M_HEADS):
        _sample_mlstm_head(hd, g_t, q_ref, k_ref, v_ref, c_ref, n_ref, m_ref,
                           h_ref, cn_ref, nn_ref, mn_ref, inter_ref, dec_seq)


def _sample_mlstm_head(hd, g_t, q_ref, k_ref, v_ref, c_ref, n_ref, m_ref,
                       h_ref, cn_ref, nn_ref, mn_ref, inter_ref, dec_seq):
    sl = slice(hd * M_DH, (hd + 1) * M_DH)
    n = CHUNK
    nb = n // dec_seq
    q = q_ref[:, sl]
    k = k_ref[:, sl] * (M_DH ** -0.5)
    v = v_ref[:, sl]
    qb = q.astype(BF16)
    kb = k.astype(BF16)
    vb = v.astype(BF16)

    li_row = g_t[hd:hd + 1, :]
    lf_row = _log_sigmoid(g_t[M_HEADS + hd:M_HEADS + hd + 1, :])
    lane = _iota2((1, n), 1)
    pos = lane % dec_seq
    b_row = lf_row
    shift = 1
    while shift < dec_seq:
        b_row = b_row + jnp.where(pos >= shift, pltpu.roll(b_row, shift, 1), 0.0)
        shift *= 2

    key = _iota2((n, n), 0)
    query = _iota2((n, n), 1)
    key_ok = (key <= query) & (key // dec_seq == query // dec_seq)
    a_row = li_row - b_row
    a_keys = jnp.broadcast_to(jnp.sum(jnp.where(key == query, a_row, 0.0), axis=1, keepdims=True), (n, n))
    a_t = jnp.where(key_ok, a_keys, NEG_INF)
    m_old = m_ref[hd, 0, 0:1, :]
    mm = jnp.maximum(m_old, jnp.max(a_t, axis=0, keepdims=True))
    w_t = jnp.exp(a_t - mm)
    mm_last = _segment_last(mm, pos, dec_seq)
    m_new = _segment_last(b_row, pos, dec_seq) + mm_last
    w_state = jnp.exp(a_row - mm_last)
    decay = jnp.exp(m_old - mm_last)
    mn_ref[hd, 0] = jnp.broadcast_to(m_new, (8, n))

    qkw_t = _dot_nt(kb, qb) * w_t
    den = jnp.sum(qkw_t, axis=0, keepdims=True)
    num_t = _dot(v.T.astype(BF16), qkw_t.astype(BF16))

    n_seq = n_ref[hd]
    seq_of_lane = _iota2((nb, n), 1) // dec_seq == _iota2((nb, n), 0)
    decay_seq = jnp.sum(jnp.where(_iota2((nb, n), 1) == _iota2((nb, n), 0) * dec_seq, decay, 0.0),
                        axis=1, keepdims=True)
    nn_ref[hd] = decay_seq * n_seq + _dot(jnp.where(seq_of_lane, w_state, 0.0).astype(BF16), kb)
    expand = jnp.where(_iota2((n, nb), 0) // dec_seq == _iota2((n, nb), 1), 1.0, 0.0).astype(BF16)
    n_rows = _dot(expand, n_seq.astype(BF16))
    qn_col = jnp.sum(q * n_rows, axis=1, keepdims=True)
    qn = jnp.sum(jnp.where(key == query, qn_col, 0.0), axis=0, keepdims=True)

    kwt = k.T * w_state
    decay_rows = jnp.broadcast_to(jnp.sum(jnp.where(key == query, decay, 0.0), axis=1, keepdims=True), (n, n))
    lane_seq = query // dec_seq
    low_rows = _iota2((8, M_DH), 0) < dec_seq
    assert 8 % dec_seq == 0 and 8 // dec_seq == 2
    for pair in range(nb // 2):
        q8 = q[8 * pair:8 * pair + 8].astype(BF16)
        inter_ref[8 * pair:8 * pair + 8, :] = jnp.where(low_rows, _dot(q8, c_ref[2 * pair, hd].astype(BF16)),
                                                        _dot(q8, c_ref[2 * pair + 1, hd].astype(BF16)))
    for s0 in range(0, nb, SEQ_BATCH):
        seqs = range(s0, s0 + SEQ_BATCH)
        updates = [_dot(jnp.where(lane_seq == s, kwt, 0.0).astype(BF16), vb) for s in seqs]
        for s, upd in zip(seqs, updates):
            cn_ref[s, hd] = decay_rows[s * dec_seq:s * dec_seq + 1, :] * c_ref[s, hd] + upd

    s_inter = jnp.exp(m_old - mm)
    num_t = num_t + inter_ref[...].T * s_inter
    den = den + qn * s_inter
    h_ref[:, sl] = (num_t * (1.0 / jnp.maximum(jnp.abs(den), jnp.exp(-(b_row + mm))))).T


def _sample_swa_kernel(aq_ref, kn_ref, vn_ref, ck_ref, cv_ref, sink_ref, o_ref, nk_ref, nv_ref, *, dec_seq):
    n_rows = A_HEADS * dec_seq
    t_c = _iota2((n_rows, N_BUF), 0) % dec_seq
    i_c = _iota2((n_rows, N_BUF), 1)
    ok_c = (i_c < N_META) | (i_c > t_c + N_META)
    t_n = _iota2((n_rows, 8), 0) % dec_seq
    i_n = _iota2((n_rows, 8), 1)
    ok_n = (i_n <= t_n) & (i_n < dec_seq)
    sink = sink_ref[:, 0:1]
    seqs = range(ck_ref.shape[0])
    rows_of = lambda s: slice(s * dec_seq, (s + 1) * dec_seq)

    low = _iota2((len(seqs) * dec_seq, 128), 1) < A_DH
    head_q = []
    for hd in range(A_HEADS):
        tile = aq_ref[:, (hd // 2) * 128:(hd // 2 + 1) * 128] * (A_DH ** -0.5)
        piece = jnp.where(low if hd % 2 == 0 else ~low, tile, 0.0)
        head_q.append(piece if hd % 2 == hd // A_GROUP else pltpu.roll(piece, A_DH, 1))
    pad_rows = jnp.zeros((8 - dec_seq, KV_W), F32)
    new_k = [jnp.concatenate([kn_ref[rows_of(s), :], pad_rows], axis=0).astype(BF16) for s in seqs]
    new_v = [jnp.concatenate([vn_ref[rows_of(s), :], pad_rows], axis=0).astype(BF16) for s in seqs]

    scores = []
    for s in seqs:
        qz = jnp.concatenate([q[rows_of(s), :] for q in head_q], axis=0).astype(BF16)
        scores.append((_dot_nt(qz, ck_ref[s].astype(BF16)), _dot_nt(qz, new_k[s])))
    probs = []
    for s_c, s_n in scores:
        s_c = jnp.where(ok_c, s_c, NEG_INF)
        s_n = jnp.where(ok_n, s_n, NEG_INF)
        mx = jnp.maximum(jnp.maximum(jnp.max(s_c, axis=1, keepdims=True),
                                     jnp.max(s_n, axis=1, keepdims=True)), sink)
        p_c = jnp.exp(s_c - mx)
        p_n = jnp.exp(s_n - mx)
        den = jnp.sum(p_c, axis=1, keepdims=True) + jnp.sum(p_n, axis=1, keepdims=True) + jnp.exp(sink - mx)
        probs.append((p_c.astype(BF16), p_n.astype(BF16), den))
    outs = [_dot(p_c, cv_ref[s].astype(BF16)) + _dot(p_n, new_v[s])
            for s, (p_c, p_n, _) in zip(seqs, probs)]
    low4 = _iota2((dec_seq, 128), 1) < A_DH
    for s, o, (_, _, den) in zip(seqs, outs, probs):
        o = o / den
        for pair in range(A_HEADS // 2):
            even, odd = (o[(2 * pair + e) * dec_seq:(2 * pair + e + 1) * dec_seq, :] for e in range(2))
            if pair // (A_GROUP // 2) == 0:
                odd = pltpu.roll(odd, A_DH, 1)
            else:
                even = pltpu.roll(even, A_DH, 1)
            o_ref[rows_of(s), pair * 128:(pair + 1) * 128] = jnp.where(low4, even, odd)
    for s in seqs:
        for cache_ref, new_ref, out_ref in ((ck_ref, kn_ref, nk_ref), (cv_ref, vn_ref, nv_ref)):
            out_ref[s, 0:N_META, :] = cache_ref[s, 0:N_META, :]
            out_ref[s, N_META:N_BUF - dec_seq, :] = cache_ref[s, N_META + dec_seq:N_BUF, :]
            out_ref[s, N_BUF - dec_seq:N_BUF, :] = new_ref[rows_of(s), :]


def _sample_out_kernel(hs_ref, half_o_ref, half_z_ref, half_az_ref, hm_ref, oa_ref, ng_ref, wo_ref, lng_ref, lnb_ref,
                       y_ref):
    mix = _gate_mix(hm_ref[...], half_o_ref[...], half_z_ref[...], oa_ref[...], half_az_ref[...], ng_ref[...])
    y_ref[...] = _out_and_norm(hs_ref[...], mix, wo_ref, lng_ref[...], lnb_ref[...])


def _sample_path(hs, u, db, dec_seq, cache_k, cache_v, state_c, state_n, state_m, a_sinks, norm_g, w_o, ln_g, ln_b):
    rows = db * dec_seq
    params = pltpu.CompilerParams(vmem_limit_bytes=VMEM_LIMIT_BYTES)

    n_groups = db // SEQ_PER_GROUP
    n_t = jnp.transpose(state_n, (1, 0, 2))
    m_t = jnp.repeat(jnp.transpose(state_m, (1, 0)), dec_seq, axis=1).reshape(M_HEADS, n_groups, 1, CHUNK)
    m_t = jnp.broadcast_to(m_t, (M_HEADS, n_groups, 8, CHUNK))

    def col_spec(col0):
        assert col0 % D_MLSTM == 0
        return pl.BlockSpec((CHUNK, D_MLSTM), lambda g: (g, col0 // D_MLSTM))

    state_spec = pl.BlockSpec((SEQ_PER_GROUP, M_HEADS, M_DH, M_DH), lambda g: (g, 0, 0, 0))
    vec_spec = pl.BlockSpec((M_HEADS, SEQ_PER_GROUP, M_DH), lambda g: (0, g, 0))
    m_spec = pl.BlockSpec((M_HEADS, 1, 8, CHUNK), lambda g: (0, g, 0, 0))
    h_m, c_new, n_new, m_new = pl.pallas_call(
        functools.partial(_sample_mlstm_kernel, dec_seq=dec_seq),
        grid=(n_groups,),
        in_specs=[col_spec(C_Q), col_spec(C_K), col_spec(C_V),
                  pl.BlockSpec((CHUNK, 128), lambda g: (g, C_G // 128)),
                  state_spec, vec_spec, m_spec],
        out_specs=[pl.BlockSpec((CHUNK, D_MLSTM), lambda g: (g, 0)), state_spec, vec_spec, m_spec],
        out_shape=(jax.ShapeDtypeStruct((rows, D_MLSTM), F32),
                   jax.ShapeDtypeStruct(state_c.shape, F32),
                   jax.ShapeDtypeStruct((M_HEADS, db, M_DH), F32),
                   jax.ShapeDtypeStruct((M_HEADS, n_groups, 8, CHUNK), F32)),
        scratch_shapes=[pltpu.VMEM((CHUNK, M_DH), F32)],
        compiler_params=pltpu.CompilerParams(dimension_semantics=("arbitrary",),
                                             vmem_limit_bytes=VMEM_LIMIT_BYTES),
        name="sample_mlstm",
    )(u, u, u, u, state_c, n_t, m_t)

    n_qrows = A_HEADS * dec_seq
    sink_rows = jnp.broadcast_to(jnp.repeat(a_sinks.astype(F32), dec_seq)[:, None], (n_qrows, 128))
    ck = cache_k.reshape(db, N_BUF, KV_W)
    cv = cache_v.reshape(db, N_BUF, KV_W)
    sb = SWA_SEQ_PER_STEP

    def token_spec(col0, width):
        assert col0 % width == 0
        return pl.BlockSpec((sb * dec_seq, width), lambda i: (i, col0 // width))

    cache_spec = pl.BlockSpec((sb, N_BUF, KV_W), lambda i: (i, 0, 0))
    o_a, nk, nv = pl.pallas_call(
        functools.partial(_sample_swa_kernel, dec_seq=dec_seq),
        grid=(db // sb,),
        in_specs=[token_spec(C_AQ, D_SWA), token_spec(C_AK, KV_W), token_spec(C_AV, KV_W), cache_spec, cache_spec,
                  pl.BlockSpec((n_qrows, 128), lambda i: (0, 0))],
        out_specs=[pl.BlockSpec((sb * dec_seq, D_SWA), lambda i: (i, 0)), cache_spec, cache_spec],
        out_shape=(jax.ShapeDtypeStruct((rows, D_SWA), F32),
                   jax.ShapeDtypeStruct((db, N_BUF, KV_W), F32),
                   jax.ShapeDtypeStruct((db, N_BUF, KV_W), F32)),
        compiler_params=pltpu.CompilerParams(dimension_semantics=("arbitrary",),
                                             vmem_limit_bytes=VMEM_LIMIT_BYTES),
        name="sample_swa",
    )(u, u, u, ck, cv, sink_rows)

    def gate_spec(col0):
        assert col0 % D_MLSTM == 0
        return pl.BlockSpec((rows, D_MLSTM), lambda i: (0, col0 // D_MLSTM))

    def whole(shape):
        return pl.BlockSpec(shape, lambda i: (0,) * len(shape))

    y = pl.pallas_call(
        _sample_out_kernel,
        grid=(1,),
        in_specs=[whole((rows, D_MODEL)), gate_spec(C_O), gate_spec(C_Z), gate_spec(C_AZ),
                  whole((rows, D_MLSTM)), whole((rows, D_SWA)), whole((1, D_MLSTM)), whole((D_MODEL, D_MODEL)),
                  whole((1, D_MODEL)), whole((1, D_MODEL))],
        out_specs=whole((rows, D_MODEL)),
        out_shape=jax.ShapeDtypeStruct((rows, D_MODEL), F32),
        compiler_params=params,
        name="sample_out",
    )(hs, u, u, u, h_m, o_a, norm_g, w_o, ln_g, ln_b)

    shape5 = (1, db, N_BUF, A_KV_HEADS, A_DH)
    return (y.reshape(db, dec_seq, D_MODEL), nk.reshape(shape5), nv.reshape(shape5), c_new[None],
            jnp.transpose(n_new, (1, 0, 2))[None], jnp.transpose(m_new[:, :, 0, ::dec_seq].reshape(M_HEADS, db), (1, 0))[None])


def kernel(x_prompt, x_sample, cache_swa_k, cache_swa_v, state_mlstm_c, state_mlstm_n, state_mlstm_m,
           meta_tokens, ln0_g, ln0_b, w_in, b_in, a_sinks, m_norm_g, w_out, ln_g, ln_b):
    assert w_in.shape[0] == DEPTH and x_prompt.shape[-1] == D_MODEL
    w_t = jnp.transpose(w_in[0].astype(F32))
    b_all = b_in[0].astype(F32)[None]
    w_o = w_out[0].astype(BF16)
    g0 = ln0_g.astype(F32)[None]
    b0 = ln0_b.astype(F32)[None]
    lg = ln_g[0].astype(F32)[None]
    lb = ln_b[0].astype(F32)[None]
    norm_g = m_norm_g[0].astype(F32)[None]
    sinks = a_sinks[0].astype(F32)

    *meta_state, hs, u = _small_projections(meta_tokens, x_sample, g0, b0, w_t, b_all)
    y_p, pk, pv, pc, pn, pm = _prompt_path(x_prompt, meta_state, g0, b0, w_t, b_all, sinks, norm_g, w_o, lg, lb)
    y_s, sk, sv, sc, sn, sm = _sample_path(hs, u, x_sample.shape[0], x_sample.shape[1], cache_swa_k[0], cache_swa_v[0],
                                           state_mlstm_c[0], state_mlstm_n[0], state_mlstm_m[0],
                                           sinks, norm_g, w_o, lg, lb)
    return (y_p, y_s, pk, pv, pc, pn, pm, sk, sv, sc, sn, sm)
```

```python
import functools

import jax
import jax.numpy as jnp
from jax import lax
from jax.experimental import pallas as pl
from jax.experimental.pallas import tpu as pltpu

F32 = jnp.float32
BF16 = jnp.bfloat16

D_MODEL = 1024
N_META = 16
M_HEADS = 4
M_DH = 128
D_MLSTM = M_HEADS * M_DH
A_HEADS = 8
A_KV_HEADS = 2
A_GROUP = A_HEADS // A_KV_HEADS
A_DH = 64
D_SWA = A_HEADS * A_DH
KV_W = A_KV_HEADS * A_DH
WINDOW = 128
CHUNK = 128
LN_EPS = 1e-5
DEPTH = 1
DN_ALPHA = (2.0 * DEPTH) ** 0.25
N_BUF = N_META + WINDOW

C_Q, C_AQ = 0, 512
N_T_BF16 = 1024
C_V, C_O, C_Z, C_AZ, C_AV, C_G = 1024, 1536, 2048, 2560, 3072, 3200
GATE_ROWS = 16
N_T = C_G + GATE_ROWS
C_K, C_AK = 3584, 4096
N_PAD = 4224
N_KEYS = N_PAD - C_K
RAW_Q, RAW_K, RAW_V, RAW_O, RAW_Z, RAW_G = (0, 512), (512, 512), (1024, 512), (1536, 512), (2048, 512), (2560, 8)
RAW_AQ, RAW_AK, RAW_AV, RAW_AZ = (2568, 512), (3080, 128), (3208, 128), (3336, 512)
N_RAW = 3848

PROJ_STEP = 512
TOKEN_SPLITS = 2
CHUNK_UNROLL = 4
VMEM_LIMIT_BYTES = 56 * 1024 * 1024
NEG_INF = float("-inf")


def _dot(a, b):
    return jnp.dot(a, b, preferred_element_type=F32)


def _dot_nt(a, b):
    return lax.dot_general(a, b, (((1,), (1,)), ((), ())), preferred_element_type=F32)


def _layer_norm(x, g, b):
    mu = jnp.mean(x, axis=-1, keepdims=True)
    xc = x - mu
    var = jnp.mean(xc * xc, axis=-1, keepdims=True)
    return xc * lax.rsqrt(var + LN_EPS) * g + b


def _log_sigmoid(x):
    return jnp.minimum(x, 0.0) - jnp.log1p(jnp.exp(-jnp.abs(x)))


def _times_sigmoid(h, half_x):
    return 0.5 * (h * jnp.tanh(half_x) + h)


def _silu_of_twice(half_x):
    return half_x * jnp.tanh(half_x) + half_x


def _iota2(shape, dim):
    return lax.broadcasted_iota(jnp.int32, shape, dim)


T_FEATURES = ((RAW_G, C_G, 1.0), (RAW_Q, C_Q, 1.0), (RAW_AQ, C_AQ, 1.0), (RAW_AV, C_AV, 1.0), (RAW_V, C_V, 1.0),
              (RAW_O, C_O, 0.5), (RAW_Z, C_Z, 0.5), (RAW_AZ, C_AZ, 0.5))
KEY_FEATURES = ((RAW_K, C_K, 1.0), (RAW_AK, C_AK, 1.0))


def _weights(wt_ref, raw):
    return wt_ref[raw[0]:raw[0] + raw[1], :].astype(BF16)


def _project(token_sets, wt_ref, b_ref):
    for _, u_ref in token_sets:
        u_ref[:, C_G:C_K] = jnp.zeros((u_ref.shape[0], C_K - C_G), F32)
    for raw, dst, scale in T_FEATURES + KEY_FEATURES:
        w = _weights(wt_ref, raw)
        for hb, u_ref in token_sets:
            res = _dot_nt(hb, w) + b_ref[:, raw[0]:raw[0] + raw[1]]
            u_ref[:, dst:dst + raw[1]] = res if scale == 1.0 else res * scale


def _project_both(hb, tok, wt_ref, bcol_ref, utb_ref, utf_ref):
    reps = hb.shape[0] // 128
    assert RAW_Z[0] + RAW_Z[1] == RAW_G[0]
    for raw, dst, scale in ((RAW_Z[0], RAW_Z[1] + RAW_G[1]), C_Z, 0.5), *T_FEATURES:
        if raw in (RAW_Z, RAW_G):
            continue
        res = _dot_nt(_weights(wt_ref, raw), hb) + jnp.concatenate([bcol_ref[raw[0]:raw[0] + raw[1], :]] * reps, axis=1)
        if raw[1] > RAW_Z[1]:
            utf_ref[C_G - N_T_BF16:C_G - N_T_BF16 + RAW_G[1], tok] = res[RAW_Z[1]:]
            res = res[0:RAW_Z[1]]
        if scale != 1.0:
            res = res * scale
        if dst < N_T_BF16:
            utb_ref[dst:dst + res.shape[0], tok] = res.astype(BF16)
        else:
            utf_ref[dst - N_T_BF16:dst - N_T_BF16 + res.shape[0], tok] = res


def _project_keys(hb, wt_ref, brow_ref, ukey_ref):
    for raw, dst, _ in KEY_FEATURES:
        ukey_ref[:, dst - C_K:dst - C_K + raw[1]] = (_dot_nt(hb, _weights(wt_ref, raw))
                                                     + brow_ref[:, raw[0]:raw[0] + raw[1]])


def _gate_rows(gates):
    return _gate_scan(jnp.concatenate([gates[r0:r0 + CHUNK].T[0:8, :] for r0 in range(0, gates.shape[0], CHUNK)],
                                      axis=0))


def _gate_rows_t(gates_t):
    return _gate_scan(jnp.concatenate([gates_t[:, c0:c0 + CHUNK] for c0 in range(0, gates_t.shape[1], CHUNK)],
                                      axis=0))


def _gate_scan(g_t):
    x = _log_sigmoid(g_t)
    lane = _iota2(x.shape, 1)
    shift = 1
    while shift < x.shape[1]:
        x = x + jnp.where(lane >= shift, pltpu.roll(x, shift, 1), 0.0)
        shift *= 2
    return g_t, x


CT_ROWS = M_DH + 8


def _mlstm_scores_and_state(q_tb, kb, v_t, li_row, b_row, m_old, ct_aug, key_ok, n_valid):
    n = kb.shape[0]
    row = _iota2((n, n), 0)
    col = _iota2((n, n), 1)
    a_row = li_row - b_row
    a_keys = jnp.broadcast_to(jnp.sum(jnp.where(row == col, a_row, 0.0), axis=1, keepdims=True), (n, n))
    a_t = jnp.where(key_ok, a_keys, NEG_INF)
    mm = jnp.maximum(m_old, jnp.max(a_t, axis=0, keepdims=True))
    w_t = jnp.exp(a_t - mm)
    lane = _iota2((1, n), 1)
    last = n_valid - 1
    mm_last = jnp.max(jnp.where(lane == last, mm, NEG_INF), axis=1, keepdims=True)
    m_new = jnp.sum(jnp.where(lane == last, b_row, 0.0), axis=1, keepdims=True) + mm_last
    w_state = jnp.exp(a_row - mm_last)
    if n_valid < n:
        w_state = jnp.where(lane < n_valid, w_state, 0.0)
    decay = jnp.exp(m_old - mm_last)
    scores_t = _dot(kb, q_tb)
    inter = _dot(ct_aug.astype(BF16), q_tb)
    ones_row = jnp.where(_iota2((CT_ROWS - M_DH, n), 0) == 0, w_state, 0.0)
    vtw = jnp.concatenate([v_t * w_state, ones_row], axis=0).astype(BF16)
    ct_aug_new = decay * ct_aug + _dot(vtw, kb)
    s_inter = jnp.exp(m_old - mm)
    floor = jnp.exp(-(b_row + mm))
    return (scores_t, w_t, v_t.astype(BF16), inter, s_inter, floor), ct_aug_new, m_new


def _mlstm_weighted_values(scores_t, w_t, vtb, inter, s_inter, floor):
    qkw_t = scores_t * w_t
    den = jnp.sum(qkw_t, axis=0, keepdims=True)
    num_t = _dot(vtb, qkw_t.astype(BF16))
    return num_t, den, inter, s_inter, floor


def _mlstm_finish(num_t, den, inter, s_inter, floor):
    num_t = num_t + inter[0:M_DH] * s_inter
    den = den + inter[M_DH:M_DH + 1] * s_inter
    return num_t * (1.0 / jnp.maximum(jnp.abs(den), floor))


def _mlstm_gate_head(h, half_o, half_z, norm_g, axis):
    hh = _times_sigmoid(h, half_o)
    mu = jnp.mean(hh, axis=axis, keepdims=True)
    hc = hh - mu
    var = jnp.mean(hc * hc, axis=axis, keepdims=True)
    return (hc * lax.rsqrt(var + LN_EPS) * norm_g * _silu_of_twice(half_z)).astype(BF16)


def _gate_mix(h_m, half_o, half_z, o_a, half_az, norm_g):
    parts = []
    for hd in range(M_HEADS):
        sl = slice(hd * M_DH, (hd + 1) * M_DH)
        parts.append(_mlstm_gate_head(h_m[:, sl], half_o[:, sl], half_z[:, sl], norm_g[:, sl], axis=-1))
    parts.append((o_a * _silu_of_twice(half_az)).astype(BF16))
    return jnp.concatenate(parts, axis=-1)


def _out_and_norm(hp, mix, wo_ref, g, b):
    z = DN_ALPHA * hp + _dot(mix, wo_ref[...])
    return _layer_norm(z, g, b)


def _small_projections_kernel(meta_ref, xs_ref, wt_ref, bias_ref, g0_ref, b0_ref,
                              ct0_ref, m0_ref, km_ref, vm_ref, vmt_ref, hs_ref, us_ref, u_ref):
    hs = _layer_norm(xs_ref[...].reshape(hs_ref.shape), g0_ref[...], b0_ref[...])
    hs_ref[...] = hs
    meta = jnp.concatenate([meta_ref[...], jnp.zeros((CHUNK - N_META, D_MODEL), F32)], axis=0)
    hp = _layer_norm(meta, g0_ref[...], b0_ref[...])
    _project(((hs.astype(BF16), us_ref), (hp.astype(BF16), u_ref)), wt_ref, bias_ref)
    row = _iota2((CHUNK, CHUNK), 0)
    col = _iota2((CHUNK, CHUNK), 1)
    key_ok = (row <= col) & (row < N_META)
    li_rows, b_rows = _gate_rows(u_ref[:, C_G:C_G + 128])
    zero_m = jnp.zeros((1, 128), F32)
    zero_ct = jnp.zeros((CT_ROWS, M_DH), F32)
    m0_ref[...] = jnp.zeros(m0_ref.shape, F32)
    for hd in range(M_HEADS):
        q = u_ref[:, C_Q + hd * M_DH:C_Q + (hd + 1) * M_DH]
        k = u_ref[:, C_K + hd * M_DH:C_K + (hd + 1) * M_DH] * (M_DH ** -0.5)
        v = u_ref[:, C_V + hd * M_DH:C_V + (hd + 1) * M_DH]
        _, ct_new, m_new = _mlstm_scores_and_state(
            q.T.astype(BF16), k.astype(BF16), v.T, li_rows[hd:hd + 1], b_rows[M_HEADS + hd:M_HEADS + hd + 1],
            zero_m, zero_ct, key_ok, N_META)
        ct0_ref[hd] = ct_new
        m0_ref[hd:hd + 1, :] = jnp.broadcast_to(m_new, (1, 128))
    km_ref[...] = u_ref[0:N_META, C_AK:C_AK + KV_W]
    vm_ref[...] = u_ref[0:N_META, C_AV:C_AV + KV_W]
    vmt_ref[...] = u_ref[:, C_AV:C_AV + KV_W].T[:, 0:N_META]


def _keep_kv_half(x, kv):
    low = _iota2(x.shape, 1) < A_DH
    return jnp.where(low if kv == 0 else ~low, x, 0.0)


def _to_token_rows(x_t, eye):
    return _dot_nt(eye, x_t).astype(BF16)


def _swa_weighted_values_t(scores, values_t, sink, own_ok, prev_ok):
    s_own = jnp.where(own_ok, scores[0], NEG_INF)
    s_prev = jnp.where(prev_ok, scores[1], NEG_INF)
    s_meta = scores[2]
    mx = jnp.maximum(jnp.maximum(jnp.max(s_own, axis=0, keepdims=True),
                                 jnp.max(s_prev, axis=0, keepdims=True)),
                     jnp.maximum(jnp.max(s_meta, axis=0, keepdims=True), sink))
    p_own, p_prev, p_meta = (jnp.exp(s - mx) for s in (s_own, s_prev, s_meta))
    den = (jnp.sum(p_own, axis=0, keepdims=True) + jnp.sum(p_prev, axis=0, keepdims=True)
           + jnp.sum(p_meta, axis=0, keepdims=True) + jnp.exp(sink - mx))
    o_t = (_dot(values_t[0], p_own.astype(BF16)) + _dot(values_t[1], p_prev.astype(BF16))
           + _dot(values_t[2], p_meta.astype(BF16)))
    return o_t, den


def _prompt_kernel(sink_ref, x_ref, wt_ref, bcol_ref, brow_ref, g0_ref, b0_ref,
                   ct0_ref, m0_ref, km_ref, vm_ref, vmt_ref, ng_ref, wo_ref, lng_ref, lnb_ref,
                   y_ref, pk_ref, pv_ref, pc_ref, pn_ref, pm_ref,
                   utb_ref, utf_ref, ukey_ref, hp_ref, mix_ref, ct_ref, kprev_ref, vtprev_ref, li_ref, cumf_ref,
                   *, tb):
    j = pl.program_id(1)
    n_chunks = tb // CHUNK

    @pl.when(j == 0)
    def _():
        ct_ref[...] = ct0_ref[...]
        pm_ref[0] = m0_ref[...]
        kprev_ref[...] = jnp.zeros(kprev_ref.shape, F32)
        vtprev_ref[...] = jnp.zeros(vtprev_ref.shape, F32)
        pk_ref[0, 0:N_META, :] = km_ref[...]
        pv_ref[0, 0:N_META, :] = vm_ref[...]

    hp = _layer_norm(x_ref[0], g0_ref[...], b0_ref[...])
    hp_ref[...] = hp
    hb = hp.astype(BF16)
    _project_keys(hb, wt_ref, brow_ref, ukey_ref)
    for t0 in range(0, tb, tb // TOKEN_SPLITS):
        tok = slice(t0, t0 + tb // TOKEN_SPLITS)
        _project_both(hb[tok], tok, wt_ref, bcol_ref, utb_ref, utf_ref)

    def feat(c0, n=M_DH):
        return slice(c0 - N_T_BF16, c0 - N_T_BF16 + n)

    li_ref[...], cumf_ref[...] = _gate_rows_t(utf_ref[feat(C_G, 8), :])

    def chunk_body(ci, carry):
        r0 = pl.multiple_of(ci * CHUNK, CHUNK)
        rows = pl.ds(r0, CHUNK)
        key = _iota2((CHUNK, CHUNK), 0)
        query = _iota2((CHUNK, CHUNK), 1)
        causal = key <= query

        gate_rows = pl.ds(pl.multiple_of(ci * 8, 8), 8)
        li_rows = li_ref[gate_rows, :]
        b_rows = cumf_ref[gate_rows, :]
        m_carry = []
        for hd in range(M_HEADS):
            q_tb = utb_ref[C_Q + hd * M_DH:C_Q + (hd + 1) * M_DH, rows]
            kb = (ukey_ref[rows, hd * M_DH:(hd + 1) * M_DH] * (M_DH ** -0.5)).astype(BF16)
            v_t = utf_ref[feat(C_V + hd * M_DH), rows]
            carry_hd, ct_new, m_new = _mlstm_scores_and_state(
                q_tb, kb, v_t, li_rows[hd:hd + 1], b_rows[M_HEADS + hd:M_HEADS + hd + 1],
                pm_ref[0, hd:hd + 1, :], ct_ref[hd], causal, CHUNK)
            ct_ref[hd] = ct_new
            pm_ref[0, hd:hd + 1, :] = jnp.broadcast_to(m_new, (1, 128))
            m_carry.append(carry_hd)

        k_own = ukey_ref[rows, C_AK - C_K:C_AK - C_K + KV_W]
        vt_own = utf_ref[feat(C_AV, KV_W), rows]
        k_prev = kprev_ref[...]
        vt_prev = vtprev_ref[...]
        prev_ok = (key > query) & ((j * n_chunks + ci) > 0)
        keys_kv = [tuple((_keep_kv_half(x, kv) * (A_DH ** -0.5)).astype(BF16) for x in (k_own, k_prev, km_ref[...]))
                   for kv in range(A_KV_HEADS)]
        values_kv = [tuple(x[kv * A_DH:(kv + 1) * A_DH].astype(BF16) for x in (vt_own, vt_prev, vmt_ref[...]))
                     for kv in range(A_KV_HEADS)]
        a_scores = []
        for hd in range(A_HEADS):
            kv = hd // A_GROUP
            q0 = C_AQ + (hd - kv) * A_DH
            q_win = utb_ref[q0:q0 + 2 * A_DH, rows]
            a_scores.append(tuple(_dot(kk, q_win) for kk in keys_kv[kv]))
        kprev_ref[...] = k_own
        vtprev_ref[...] = vt_own

        m_carry = [_mlstm_weighted_values(*c) for c in m_carry]
        a_out = [_swa_weighted_values_t(a_scores[hd], values_kv[hd // A_GROUP], sink_ref[hd], causal, prev_ok)
                 for hd in range(A_HEADS)]

        mix_t = []
        for hd in range(M_HEADS):
            mix_t.append(_mlstm_gate_head(_mlstm_finish(*m_carry[hd]), utf_ref[feat(C_O + hd * M_DH), rows],
                                          utf_ref[feat(C_Z + hd * M_DH), rows],
                                          ng_ref[hd * M_DH:(hd + 1) * M_DH, :], axis=0))
        for tile in range(A_HEADS // 2):
            o_t = jnp.concatenate([o * (1.0 / den) for o, den in a_out[2 * tile:2 * tile + 2]], axis=0)
            mix_t.append((o_t * _silu_of_twice(utf_ref[feat(C_AZ + tile * 128), rows])).astype(BF16))
        eye = jnp.where(key == query, 1.0, 0.0).astype(BF16)
        mix_rows = [_to_token_rows(x_t, eye) for x_t in mix_t]
        for i, x in enumerate(mix_rows):
            mix_ref[rows, i * 128:(i + 1) * 128] = x
        return carry

    lax.fori_loop(0, n_chunks, chunk_body, 0, unroll=CHUNK_UNROLL)

    for t0 in range(0, tb, tb // TOKEN_SPLITS):
        tok = slice(t0, t0 + tb // TOKEN_SPLITS)
        y_ref[0, tok, :] = _out_and_norm(hp_ref[tok, :], mix_ref[tok, :], wo_ref, lng_ref[...], lnb_ref[...])

    @pl.when(j == pl.num_programs(1) - 1)
    def _():
        pk_ref[0, N_META:N_BUF, :] = ukey_ref[tb - WINDOW:tb, C_AK - C_K:C_AK - C_K + KV_W]
        pv_ref[0, N_META:N_BUF, :] = utf_ref[feat(C_AV, KV_W), tb - WINDOW:tb].T
        for hd in range(M_HEADS):
            pc_ref[0, hd] = ct_ref[hd, 0:M_DH, :].T
            pn_ref[0, hd:hd + 1, :] = ct_ref[hd, M_DH:M_DH + 1, :]
        pn_ref[0, M_HEADS:8, :] = jnp.zeros((8 - M_HEADS, M_DH), F32)


def _const_spec(shape):
    return pl.BlockSpec(shape, lambda *_: (0,) * len(shape))


def _small_projections(meta_tokens, x_sample, ln0_g, ln0_b, w_t, b_all):
    rows = x_sample.shape[0] * x_sample.shape[1]
    return pl.pallas_call(
        _small_projections_kernel,
        out_shape=(jax.ShapeDtypeStruct((M_HEADS, CT_ROWS, M_DH), F32),
                   jax.ShapeDtypeStruct((8, 128), F32),
                   jax.ShapeDtypeStruct((N_META, KV_W), F32),
                   jax.ShapeDtypeStruct((N_META, KV_W), F32),
                   jax.ShapeDtypeStruct((KV_W, N_META), F32),
                   jax.ShapeDtypeStruct((rows, D_MODEL), F32),
                   jax.ShapeDtypeStruct((rows, N_PAD), F32)),
        scratch_shapes=[pltpu.VMEM((CHUNK, N_PAD), F32)],
        compiler_params=pltpu.CompilerParams(vmem_limit_bytes=VMEM_LIMIT_BYTES),
        name="small_projections",
    )(meta_tokens.astype(F32), x_sample, w_t, b_all, ln0_g, ln0_b)


def _prompt_path(x_prompt, meta_state, ln0_g, ln0_b, w_t, b_all, sinks, norm_g, w_o, ln_g, ln_b, tb=512):
    batch, seq, _ = x_prompt.shape
    ct0, m0, km, vm, vmt = meta_state

    nj = seq // tb
    in_specs = [
        pl.BlockSpec(memory_space=pltpu.SMEM),
        pl.BlockSpec((1, tb, D_MODEL), lambda b, j: (b, j, 0)),
        pl.BlockSpec((N_RAW, D_MODEL), lambda b, j: (0, 0), pipeline_mode=pl.Buffered(1)),
        _const_spec((N_RAW, 128)), _const_spec((1, N_RAW)),
        _const_spec((1, D_MODEL)), _const_spec((1, D_MODEL)),
        _const_spec((M_HEADS, CT_ROWS, M_DH)), _const_spec((8, 128)),
        _const_spec((N_META, KV_W)), _const_spec((N_META, KV_W)), _const_spec((KV_W, N_META)),
        _const_spec((D_MLSTM, 128)),
        _const_spec((D_MODEL, D_MODEL)),
        _const_spec((1, D_MODEL)), _const_spec((1, D_MODEL)),
    ]
    out_specs = [
        pl.BlockSpec((1, tb, D_MODEL), lambda b, j: (b, j, 0)),
        pl.BlockSpec((1, N_BUF, KV_W), lambda b, j: (b, 0, 0)),
        pl.BlockSpec((1, N_BUF, KV_W), lambda b, j: (b, 0, 0)),
        pl.BlockSpec((1, M_HEADS, M_DH, M_DH), lambda b, j: (b, 0, 0, 0)),
        pl.BlockSpec((1, 8, M_DH), lambda b, j: (b, 0, 0)),
        pl.BlockSpec((1, 8, 128), lambda b, j: (b, 0, 0)),
    ]
    out_shape = (
        jax.ShapeDtypeStruct((batch, seq, D_MODEL), F32),
        jax.ShapeDtypeStruct((batch, N_BUF, KV_W), F32),
        jax.ShapeDtypeStruct((batch, N_BUF, KV_W), F32),
        jax.ShapeDtypeStruct((batch, M_HEADS, M_DH, M_DH), F32),
        jax.ShapeDtypeStruct((batch, 8, M_DH), F32),
        jax.ShapeDtypeStruct((batch, 8, 128), F32),
    )
    y, pk, pv, pc, pn, pm = pl.pallas_call(
        functools.partial(_prompt_kernel, tb=tb),
        grid=(batch, nj),
        in_specs=in_specs,
        out_specs=out_specs,
        out_shape=out_shape,
        scratch_shapes=[pltpu.VMEM((N_T_BF16, tb), BF16), pltpu.VMEM((N_T - N_T_BF16, tb), F32),
                        pltpu.VMEM((tb, N_KEYS), F32),
                        pltpu.VMEM((tb, D_MODEL), F32), pltpu.VMEM((tb, D_MODEL), BF16),
                        pltpu.VMEM((M_HEADS, CT_ROWS, M_DH), F32),
                        pltpu.VMEM((CHUNK, KV_W), F32), pltpu.VMEM((KV_W, CHUNK), F32),
                        pltpu.VMEM((tb // CHUNK * 8, CHUNK), F32), pltpu.VMEM((tb // CHUNK * 8, CHUNK), F32)],
        compiler_params=pltpu.CompilerParams(dimension_semantics=("arbitrary", "arbitrary"),
                                             vmem_limit_bytes=VMEM_LIMIT_BYTES),
        name="prompt_layer",
    )(sinks, x_prompt, w_t, jnp.broadcast_to(b_all.reshape(N_RAW, 1), (N_RAW, 128)), b_all, ln0_g, ln0_b,
      ct0, m0, km, vm, vmt, jnp.broadcast_to(norm_g.reshape(D_MLSTM, 1), (D_MLSTM, 128)), w_o, ln_g, ln_b)
    pk = pk.reshape(1, batch, N_BUF, A_KV_HEADS, A_DH)
    pv = pv.reshape(1, batch, N_BUF, A_KV_HEADS, A_DH)
    return y, pk, pv, pc[None], pn[:, :M_HEADS][None], pm[:, :M_HEADS, 0][None]


SEQ_PER_GROUP = 32
SWA_SEQ_PER_STEP = 8
SEQ_BATCH = 8


def _segment_last(x, pos, seg_len):
    n = x.shape[1]
    step = 1
    while step < seg_len:
        x = jnp.where((pos // step) % 2 == 0, pltpu.roll(x, n - step, 1), x)
        step *= 2
    return x


def _sample_mlstm_kernel(q_ref, k_ref, v_ref, g_ref, c_ref, n_ref, m_ref,
                         h_ref, cn_ref, nn_ref, mn_ref, inter_ref, *, dec_seq):
    g_t = g_ref[...].T[0:8, :]
    for hd in range(M_HEADS):
        _sample_mlstm_head(hd, g_t, q_ref, k_ref, v_ref, c_ref, n_ref, m_ref,
                           h_ref, cn_ref, nn_ref, mn_ref, inter_ref, dec_seq)


def _sample_mlstm_head(hd, g_t, q_ref, k_ref, v_ref, c_ref, n_ref, m_ref,
                       h_ref, cn_ref, nn_ref, mn_ref, inter_ref, dec_seq):
    sl = slice(hd * M_DH, (hd + 1) * M_DH)
    n = CHUNK
    nb = n // dec_seq
    q = q_ref[:, sl]
    k = k_ref[:, sl] * (M_DH ** -0.5)
    v = v_ref[:, sl]
    qb = q.astype(BF16)
    kb = k.astype(BF16)
    vb = v.astype(BF16)

    li_row = g_t[hd:hd + 1, :]
    lf_row = _log_sigmoid(g_t[M_HEADS + hd:M_HEADS + hd + 1, :])
    lane = _iota2((1, n), 1)
    pos = lane % dec_seq
    b_row = lf_row
    shift = 1
    while shift < dec_seq:
        b_row = b_row + jnp.where(pos >= shift, pltpu.roll(b_row, shift, 1), 0.0)
        shift *= 2

    key = _iota2((n, n), 0)
    query = _iota2((n, n), 1)
    key_ok = (key <= query) & (key // dec_seq == query // dec_seq)
    a_row = li_row - b_row
    a_keys = jnp.broadcast_to(jnp.sum(jnp.where(key == query, a_row, 0.0), axis=1, keepdims=True), (n, n))
    a_t = jnp.where(key_ok, a_keys, NEG_INF)
    m_old = m_ref[hd, 0, 0:1, :]
    mm = jnp.maximum(m_old, jnp.max(a_t, axis=0, keepdims=True))
    w_t = jnp.exp(a_t - mm)
    mm_last = _segment_last(mm, pos, dec_seq)
    m_new = _segment_last(b_row, pos, dec_seq) + mm_last
    w_state = jnp.exp(a_row - mm_last)
    decay = jnp.exp(m_old - mm_last)
    mn_ref[hd, 0] = jnp.broadcast_to(m_new, (8, n))

    qkw_t = _dot_nt(kb, qb) * w_t
    den = jnp.sum(qkw_t, axis=0, keepdims=True)
    num_t = _dot(v.T.astype(BF16), qkw_t.astype(BF16))

    n_seq = n_ref[hd]
    seq_of_lane = _iota2((nb, n), 1) // dec_seq == _iota2((nb, n), 0)
    decay_seq = jnp.sum(jnp.where(_iota2((nb, n), 1) == _iota2((nb, n), 0) * dec_seq, decay, 0.0),
                        axis=1, keepdims=True)
    nn_ref[hd] = decay_seq * n_seq + _dot(jnp.where(seq_of_lane, w_state, 0.0).astype(BF16), kb)
    expand = jnp.where(_iota2((n, nb), 0) // dec_seq == _iota2((n, nb), 1), 1.0, 0.0).astype(BF16)
    n_rows = _dot(expand, n_seq.astype(BF16))
    qn_col = jnp.sum(q * n_rows, axis=1, keepdims=True)
    qn = jnp.sum(jnp.where(key == query, qn_col, 0.0), axis=0, keepdims=True)

    kwt = k.T * w_state
    decay_rows = jnp.broadcast_to(jnp.sum(jnp.where(key == query, decay, 0.0), axis=1, keepdims=True), (n, n))
    lane_seq = query // dec_seq
    low_rows = _iota2((8, M_DH), 0) < dec_seq
    assert 8 % dec_seq == 0 and 8 // dec_seq == 2
    for pair in range(nb // 2):
        q8 = q[8 * pair:8 * pair + 8].astype(BF16)
        inter_ref[8 * pair:8 * pair + 8, :] = jnp.where(low_rows, _dot(q8, c_ref[2 * pair, hd].astype(BF16)),
                                                        _dot(q8, c_ref[2 * pair + 1, hd].astype(BF16)))
    for s0 in range(0, nb, SEQ_BATCH):
        seqs = range(s0, s0 + SEQ_BATCH)
        updates = [_dot(jnp.where(lane_seq == s, kwt, 0.0).astype(BF16), vb) for s in seqs]
        for s, upd in zip(seqs, updates):
            cn_ref[s, hd] = decay_rows[s * dec_seq:s * dec_seq + 1, :] * c_ref[s, hd] + upd

    s_inter = jnp.exp(m_old - mm)
    num_t = num_t + inter_ref[...].T * s_inter
    den = den + qn * s_inter
    h_ref[:, sl] = (num_t * (1.0 / jnp.maximum(jnp.abs(den), jnp.exp(-(b_row + mm))))).T


def _sample_swa_kernel(aq_ref, kn_ref, vn_ref, ck_ref, cv_ref, sink_ref, o_ref, nk_ref, nv_ref, *, dec_seq):
    n_rows = A_HEADS * dec_seq
    t_c = _iota2((n_rows, N_BUF), 0) % dec_seq
    i_c = _iota2((n_rows, N_BUF), 1)
    ok_c = (i_c < N_META) | (i_c > t_c + N_META)
    t_n = _iota2((n_rows, 8), 0) % dec_seq
    i_n = _iota2((n_rows, 8), 1)
    ok_n = (i_n <= t_n) & (i_n < dec_seq)
    sink = sink_ref[:, 0:1]
    seqs = range(ck_ref.shape[0])
    rows_of = lambda s: slice(s * dec_seq, (s + 1) * dec_seq)

    low = _iota2((len(seqs) * dec_seq, 128), 1) < A_DH
    head_q = []
    for hd in range(A_HEADS):
        tile = aq_ref[:, (hd // 2) * 128:(hd // 2 + 1) * 128] * (A_DH ** -0.5)
        piece = jnp.where(low if hd % 2 == 0 else ~low, tile, 0.0)
        head_q.append(piece if hd % 2 == hd // A_GROUP else pltpu.roll(piece, A_DH, 1))
    pad_rows = jnp.zeros((8 - dec_seq, KV_W), F32)
    new_k = [jnp.concatenate([kn_ref[rows_of(s), :], pad_rows], axis=0).astype(BF16) for s in seqs]
    new_v = [jnp.concatenate([vn_ref[rows_of(s), :], pad_rows], axis=0).astype(BF16) for s in seqs]

    scores = []
    for s in seqs:
        qz = jnp.concatenate([q[rows_of(s), :] for q in head_q], axis=0).astype(BF16)
        scores.append((_dot_nt(qz, ck_ref[s].astype(BF16)), _dot_nt(qz, new_k[s])))
    probs = []
    for s_c, s_n in scores:
        s_c = jnp.where(ok_c, s_c, NEG_INF)
        s_n = jnp.where(ok_n, s_n, NEG_INF)
        mx = jnp.maximum(jnp.maximum(jnp.max(s_c, axis=1, keepdims=True),
                                     jnp.max(s_n, axis=1, keepdims=True)), sink)
        p_c = jnp.exp(s_c - mx)
        p_n = jnp.exp(s_n - mx)
        den = jnp.sum(p_c, axis=1, keepdims=True) + jnp.sum(p_n, axis=1, keepdims=True) + jnp.exp(sink - mx)
        probs.append((p_c.astype(BF16), p_n.astype(BF16), den))
    outs = [_dot(p_c, cv_ref[s].astype(BF16)) + _dot(p_n, new_v[s])
            for s, (p_c, p_n, _) in zip(seqs, probs)]
    low4 = _iota2((dec_seq, 128), 1) < A_DH
    for s, o, (_, _, den) in zip(seqs, outs, probs):
        o = o / den
        for pair in range(A_HEADS // 2):
            even, odd = (o[(2 * pair + e) * dec_seq:(2 * pair + e + 1) * dec_seq, :] for e in range(2))
            if pair // (A_GROUP // 2) == 0:
                odd = pltpu.roll(odd, A_DH, 1)
            else:
                even = pltpu.roll(even, A_DH, 1)
            o_ref[rows_of(s), pair * 128:(pair + 1) * 128] = jnp.where(low4, even, odd)
    for s in seqs:
        for cache_ref, new_ref, out_ref in ((ck_ref, kn_ref, nk_ref), (cv_ref, vn_ref, nv_ref)):
            out_ref[s, 0:N_META, :] = cache_ref[s, 0:N_META, :]
            out_ref[s, N_META:N_BUF - dec_seq, :] = cache_ref[s, N_META + dec_seq:N_BUF, :]
            out_ref[s, N_BUF - dec_seq:N_BUF, :] = new_ref[rows_of(s), :]


def _sample_out_kernel(hs_ref, half_o_ref, half_z_ref, half_az_ref, hm_ref, oa_ref, ng_ref, wo_ref, lng_ref, lnb_ref,
                       y_ref):
    mix = _gate_mix(hm_ref[...], half_o_ref[...], half_z_ref[...], oa_ref[...], half_az_ref[...], ng_ref[...])
    y_ref[...] = _out_and_norm(hs_ref[...], mix, wo_ref, lng_ref[...], lnb_ref[...]).reshape(y_ref.shape)


def _sample_path(hs, u, db, dec_seq, cache_k, cache_v, state_c, state_n, state_m, a_sinks, norm_g, w_o, ln_g, ln_b):
    rows = db * dec_seq
    params = pltpu.CompilerParams(vmem_limit_bytes=VMEM_LIMIT_BYTES)

    n_groups = db // SEQ_PER_GROUP
    n_t = jnp.transpose(state_n, (1, 0, 2))
    m_t = jnp.repeat(jnp.transpose(state_m, (1, 0)), dec_seq, axis=1).reshape(M_HEADS, n_groups, 1, CHUNK)
    m_t = jnp.broadcast_to(m_t, (M_HEADS, n_groups, 8, CHUNK))

    def col_spec(col0):
        assert col0 % D_MLSTM == 0
        return pl.BlockSpec((CHUNK, D_MLSTM), lambda g: (g, col0 // D_MLSTM))

    state_spec = pl.BlockSpec((SEQ_PER_GROUP, M_HEADS, M_DH, M_DH), lambda g: (g, 0, 0, 0))
    vec_spec = pl.BlockSpec((M_HEADS, SEQ_PER_GROUP, M_DH), lambda g: (0, g, 0))
    m_spec = pl.BlockSpec((M_HEADS, 1, 8, CHUNK), lambda g: (0, g, 0, 0))
    h_m, c_new, n_new, m_new = pl.pallas_call(
        functools.partial(_sample_mlstm_kernel, dec_seq=dec_seq),
        grid=(n_groups,),
        in_specs=[col_spec(C_Q), col_spec(C_K), col_spec(C_V),
                  pl.BlockSpec((CHUNK, 128), lambda g: (g, C_G // 128)),
                  state_spec, vec_spec, m_spec],
        out_specs=[pl.BlockSpec((CHUNK, D_MLSTM), lambda g: (g, 0)), state_spec, vec_spec, m_spec],
        out_shape=(jax.ShapeDtypeStruct((rows, D_MLSTM), F32),
                   jax.ShapeDtypeStruct(state_c.shape, F32),
                   jax.ShapeDtypeStruct((M_HEADS, db, M_DH), F32),
                   jax.ShapeDtypeStruct((M_HEADS, n_groups, 8, CHUNK), F32)),
        scratch_shapes=[pltpu.VMEM((CHUNK, M_DH), F32)],
        compiler_params=pltpu.CompilerParams(dimension_semantics=("arbitrary",),
                                             vmem_limit_bytes=VMEM_LIMIT_BYTES),
        name="sample_mlstm",
    )(u, u, u, u, state_c, n_t, m_t)

    n_qrows = A_HEADS * dec_seq
    sink_rows = jnp.broadcast_to(jnp.repeat(a_sinks.astype(F32), dec_seq)[:, None], (n_qrows, 128))
    ck = cache_k.reshape(db, N_BUF, KV_W)
    cv = cache_v.reshape(db, N_BUF, KV_W)
    sb = SWA_SEQ_PER_STEP

    def token_spec(col0, width):
        assert col0 % width == 0
        return pl.BlockSpec((sb * dec_seq, width), lambda i: (i, col0 // width))

    cache_spec = pl.BlockSpec((sb, N_BUF, KV_W), lambda i: (i, 0, 0))
    o_a, nk, nv = pl.pallas_call(
        functools.partial(_sample_swa_kernel, dec_seq=dec_seq),
        grid=(db // sb,),
        in_specs=[token_spec(C_AQ, D_SWA), token_spec(C_AK, KV_W), token_spec(C_AV, KV_W), cache_spec, cache_spec,
                  pl.BlockSpec((n_qrows, 128), lambda i: (0, 0))],
        out_specs=[pl.BlockSpec((sb * dec_seq, D_SWA), lambda i: (i, 0)), cache_spec, cache_spec],
        out_shape=(jax.ShapeDtypeStruct((rows, D_SWA), F32),
                   jax.ShapeDtypeStruct((db, N_BUF, KV_W), F32),
                   jax.ShapeDtypeStruct((db, N_BUF, KV_W), F32)),
        compiler_params=pltpu.CompilerParams(dimension_semantics=("arbitrary",),
                                             vmem_limit_bytes=VMEM_LIMIT_BYTES),
        name="sample_swa",
    )(u, u, u, ck, cv, sink_rows)

    def gate_spec(col0):
        assert col0 % D_MLSTM == 0
        return pl.BlockSpec((rows, D_MLSTM), lambda i: (0, col0 // D_MLSTM))

    def whole(shape):
        return pl.BlockSpec(shape, lambda i: (0,) * len(shape))

    y = pl.pallas_call(
        _sample_out_kernel,
        grid=(1,),
        in_specs=[whole((rows, D_MODEL)), gate_spec(C_O), gate_spec(C_Z), gate_spec(C_AZ),
                  whole((rows, D_MLSTM)), whole((rows, D_SWA)), whole((1, D_MLSTM)), whole((D_MODEL, D_MODEL)),
                  whole((1, D_MODEL)), whole((1, D_MODEL))],
        out_specs=whole((db, dec_seq, D_MODEL)),
        out_shape=jax.ShapeDtypeStruct((db, dec_seq, D_MODEL), F32),
        compiler_params=params,
        name="sample_out",
    )(hs, u, u, u, h_m, o_a, norm_g, w_o, ln_g, ln_b)

    shape5 = (1, db, N_BUF, A_KV_HEADS, A_DH)
    return (y, nk.reshape(shape5), nv.reshape(shape5), c_new[None],
            jnp.transpose(n_new, (1, 0, 2))[None], jnp.transpose(m_new[:, :, 0, ::dec_seq].reshape(M_HEADS, db), (1, 0))[None])


def kernel(x_prompt, x_sample, cache_swa_k, cache_swa_v, state_mlstm_c, state_mlstm_n, state_mlstm_m,
           meta_tokens, ln0_g, ln0_b, w_in, b_in, a_sinks, m_norm_g, w_out, ln_g, ln_b):
    assert w_in.shape[0] == DEPTH and x_prompt.shape[-1] == D_MODEL
    w_t = jnp.transpose(w_in[0].astype(F32))
    b_all = b_in[0].astype(F32)[None]
    w_o = w_out[0].astype(BF16)
    g0 = ln0_g.astype(F32)[None]
    b0 = ln0_b.astype(F32)[None]
    lg = ln_g[0].astype(F32)[None]
    lb = ln_b[0].astype(F32)[None]
    norm_g = m_norm_g[0].astype(F32)[None]
    sinks = a_sinks[0].astype(F32)

    *meta_state, hs, u = _small_projections(meta_tokens, x_sample, g0, b0, w_t, b_all)
    y_p, pk, pv, pc, pn, pm = _prompt_path(x_prompt, meta_state, g0, b0, w_t, b_all, sinks, norm_g, w_o, lg, lb)
    y_s, sk, sv, sc, sn, sm = _sample_path(hs, u, x_sample.shape[0], x_sample.shape[1], cache_swa_k[0], cache_swa_v[0],
                                           state_mlstm_c[0], state_mlstm_n[0], state_mlstm_m[0],
                                           sinks, norm_g, w_o, lg, lb)
    return (y_p, y_s, pk, pv, pc, pn, pm, sk, sv, sc, sn, sm)
```

```python
import functools

import jax
import jax.numpy as jnp
from jax import lax
from jax.experimental import pallas as pl
from jax.experimental.pallas import tpu as pltpu

F32 = jnp.float32
BF16 = jnp.bfloat16

D_MODEL = 1024
N_META = 16
M_HEADS = 4
M_DH = 128
D_MLSTM = M_HEADS * M_DH
A_HEADS = 8
A_KV_HEADS = 2
A_GROUP = A_HEADS // A_KV_HEADS
A_DH = 64
D_SWA = A_HEADS * A_DH
KV_W = A_KV_HEADS * A_DH
WINDOW = 128
CHUNK = 128
LN_EPS = 1e-5
DEPTH = 1
DN_ALPHA = (2.0 * DEPTH) ** 0.25
N_BUF = N_META + WINDOW

C_Q, C_AQ = 0, 512
N_T_BF16 = 1024
C_V, C_O, C_Z, C_AZ, C_AV, C_G = 1024, 1536, 2048, 2560, 3072, 3200
GATE_ROWS = 16
N_T = C_G + GATE_ROWS
C_K, C_AK = 3584, 4096
N_PAD = 4224
N_KEYS = N_PAD - C_K
RAW_Q, RAW_K, RAW_V, RAW_O, RAW_Z, RAW_G = (0, 512), (512, 512), (1024, 512), (1536, 512), (2048, 512), (2560, 8)
RAW_AQ, RAW_AK, RAW_AV, RAW_AZ = (2568, 512), (3080, 128), (3208, 128), (3336, 512)
N_RAW = 3848

PROJ_STEP = 512
TOKEN_SPLITS = 2
CHUNK_UNROLL = 4
VMEM_LIMIT_BYTES = 56 * 1024 * 1024
NEG_INF = float("-inf")


def _dot(a, b):
    return jnp.dot(a, b, preferred_element_type=F32)


def _dot_nt(a, b):
    return lax.dot_general(a, b, (((1,), (1,)), ((), ())), preferred_element_type=F32)


def _layer_norm(x, g, b):
    mu = jnp.mean(x, axis=-1, keepdims=True)
    xc = x - mu
    var = jnp.mean(xc * xc, axis=-1, keepdims=True)
    return xc * lax.rsqrt(var + LN_EPS) * g + b


def _log_sigmoid(x):
    return jnp.minimum(x, 0.0) - jnp.log1p(jnp.exp(-jnp.abs(x)))


def _times_sigmoid(h, half_x):
    return 0.5 * (h * jnp.tanh(half_x) + h)


def _silu_of_twice(half_x):
    return half_x * jnp.tanh(half_x) + half_x


def _iota2(shape, dim):
    return lax.broadcasted_iota(jnp.int32, shape, dim)


T_FEATURES = ((RAW_G, C_G, 1.0), (RAW_Q, C_Q, 1.0), (RAW_AQ, C_AQ, 1.0), (RAW_AV, C_AV, 1.0), (RAW_V, C_V, 1.0),
              (RAW_O, C_O, 0.5), (RAW_Z, C_Z, 0.5), (RAW_AZ, C_AZ, 0.5))
KEY_FEATURES = ((RAW_K, C_K, 1.0), (RAW_AK, C_AK, 1.0))


def _weights(wt_ref, raw):
    return wt_ref[raw[0]:raw[0] + raw[1], :].astype(BF16)


def _project(token_sets, wt_ref, b_ref):
    for _, u_ref in token_sets:
        u_ref[:, C_G:C_K] = jnp.zeros((u_ref.shape[0], C_K - C_G), F32)
    for raw, dst, scale in T_FEATURES + KEY_FEATURES:
        w = _weights(wt_ref, raw)
        for hb, u_ref in token_sets:
            res = _dot_nt(hb, w) + b_ref[:, raw[0]:raw[0] + raw[1]]
            u_ref[:, dst:dst + raw[1]] = res if scale == 1.0 else res * scale


def _project_both(hb, tok, wt_ref, bcol_ref, utb_ref, utf_ref):
    reps = hb.shape[0] // 128
    assert RAW_Z[0] + RAW_Z[1] == RAW_G[0]
    for raw, dst, scale in ((RAW_Z[0], RAW_Z[1] + RAW_G[1]), C_Z, 0.5), *T_FEATURES:
        if raw in (RAW_Z, RAW_G):
            continue
        res = _dot_nt(_weights(wt_ref, raw), hb) + jnp.concatenate([bcol_ref[raw[0]:raw[0] + raw[1], :]] * reps, axis=1)
        if raw[1] > RAW_Z[1]:
            utf_ref[C_G - N_T_BF16:C_G - N_T_BF16 + RAW_G[1], tok] = res[RAW_Z[1]:]
            res = res[0:RAW_Z[1]]
        if scale != 1.0:
            res = res * scale
        if dst < N_T_BF16:
            utb_ref[dst:dst + res.shape[0], tok] = res.astype(BF16)
        else:
            utf_ref[dst - N_T_BF16:dst - N_T_BF16 + res.shape[0], tok] = res


def _project_keys(hb, wt_ref, brow_ref, ukey_ref):
    for raw, dst, _ in KEY_FEATURES:
        ukey_ref[:, dst - C_K:dst - C_K + raw[1]] = (_dot_nt(hb, _weights(wt_ref, raw))
                                                     + brow_ref[:, raw[0]:raw[0] + raw[1]])


def _gate_rows(gates):
    return _gate_scan(jnp.concatenate([gates[r0:r0 + CHUNK].T[0:8, :] for r0 in range(0, gates.shape[0], CHUNK)],
                                      axis=0))


def _gate_rows_t(gates_t):
    return _gate_scan(jnp.concatenate([gates_t[:, c0:c0 + CHUNK] for c0 in range(0, gates_t.shape[1], CHUNK)],
                                      axis=0))


def _gate_scan(g_t):
    x = _log_sigmoid(g_t)
    lane = _iota2(x.shape, 1)
    shift = 1
    while shift < x.shape[1]:
        x = x + jnp.where(lane >= shift, pltpu.roll(x, shift, 1), 0.0)
        shift *= 2
    return g_t, x


CT_ROWS = M_DH + 8


def _mlstm_scores_and_state(q_tb, kb, v_t, li_row, b_row, m_old, ct_aug, key_ok, n_valid):
    n = kb.shape[0]
    row = _iota2((n, n), 0)
    col = _iota2((n, n), 1)
    a_row = li_row - b_row
    a_keys = jnp.broadcast_to(jnp.sum(jnp.where(row == col, a_row, 0.0), axis=1, keepdims=True), (n, n))
    a_t = jnp.where(key_ok, a_keys, NEG_INF)
    mm = jnp.maximum(m_old, jnp.max(a_t, axis=0, keepdims=True))
    w_t = jnp.exp(a_t - mm)
    lane = _iota2((1, n), 1)
    last = n_valid - 1
    mm_last = jnp.max(jnp.where(lane == last, mm, NEG_INF), axis=1, keepdims=True)
    m_new = jnp.sum(jnp.where(lane == last, b_row, 0.0), axis=1, keepdims=True) + mm_last
    w_state = jnp.exp(a_row - mm_last)
    if n_valid < n:
        w_state = jnp.where(lane < n_valid, w_state, 0.0)
    decay = jnp.exp(m_old - mm_last)
    scores_t = _dot(kb, q_tb)
    inter = _dot(ct_aug.astype(BF16), q_tb)
    ones_row = jnp.where(_iota2((CT_ROWS - M_DH, n), 0) == 0, w_state, 0.0)
    vtw = jnp.concatenate([v_t * w_state, ones_row], axis=0).astype(BF16)
    ct_aug_new = decay * ct_aug + _dot(vtw, kb)
    s_inter = jnp.exp(m_old - mm)
    floor = jnp.exp(-(b_row + mm))
    return (scores_t, w_t, v_t.astype(BF16), inter, s_inter, floor), ct_aug_new, m_new


def _mlstm_weighted_values(scores_t, w_t, vtb, inter, s_inter, floor):
    qkw_t = scores_t * w_t
    den = jnp.sum(qkw_t, axis=0, keepdims=True)
    num_t = _dot(vtb, qkw_t.astype(BF16))
    return num_t, den, inter, s_inter, floor


def _mlstm_finish(num_t, den, inter, s_inter, floor):
    num_t = num_t + inter[0:M_DH] * s_inter
    den = den + inter[M_DH:M_DH + 1] * s_inter
    return num_t * (1.0 / jnp.maximum(jnp.abs(den), floor))


def _mlstm_gate_head(h, half_o, half_z, norm_g, axis):
    hh = _times_sigmoid(h, half_o)
    mu = jnp.mean(hh, axis=axis, keepdims=True)
    hc = hh - mu
    var = jnp.mean(hc * hc, axis=axis, keepdims=True)
    return (hc * lax.rsqrt(var + LN_EPS) * norm_g * _silu_of_twice(half_z)).astype(BF16)


def _gate_mix(h_m, half_o, half_z, o_a, half_az, norm_g):
    parts = []
    for hd in range(M_HEADS):
        sl = slice(hd * M_DH, (hd + 1) * M_DH)
        parts.append(_mlstm_gate_head(h_m[:, sl], half_o[:, sl], half_z[:, sl], norm_g[:, sl], axis=-1))
    parts.append((o_a * _silu_of_twice(half_az)).astype(BF16))
    return jnp.concatenate(parts, axis=-1)


def _out_and_norm(hp, mix, wo_ref, g, b):
    z = DN_ALPHA * hp + _dot(mix, wo_ref[...])
    return _layer_norm(z, g, b)


def _small_projections_kernel(meta_ref, xs_ref, wt_ref, bias_ref, g0_ref, b0_ref,
                              ct0_ref, m0_ref, km_ref, vm_ref, vmt_ref, hs_ref, us_ref, u_ref):
    hs = _layer_norm(xs_ref[...].reshape(hs_ref.shape), g0_ref[...], b0_ref[...])
    hs_ref[...] = hs
    meta = jnp.concatenate([meta_ref[...], jnp.zeros((CHUNK - N_META, D_MODEL), F32)], axis=0)
    hp = _layer_norm(meta, g0_ref[...], b0_ref[...])
    _project(((hs.astype(BF16), us_ref), (hp.astype(BF16), u_ref)), wt_ref, bias_ref)
    row = _iota2((CHUNK, CHUNK), 0)
    col = _iota2((CHUNK, CHUNK), 1)
    key_ok = (row <= col) & (row < N_META)
    li_rows, b_rows = _gate_rows(u_ref[:, C_G:C_G + 128])
    zero_m = jnp.zeros((1, 128), F32)
    zero_ct = jnp.zeros((CT_ROWS, M_DH), F32)
    m0_ref[...] = jnp.zeros(m0_ref.shape, F32)
    for hd in range(M_HEADS):
        q = u_ref[:, C_Q + hd * M_DH:C_Q + (hd + 1) * M_DH]
        k = u_ref[:, C_K + hd * M_DH:C_K + (hd + 1) * M_DH] * (M_DH ** -0.5)
        v = u_ref[:, C_V + hd * M_DH:C_V + (hd + 1) * M_DH]
        _, ct_new, m_new = _mlstm_scores_and_state(
            q.T.astype(BF16), k.astype(BF16), v.T, li_rows[hd:hd + 1], b_rows[M_HEADS + hd:M_HEADS + hd + 1],
            zero_m, zero_ct, key_ok, N_META)
        ct0_ref[hd] = ct_new
        m0_ref[hd:hd + 1, :] = jnp.broadcast_to(m_new, (1, 128))
    km_ref[...] = u_ref[0:N_META, C_AK:C_AK + KV_W]
    vm_ref[...] = u_ref[0:N_META, C_AV:C_AV + KV_W]
    vmt_ref[...] = u_ref[:, C_AV:C_AV + KV_W].T[:, 0:N_META]


def _keep_kv_half(x, kv):
    low = _iota2(x.shape, 1) < A_DH
    return jnp.where(low if kv == 0 else ~low, x, 0.0)


def _to_token_rows(x_t, eye):
    return _dot_nt(eye, x_t).astype(BF16)


def _swa_weighted_values_t(scores, values_t, sink, own_ok, prev_ok):
    s_own = jnp.where(own_ok, scores[0], NEG_INF)
    s_prev = jnp.where(prev_ok, scores[1], NEG_INF)
    s_meta = scores[2]
    mx = jnp.maximum(jnp.maximum(jnp.max(s_own, axis=0, keepdims=True),
                                 jnp.max(s_prev, axis=0, keepdims=True)),
                     jnp.maximum(jnp.max(s_meta, axis=0, keepdims=True), sink))
    p_own, p_prev, p_meta = (jnp.exp(s - mx) for s in (s_own, s_prev, s_meta))
    den = (jnp.sum(p_own, axis=0, keepdims=True) + jnp.sum(p_prev, axis=0, keepdims=True)
           + jnp.sum(p_meta, axis=0, keepdims=True) + jnp.exp(sink - mx))
    o_t = (_dot(values_t[0], p_own.astype(BF16)) + _dot(values_t[1], p_prev.astype(BF16))
           + _dot(values_t[2], p_meta.astype(BF16)))
    return o_t, den


def _prompt_kernel(sink_ref, x_ref, wt_ref, bcol_ref, brow_ref, g0_ref, b0_ref,
                   ct0_ref, m0_ref, km_ref, vm_ref, vmt_ref, ng_ref, wo_ref, lng_ref, lnb_ref,
                   y_ref, pk_ref, pv_ref, pc_ref, pn_ref, pm_ref,
                   utb_ref, utf_ref, ukey_ref, hp_ref, mix_ref, ct_ref, kprev_ref, vtprev_ref, li_ref, cumf_ref,
                   *, tb):
    j = pl.program_id(1)
    n_chunks = tb // CHUNK

    @pl.when(j == 0)
    def _():
        ct_ref[...] = ct0_ref[...]
        pm_ref[0] = m0_ref[...]
        kprev_ref[...] = jnp.zeros(kprev_ref.shape, F32)
        vtprev_ref[...] = jnp.zeros(vtprev_ref.shape, F32)
        pk_ref[0, 0:N_META, :] = km_ref[...]
        pv_ref[0, 0:N_META, :] = vm_ref[...]

    hp = _layer_norm(x_ref[0], g0_ref[...], b0_ref[...])
    hp_ref[...] = hp
    hb = hp.astype(BF16)
    _project_keys(hb, wt_ref, brow_ref, ukey_ref)
    for t0 in range(0, tb, tb // TOKEN_SPLITS):
        tok = slice(t0, t0 + tb // TOKEN_SPLITS)
        _project_both(hb[tok], tok, wt_ref, bcol_ref, utb_ref, utf_ref)

    def feat(c0, n=M_DH):
        return slice(c0 - N_T_BF16, c0 - N_T_BF16 + n)

    li_ref[...], cumf_ref[...] = _gate_rows_t(utf_ref[feat(C_G, 8), :])

    def chunk_body(ci, carry):
        r0 = pl.multiple_of(ci * CHUNK, CHUNK)
        rows = pl.ds(r0, CHUNK)
        key = _iota2((CHUNK, CHUNK), 0)
        query = _iota2((CHUNK, CHUNK), 1)
        causal = key <= query

        gate_rows = pl.ds(pl.multiple_of(ci * 8, 8), 8)
        li_rows = li_ref[gate_rows, :]
        b_rows = cumf_ref[gate_rows, :]
        m_carry = []
        for hd in range(M_HEADS):
            q_tb = utb_ref[C_Q + hd * M_DH:C_Q + (hd + 1) * M_DH, rows]
            kb = (ukey_ref[rows, hd * M_DH:(hd + 1) * M_DH] * (M_DH ** -0.5)).astype(BF16)
            v_t = utf_ref[feat(C_V + hd * M_DH), rows]
            carry_hd, ct_new, m_new = _mlstm_scores_and_state(
                q_tb, kb, v_t, li_rows[hd:hd + 1], b_rows[M_HEADS + hd:M_HEADS + hd + 1],
                pm_ref[0, hd:hd + 1, :], ct_ref[hd], causal, CHUNK)
            ct_ref[hd] = ct_new
            pm_ref[0, hd:hd + 1, :] = jnp.broadcast_to(m_new, (1, 128))
            m_carry.append(carry_hd)

        k_own = ukey_ref[rows, C_AK - C_K:C_AK - C_K + KV_W]
        vt_own = utf_ref[feat(C_AV, KV_W), rows]
        k_prev = kprev_ref[...]
        vt_prev = vtprev_ref[...]
        prev_ok = (key > query) & ((j * n_chunks + ci) > 0)
        keys_kv = [tuple((_keep_kv_half(x, kv) * (A_DH ** -0.5)).astype(BF16) for x in (k_own, k_prev, km_ref[...]))
                   for kv in range(A_KV_HEADS)]
        values_kv = [tuple(x[kv * A_DH:(kv + 1) * A_DH].astype(BF16) for x in (vt_own, vt_prev, vmt_ref[...]))
                     for kv in range(A_KV_HEADS)]
        a_scores = []
        for hd in range(A_HEADS):
            kv = hd // A_GROUP
            q0 = C_AQ + (hd - kv) * A_DH
            q_win = utb_ref[q0:q0 + 2 * A_DH, rows]
            a_scores.append(tuple(_dot(kk, q_win) for kk in keys_kv[kv]))
        kprev_ref[...] = k_own
        vtprev_ref[...] = vt_own

        m_carry = [_mlstm_weighted_values(*c) for c in m_carry]
        a_out = [_swa_weighted_values_t(a_scores[hd], values_kv[hd // A_GROUP], sink_ref[hd], causal, prev_ok)
                 for hd in range(A_HEADS)]

        mix_t = []
        for hd in range(M_HEADS):
            mix_t.append(_mlstm_gate_head(_mlstm_finish(*m_carry[hd]), utf_ref[feat(C_O + hd * M_DH), rows],
                                          utf_ref[feat(C_Z + hd * M_DH), rows],
                                          ng_ref[hd * M_DH:(hd + 1) * M_DH, :], axis=0))
        for tile in range(A_HEADS // 2):
            o_t = jnp.concatenate([o * (1.0 / den) for o, den in a_out[2 * tile:2 * tile + 2]], axis=0)
            mix_t.append((o_t * _silu_of_twice(utf_ref[feat(C_AZ + tile * 128), rows])).astype(BF16))
        eye = jnp.where(key == query, 1.0, 0.0).astype(BF16)
        mix_rows = [_to_token_rows(x_t, eye) for x_t in mix_t]
        for i, x in enumerate(mix_rows):
            mix_ref[rows, i * 128:(i + 1) * 128] = x
        return carry

    lax.fori_loop(0, n_chunks, chunk_body, 0, unroll=CHUNK_UNROLL)

    for t0 in range(0, tb, tb // TOKEN_SPLITS):
        tok = slice(t0, t0 + tb // TOKEN_SPLITS)
        y_ref[0, tok, :] = _out_and_norm(hp_ref[tok, :], mix_ref[tok, :], wo_ref, lng_ref[...], lnb_ref[...])

    @pl.when(j == pl.num_programs(1) - 1)
    def _():
        pk_ref[0, N_META:N_BUF, :] = ukey_ref[tb - WINDOW:tb, C_AK - C_K:C_AK - C_K + KV_W]
        pv_ref[0, N_META:N_BUF, :] = utf_ref[feat(C_AV, KV_W), tb - WINDOW:tb].T
        for hd in range(M_HEADS):
            pc_ref[0, hd] = ct_ref[hd, 0:M_DH, :].T
            pn_ref[0, hd:hd + 1, :] = ct_ref[hd, M_DH:M_DH + 1, :]
        pn_ref[0, M_HEADS:8, :] = jnp.zeros((8 - M_HEADS, M_DH), F32)


def _const_spec(shape):
    return pl.BlockSpec(shape, lambda *_: (0,) * len(shape))


def _small_projections(meta_tokens, x_sample, ln0_g, ln0_b, w_t, b_all):
    rows = x_sample.shape[0] * x_sample.shape[1]
    return pl.pallas_call(
        _small_projections_kernel,
        out_shape=(jax.ShapeDtypeStruct((M_HEADS, CT_ROWS, M_DH), F32),
                   jax.ShapeDtypeStruct((8, 128), F32),
                   jax.ShapeDtypeStruct((N_META, KV_W), F32),
                   jax.ShapeDtypeStruct((N_META, KV_W), F32),
                   jax.ShapeDtypeStruct((KV_W, N_META), F32),
                   jax.ShapeDtypeStruct((rows, D_MODEL), F32),
                   jax.ShapeDtypeStruct((rows, N_PAD), F32)),
        scratch_shapes=[pltpu.VMEM((CHUNK, N_PAD), F32)],
        compiler_params=pltpu.CompilerParams(vmem_limit_bytes=VMEM_LIMIT_BYTES),
        name="small_projections",
    )(meta_tokens.astype(F32), x_sample, w_t, b_all, ln0_g, ln0_b)


def _prompt_path(x_prompt, meta_state, ln0_g, ln0_b, w_t, b_all, sinks, norm_g, w_o, ln_g, ln_b, tb=512):
    batch, seq, _ = x_prompt.shape
    ct0, m0, km, vm, vmt = meta_state

    nj = seq // tb
    in_specs = [
        pl.BlockSpec(memory_space=pltpu.SMEM),
        pl.BlockSpec((1, tb, D_MODEL), lambda b, j: (b, j, 0)),
        pl.BlockSpec((N_RAW, D_MODEL), lambda b, j: (0, 0), pipeline_mode=pl.Buffered(1)),
        _const_spec((N_RAW, 128)), _const_spec((1, N_RAW)),
        _const_spec((1, D_MODEL)), _const_spec((1, D_MODEL)),
        _const_spec((M_HEADS, CT_ROWS, M_DH)), _const_spec((8, 128)),
        _const_spec((N_META, KV_W)), _const_spec((N_META, KV_W)), _const_spec((KV_W, N_META)),
        _const_spec((D_MLSTM, 128)),
        _const_spec((D_MODEL, D_MODEL)),
        _const_spec((1, D_MODEL)), _const_spec((1, D_MODEL)),
    ]
    out_specs = [
        pl.BlockSpec((1, tb, D_MODEL), lambda b, j: (b, j, 0)),
        pl.BlockSpec((1, N_BUF, KV_W), lambda b, j: (b, 0, 0)),
        pl.BlockSpec((1, N_BUF, KV_W), lambda b, j: (b, 0, 0)),
        pl.BlockSpec((1, M_HEADS, M_DH, M_DH), lambda b, j: (b, 0, 0, 0)),
        pl.BlockSpec((1, 8, M_DH), lambda b, j: (b, 0, 0)),
        pl.BlockSpec((1, 8, 128), lambda b, j: (b, 0, 0)),
    ]
    out_shape = (
        jax.ShapeDtypeStruct((batch, seq, D_MODEL), F32),
        jax.ShapeDtypeStruct((batch, N_BUF, KV_W), F32),
        jax.ShapeDtypeStruct((batch, N_BUF, KV_W), F32),
        jax.ShapeDtypeStruct((batch, M_HEADS, M_DH, M_DH), F32),
        jax.ShapeDtypeStruct((batch, 8, M_DH), F32),
        jax.ShapeDtypeStruct((batch, 8, 128), F32),
    )
    y, pk, pv, pc, pn, pm = pl.pallas_call(
        functools.partial(_prompt_kernel, tb=tb),
        grid=(batch, nj),
        in_specs=in_specs,
        out_specs=out_specs,
        out_shape=out_shape,
        scratch_shapes=[pltpu.VMEM((N_T_BF16, tb), BF16), pltpu.VMEM((N_T - N_T_BF16, tb), F32),
                        pltpu.VMEM((tb, N_KEYS), F32),
                        pltpu.VMEM((tb, D_MODEL), F32), pltpu.VMEM((tb, D_MODEL), BF16),
                        pltpu.VMEM((M_HEADS, CT_ROWS, M_DH), F32),
                        pltpu.VMEM((CHUNK, KV_W), F32), pltpu.VMEM((KV_W, CHUNK), F32),
                        pltpu.VMEM((tb // CHUNK * 8, CHUNK), F32), pltpu.VMEM((tb // CHUNK * 8, CHUNK), F32)],
        compiler_params=pltpu.CompilerParams(dimension_semantics=("arbitrary", "arbitrary"),
                                             vmem_limit_bytes=VMEM_LIMIT_BYTES),
        name="prompt_layer",
    )(sinks, x_prompt, w_t, jnp.broadcast_to(b_all.reshape(N_RAW, 1), (N_RAW, 128)), b_all, ln0_g, ln0_b,
      ct0, m0, km, vm, vmt, jnp.broadcast_to(norm_g.reshape(D_MLSTM, 1), (D_MLSTM, 128)), w_o, ln_g, ln_b)
    pk = pk.reshape(1, batch, N_BUF, A_KV_HEADS, A_DH)
    pv = pv.reshape(1, batch, N_BUF, A_KV_HEADS, A_DH)
    return y, pk, pv, pc[None], pn[:, :M_HEADS][None], pm[:, :M_HEADS, 0][None]


SEQ_PER_GROUP = 32
SWA_SEQ_PER_STEP = 8
SEQ_BATCH = 8


def _segment_last(x, pos, seg_len):
    n = x.shape[1]
    step = 1
    while step < seg_len:
        x = jnp.where((pos // step) % 2 == 0, pltpu.roll(x, n - step, 1), x)
        step *= 2
    return x


def _sample_mlstm_kernel(q_ref, k_ref, v_ref, g_ref, c_ref, n_ref, m_ref,
                         h_ref, cn_ref, nn_ref, mn_ref, inter_ref, *, dec_seq):
    g_t = g_ref[...].T[0:8, :]
    for hd in range(M_HEADS):
        _sample_mlstm_head(hd, g_t, q_ref, k_ref, v_ref, c_ref, n_ref, m_ref,
                           h_ref, cn_ref, nn_ref, mn_ref, inter_ref, dec_seq)


def _sample_mlstm_head(hd, g_t, q_ref, k_ref, v_ref, c_ref, n_ref, m_ref,
                       h_ref, cn_ref, nn_ref, mn_ref, inter_ref, dec_seq):
    sl = slice(hd * M_DH, (hd + 1) * M_DH)
    n = CHUNK
    nb = n // dec_seq
    q = q_ref[:, sl]
    k = k_ref[:, sl] * (M_DH ** -0.5)
    v = v_ref[:, sl]
    qb = q.astype(BF16)
    kb = k.astype(BF16)
    vb = v.astype(BF16)

    li_row = g_t[hd:hd + 1, :]
    lf_row = _log_sigmoid(g_t[M_HEADS + hd:M_HEADS + hd + 1, :])
    lane = _iota2((1, n), 1)
    pos = lane % dec_seq
    b_row = lf_row
    shift = 1
    while shift < dec_seq:
        b_row = b_row + jnp.where(pos >= shift, pltpu.roll(b_row, shift, 1), 0.0)
        shift *= 2

    key = _iota2((n, n), 0)
    query = _iota2((n, n), 1)
    key_ok = (key <= query) & (key // dec_seq == query // dec_seq)
    a_row = li_row - b_row
    a_keys = jnp.broadcast_to(jnp.sum(jnp.where(key == query, a_row, 0.0), axis=1, keepdims=True), (n, n))
    a_t = jnp.where(key_ok, a_keys, NEG_INF)
    seq_of_lane = _iota2((nb, n), 1) // dec_seq == _iota2((nb, n), 0)
    first_lane = _iota2((nb, n), 1) == _iota2((nb, n), 0) * dec_seq
    m_old = jnp.sum(jnp.where(seq_of_lane, m_ref[:, hd:hd + 1], 0.0), axis=0, keepdims=True)
    mm = jnp.maximum(m_old, jnp.max(a_t, axis=0, keepdims=True))
    w_t = jnp.exp(a_t - mm)
    mm_last = _segment_last(mm, pos, dec_seq)
    m_new = _segment_last(b_row, pos, dec_seq) + mm_last
    w_state = jnp.exp(a_row - mm_last)
    decay = jnp.exp(m_old - mm_last)
    mn_ref[:, hd:hd + 1] = jnp.sum(jnp.where(first_lane, m_new, 0.0), axis=1, keepdims=True)

    qkw_t = _dot_nt(kb, qb) * w_t
    den = jnp.sum(qkw_t, axis=0, keepdims=True)
    num_t = _dot(v.T.astype(BF16), qkw_t.astype(BF16))

    n_seq = n_ref[:, hd, :]
    decay_seq = jnp.sum(jnp.where(first_lane, decay, 0.0), axis=1, keepdims=True)
    nn_ref[:, hd, :] = decay_seq * n_seq + _dot(jnp.where(seq_of_lane, w_state, 0.0).astype(BF16), kb)
    expand = jnp.where(_iota2((n, nb), 0) // dec_seq == _iota2((n, nb), 1), 1.0, 0.0).astype(BF16)
    n_rows = _dot(expand, n_seq.astype(BF16))
    qn_col = jnp.sum(q * n_rows, axis=1, keepdims=True)
    qn = jnp.sum(jnp.where(key == query, qn_col, 0.0), axis=0, keepdims=True)

    kwt = k.T * w_state
    decay_rows = jnp.broadcast_to(jnp.sum(jnp.where(key == query, decay, 0.0), axis=1, keepdims=True), (n, n))
    lane_seq = query // dec_seq
    low_rows = _iota2((8, M_DH), 0) < dec_seq
    assert 8 % dec_seq == 0 and 8 // dec_seq == 2
    for pair in range(nb // 2):
        q8 = q[8 * pair:8 * pair + 8].astype(BF16)
        inter_ref[8 * pair:8 * pair + 8, :] = jnp.where(low_rows, _dot(q8, c_ref[2 * pair, hd].astype(BF16)),
                                                        _dot(q8, c_ref[2 * pair + 1, hd].astype(BF16)))
    for s0 in range(0, nb, SEQ_BATCH):
        seqs = range(s0, s0 + SEQ_BATCH)
        updates = [_dot(jnp.where(lane_seq == s, kwt, 0.0).astype(BF16), vb) for s in seqs]
        for s, upd in zip(seqs, updates):
            cn_ref[s, hd] = decay_rows[s * dec_seq:s * dec_seq + 1, :] * c_ref[s, hd] + upd

    s_inter = jnp.exp(m_old - mm)
    num_t = num_t + inter_ref[...].T * s_inter
    den = den + qn * s_inter
    h_ref[:, sl] = (num_t * (1.0 / jnp.maximum(jnp.abs(den), jnp.exp(-(b_row + mm))))).T


def _sample_swa_kernel(aq_ref, kn_ref, vn_ref, ck_ref, cv_ref, sink_ref, o_ref, nk_ref, nv_ref, *, dec_seq):
    n_rows = A_HEADS * dec_seq
    t_c = _iota2((n_rows, N_BUF), 0) % dec_seq
    i_c = _iota2((n_rows, N_BUF), 1)
    ok_c = (i_c < N_META) | (i_c > t_c + N_META)
    t_n = _iota2((n_rows, 8), 0) % dec_seq
    i_n = _iota2((n_rows, 8), 1)
    ok_n = (i_n <= t_n) & (i_n < dec_seq)
    sink = sink_ref[:, 0:1]
    seqs = range(ck_ref.shape[0])
    rows_of = lambda s: slice(s * dec_seq, (s + 1) * dec_seq)

    low = _iota2((len(seqs) * dec_seq, 128), 1) < A_DH
    head_q = []
    for hd in range(A_HEADS):
        tile = aq_ref[:, (hd // 2) * 128:(hd // 2 + 1) * 128] * (A_DH ** -0.5)
        piece = jnp.where(low if hd % 2 == 0 else ~low, tile, 0.0)
        head_q.append(piece if hd % 2 == hd // A_GROUP else pltpu.roll(piece, A_DH, 1))
    pad_rows = jnp.zeros((8 - dec_seq, KV_W), F32)
    new_k = [jnp.concatenate([kn_ref[rows_of(s), :], pad_rows], axis=0).astype(BF16) for s in seqs]
    new_v = [jnp.concatenate([vn_ref[rows_of(s), :], pad_rows], axis=0).astype(BF16) for s in seqs]

    scores = []
    for s in seqs:
        qz = jnp.concatenate([q[rows_of(s), :] for q in head_q], axis=0).astype(BF16)
        scores.append((_dot_nt(qz, ck_ref[s].astype(BF16)), _dot_nt(qz, new_k[s])))
    probs = []
    for s_c, s_n in scores:
        s_c = jnp.where(ok_c, s_c, NEG_INF)
        s_n = jnp.where(ok_n, s_n, NEG_INF)
        mx = jnp.maximum(jnp.maximum(jnp.max(s_c, axis=1, keepdims=True),
                                     jnp.max(s_n, axis=1, keepdims=True)), sink)
        p_c = jnp.exp(s_c - mx)
        p_n = jnp.exp(s_n - mx)
        den = jnp.sum(p_c, axis=1, keepdims=True) + jnp.sum(p_n, axis=1, keepdims=True) + jnp.exp(sink - mx)
        probs.append((p_c.astype(BF16), p_n.astype(BF16), den))
    outs = [_dot(p_c, cv_ref[s].astype(BF16)) + _dot(p_n, new_v[s])
            for s, (p_c, p_n, _) in zip(seqs, probs)]
    low4 = _iota2((dec_seq, 128), 1) < A_DH
    for s, o, (_, _, den) in zip(seqs, outs, probs):
        o = o / den
        for pair in range(A_HEADS // 2):
            even, odd = (o[(2 * pair + e) * dec_seq:(2 * pair + e + 1) * dec_seq, :] for e in range(2))
            if pair // (A_GROUP // 2) == 0:
                odd = pltpu.roll(odd, A_DH, 1)
            else:
                even = pltpu.roll(even, A_DH, 1)
            o_ref[rows_of(s), pair * 128:(pair + 1) * 128] = jnp.where(low4, even, odd)
    for s in seqs:
        for cache_ref, new_ref, out_ref in ((ck_ref, kn_ref, nk_ref), (cv_ref, vn_ref, nv_ref)):
            out_ref[s, 0:N_META, :] = cache_ref[s, 0:N_META, :]
            out_ref[s, N_META:N_BUF - dec_seq, :] = cache_ref[s, N_META + dec_seq:N_BUF, :]
            out_ref[s, N_BUF - dec_seq:N_BUF, :] = new_ref[rows_of(s), :]


def _sample_out_kernel(hs_ref, half_o_ref, half_z_ref, half_az_ref, hm_ref, oa_ref, ng_ref, wo_ref, lng_ref, lnb_ref,
                       y_ref):
    mix = _gate_mix(hm_ref[...], half_o_ref[...], half_z_ref[...], oa_ref[...], half_az_ref[...], ng_ref[...])
    y_ref[...] = _out_and_norm(hs_ref[...], mix, wo_ref, lng_ref[...], lnb_ref[...]).reshape(y_ref.shape)


def _sample_path(hs, u, db, dec_seq, cache_k, cache_v, state_c, state_n, state_m, a_sinks, norm_g, w_o, ln_g, ln_b):
    rows = db * dec_seq
    params = pltpu.CompilerParams(vmem_limit_bytes=VMEM_LIMIT_BYTES)

    n_groups = db // SEQ_PER_GROUP

    def col_spec(col0):
        assert col0 % D_MLSTM == 0
        return pl.BlockSpec((CHUNK, D_MLSTM), lambda g: (g, col0 // D_MLSTM))

    state_spec = pl.BlockSpec((SEQ_PER_GROUP, M_HEADS, M_DH, M_DH), lambda g: (g, 0, 0, 0))
    vec_spec = pl.BlockSpec((SEQ_PER_GROUP, M_HEADS, M_DH), lambda g: (g, 0, 0))
    m_spec = pl.BlockSpec((SEQ_PER_GROUP, M_HEADS), lambda g: (g, 0))
    h_m, c_new, n_new, m_new = pl.pallas_call(
        functools.partial(_sample_mlstm_kernel, dec_seq=dec_seq),
        grid=(n_groups,),
        in_specs=[col_spec(C_Q), col_spec(C_K), col_spec(C_V),
                  pl.BlockSpec((CHUNK, 128), lambda g: (g, C_G // 128)),
                  state_spec, vec_spec, m_spec],
        out_specs=[pl.BlockSpec((CHUNK, D_MLSTM), lambda g: (g, 0)), state_spec, vec_spec, m_spec],
        out_shape=(jax.ShapeDtypeStruct((rows, D_MLSTM), F32),
                   jax.ShapeDtypeStruct(state_c.shape, F32),
                   jax.ShapeDtypeStruct(state_n.shape, F32),
                   jax.ShapeDtypeStruct(state_m.shape, F32)),
        scratch_shapes=[pltpu.VMEM((CHUNK, M_DH), F32)],
        compiler_params=pltpu.CompilerParams(dimension_semantics=("arbitrary",),
                                             vmem_limit_bytes=VMEM_LIMIT_BYTES),
        name="sample_mlstm",
    )(u, u, u, u, state_c, state_n, state_m)

    n_qrows = A_HEADS * dec_seq
    sink_rows = jnp.broadcast_to(jnp.repeat(a_sinks.astype(F32), dec_seq)[:, None], (n_qrows, 128))
    ck = cache_k.reshape(db, N_BUF, KV_W)
    cv = cache_v.reshape(db, N_BUF, KV_W)
    sb = SWA_SEQ_PER_STEP

    def token_spec(col0, width):
        assert col0 % width == 0
        return pl.BlockSpec((sb * dec_seq, width), lambda i: (i, col0 // width))

    cache_spec = pl.BlockSpec((sb, N_BUF, KV_W), lambda i: (i, 0, 0))
    o_a, nk, nv = pl.pallas_call(
        functools.partial(_sample_swa_kernel, dec_seq=dec_seq),
        grid=(db // sb,),
        in_specs=[token_spec(C_AQ, D_SWA), token_spec(C_AK, KV_W), token_spec(C_AV, KV_W), cache_spec, cache_spec,
                  pl.BlockSpec((n_qrows, 128), lambda i: (0, 0))],
        out_specs=[pl.BlockSpec((sb * dec_seq, D_SWA), lambda i: (i, 0)), cache_spec, cache_spec],
        out_shape=(jax.ShapeDtypeStruct((rows, D_SWA), F32),
                   jax.ShapeDtypeStruct((db, N_BUF, KV_W), F32),
                   jax.ShapeDtypeStruct((db, N_BUF, KV_W), F32)),
        compiler_params=pltpu.CompilerParams(dimension_semantics=("arbitrary",),
                                             vmem_limit_bytes=VMEM_LIMIT_BYTES),
        name="sample_swa",
    )(u, u, u, ck, cv, sink_rows)

    def gate_spec(col0):
        assert col0 % D_MLSTM == 0
        return pl.BlockSpec((rows, D_MLSTM), lambda i: (0, col0 // D_MLSTM))

    def whole(shape):
        return pl.BlockSpec(shape, lambda i: (0,) * len(shape))

    y = pl.pallas_call(
        _sample_out_kernel,
        grid=(1,),
        in_specs=[whole((rows, D_MODEL)), gate_spec(C_O), gate_spec(C_Z), gate_spec(C_AZ),
                  whole((rows, D_MLSTM)), whole((rows, D_SWA)), whole((1, D_MLSTM)), whole((D_MODEL, D_MODEL)),
                  whole((1, D_MODEL)), whole((1, D_MODEL))],
        out_specs=whole((db, dec_seq, D_MODEL)),
        out_shape=jax.ShapeDtypeStruct((db, dec_seq, D_MODEL), F32),
        compiler_params=params,
        name="sample_out",
    )(hs, u, u, u, h_m, o_a, norm_g, w_o, ln_g, ln_b)

    shape5 = (1, db, N_BUF, A_KV_HEADS, A_DH)
    return (y, nk.reshape(shape5), nv.reshape(shape5), c_new[None],
            n_new[None], m_new[None])


def kernel(x_prompt, x_sample, cache_swa_k, cache_swa_v, state_mlstm_c, state_mlstm_n, state_mlstm_m,
           meta_tokens, ln0_g, ln0_b, w_in, b_in, a_sinks, m_norm_g, w_out, ln_g, ln_b):
    assert w_in.shape[0] == DEPTH and x_prompt.shape[-1] == D_MODEL
    w_t = jnp.transpose(w_in[0].astype(F32))
    b_all = b_in[0].astype(F32)[None]
    w_o = w_out[0].astype(BF16)
    g0 = ln0_g.astype(F32)[None]
    b0 = ln0_b.astype(F32)[None]
    lg = ln_g[0].astype(F32)[None]
    lb = ln_b[0].astype(F32)[None]
    norm_g = m_norm_g[0].astype(F32)[None]
    sinks = a_sinks[0].astype(F32)

    *meta_state, hs, u = _small_projections(meta_tokens, x_sample, g0, b0, w_t, b_all)
    y_p, pk, pv, pc, pn, pm = _prompt_path(x_prompt, meta_state, g0, b0, w_t, b_all, sinks, norm_g, w_o, lg, lb)
    y_s, sk, sv, sc, sn, sm = _sample_path(hs, u, x_sample.shape[0], x_sample.shape[1], cache_swa_k[0], cache_swa_v[0],
                                           state_mlstm_c[0], state_mlstm_n[0], state_mlstm_m[0],
                                           sinks, norm_g, w_o, lg, lb)
    return (y_p, y_s, pk, pv, pc, pn, pm, sk, sv, sc, sn, sm)
```

```python
import functools

import jax
import jax.numpy as jnp
from jax import lax
from jax.experimental import pallas as pl
from jax.experimental.pallas import tpu as pltpu

F32 = jnp.float32
BF16 = jnp.bfloat16

D_MODEL = 1024
N_META = 16
M_HEADS = 4
M_DH = 128
D_MLSTM = M_HEADS * M_DH
A_HEADS = 8
A_KV_HEADS = 2
A_GROUP = A_HEADS // A_KV_HEADS
A_DH = 64
D_SWA = A_HEADS * A_DH
KV_W = A_KV_HEADS * A_DH
WINDOW = 128
CHUNK = 128
LN_EPS = 1e-5
DEPTH = 1
DN_ALPHA = (2.0 * DEPTH) ** 0.25
N_BUF = N_META + WINDOW

C_Q, C_AQ = 0, 512
N_T_BF16 = 1024
C_V, C_O, C_Z, C_AZ, C_AV, C_G = 1024, 1536, 2048, 2560, 3072, 3200
GATE_ROWS = 16
N_T = C_G + GATE_ROWS
C_K, C_AK = 3584, 4096
N_PAD = 4224
N_KEYS = N_PAD - C_K
RAW_Q, RAW_K, RAW_V, RAW_O, RAW_Z, RAW_G = (0, 512), (512, 512), (1024, 512), (1536, 512), (2048, 512), (2560, 8)
RAW_AQ, RAW_AK, RAW_AV, RAW_AZ = (2568, 512), (3080, 128), (3208, 128), (3336, 512)
N_RAW = 3848

TOKEN_SPLITS = 2
OUT_SPLITS = 4
CHUNK_UNROLL = 4
VMEM_LIMIT_BYTES = 56 * 1024 * 1024
NEG_INF = float("-inf")


def _dot(a, b):
    return jnp.dot(a, b, preferred_element_type=F32)


def _dot_nt(a, b):
    return lax.dot_general(a, b, (((1,), (1,)), ((), ())), preferred_element_type=F32)


def _layer_norm(x, g, b):
    mu = jnp.mean(x, axis=-1, keepdims=True)
    xc = x - mu
    var = jnp.mean(xc * xc, axis=-1, keepdims=True)
    return xc * lax.rsqrt(var + LN_EPS) * g + b


def _log_sigmoid(x):
    return jnp.minimum(x, 0.0) - jnp.log1p(jnp.exp(-jnp.abs(x)))


def _times_sigmoid(h, half_x):
    return 0.5 * (h * jnp.tanh(half_x) + h)


def _silu_of_twice(half_x):
    return half_x * jnp.tanh(half_x) + half_x


def _iota2(shape, dim):
    return lax.broadcasted_iota(jnp.int32, shape, dim)


T_FEATURES = ((RAW_G, C_G, 1.0), (RAW_Q, C_Q, 1.0), (RAW_AQ, C_AQ, A_DH ** -0.5), (RAW_AV, C_AV, 1.0), (RAW_V, C_V, 1.0),
              (RAW_O, C_O, 0.5), (RAW_Z, C_Z, 0.5), (RAW_AZ, C_AZ, 0.5))
KEY_FEATURES = ((RAW_K, C_K, 1.0), (RAW_AK, C_AK, 1.0))


def _weights(wt_ref, raw):
    return wt_ref[raw[0]:raw[0] + raw[1], :].astype(BF16)


def _project(token_sets, wt_hbm_ref, wt_ref, sem_ref, b_ref):
    features = T_FEATURES + KEY_FEATURES
    copies = [pltpu.make_async_copy(wt_hbm_ref.at[raw[0]:raw[0] + raw[1]], wt_ref.at[raw[0]:raw[0] + raw[1]],
                                    sem_ref.at[i]) for i, (raw, _, _) in enumerate(features)]
    for copy in copies:
        copy.start()
    for _, u_ref in token_sets:
        u_ref[:, C_G:C_K] = jnp.zeros((u_ref.shape[0], C_K - C_G), F32)
    for copy, (raw, dst, scale) in zip(copies, features):
        copy.wait()
        w = _weights(wt_ref, raw)
        for hb, u_ref in token_sets:
            res = _dot_nt(hb, w) + b_ref[:, raw[0]:raw[0] + raw[1]]
            u_ref[:, dst:dst + raw[1]] = res if scale == 1.0 else res * scale


def _project_both(hb, tok, wt_ref, bcol_ref, utb_ref, utf_ref):
    reps = hb.shape[0] // 128
    assert RAW_Z[0] + RAW_Z[1] == RAW_G[0]
    for raw, dst, scale in ((RAW_Z[0], RAW_Z[1] + RAW_G[1]), C_Z, 0.5), *T_FEATURES:
        if raw in (RAW_Z, RAW_G):
            continue
        res = _dot_nt(_weights(wt_ref, raw), hb) + jnp.concatenate([bcol_ref[raw[0]:raw[0] + raw[1], :]] * reps, axis=1)
        if raw[1] > RAW_Z[1]:
            utf_ref[C_G - N_T_BF16:C_G - N_T_BF16 + RAW_G[1], tok] = res[RAW_Z[1]:]
            res = res[0:RAW_Z[1]]
        if scale != 1.0:
            res = res * scale
        if dst < N_T_BF16:
            utb_ref[dst:dst + res.shape[0], tok] = res.astype(BF16)
        else:
            utf_ref[dst - N_T_BF16:dst - N_T_BF16 + res.shape[0], tok] = res


def _project_keys(hb, tok, wt_ref, brow_ref, ukey_ref):
    for raw, dst, _ in KEY_FEATURES:
        ukey_ref[tok, dst - C_K:dst - C_K + raw[1]] = (_dot_nt(hb, _weights(wt_ref, raw))
                                                       + brow_ref[:, raw[0]:raw[0] + raw[1]])


def _gate_rows(gates):
    return _gate_scan(jnp.concatenate([gates[r0:r0 + CHUNK].T[0:8, :] for r0 in range(0, gates.shape[0], CHUNK)],
                                      axis=0))


def _gate_rows_t(gates_t):
    return _gate_scan(jnp.concatenate([gates_t[:, c0:c0 + CHUNK] for c0 in range(0, gates_t.shape[1], CHUNK)],
                                      axis=0))


def _gate_scan(g_t):
    x = _log_sigmoid(g_t)
    lane = _iota2(x.shape, 1)
    shift = 1
    while shift < x.shape[1]:
        x = x + jnp.where(lane >= shift, pltpu.roll(x, shift, 1), 0.0)
        shift *= 2
    return g_t, x


CT_ROWS = M_DH + 8


def _mlstm_scores_and_state(q_tb, kb, v_t, li_row, b_row, m_old, ct_aug, key_ok, n_valid):
    n = kb.shape[0]
    row = _iota2((n, n), 0)
    col = _iota2((n, n), 1)
    a_row = li_row - b_row
    a_keys = jnp.broadcast_to(jnp.sum(jnp.where(row == col, a_row, 0.0), axis=1, keepdims=True), (n, n))
    a_t = jnp.where(key_ok, a_keys, NEG_INF)
    mm = jnp.maximum(m_old, jnp.max(a_t, axis=0, keepdims=True))
    w_t = jnp.exp(a_t - mm)
    lane = _iota2((1, n), 1)
    last = n_valid - 1
    mm_last = jnp.max(jnp.where(lane == last, mm, NEG_INF), axis=1, keepdims=True)
    m_new = jnp.sum(jnp.where(lane == last, b_row, 0.0), axis=1, keepdims=True) + mm_last
    w_state = jnp.exp(a_row - mm_last)
    if n_valid < n:
        w_state = jnp.where(lane < n_valid, w_state, 0.0)
    decay = jnp.exp(m_old - mm_last)
    scores_t = _dot(kb, q_tb)
    inter = _dot(ct_aug.astype(BF16), q_tb)
    ones_row = jnp.where(_iota2((CT_ROWS - M_DH, n), 0) == 0, w_state, 0.0)
    vtw = jnp.concatenate([v_t * w_state, ones_row], axis=0).astype(BF16)
    ct_aug_new = decay * ct_aug + _dot(vtw, kb)
    s_inter = jnp.exp(m_old - mm)
    floor = jnp.exp(-(b_row + mm))
    return (scores_t, w_t, v_t.astype(BF16), inter, s_inter, floor), ct_aug_new, m_new


def _mlstm_weighted_values(scores_t, w_t, vtb, inter, s_inter, floor):
    qkw_t = scores_t * w_t
    den = jnp.sum(qkw_t, axis=0, keepdims=True)
    num_t = _dot(vtb, qkw_t.astype(BF16))
    return num_t, den, inter, s_inter, floor


def _mlstm_finish(num_t, den, inter, s_inter, floor):
    num_t = num_t + inter[0:M_DH] * s_inter
    den = den + inter[M_DH:M_DH + 1] * s_inter
    return num_t * (1.0 / jnp.maximum(jnp.abs(den), floor))


def _mlstm_gate_head(h, half_o, half_z, norm_g, axis):
    hh = _times_sigmoid(h, half_o)
    mu = jnp.mean(hh, axis=axis, keepdims=True)
    hc = hh - mu
    var = jnp.mean(hc * hc, axis=axis, keepdims=True)
    return hc * lax.rsqrt(var + LN_EPS) * norm_g * _silu_of_twice(half_z)


def _gate_mix(h_m, half_o, half_z, o_a, half_az, norm_g):
    parts = []
    for hd in range(M_HEADS):
        sl = slice(hd * M_DH, (hd + 1) * M_DH)
        parts.append(_mlstm_gate_head(h_m[:, sl], half_o[:, sl], half_z[:, sl], norm_g[:, sl], axis=-1).astype(BF16))
    parts.append((o_a * _silu_of_twice(half_az)).astype(BF16))
    return jnp.concatenate(parts, axis=-1)


def _out_and_norm(hp, mix, wo_ref, g, b):
    z = DN_ALPHA * hp + _dot(mix, wo_ref[...])
    return _layer_norm(z, g, b)


def _small_projections_kernel(meta_ref, xs_ref, wt_hbm_ref, bias_ref, g0_ref, b0_ref,
                              ct0_ref, m0_ref, km_ref, vm_ref, vmt_ref, hs_ref, us_ref, u_ref, wt_ref, sem_ref):
    hs = _layer_norm(xs_ref[...].reshape(hs_ref.shape), g0_ref[...], b0_ref[...])
    hs_ref[...] = hs
    meta = jnp.concatenate([meta_ref[...], jnp.zeros((CHUNK - N_META, D_MODEL), F32)], axis=0)
    hp = _layer_norm(meta, g0_ref[...], b0_ref[...])
    _project(((hs.astype(BF16), us_ref), (hp.astype(BF16), u_ref)), wt_hbm_ref, wt_ref, sem_ref, bias_ref)
    row = _iota2((CHUNK, CHUNK), 0)
    col = _iota2((CHUNK, CHUNK), 1)
    key_ok = (row <= col) & (row < N_META)
    li_rows, b_rows = _gate_rows(u_ref[:, C_G:C_G + 128])
    zero_m = jnp.zeros((1, 128), F32)
    zero_ct = jnp.zeros((CT_ROWS, M_DH), F32)
    m0_ref[...] = jnp.zeros(m0_ref.shape, F32)
    for hd in range(M_HEADS):
        q = u_ref[:, C_Q + hd * M_DH:C_Q + (hd + 1) * M_DH]
        k = u_ref[:, C_K + hd * M_DH:C_K + (hd + 1) * M_DH] * (M_DH ** -0.5)
        v = u_ref[:, C_V + hd * M_DH:C_V + (hd + 1) * M_DH]
        _, ct_new, m_new = _mlstm_scores_and_state(
            q.T.astype(BF16), k.astype(BF16), v.T, li_rows[hd:hd + 1], b_rows[M_HEADS + hd:M_HEADS + hd + 1],
            zero_m, zero_ct, key_ok, N_META)
        ct0_ref[hd] = ct_new
        m0_ref[hd:hd + 1, :] = jnp.broadcast_to(m_new, (1, 128))
    km_ref[...] = u_ref[0:N_META, C_AK:C_AK + KV_W]
    vm_ref[...] = u_ref[0:N_META, C_AV:C_AV + KV_W]
    vmt_ref[...] = u_ref[:, C_AV:C_AV + KV_W].T[:, 0:N_META]


def _keep_kv_half(x, kv):
    low = _iota2(x.shape, 1) < A_DH
    return jnp.where(low if kv == 0 else ~low, x, 0.0)


def _swa_weighted_values_t(scores, values_t, sink, own_ok, prev_ok):
    s_own = jnp.where(own_ok, scores[0], NEG_INF)
    s_prev = jnp.where(prev_ok, scores[1], NEG_INF)
    s_meta = scores[2]
    mx = jnp.maximum(jnp.maximum(jnp.max(s_own, axis=0, keepdims=True),
                                 jnp.max(s_prev, axis=0, keepdims=True)),
                     jnp.maximum(jnp.max(s_meta, axis=0, keepdims=True), sink))
    p_own, p_prev, p_meta = (jnp.exp(s - mx) for s in (s_own, s_prev, s_meta))
    den = (jnp.sum(p_own, axis=0, keepdims=True) + jnp.sum(p_prev, axis=0, keepdims=True)
           + jnp.sum(p_meta, axis=0, keepdims=True) + jnp.exp(sink - mx))
    o_t = (_dot(values_t[0], p_own.astype(BF16)) + _dot(values_t[1], p_prev.astype(BF16))
           + _dot(values_t[2], p_meta.astype(BF16)))
    return o_t, den


def _prompt_kernel(sink_ref, x_ref, wt_ref, bcol_ref, brow_ref, g0_ref, b0_ref,
                   ct0_ref, m0_ref, km_ref, vm_ref, vmt_ref, ng_ref, wo_ref, lng_ref, lnb_ref,
                   y_ref, pk_ref, pv_ref, pc_ref, pn_ref, pm_ref,
                   utb_ref, utf_ref, ukey_ref, hp_ref, mix_ref, ct_ref, kprev_ref, vtprev_ref, li_ref, cumf_ref,
                   *, tb):
    j = pl.program_id(1)
    n_chunks = tb // CHUNK

    @pl.when(j == 0)
    def _():
        ct_ref[...] = ct0_ref[...]
        pm_ref[0] = m0_ref[...]
        kprev_ref[...] = jnp.zeros(kprev_ref.shape, F32)
        vtprev_ref[...] = jnp.zeros(vtprev_ref.shape, F32)
        pk_ref[0, 0:N_META, :] = km_ref[...]
        pv_ref[0, 0:N_META, :] = vm_ref[...]

    for t0 in range(0, tb, tb // TOKEN_SPLITS):
        tok = slice(t0, t0 + tb // TOKEN_SPLITS)
        hp = _layer_norm(x_ref[0, tok, :], g0_ref[...], b0_ref[...])
        hp_ref[tok, :] = hp
        hb = hp.astype(BF16)
        _project_keys(hb, tok, wt_ref, brow_ref, ukey_ref)
        _project_both(hb, tok, wt_ref, bcol_ref, utb_ref, utf_ref)

    def feat(c0, n=M_DH):
        return slice(c0 - N_T_BF16, c0 - N_T_BF16 + n)

    li_ref[...], cumf_ref[...] = _gate_rows_t(utf_ref[feat(C_G, 8), :])

    def chunk_body(ci, carry):
        r0 = pl.multiple_of(ci * CHUNK, CHUNK)
        rows = pl.ds(r0, CHUNK)
        key = _iota2((CHUNK, CHUNK), 0)
        query = _iota2((CHUNK, CHUNK), 1)
        causal = key <= query

        gate_rows = pl.ds(pl.multiple_of(ci * 8, 8), 8)
        li_rows = li_ref[gate_rows, :]
        b_rows = cumf_ref[gate_rows, :]
        m_carry = []
        for hd in range(M_HEADS):
            q_tb = utb_ref[C_Q + hd * M_DH:C_Q + (hd + 1) * M_DH, rows]
            kb = (ukey_ref[rows, hd * M_DH:(hd + 1) * M_DH] * (M_DH ** -0.5)).astype(BF16)
            v_t = utf_ref[feat(C_V + hd * M_DH), rows]
            carry_hd, ct_new, m_new = _mlstm_scores_and_state(
                q_tb, kb, v_t, li_rows[hd:hd + 1], b_rows[M_HEADS + hd:M_HEADS + hd + 1],
                pm_ref[0, hd:hd + 1, :], ct_ref[hd], causal, CHUNK)
            ct_ref[hd] = ct_new
            pm_ref[0, hd:hd + 1, :] = jnp.broadcast_to(m_new, (1, 128))
            m_carry.append(carry_hd)

        k_own = ukey_ref[rows, C_AK - C_K:C_AK - C_K + KV_W]
        vt_own = utf_ref[feat(C_AV, KV_W), rows]
        k_prev = kprev_ref[...]
        vt_prev = vtprev_ref[...]
        prev_ok = (key > query) & ((j * n_chunks + ci) > 0)
        keys_kv = [tuple(_keep_kv_half(x, kv).astype(BF16) for x in (k_own, k_prev, km_ref[...]))
                   for kv in range(A_KV_HEADS)]
        values_kv = [tuple(x[kv * A_DH:(kv + 1) * A_DH].astype(BF16) for x in (vt_own, vt_prev, vmt_ref[...]))
                     for kv in range(A_KV_HEADS)]
        a_scores = []
        for hd in range(A_HEADS):
            kv = hd // A_GROUP
            q0 = C_AQ + (hd - kv) * A_DH
            q_win = utb_ref[q0:q0 + 2 * A_DH, rows]
            a_scores.append(tuple(_dot(kk, q_win) for kk in keys_kv[kv]))
        kprev_ref[...] = k_own
        vtprev_ref[...] = vt_own

        m_carry = [_mlstm_weighted_values(*c) for c in m_carry]
        a_out = [_swa_weighted_values_t(a_scores[hd], values_kv[hd // A_GROUP], sink_ref[hd], causal, prev_ok)
                 for hd in range(A_HEADS)]

        mix_t = []
        for hd in range(M_HEADS):
            mix_t.append(_mlstm_gate_head(_mlstm_finish(*m_carry[hd]), utf_ref[feat(C_O + hd * M_DH), rows],
                                          utf_ref[feat(C_Z + hd * M_DH), rows],
                                          ng_ref[hd * M_DH:(hd + 1) * M_DH, :], axis=0))
        for tile in range(A_HEADS // 2):
            o_t = jnp.concatenate([o * (1.0 / den) for o, den in a_out[2 * tile:2 * tile + 2]], axis=0)
            mix_t.append(o_t * _silu_of_twice(utf_ref[feat(C_AZ + tile * 128), rows]))
        mix_rows = [x_t.T.astype(BF16) for x_t in mix_t]
        for i, x in enumerate(mix_rows):
            mix_ref[rows, i * 128:(i + 1) * 128] = x
        return carry

    lax.fori_loop(0, n_chunks, chunk_body, 0, unroll=CHUNK_UNROLL)

    for t0 in range(0, tb, tb // OUT_SPLITS):
        tok = slice(t0, t0 + tb // OUT_SPLITS)
        y_ref[0, tok, :] = _out_and_norm(hp_ref[tok, :], mix_ref[tok, :], wo_ref, lng_ref[...], lnb_ref[...])

    @pl.when(j == pl.num_programs(1) - 1)
    def _():
        pk_ref[0, N_META:N_BUF, :] = ukey_ref[tb - WINDOW:tb, C_AK - C_K:C_AK - C_K + KV_W]
        pv_ref[0, N_META:N_BUF, :] = utf_ref[feat(C_AV, KV_W), tb - WINDOW:tb].T
        for hd in range(M_HEADS):
            pc_ref[0, hd] = ct_ref[hd, 0:M_DH, :].T
            pn_ref[0, hd:hd + 1, :] = ct_ref[hd, M_DH:M_DH + 1, :]
        pn_ref[0, M_HEADS:8, :] = jnp.zeros((8 - M_HEADS, M_DH), F32)


def _const_spec(shape):
    return pl.BlockSpec(shape, lambda *_: (0,) * len(shape))


def _small_projections(meta_tokens, x_sample, ln0_g, ln0_b, w_t, b_all):
    rows = x_sample.shape[0] * x_sample.shape[1]
    vmem = pl.BlockSpec(memory_space=pltpu.VMEM)
    return pl.pallas_call(
        _small_projections_kernel,
        out_shape=(jax.ShapeDtypeStruct((M_HEADS, CT_ROWS, M_DH), F32),
                   jax.ShapeDtypeStruct((8, 128), F32),
                   jax.ShapeDtypeStruct((N_META, KV_W), F32),
                   jax.ShapeDtypeStruct((N_META, KV_W), F32),
                   jax.ShapeDtypeStruct((KV_W, N_META), F32),
                   jax.ShapeDtypeStruct((rows, D_MODEL), F32),
                   jax.ShapeDtypeStruct((rows, N_PAD), F32)),
        in_specs=[vmem, vmem, pl.BlockSpec(memory_space=pl.ANY), vmem, vmem, vmem],
        scratch_shapes=[pltpu.VMEM((CHUNK, N_PAD), F32), pltpu.VMEM((N_RAW, D_MODEL), F32),
                        pltpu.SemaphoreType.DMA((len(T_FEATURES + KEY_FEATURES),))],
        compiler_params=pltpu.CompilerParams(vmem_limit_bytes=VMEM_LIMIT_BYTES),
        name="small_projections",
    )(meta_tokens.astype(F32), x_sample, w_t, b_all, ln0_g, ln0_b)


def _prompt_path(x_prompt, meta_state, ln0_g, ln0_b, w_t, b_all, sinks, norm_g, w_o, ln_g, ln_b, tb=512):
    batch, seq, _ = x_prompt.shape
    ct0, m0, km, vm, vmt = meta_state

    nj = seq // tb
    in_specs = [
        pl.BlockSpec(memory_space=pltpu.SMEM),
        pl.BlockSpec((1, tb, D_MODEL), lambda b, j: (b, j, 0)),
        pl.BlockSpec((N_RAW, D_MODEL), lambda b, j: (0, 0), pipeline_mode=pl.Buffered(1)),
        _const_spec((N_RAW, 128)), _const_spec((1, N_RAW)),
        _const_spec((1, D_MODEL)), _const_spec((1, D_MODEL)),
        _const_spec((M_HEADS, CT_ROWS, M_DH)), _const_spec((8, 128)),
        _const_spec((N_META, KV_W)), _const_spec((N_META, KV_W)), _const_spec((KV_W, N_META)),
        _const_spec((D_MLSTM, 128)),
        _const_spec((D_MODEL, D_MODEL)),
        _const_spec((1, D_MODEL)), _const_spec((1, D_MODEL)),
    ]
    out_specs = [
        pl.BlockSpec((1, tb, D_MODEL), lambda b, j: (b, j, 0)),
        pl.BlockSpec((1, N_BUF, KV_W), lambda b, j: (b, 0, 0)),
        pl.BlockSpec((1, N_BUF, KV_W), lambda b, j: (b, 0, 0)),
        pl.BlockSpec((1, M_HEADS, M_DH, M_DH), lambda b, j: (b, 0, 0, 0)),
        pl.BlockSpec((1, 8, M_DH), lambda b, j: (b, 0, 0)),
        pl.BlockSpec((1, 8, 128), lambda b, j: (b, 0, 0)),
    ]
    out_shape = (
        jax.ShapeDtypeStruct((batch, seq, D_MODEL), F32),
        jax.ShapeDtypeStruct((batch, N_BUF, KV_W), F32),
        jax.ShapeDtypeStruct((batch, N_BUF, KV_W), F32),
        jax.ShapeDtypeStruct((batch, M_HEADS, M_DH, M_DH), F32),
        jax.ShapeDtypeStruct((batch, 8, M_DH), F32),
        jax.ShapeDtypeStruct((batch, 8, 128), F32),
    )
    y, pk, pv, pc, pn, pm = pl.pallas_call(
        functools.partial(_prompt_kernel, tb=tb),
        grid=(batch, nj),
        in_specs=in_specs,
        out_specs=out_specs,
        out_shape=out_shape,
        scratch_shapes=[pltpu.VMEM((N_T_BF16, tb), BF16), pltpu.VMEM((N_T - N_T_BF16, tb), F32),
                        pltpu.VMEM((tb, N_KEYS), F32),
                        pltpu.VMEM((tb, D_MODEL), F32), pltpu.VMEM((tb, D_MODEL), BF16),
                        pltpu.VMEM((M_HEADS, CT_ROWS, M_DH), F32),
                        pltpu.VMEM((CHUNK, KV_W), F32), pltpu.VMEM((KV_W, CHUNK), F32),
                        pltpu.VMEM((tb // CHUNK * 8, CHUNK), F32), pltpu.VMEM((tb // CHUNK * 8, CHUNK), F32)],
        compiler_params=pltpu.CompilerParams(dimension_semantics=("arbitrary", "arbitrary"),
                                             vmem_limit_bytes=VMEM_LIMIT_BYTES),
        name="prompt_layer",
    )(sinks, x_prompt, w_t, jnp.broadcast_to(b_all.reshape(N_RAW, 1), (N_RAW, 128)), b_all, ln0_g, ln0_b,
      ct0, m0, km, vm, vmt, jnp.broadcast_to(norm_g.reshape(D_MLSTM, 1), (D_MLSTM, 128)), w_o, ln_g, ln_b)
    pk = pk.reshape(1, batch, N_BUF, A_KV_HEADS, A_DH)
    pv = pv.reshape(1, batch, N_BUF, A_KV_HEADS, A_DH)
    return y, pk, pv, pc[None], pn[:, :M_HEADS][None], pm[:, :M_HEADS, 0][None]


SEQ_PER_GROUP = 32
SWA_SEQ_PER_STEP = 8
SEQ_BATCH = 8


def _segment_last(x, pos, seg_len):
    n = x.shape[1]
    step = 1
    while step < seg_len:
        x = jnp.where((pos // step) % 2 == 0, pltpu.roll(x, n - step, 1), x)
        step *= 2
    return x


def _sample_mlstm_kernel(q_ref, k_ref, v_ref, g_ref, c_ref, n_ref, m_ref,
                         h_ref, cn_ref, nn_ref, mn_ref, inter_ref, *, dec_seq):
    g_t = g_ref[...].T[0:8, :]
    for hd in range(M_HEADS):
        _sample_mlstm_head(hd, g_t, q_ref, k_ref, v_ref, c_ref, n_ref, m_ref,
                           h_ref, cn_ref, nn_ref, mn_ref, inter_ref, dec_seq)


def _sample_mlstm_head(hd, g_t, q_ref, k_ref, v_ref, c_ref, n_ref, m_ref,
                       h_ref, cn_ref, nn_ref, mn_ref, inter_ref, dec_seq):
    sl = slice(hd * M_DH, (hd + 1) * M_DH)
    n = CHUNK
    nb = n // dec_seq
    q = q_ref[:, sl]
    k = k_ref[:, sl] * (M_DH ** -0.5)
    v = v_ref[:, sl]
    qb = q.astype(BF16)
    kb = k.astype(BF16)
    vb = v.astype(BF16)

    li_row = g_t[hd:hd + 1, :]
    lf_row = _log_sigmoid(g_t[M_HEADS + hd:M_HEADS + hd + 1, :])
    lane = _iota2((1, n), 1)
    pos = lane % dec_seq
    b_row = lf_row
    shift = 1
    while shift < dec_seq:
        b_row = b_row + jnp.where(pos >= shift, pltpu.roll(b_row, shift, 1), 0.0)
        shift *= 2

    key = _iota2((n, n), 0)
    query = _iota2((n, n), 1)
    key_ok = (key <= query) & (key // dec_seq == query // dec_seq)
    a_row = li_row - b_row
    a_keys = jnp.broadcast_to(jnp.sum(jnp.where(key == query, a_row, 0.0), axis=1, keepdims=True), (n, n))
    a_t = jnp.where(key_ok, a_keys, NEG_INF)
    seq_of_lane = _iota2((nb, n), 1) // dec_seq == _iota2((nb, n), 0)
    first_lane = _iota2((nb, n), 1) == _iota2((nb, n), 0) * dec_seq
    m_old = jnp.sum(jnp.where(seq_of_lane, m_ref[:, hd:hd + 1], 0.0), axis=0, keepdims=True)
    mm = jnp.maximum(m_old, jnp.max(a_t, axis=0, keepdims=True))
    w_t = jnp.exp(a_t - mm)
    mm_last = _segment_last(mm, pos, dec_seq)
    m_new = _segment_last(b_row, pos, dec_seq) + mm_last
    w_state = jnp.exp(a_row - mm_last)
    decay = jnp.exp(m_old - mm_last)
    mn_ref[:, hd:hd + 1] = jnp.sum(jnp.where(first_lane, m_new, 0.0), axis=1, keepdims=True)

    qkw_t = _dot_nt(kb, qb) * w_t
    den = jnp.sum(qkw_t, axis=0, keepdims=True)
    num_t = _dot(v.T.astype(BF16), qkw_t.astype(BF16))

    n_seq = n_ref[:, hd, :]
    decay_seq = jnp.sum(jnp.where(first_lane, decay, 0.0), axis=1, keepdims=True)
    nn_ref[:, hd, :] = decay_seq * n_seq + _dot(jnp.where(seq_of_lane, w_state, 0.0).astype(BF16), kb)
    expand = jnp.where(_iota2((n, nb), 0) // dec_seq == _iota2((n, nb), 1), 1.0, 0.0).astype(BF16)
    n_rows = _dot(expand, n_seq.astype(BF16))
    qn_col = jnp.sum(q * n_rows, axis=1, keepdims=True)
    qn = jnp.sum(jnp.where(key == query, qn_col, 0.0), axis=0, keepdims=True)

    kwt = k.T * w_state
    decay_rows = jnp.broadcast_to(jnp.sum(jnp.where(key == query, decay, 0.0), axis=1, keepdims=True), (n, n))
    lane_seq = query // dec_seq
    low_rows = _iota2((8, M_DH), 0) < dec_seq
    assert 8 % dec_seq == 0 and 8 // dec_seq == 2
    for pair in range(nb // 2):
        q8 = q[8 * pair:8 * pair + 8].astype(BF16)
        inter_ref[8 * pair:8 * pair + 8, :] = jnp.where(low_rows, _dot(q8, c_ref[2 * pair, hd].astype(BF16)),
                                                        _dot(q8, c_ref[2 * pair + 1, hd].astype(BF16)))
    for s0 in range(0, nb, SEQ_BATCH):
        seqs = range(s0, s0 + SEQ_BATCH)
        updates = [_dot(jnp.where(lane_seq == s, kwt, 0.0).astype(BF16), vb) for s in seqs]
        for s, upd in zip(seqs, updates):
            cn_ref[s, hd] = decay_rows[s * dec_seq:s * dec_seq + 1, :] * c_ref[s, hd] + upd

    s_inter = jnp.exp(m_old - mm)
    num_t = num_t + inter_ref[...].T * s_inter
    den = den + qn * s_inter
    h_ref[:, sl] = (num_t * (1.0 / jnp.maximum(jnp.abs(den), jnp.exp(-(b_row + mm))))).T


def _sample_swa_kernel(aq_ref, kn_ref, vn_ref, ck_ref, cv_ref, sink_ref, o_ref, nk_ref, nv_ref, *, dec_seq):
    n_rows = A_HEADS * dec_seq
    t_c = _iota2((n_rows, N_BUF), 0) % dec_seq
    i_c = _iota2((n_rows, N_BUF), 1)
    ok_c = (i_c < N_META) | (i_c > t_c + N_META)
    t_n = _iota2((n_rows, 8), 0) % dec_seq
    i_n = _iota2((n_rows, 8), 1)
    ok_n = (i_n <= t_n) & (i_n < dec_seq)
    sink = sink_ref[:, 0:1]
    seqs = range(ck_ref.shape[0])
    rows_of = lambda s: slice(s * dec_seq, (s + 1) * dec_seq)

    low = _iota2((len(seqs) * dec_seq, 128), 1) < A_DH
    head_q = []
    for hd in range(A_HEADS):
        tile = aq_ref[:, (hd // 2) * 128:(hd // 2 + 1) * 128]
        piece = jnp.where(low if hd % 2 == 0 else ~low, tile, 0.0)
        head_q.append(piece if hd % 2 == hd // A_GROUP else pltpu.roll(piece, A_DH, 1))
    pad_rows = jnp.zeros((8 - dec_seq, KV_W), F32)
    new_k = [jnp.concatenate([kn_ref[rows_of(s), :], pad_rows], axis=0).astype(BF16) for s in seqs]
    new_v = [jnp.concatenate([vn_ref[rows_of(s), :], pad_rows], axis=0).astype(BF16) for s in seqs]

    scores = []
    for s in seqs:
        qz = jnp.concatenate([q[rows_of(s), :] for q in head_q], axis=0).astype(BF16)
        scores.append((_dot_nt(qz, ck_ref[s].astype(BF16)), _dot_nt(qz, new_k[s])))
    probs = []
    for s_c, s_n in scores:
        s_c = jnp.where(ok_c, s_c, NEG_INF)
        s_n = jnp.where(ok_n, s_n, NEG_INF)
        mx = jnp.maximum(jnp.maximum(jnp.max(s_c, axis=1, keepdims=True),
                                     jnp.max(s_n, axis=1, keepdims=True)), sink)
        p_c = jnp.exp(s_c - mx)
        p_n = jnp.exp(s_n - mx)
        den = jnp.sum(p_c, axis=1, keepdims=True) + jnp.sum(p_n, axis=1, keepdims=True) + jnp.exp(sink - mx)
        probs.append((p_c.astype(BF16), p_n.astype(BF16), den))
    outs = [_dot(p_c, cv_ref[s].astype(BF16)) + _dot(p_n, new_v[s])
            for s, (p_c, p_n, _) in zip(seqs, probs)]
    low4 = _iota2((dec_seq, 128), 1) < A_DH
    for s, o, (_, _, den) in zip(seqs, outs, probs):
        o = o / den
        for pair in range(A_HEADS // 2):
            even, odd = (o[(2 * pair + e) * dec_seq:(2 * pair + e + 1) * dec_seq, :] for e in range(2))
            if pair // (A_GROUP // 2) == 0:
                odd = pltpu.roll(odd, A_DH, 1)
            else:
                even = pltpu.roll(even, A_DH, 1)
            o_ref[rows_of(s), pair * 128:(pair + 1) * 128] = jnp.where(low4, even, odd)
    for s in seqs:
        for cache_ref, new_ref, out_ref in ((ck_ref, kn_ref, nk_ref), (cv_ref, vn_ref, nv_ref)):
            out_ref[s, 0:N_META, :] = cache_ref[s, 0:N_META, :]
            out_ref[s, N_META:N_BUF - dec_seq, :] = cache_ref[s, N_META + dec_seq:N_BUF, :]
            out_ref[s, N_BUF - dec_seq:N_BUF, :] = new_ref[rows_of(s), :]


def _sample_out_kernel(hs_ref, half_o_ref, half_z_ref, half_az_ref, hm_ref, oa_ref, ng_ref, wo_ref, lng_ref, lnb_ref,
                       y_ref):
    mix = _gate_mix(hm_ref[...], half_o_ref[...], half_z_ref[...], oa_ref[...], half_az_ref[...], ng_ref[...])
    y_ref[...] = _out_and_norm(hs_ref[...], mix, wo_ref, lng_ref[...], lnb_ref[...]).reshape(y_ref.shape)


def _sample_path(hs, u, db, dec_seq, cache_k, cache_v, state_c, state_n, state_m, a_sinks, norm_g, w_o, ln_g, ln_b):
    rows = db * dec_seq
    params = pltpu.CompilerParams(vmem_limit_bytes=VMEM_LIMIT_BYTES)

    n_groups = db // SEQ_PER_GROUP

    def col_spec(col0):
        assert col0 % D_MLSTM == 0
        return pl.BlockSpec((CHUNK, D_MLSTM), lambda g: (g, col0 // D_MLSTM))

    state_spec = pl.BlockSpec((SEQ_PER_GROUP, M_HEADS, M_DH, M_DH), lambda g: (g, 0, 0, 0))
    vec_spec = pl.BlockSpec((SEQ_PER_GROUP, M_HEADS, M_DH), lambda g: (g, 0, 0))
    m_spec = pl.BlockSpec((SEQ_PER_GROUP, M_HEADS), lambda g: (g, 0))
    h_m, c_new, n_new, m_new = pl.pallas_call(
        functools.partial(_sample_mlstm_kernel, dec_seq=dec_seq),
        grid=(n_groups,),
        in_specs=[col_spec(C_Q), col_spec(C_K), col_spec(C_V),
                  pl.BlockSpec((CHUNK, 128), lambda g: (g, C_G // 128)),
                  state_spec, vec_spec, m_spec],
        out_specs=[pl.BlockSpec((CHUNK, D_MLSTM), lambda g: (g, 0)), state_spec, vec_spec, m_spec],
        out_shape=(jax.ShapeDtypeStruct((rows, D_MLSTM), F32),
                   jax.ShapeDtypeStruct(state_c.shape, F32),
                   jax.ShapeDtypeStruct(state_n.shape, F32),
                   jax.ShapeDtypeStruct(state_m.shape, F32)),
        scratch_shapes=[pltpu.VMEM((CHUNK, M_DH), F32)],
        compiler_params=pltpu.CompilerParams(dimension_semantics=("arbitrary",),
                                             vmem_limit_bytes=VMEM_LIMIT_BYTES),
        name="sample_mlstm",
    )(u, u, u, u, state_c, state_n, state_m)

    n_qrows = A_HEADS * dec_seq
    sink_rows = jnp.broadcast_to(jnp.repeat(a_sinks.astype(F32), dec_seq)[:, None], (n_qrows, 128))
    ck = cache_k.reshape(db, N_BUF, KV_W)
    cv = cache_v.reshape(db, N_BUF, KV_W)
    sb = SWA_SEQ_PER_STEP

    def token_spec(col0, width):
        assert col0 % width == 0
        return pl.BlockSpec((sb * dec_seq, width), lambda i: (i, col0 // width))

    cache_spec = pl.BlockSpec((sb, N_BUF, KV_W), lambda i: (i, 0, 0))
    o_a, nk, nv = pl.pallas_call(
        functools.partial(_sample_swa_kernel, dec_seq=dec_seq),
        grid=(db // sb,),
        in_specs=[token_spec(C_AQ, D_SWA), token_spec(C_AK, KV_W), token_spec(C_AV, KV_W), cache_spec, cache_spec,
                  pl.BlockSpec((n_qrows, 128), lambda i: (0, 0))],
        out_specs=[pl.BlockSpec((sb * dec_seq, D_SWA), lambda i: (i, 0)), cache_spec, cache_spec],
        out_shape=(jax.ShapeDtypeStruct((rows, D_SWA), F32),
                   jax.ShapeDtypeStruct((db, N_BUF, KV_W), F32),
                   jax.ShapeDtypeStruct((db, N_BUF, KV_W), F32)),
        compiler_params=pltpu.CompilerParams(dimension_semantics=("arbitrary",),
                                             vmem_limit_bytes=VMEM_LIMIT_BYTES),
        name="sample_swa",
    )(u, u, u, ck, cv, sink_rows)

    def gate_spec(col0):
        assert col0 % D_MLSTM == 0
        return pl.BlockSpec((rows, D_MLSTM), lambda i: (0, col0 // D_MLSTM))

    def whole(shape):
        return pl.BlockSpec(shape, lambda i: (0,) * len(shape))

    y = pl.pallas_call(
        _sample_out_kernel,
        grid=(1,),
        in_specs=[whole((rows, D_MODEL)), gate_spec(C_O), gate_spec(C_Z), gate_spec(C_AZ),
                  whole((rows, D_MLSTM)), whole((rows, D_SWA)), whole((1, D_MLSTM)), whole((D_MODEL, D_MODEL)),
                  whole((1, D_MODEL)), whole((1, D_MODEL))],
        out_specs=whole((db, dec_seq, D_MODEL)),
        out_shape=jax.ShapeDtypeStruct((db, dec_seq, D_MODEL), F32),
        compiler_params=params,
        name="sample_out",
    )(hs, u, u, u, h_m, o_a, norm_g, w_o, ln_g, ln_b)

    shape5 = (1, db, N_BUF, A_KV_HEADS, A_DH)
    return (y, nk.reshape(shape5), nv.reshape(shape5), c_new[None],
            n_new[None], m_new[None])


def kernel(x_prompt, x_sample, cache_swa_k, cache_swa_v, state_mlstm_c, state_mlstm_n, state_mlstm_m,
           meta_tokens, ln0_g, ln0_b, w_in, b_in, a_sinks, m_norm_g, w_out, ln_g, ln_b):
    assert w_in.shape[0] == DEPTH and x_prompt.shape[-1] == D_MODEL
    w_t = jnp.transpose(w_in[0].astype(F32))
    b_all = b_in[0].astype(F32)[None]
    w_o = w_out[0].astype(BF16)
    g0 = ln0_g.astype(F32)[None]
    b0 = ln0_b.astype(F32)[None]
    lg = ln_g[0].astype(F32)[None]
    lb = ln_b[0].astype(F32)[None]
    norm_g = m_norm_g[0].astype(F32)[None]
    sinks = a_sinks[0].astype(F32)

    *meta_state, hs, u = _small_projections(meta_tokens, x_sample, g0, b0, w_t, b_all)
    y_p, pk, pv, pc, pn, pm = _prompt_path(x_prompt, meta_state, g0, b0, w_t, b_all, sinks, norm_g, w_o, lg, lb)
    y_s, sk, sv, sc, sn, sm = _sample_path(hs, u, x_sample.shape[0], x_sample.shape[1], cache_swa_k[0], cache_swa_v[0],
                                           state_mlstm_c[0], state_mlstm_n[0], state_mlstm_m[0],
                                           sinks, norm_g, w_o, lg, lb)
    return (y_p, y_s, pk, pv, pc, pn, pm, sk, sv, sc, sn, sm)
```

```python
import functools

import jax
import jax.numpy as jnp
from jax import lax
from jax.experimental import pallas as pl
from jax.experimental.pallas import tpu as pltpu

F32 = jnp.float32
BF16 = jnp.bfloat16

D_MODEL = 1024
N_META = 16
M_HEADS = 4
M_DH = 128
D_MLSTM = M_HEADS * M_DH
A_HEADS = 8
A_KV_HEADS = 2
A_GROUP = A_HEADS // A_KV_HEADS
A_DH = 64
D_SWA = A_HEADS * A_DH
KV_W = A_KV_HEADS * A_DH
WINDOW = 128
CHUNK = 128
LN_EPS = 1e-5
DEPTH = 1
DN_ALPHA = (2.0 * DEPTH) ** 0.25
N_BUF = N_META + WINDOW

C_Q, C_AQ = 0, 512
N_T_BF16 = 1024
C_V, C_O, C_Z, C_AZ, C_AV, C_G = 1024, 1536, 2048, 2560, 3072, 3200
GATE_ROWS = 16
N_T = C_G + GATE_ROWS
C_K, C_AK = 3584, 4096
N_PAD = 4224
N_KEYS = N_PAD - C_K
RAW_Q, RAW_K, RAW_V, RAW_O, RAW_Z, RAW_G = (0, 512), (512, 512), (1024, 512), (1536, 512), (2048, 512), (2560, 8)
RAW_AQ, RAW_AK, RAW_AV, RAW_AZ = (2568, 512), (3080, 128), (3208, 128), (3336, 512)
N_RAW = 3848

TOKEN_SPLITS = 2
OUT_SPLITS = 4
VMEM_LIMIT_BYTES = 56 * 1024 * 1024
NEG_INF = float("-inf")


def _dot(a, b):
    return jnp.dot(a, b, preferred_element_type=F32)


def _dot_nt(a, b):
    return lax.dot_general(a, b, (((1,), (1,)), ((), ())), preferred_element_type=F32)


def _layer_norm(x, g, b):
    mu = jnp.mean(x, axis=-1, keepdims=True)
    xc = x - mu
    var = jnp.mean(xc * xc, axis=-1, keepdims=True)
    return xc * lax.rsqrt(var + LN_EPS) * g + b


def _log_sigmoid(x):
    return jnp.minimum(x, 0.0) - jnp.log1p(jnp.exp(-jnp.abs(x)))


def _times_sigmoid(h, half_x):
    return 0.5 * (h * jnp.tanh(half_x) + h)


def _silu_of_twice(half_x):
    return half_x * jnp.tanh(half_x) + half_x


def _iota2(shape, dim):
    return lax.broadcasted_iota(jnp.int32, shape, dim)


T_FEATURES = ((RAW_G, C_G, 1.0), (RAW_Q, C_Q, 1.0), (RAW_AQ, C_AQ, A_DH ** -0.5), (RAW_AV, C_AV, 1.0), (RAW_V, C_V, 1.0),
              (RAW_O, C_O, 0.5), (RAW_Z, C_Z, 0.5), (RAW_AZ, C_AZ, 0.5))
KEY_FEATURES = ((RAW_K, C_K, 1.0), (RAW_AK, C_AK, 1.0))


def _weights(wt_ref, raw):
    return wt_ref[raw[0]:raw[0] + raw[1], :].astype(BF16)


def _project(token_sets, wt_ref, b_ref):
    for _, u_ref in token_sets:
        u_ref[:, C_G:C_K] = jnp.zeros((u_ref.shape[0], C_K - C_G), F32)
    for raw, dst, scale in T_FEATURES + KEY_FEATURES:
        w = _weights(wt_ref, raw)
        for hb, u_ref in token_sets:
            res = _dot_nt(hb, w) + b_ref[:, raw[0]:raw[0] + raw[1]]
            u_ref[:, dst:dst + raw[1]] = res if scale == 1.0 else res * scale


def _project_both(hb, tok, wt_ref, bcol_ref, utb_ref, utf_ref):
    reps = hb.shape[0] // 128
    assert RAW_Z[0] + RAW_Z[1] == RAW_G[0]
    for raw, dst, scale in ((RAW_Z[0], RAW_Z[1] + RAW_G[1]), C_Z, 0.5), *T_FEATURES:
        if raw in (RAW_Z, RAW_G):
            continue
        res = _dot_nt(_weights(wt_ref, raw), hb) + jnp.concatenate([bcol_ref[raw[0]:raw[0] + raw[1], :]] * reps, axis=1)
        if raw[1] > RAW_Z[1]:
            utf_ref[C_G - N_T_BF16:C_G - N_T_BF16 + RAW_G[1], tok] = res[RAW_Z[1]:]
            res = res[0:RAW_Z[1]]
        if scale != 1.0:
            res = res * scale
        if dst < N_T_BF16:
            utb_ref[dst:dst + res.shape[0], tok] = res.astype(BF16)
        else:
            utf_ref[dst - N_T_BF16:dst - N_T_BF16 + res.shape[0], tok] = res


def _project_keys(hb, tok, wt_ref, brow_ref, ukey_ref):
    for raw, dst, _ in KEY_FEATURES:
        ukey_ref[tok, dst - C_K:dst - C_K + raw[1]] = (_dot_nt(hb, _weights(wt_ref, raw))
                                                       + brow_ref[:, raw[0]:raw[0] + raw[1]])


def _gate_rows(gates):
    return _gate_scan(jnp.concatenate([gates[r0:r0 + CHUNK].T[0:8, :] for r0 in range(0, gates.shape[0], CHUNK)],
                                      axis=0))


def _gate_rows_t(gates_t):
    return _gate_scan(jnp.concatenate([gates_t[:, c0:c0 + CHUNK] for c0 in range(0, gates_t.shape[1], CHUNK)],
                                      axis=0))


def _gate_scan(g_t):
    x = _log_sigmoid(g_t)
    lane = _iota2(x.shape, 1)
    shift = 1
    while shift < x.shape[1]:
        x = x + jnp.where(lane >= shift, pltpu.roll(x, shift, 1), 0.0)
        shift *= 2
    return g_t, x


CT_ROWS = M_DH + 8


def _mlstm_scores_and_state(q_tb, kb, v_t, li_row, b_row, m_old, ct_aug, key_ok, n_valid):
    n = kb.shape[0]
    row = _iota2((n, n), 0)
    col = _iota2((n, n), 1)
    a_row = li_row - b_row
    a_keys = jnp.broadcast_to(jnp.sum(jnp.where(row == col, a_row, 0.0), axis=1, keepdims=True), (n, n))
    a_t = jnp.where(key_ok, a_keys, NEG_INF)
    mm = jnp.maximum(m_old, jnp.max(a_t, axis=0, keepdims=True))
    w_t = jnp.exp(a_t - mm)
    lane = _iota2((1, n), 1)
    last = n_valid - 1
    mm_last = jnp.max(jnp.where(lane == last, mm, NEG_INF), axis=1, keepdims=True)
    m_new = jnp.sum(jnp.where(lane == last, b_row, 0.0), axis=1, keepdims=True) + mm_last
    w_state = jnp.exp(a_row - mm_last)
    if n_valid < n:
        w_state = jnp.where(lane < n_valid, w_state, 0.0)
    decay = jnp.exp(m_old - mm_last)
    scores_t = _dot(kb, q_tb)
    inter = _dot(ct_aug.astype(BF16), q_tb)
    ones_row = jnp.where(_iota2((CT_ROWS - M_DH, n), 0) == 0, w_state, 0.0)
    vtw = jnp.concatenate([v_t * w_state, ones_row], axis=0).astype(BF16)
    ct_aug_new = decay * ct_aug + _dot(vtw, kb)
    s_inter = jnp.exp(m_old - mm)
    floor = jnp.exp(-(b_row + mm))
    return (scores_t, w_t, v_t.astype(BF16), inter, s_inter, floor), ct_aug_new, m_new


def _mlstm_weighted_values(scores_t, w_t, vtb, inter, s_inter, floor):
    qkw_t = scores_t * w_t
    den = jnp.sum(qkw_t, axis=0, keepdims=True)
    num_t = _dot(vtb, qkw_t.astype(BF16))
    return num_t, den, inter, s_inter, floor


def _mlstm_finish(num_t, den, inter, s_inter, floor):
    num_t = num_t + inter[0:M_DH] * s_inter
    den = den + inter[M_DH:M_DH + 1] * s_inter
    return num_t * (1.0 / jnp.maximum(jnp.abs(den), floor))


def _mlstm_gate_head(h, half_o, half_z, norm_g, axis):
    hh = _times_sigmoid(h, half_o)
    mu = jnp.mean(hh, axis=axis, keepdims=True)
    hc = hh - mu
    var = jnp.mean(hc * hc, axis=axis, keepdims=True)
    return hc * lax.rsqrt(var + LN_EPS) * norm_g * _silu_of_twice(half_z)


def _gate_mix(h_m, half_o, half_z, o_a, half_az, norm_g):
    parts = []
    for hd in range(M_HEADS):
        sl = slice(hd * M_DH, (hd + 1) * M_DH)
        parts.append(_mlstm_gate_head(h_m[:, sl], half_o[:, sl], half_z[:, sl], norm_g[:, sl], axis=-1).astype(BF16))
    parts.append((o_a * _silu_of_twice(half_az)).astype(BF16))
    return jnp.concatenate(parts, axis=-1)


def _out_and_norm(hp, mix, wo_ref, g, b):
    z = DN_ALPHA * hp + _dot(mix, wo_ref[...])
    return _layer_norm(z, g, b)


def _small_projections_kernel(meta_ref, xs_ref, wt_ref, bias_ref, g0_ref, b0_ref,
                              ct0_ref, m0_ref, km_ref, vm_ref, vmt_ref, hs_ref, us_ref, u_ref):
    hs = _layer_norm(xs_ref[...].reshape(hs_ref.shape), g0_ref[...], b0_ref[...])
    hs_ref[...] = hs
    meta = jnp.concatenate([meta_ref[...], jnp.zeros((CHUNK - N_META, D_MODEL), F32)], axis=0)
    hp = _layer_norm(meta, g0_ref[...], b0_ref[...])
    _project(((hs.astype(BF16), us_ref), (hp.astype(BF16), u_ref)), wt_ref, bias_ref)
    row = _iota2((CHUNK, CHUNK), 0)
    col = _iota2((CHUNK, CHUNK), 1)
    key_ok = (row <= col) & (row < N_META)
    li_rows, b_rows = _gate_rows(u_ref[:, C_G:C_G + 128])
    zero_m = jnp.zeros((1, 128), F32)
    zero_ct = jnp.zeros((CT_ROWS, M_DH), F32)
    m0_ref[...] = jnp.zeros(m0_ref.shape, F32)
    for hd in range(M_HEADS):
        q = u_ref[:, C_Q + hd * M_DH:C_Q + (hd + 1) * M_DH]
        k = u_ref[:, C_K + hd * M_DH:C_K + (hd + 1) * M_DH] * (M_DH ** -0.5)
        v = u_ref[:, C_V + hd * M_DH:C_V + (hd + 1) * M_DH]
        _, ct_new, m_new = _mlstm_scores_and_state(
            q.T.astype(BF16), k.astype(BF16), v.T, li_rows[hd:hd + 1], b_rows[M_HEADS + hd:M_HEADS + hd + 1],
            zero_m, zero_ct, key_ok, N_META)
        ct0_ref[hd] = ct_new
        m0_ref[hd:hd + 1, :] = jnp.broadcast_to(m_new, (1, 128))
    km_ref[...] = u_ref[0:N_META, C_AK:C_AK + KV_W]
    vm_ref[...] = u_ref[0:N_META, C_AV:C_AV + KV_W]
    vmt_ref[...] = u_ref[:, C_AV:C_AV + KV_W].T[:, 0:N_META]


def _keep_kv_half(x, kv):
    low = _iota2(x.shape, 1) < A_DH
    return jnp.where(low if kv == 0 else ~low, x, 0.0)


def _swa_weighted_values_t(scores, values_t, sink, own_ok, prev_ok):
    s_own = jnp.where(own_ok, scores[0], NEG_INF)
    s_prev = jnp.where(prev_ok, scores[1], NEG_INF)
    s_meta = scores[2]
    mx = jnp.maximum(jnp.maximum(jnp.max(s_own, axis=0, keepdims=True),
                                 jnp.max(s_prev, axis=0, keepdims=True)),
                     jnp.maximum(jnp.max(s_meta, axis=0, keepdims=True), sink))
    p_own, p_prev, p_meta = (jnp.exp(s - mx) for s in (s_own, s_prev, s_meta))
    den = (jnp.sum(p_own, axis=0, keepdims=True) + jnp.sum(p_prev, axis=0, keepdims=True)
           + jnp.sum(p_meta, axis=0, keepdims=True) + jnp.exp(sink - mx))
    o_t = (_dot(values_t[0], p_own.astype(BF16)) + _dot(values_t[1], p_prev.astype(BF16))
           + _dot(values_t[2], p_meta.astype(BF16)))
    return o_t, den


def _prompt_kernel(sink_ref, x_ref, wt_ref, bcol_ref, brow_ref, g0_ref, b0_ref,
                   ct0_ref, m0_ref, km_ref, vm_ref, vmt_ref, ng_ref, wo_ref, lng_ref, lnb_ref,
                   y_ref, pk_ref, pv_ref, pc_ref, pn_ref, pm_ref,
                   utb_ref, utf_ref, ukey_ref, hp_ref, mix_ref, ct_ref, kprev_ref, vtprev_ref, li_ref, cumf_ref,
                   *, tb):
    j = pl.program_id(1)
    n_chunks = tb // CHUNK

    @pl.when(j == 0)
    def _():
        ct_ref[...] = ct0_ref[...]
        pm_ref[0] = m0_ref[...]
        kprev_ref[...] = jnp.zeros(kprev_ref.shape, F32)
        vtprev_ref[...] = jnp.zeros(vtprev_ref.shape, F32)
        pk_ref[0, 0:N_META, :] = km_ref[...]
        pv_ref[0, 0:N_META, :] = vm_ref[...]

    for t0 in range(0, tb, tb // TOKEN_SPLITS):
        tok = slice(t0, t0 + tb // TOKEN_SPLITS)
        hp = _layer_norm(x_ref[0, tok, :], g0_ref[...], b0_ref[...])
        hp_ref[tok, :] = hp
        hb = hp.astype(BF16)
        _project_keys(hb, tok, wt_ref, brow_ref, ukey_ref)
        _project_both(hb, tok, wt_ref, bcol_ref, utb_ref, utf_ref)

    def feat(c0, n=M_DH):
        return slice(c0 - N_T_BF16, c0 - N_T_BF16 + n)

    li_ref[...], cumf_ref[...] = _gate_rows_t(utf_ref[feat(C_G, 8), :])

    key = _iota2((CHUNK, CHUNK), 0)
    query = _iota2((CHUNK, CHUNK), 1)
    causal = key <= query

    def chunk_rows(ci):
        return slice(ci * CHUNK, (ci + 1) * CHUNK)

    def attention_scores(ci):
        rows = chunk_rows(ci)
        k_own = ukey_ref[rows, C_AK - C_K:C_AK - C_K + KV_W]
        vt_own = utf_ref[feat(C_AV, KV_W), rows]
        k_prev = kprev_ref[...]
        vt_prev = vtprev_ref[...]
        prev_ok = (key > query) & ((j * n_chunks + ci) > 0)
        keys_kv = [tuple(_keep_kv_half(x, kv).astype(BF16) for x in (k_own, k_prev, km_ref[...]))
                   for kv in range(A_KV_HEADS)]
        values_kv = [tuple(x[kv * A_DH:(kv + 1) * A_DH].astype(BF16) for x in (vt_own, vt_prev, vmt_ref[...]))
                     for kv in range(A_KV_HEADS)]
        a_scores = []
        for hd in range(A_HEADS):
            kv = hd // A_GROUP
            q0 = C_AQ + (hd - kv) * A_DH
            q_win = utb_ref[q0:q0 + 2 * A_DH, rows]
            a_scores.append(tuple(_dot(kk, q_win) for kk in keys_kv[kv]))
        kprev_ref[...] = k_own
        vtprev_ref[...] = vt_own
        return prev_ok, a_scores, values_kv

    def mlstm_first_stage(ci):
        rows = chunk_rows(ci)
        gate_rows = slice(ci * 8, (ci + 1) * 8)
        li_rows = li_ref[gate_rows, :]
        b_rows = cumf_ref[gate_rows, :]
        m_carry = []
        for hd in range(M_HEADS):
            q_tb = utb_ref[C_Q + hd * M_DH:C_Q + (hd + 1) * M_DH, rows]
            kb = (ukey_ref[rows, hd * M_DH:(hd + 1) * M_DH] * (M_DH ** -0.5)).astype(BF16)
            v_t = utf_ref[feat(C_V + hd * M_DH), rows]
            carry_hd, ct_new, m_new = _mlstm_scores_and_state(
                q_tb, kb, v_t, li_rows[hd:hd + 1], b_rows[M_HEADS + hd:M_HEADS + hd + 1],
                pm_ref[0, hd:hd + 1, :], ct_ref[hd], causal, CHUNK)
            ct_ref[hd] = ct_new
            pm_ref[0, hd:hd + 1, :] = jnp.broadcast_to(m_new, (1, 128))
            m_carry.append(carry_hd)
        return m_carry

    def mlstm_second_stage(m_carry):
        return [_mlstm_weighted_values(*c) for c in m_carry]

    def chunk_finish(ci, prev_ok, a_scores, values_kv, m_carry):
        rows = chunk_rows(ci)
        a_out = [_swa_weighted_values_t(a_scores[hd], values_kv[hd // A_GROUP], sink_ref[hd], causal, prev_ok)
                 for hd in range(A_HEADS)]

        mix_t = []
        for hd in range(M_HEADS):
            mix_t.append(_mlstm_gate_head(_mlstm_finish(*m_carry[hd]), utf_ref[feat(C_O + hd * M_DH), rows],
                                          utf_ref[feat(C_Z + hd * M_DH), rows],
                                          ng_ref[hd * M_DH:(hd + 1) * M_DH, :], axis=0))
        for tile in range(A_HEADS // 2):
            o_t = jnp.concatenate([o * (1.0 / den) for o, den in a_out[2 * tile:2 * tile + 2]], axis=0)
            mix_t.append(o_t * _silu_of_twice(utf_ref[feat(C_AZ + tile * 128), rows]))
        mix_rows = [x_t.T.astype(BF16) for x_t in mix_t]
        for i, x in enumerate(mix_rows):
            mix_ref[rows, i * 128:(i + 1) * 128] = x

    for ci in range(n_chunks):
        attn = attention_scores(ci)
        m_carry = mlstm_second_stage(mlstm_first_stage(ci))
        chunk_finish(ci, *attn, m_carry)

    for t0 in range(0, tb, tb // OUT_SPLITS):
        tok = slice(t0, t0 + tb // OUT_SPLITS)
        y_ref[0, tok, :] = _out_and_norm(hp_ref[tok, :], mix_ref[tok, :], wo_ref, lng_ref[...], lnb_ref[...])

    @pl.when(j == pl.num_programs(1) - 1)
    def _():
        pk_ref[0, N_META:N_BUF, :] = ukey_ref[tb - WINDOW:tb, C_AK - C_K:C_AK - C_K + KV_W]
        pv_ref[0, N_META:N_BUF, :] = utf_ref[feat(C_AV, KV_W), tb - WINDOW:tb].T
        for hd in range(M_HEADS):
            pc_ref[0, hd] = ct_ref[hd, 0:M_DH, :].T
            pn_ref[0, hd:hd + 1, :] = ct_ref[hd, M_DH:M_DH + 1, :]
        pn_ref[0, M_HEADS:8, :] = jnp.zeros((8 - M_HEADS, M_DH), F32)


def _const_spec(shape):
    return pl.BlockSpec(shape, lambda *_: (0,) * len(shape))


def _small_projections(meta_tokens, x_sample, ln0_g, ln0_b, w_t, b_all):
    rows = x_sample.shape[0] * x_sample.shape[1]
    return pl.pallas_call(
        _small_projections_kernel,
        out_shape=(jax.ShapeDtypeStruct((M_HEADS, CT_ROWS, M_DH), F32),
                   jax.ShapeDtypeStruct((8, 128), F32),
                   jax.ShapeDtypeStruct((N_META, KV_W), F32),
                   jax.ShapeDtypeStruct((N_META, KV_W), F32),
                   jax.ShapeDtypeStruct((KV_W, N_META), F32),
                   jax.ShapeDtypeStruct((rows, D_MODEL), F32),
                   jax.ShapeDtypeStruct((rows, N_PAD), F32)),
        scratch_shapes=[pltpu.VMEM((CHUNK, N_PAD), F32)],
        compiler_params=pltpu.CompilerParams(vmem_limit_bytes=VMEM_LIMIT_BYTES),
        name="small_projections",
    )(meta_tokens.astype(F32), x_sample, w_t, b_all, ln0_g, ln0_b)


def _prompt_path(x_prompt, meta_state, ln0_g, ln0_b, w_t, b_all, sinks, norm_g, w_o, ln_g, ln_b, tb=512):
    batch, seq, _ = x_prompt.shape
    ct0, m0, km, vm, vmt = meta_state

    nj = seq // tb
    in_specs = [
        pl.BlockSpec(memory_space=pltpu.SMEM),
        pl.BlockSpec((1, tb, D_MODEL), lambda b, j: (b, j, 0)),
        pl.BlockSpec((N_RAW, D_MODEL), lambda b, j: (0, 0), pipeline_mode=pl.Buffered(1)),
        _const_spec((N_RAW, 128)), _const_spec((1, N_RAW)),
        _const_spec((1, D_MODEL)), _const_spec((1, D_MODEL)),
        _const_spec((M_HEADS, CT_ROWS, M_DH)), _const_spec((8, 128)),
        _const_spec((N_META, KV_W)), _const_spec((N_META, KV_W)), _const_spec((KV_W, N_META)),
        _const_spec((D_MLSTM, 128)),
        _const_spec((D_MODEL, D_MODEL)),
        _const_spec((1, D_MODEL)), _const_spec((1, D_MODEL)),
    ]
    out_specs = [
        pl.BlockSpec((1, tb, D_MODEL), lambda b, j: (b, j, 0)),
        pl.BlockSpec((1, N_BUF, KV_W), lambda b, j: (b, 0, 0)),
        pl.BlockSpec((1, N_BUF, KV_W), lambda b, j: (b, 0, 0)),
        pl.BlockSpec((1, M_HEADS, M_DH, M_DH), lambda b, j: (b, 0, 0, 0)),
        pl.BlockSpec((1, 8, M_DH), lambda b, j: (b, 0, 0)),
        pl.BlockSpec((1, 8, 128), lambda b, j: (b, 0, 0)),
    ]
    out_shape = (
        jax.ShapeDtypeStruct((batch, seq, D_MODEL), F32),
        jax.ShapeDtypeStruct((batch, N_BUF, KV_W), F32),
        jax.ShapeDtypeStruct((batch, N_BUF, KV_W), F32),
        jax.ShapeDtypeStruct((batch, M_HEADS, M_DH, M_DH), F32),
        jax.ShapeDtypeStruct((batch, 8, M_DH), F32),
        jax.ShapeDtypeStruct((batch, 8, 128), F32),
    )
    y, pk, pv, pc, pn, pm = pl.pallas_call(
        functools.partial(_prompt_kernel, tb=tb),
        grid=(batch, nj),
        in_specs=in_specs,
        out_specs=out_specs,
        out_shape=out_shape,
        scratch_shapes=[pltpu.VMEM((N_T_BF16, tb), BF16), pltpu.VMEM((N_T - N_T_BF16, tb), F32),
                        pltpu.VMEM((tb, N_KEYS), F32),
                        pltpu.VMEM((tb, D_MODEL), F32), pltpu.VMEM((tb, D_MODEL), BF16),
                        pltpu.VMEM((M_HEADS, CT_ROWS, M_DH), F32),
                        pltpu.VMEM((CHUNK, KV_W), F32), pltpu.VMEM((KV_W, CHUNK), F32),
                        pltpu.VMEM((tb // CHUNK * 8, CHUNK), F32), pltpu.VMEM((tb // CHUNK * 8, CHUNK), F32)],
        compiler_params=pltpu.CompilerParams(dimension_semantics=("arbitrary", "arbitrary"),
                                             vmem_limit_bytes=VMEM_LIMIT_BYTES),
        name="prompt_layer",
    )(sinks, x_prompt, w_t, jnp.broadcast_to(b_all.reshape(N_RAW, 1), (N_RAW, 128)), b_all, ln0_g, ln0_b,
      ct0, m0, km, vm, vmt, jnp.broadcast_to(norm_g.reshape(D_MLSTM, 1), (D_MLSTM, 128)), w_o, ln_g, ln_b)
    pk = pk.reshape(1, batch, N_BUF, A_KV_HEADS, A_DH)
    pv = pv.reshape(1, batch, N_BUF, A_KV_HEADS, A_DH)
    return y, pk, pv, pc[None], pn[:, :M_HEADS][None], pm[:, :M_HEADS, 0][None]


SEQ_PER_GROUP = 32
SWA_SEQ_PER_STEP = 8
SEQ_BATCH = 8


def _segment_last(x, pos, seg_len):
    n = x.shape[1]
    step = 1
    while step < seg_len:
        x = jnp.where((pos // step) % 2 == 0, pltpu.roll(x, n - step, 1), x)
        step *= 2
    return x


def _sample_mlstm_kernel(q_ref, k_ref, v_ref, g_ref, c_ref, n_ref, m_ref,
                         h_ref, cn_ref, nn_ref, mn_ref, inter_ref, *, dec_seq):
    g_t = g_ref[...].T[0:8, :]
    for hd in range(M_HEADS):
        _sample_mlstm_head(hd, g_t, q_ref, k_ref, v_ref, c_ref, n_ref, m_ref,
                           h_ref, cn_ref, nn_ref, mn_ref, inter_ref, dec_seq)


def _sample_mlstm_head(hd, g_t, q_ref, k_ref, v_ref, c_ref, n_ref, m_ref,
                       h_ref, cn_ref, nn_ref, mn_ref, inter_ref, dec_seq):
    sl = slice(hd * M_DH, (hd + 1) * M_DH)
    n = CHUNK
    nb = n // dec_seq
    q = q_ref[:, sl]
    k = k_ref[:, sl] * (M_DH ** -0.5)
    v = v_ref[:, sl]
    qb = q.astype(BF16)
    kb = k.astype(BF16)
    vb = v.astype(BF16)

    li_row = g_t[hd:hd + 1, :]
    lf_row = _log_sigmoid(g_t[M_HEADS + hd:M_HEADS + hd + 1, :])
    lane = _iota2((1, n), 1)
    pos = lane % dec_seq
    b_row = lf_row
    shift = 1
    while shift < dec_seq:
        b_row = b_row + jnp.where(pos >= shift, pltpu.roll(b_row, shift, 1), 0.0)
        shift *= 2

    key = _iota2((n, n), 0)
    query = _iota2((n, n), 1)
    key_ok = (key <= query) & (key // dec_seq == query // dec_seq)
    a_row = li_row - b_row
    a_keys = jnp.broadcast_to(jnp.sum(jnp.where(key == query, a_row, 0.0), axis=1, keepdims=True), (n, n))
    a_t = jnp.where(key_ok, a_keys, NEG_INF)
    seq_of_lane = _iota2((nb, n), 1) // dec_seq == _iota2((nb, n), 0)
    first_lane = _iota2((nb, n), 1) == _iota2((nb, n), 0) * dec_seq
    m_old = jnp.sum(jnp.where(seq_of_lane, m_ref[:, hd:hd + 1], 0.0), axis=0, keepdims=True)
    mm = jnp.maximum(m_old, jnp.max(a_t, axis=0, keepdims=True))
    w_t = jnp.exp(a_t - mm)
    mm_last = _segment_last(mm, pos, dec_seq)
    m_new = _segment_last(b_row, pos, dec_seq) + mm_last
    w_state = jnp.exp(a_row - mm_last)
    decay = jnp.exp(m_old - mm_last)
    mn_ref[:, hd:hd + 1] = jnp.sum(jnp.where(first_lane, m_new, 0.0), axis=1, keepdims=True)

    qkw_t = _dot_nt(kb, qb) * w_t
    den = jnp.sum(qkw_t, axis=0, keepdims=True)
    num_t = _dot(v.T.astype(BF16), qkw_t.astype(BF16))

    n_seq = n_ref[:, hd, :]
    decay_seq = jnp.sum(jnp.where(first_lane, decay, 0.0), axis=1, keepdims=True)
    nn_ref[:, hd, :] = decay_seq * n_seq + _dot(jnp.where(seq_of_lane, w_state, 0.0).astype(BF16), kb)
    expand = jnp.where(_iota2((n, nb), 0) // dec_seq == _iota2((n, nb), 1), 1.0, 0.0).astype(BF16)
    n_rows = _dot(expand, n_seq.astype(BF16))
    qn_col = jnp.sum(q * n_rows, axis=1, keepdims=True)
    qn = jnp.sum(jnp.where(key == query, qn_col, 0.0), axis=0, keepdims=True)

    kwt = k.T * w_state
    decay_rows = jnp.broadcast_to(jnp.sum(jnp.where(key == query, decay, 0.0), axis=1, keepdims=True), (n, n))
    lane_seq = query // dec_seq
    low_rows = _iota2((8, M_DH), 0) < dec_seq
    assert 8 % dec_seq == 0 and 8 // dec_seq == 2
    for pair in range(nb // 2):
        q8 = q[8 * pair:8 * pair + 8].astype(BF16)
        inter_ref[8 * pair:8 * pair + 8, :] = jnp.where(low_rows, _dot(q8, c_ref[2 * pair, hd].astype(BF16)),
                                                        _dot(q8, c_ref[2 * pair + 1, hd].astype(BF16)))
    for s0 in range(0, nb, SEQ_BATCH):
        seqs = range(s0, s0 + SEQ_BATCH)
        updates = [_dot(jnp.where(lane_seq == s, kwt, 0.0).astype(BF16), vb) for s in seqs]
        for s, upd in zip(seqs, updates):
            cn_ref[s, hd] = decay_rows[s * dec_seq:s * dec_seq + 1, :] * c_ref[s, hd] + upd

    s_inter = jnp.exp(m_old - mm)
    num_t = num_t + inter_ref[...].T * s_inter
    den = den + qn * s_inter
    h_ref[:, sl] = (num_t * (1.0 / jnp.maximum(jnp.abs(den), jnp.exp(-(b_row + mm))))).T


def _sample_swa_kernel(aq_ref, kn_ref, vn_ref, ck_ref, cv_ref, sink_ref, o_ref, nk_ref, nv_ref, *, dec_seq):
    n_rows = A_HEADS * dec_seq
    t_c = _iota2((n_rows, N_BUF), 0) % dec_seq
    i_c = _iota2((n_rows, N_BUF), 1)
    ok_c = (i_c < N_META) | (i_c > t_c + N_META)
    t_n = _iota2((n_rows, 8), 0) % dec_seq
    i_n = _iota2((n_rows, 8), 1)
    ok_n = (i_n <= t_n) & (i_n < dec_seq)
    sink = sink_ref[:, 0:1]
    seqs = range(ck_ref.shape[0])
    rows_of = lambda s: slice(s * dec_seq, (s + 1) * dec_seq)

    low = _iota2((len(seqs) * dec_seq, 128), 1) < A_DH
    head_q = []
    for hd in range(A_HEADS):
        tile = aq_ref[:, (hd // 2) * 128:(hd // 2 + 1) * 128]
        piece = jnp.where(low if hd % 2 == 0 else ~low, tile, 0.0)
        head_q.append(piece if hd % 2 == hd // A_GROUP else pltpu.roll(piece, A_DH, 1))
    pad_rows = jnp.zeros((8 - dec_seq, KV_W), F32)
    new_k = [jnp.concatenate([kn_ref[rows_of(s), :], pad_rows], axis=0).astype(BF16) for s in seqs]
    new_v = [jnp.concatenate([vn_ref[rows_of(s), :], pad_rows], axis=0).astype(BF16) for s in seqs]

    scores = []
    for s in seqs:
        qz = jnp.concatenate([q[rows_of(s), :] for q in head_q], axis=0).astype(BF16)
        scores.append((_dot_nt(qz, ck_ref[s].astype(BF16)), _dot_nt(qz, new_k[s])))
    probs = []
    for s_c, s_n in scores:
        s_c = jnp.where(ok_c, s_c, NEG_INF)
        s_n = jnp.where(ok_n, s_n, NEG_INF)
        mx = jnp.maximum(jnp.maximum(jnp.max(s_c, axis=1, keepdims=True),
                                     jnp.max(s_n, axis=1, keepdims=True)), sink)
        p_c = jnp.exp(s_c - mx)
        p_n = jnp.exp(s_n - mx)
        den = jnp.sum(p_c, axis=1, keepdims=True) + jnp.sum(p_n, axis=1, keepdims=True) + jnp.exp(sink - mx)
        probs.append((p_c.astype(BF16), p_n.astype(BF16), den))
    outs = [_dot(p_c, cv_ref[s].astype(BF16)) + _dot(p_n, new_v[s])
            for s, (p_c, p_n, _) in zip(seqs, probs)]
    low4 = _iota2((dec_seq, 128), 1) < A_DH
    for s, o, (_, _, den) in zip(seqs, outs, probs):
        o = o / den
        for pair in range(A_HEADS // 2):
            even, odd = (o[(2 * pair + e) * dec_seq:(2 * pair + e + 1) * dec_seq, :] for e in range(2))
            if pair // (A_GROUP // 2) == 0:
                odd = pltpu.roll(odd, A_DH, 1)
            else:
                even = pltpu.roll(even, A_DH, 1)
            o_ref[rows_of(s), pair * 128:(pair + 1) * 128] = jnp.where(low4, even, odd)
    for s in seqs:
        for cache_ref, new_ref, out_ref in ((ck_ref, kn_ref, nk_ref), (cv_ref, vn_ref, nv_ref)):
            out_ref[s, 0:N_META, :] = cache_ref[s, 0:N_META, :]
            out_ref[s, N_META:N_BUF - dec_seq, :] = cache_ref[s, N_META + dec_seq:N_BUF, :]
            out_ref[s, N_BUF - dec_seq:N_BUF, :] = new_ref[rows_of(s), :]


def _sample_out_kernel(hs_ref, half_o_ref, half_z_ref, half_az_ref, hm_ref, oa_ref, ng_ref, wo_ref, lng_ref, lnb_ref,
                       y_ref):
    mix = _gate_mix(hm_ref[...], half_o_ref[...], half_z_ref[...], oa_ref[...], half_az_ref[...], ng_ref[...])
    y_ref[...] = _out_and_norm(hs_ref[...], mix, wo_ref, lng_ref[...], lnb_ref[...]).reshape(y_ref.shape)


def _sample_path(hs, u, db, dec_seq, cache_k, cache_v, state_c, state_n, state_m, a_sinks, norm_g, w_o, ln_g, ln_b):
    rows = db * dec_seq
    params = pltpu.CompilerParams(vmem_limit_bytes=VMEM_LIMIT_BYTES)

    n_groups = db // SEQ_PER_GROUP

    def col_spec(col0):
        assert col0 % D_MLSTM == 0
        return pl.BlockSpec((CHUNK, D_MLSTM), lambda g: (g, col0 // D_MLSTM))

    state_spec = pl.BlockSpec((SEQ_PER_GROUP, M_HEADS, M_DH, M_DH), lambda g: (g, 0, 0, 0))
    vec_spec = pl.BlockSpec((SEQ_PER_GROUP, M_HEADS, M_DH), lambda g: (g, 0, 0))
    m_spec = pl.BlockSpec((SEQ_PER_GROUP, M_HEADS), lambda g: (g, 0))
    h_m, c_new, n_new, m_new = pl.pallas_call(
        functools.partial(_sample_mlstm_kernel, dec_seq=dec_seq),
        grid=(n_groups,),
        in_specs=[col_spec(C_Q), col_spec(C_K), col_spec(C_V),
                  pl.BlockSpec((CHUNK, 128), lambda g: (g, C_G // 128)),
                  state_spec, vec_spec, m_spec],
        out_specs=[pl.BlockSpec((CHUNK, D_MLSTM), lambda g: (g, 0)), state_spec, vec_spec, m_spec],
        out_shape=(jax.ShapeDtypeStruct((rows, D_MLSTM), F32),
                   jax.ShapeDtypeStruct(state_c.shape, F32),
                   jax.ShapeDtypeStruct(state_n.shape, F32),
                   jax.ShapeDtypeStruct(state_m.shape, F32)),
        scratch_shapes=[pltpu.VMEM((CHUNK, M_DH), F32)],
        compiler_params=pltpu.CompilerParams(dimension_semantics=("arbitrary",),
                                             vmem_limit_bytes=VMEM_LIMIT_BYTES),
        name="sample_mlstm",
    )(u, u, u, u, state_c, state_n, state_m)

    n_qrows = A_HEADS * dec_seq
    sink_rows = jnp.broadcast_to(jnp.repeat(a_sinks.astype(F32), dec_seq)[:, None], (n_qrows, 128))
    ck = cache_k.reshape(db, N_BUF, KV_W)
    cv = cache_v.reshape(db, N_BUF, KV_W)
    sb = SWA_SEQ_PER_STEP

    def token_spec(col0, width):
        assert col0 % width == 0
        return pl.BlockSpec((sb * dec_seq, width), lambda i: (i, col0 // width))

    cache_spec = pl.BlockSpec((sb, N_BUF, KV_W), lambda i: (i, 0, 0))
    o_a, nk, nv = pl.pallas_call(
        functools.partial(_sample_swa_kernel, dec_seq=dec_seq),
        grid=(db // sb,),
        in_specs=[token_spec(C_AQ, D_SWA), token_spec(C_AK, KV_W), token_spec(C_AV, KV_W), cache_spec, cache_spec,
                  pl.BlockSpec((n_qrows, 128), lambda i: (0, 0))],
        out_specs=[pl.BlockSpec((sb * dec_seq, D_SWA), lambda i: (i, 0)), cache_spec, cache_spec],
        out_shape=(jax.ShapeDtypeStruct((rows, D_SWA), F32),
                   jax.ShapeDtypeStruct((db, N_BUF, KV_W), F32),
                   jax.ShapeDtypeStruct((db, N_BUF, KV_W), F32)),
        compiler_params=pltpu.CompilerParams(dimension_semantics=("arbitrary",),
                                             vmem_limit_bytes=VMEM_LIMIT_BYTES),
        name="sample_swa",
    )(u, u, u, ck, cv, sink_rows)

    def gate_spec(col0):
        assert col0 % D_MLSTM == 0
        return pl.BlockSpec((rows, D_MLSTM), lambda i: (0, col0 // D_MLSTM))

    def whole(shape):
        return pl.BlockSpec(shape, lambda i: (0,) * len(shape))

    y = pl.pallas_call(
        _sample_out_kernel,
        grid=(1,),
        in_specs=[whole((rows, D_MODEL)), gate_spec(C_O), gate_spec(C_Z), gate_spec(C_AZ),
                  whole((rows, D_MLSTM)), whole((rows, D_SWA)), whole((1, D_MLSTM)), whole((D_MODEL, D_MODEL)),
                  whole((1, D_MODEL)), whole((1, D_MODEL))],
        out_specs=whole((db, dec_seq, D_MODEL)),
        out_shape=jax.ShapeDtypeStruct((db, dec_seq, D_MODEL), F32),
        compiler_params=params,
        name="sample_out",
    )(hs, u, u, u, h_m, o_a, norm_g, w_o, ln_g, ln_b)

    shape5 = (1, db, N_BUF, A_KV_HEADS, A_DH)
    return (y, nk.reshape(shape5), nv.reshape(shape5), c_new[None],
            n_new[None], m_new[None])


def kernel(x_prompt, x_sample, cache_swa_k, cache_swa_v, state_mlstm_c, state_mlstm_n, state_mlstm_m,
           meta_tokens, ln0_g, ln0_b, w_in, b_in, a_sinks, m_norm_g, w_out, ln_g, ln_b):
    assert w_in.shape[0] == DEPTH and x_prompt.shape[-1] == D_MODEL
    w_t = jnp.transpose(w_in[0].astype(F32))
    b_all = b_in[0].astype(F32)[None]
    w_o = w_out[0].astype(BF16)
    g0 = ln0_g.astype(F32)[None]
    b0 = ln0_b.astype(F32)[None]
    lg = ln_g[0].astype(F32)[None]
    lb = ln_b[0].astype(F32)[None]
    norm_g = m_norm_g[0].astype(F32)[None]
    sinks = a_sinks[0].astype(F32)

    *meta_state, hs, u = _small_projections(meta_tokens, x_sample, g0, b0, w_t, b_all)
    y_p, pk, pv, pc, pn, pm = _prompt_path(x_prompt, meta_state, g0, b0, w_t, b_all, sinks, norm_g, w_o, lg, lb)
    y_s, sk, sv, sc, sn, sm = _sample_path(hs, u, x_sample.shape[0], x_sample.shape[1], cache_swa_k[0], cache_swa_v[0],
                                           state_mlstm_c[0], state_mlstm_n[0], state_mlstm_m[0],
                                           sinks, norm_g, w_o, lg, lb)
    return (y_p, y_s, pk, pv, pc, pn, pm, sk, sv, sc, sn, sm)
```

```python
import functools

import jax
import jax.numpy as jnp
from jax import lax
from jax.experimental import pallas as pl
from jax.experimental.pallas import tpu as pltpu

F32 = jnp.float32
BF16 = jnp.bfloat16

D_MODEL = 1024
N_META = 16
M_HEADS = 4
M_DH = 128
D_MLSTM = M_HEADS * M_DH
A_HEADS = 8
A_KV_HEADS = 2
A_GROUP = A_HEADS // A_KV_HEADS
A_DH = 64
D_SWA = A_HEADS * A_DH
KV_W = A_KV_HEADS * A_DH
WINDOW = 128
CHUNK = 128
LN_EPS = 1e-5
DEPTH = 1
DN_ALPHA = (2.0 * DEPTH) ** 0.25
N_BUF = N_META + WINDOW

C_Q, C_AQ = 0, 512
N_T_BF16 = 1024
C_V, C_O, C_Z, C_AZ, C_AV, C_G = 1024, 1536, 2048, 2560, 3072, 3200
GATE_ROWS = 16
N_T = C_G + GATE_ROWS
C_K, C_AK = 3584, 4096
N_PAD = 4224
N_KEYS = N_PAD - C_K
RAW_Q, RAW_K, RAW_V, RAW_O, RAW_Z, RAW_G = (0, 512), (512, 512), (1024, 512), (1536, 512), (2048, 512), (2560, 8)
RAW_AQ, RAW_AK, RAW_AV, RAW_AZ = (2568, 512), (3080, 128), (3208, 128), (3336, 512)
N_RAW = 3848

TOKEN_SPLITS = 2
OUT_SPLITS = 4
VMEM_LIMIT_BYTES = 56 * 1024 * 1024
NEG_INF = float("-inf")


def _dot(a, b):
    return jnp.dot(a, b, preferred_element_type=F32)


def _dot_nt(a, b):
    return lax.dot_general(a, b, (((1,), (1,)), ((), ())), preferred_element_type=F32)


def _layer_norm(x, g, b):
    mu = jnp.mean(x, axis=-1, keepdims=True)
    xc = x - mu
    var = jnp.mean(xc * xc, axis=-1, keepdims=True)
    return xc * lax.rsqrt(var + LN_EPS) * g + b


def _log_sigmoid(x):
    return jnp.minimum(x, 0.0) - jnp.log1p(jnp.exp(-jnp.abs(x)))


def _times_sigmoid(h, half_x):
    return 0.5 * (h * jnp.tanh(half_x) + h)


def _silu_of_twice(half_x):
    return half_x * jnp.tanh(half_x) + half_x


def _iota2(shape, dim):
    return lax.broadcasted_iota(jnp.int32, shape, dim)


T_FEATURES = ((RAW_G, C_G, 1.0), (RAW_Q, C_Q, 1.0), (RAW_AQ, C_AQ, A_DH ** -0.5), (RAW_AV, C_AV, 1.0), (RAW_V, C_V, 1.0),
              (RAW_O, C_O, 0.5), (RAW_Z, C_Z, 0.5), (RAW_AZ, C_AZ, 0.5))
KEY_FEATURES = ((RAW_K, C_K, 1.0), (RAW_AK, C_AK, 1.0))
_T = {raw: (raw, dst, scale) for raw, dst, scale in T_FEATURES}
T_GROUPS = ((_T[RAW_Z], _T[RAW_G]), (_T[RAW_Q],), (_T[RAW_AQ],), (_T[RAW_AV], _T[RAW_AZ]), (_T[RAW_V], _T[RAW_O]))


def _weights(wt_ref, raw):
    return wt_ref[raw[0]:raw[0] + raw[1], :].astype(BF16)


def _project(token_sets, wt_ref, b_ref):
    for _, u_ref in token_sets:
        u_ref[:, C_G:C_K] = jnp.zeros((u_ref.shape[0], C_K - C_G), F32)
    for raw, dst, scale in T_FEATURES + KEY_FEATURES:
        w = _weights(wt_ref, raw)
        for hb, u_ref in token_sets:
            res = _dot_nt(hb, w) + b_ref[:, raw[0]:raw[0] + raw[1]]
            u_ref[:, dst:dst + raw[1]] = res if scale == 1.0 else res * scale


def _project_both(hb, tok, wt_ref, bcol_ref, utb_ref, utf_ref):
    reps = hb.shape[0] // 128
    for group in T_GROUPS:
        r0, r1 = group[0][0][0], group[-1][0][0] + group[-1][0][1]
        assert sum(raw[1] for raw, _, _ in group) == r1 - r0
        res_all = _dot_nt(_weights(wt_ref, (r0, r1 - r0)), hb) + jnp.concatenate([bcol_ref[r0:r1, :]] * reps, axis=1)
        for raw, dst, scale in group:
            res = res_all[raw[0] - r0:raw[0] - r0 + raw[1]]
            if scale != 1.0:
                res = res * scale
            if dst < N_T_BF16:
                utb_ref[dst:dst + raw[1], tok] = res.astype(BF16)
            else:
                utf_ref[dst - N_T_BF16:dst - N_T_BF16 + raw[1], tok] = res


def _project_keys(hb, tok, wt_ref, brow_ref, ukey_ref):
    for raw, dst, _ in KEY_FEATURES:
        ukey_ref[tok, dst - C_K:dst - C_K + raw[1]] = (_dot_nt(hb, _weights(wt_ref, raw))
                                                       + brow_ref[:, raw[0]:raw[0] + raw[1]])


def _gate_rows(gates):
    return _gate_scan(jnp.concatenate([gates[r0:r0 + CHUNK].T[0:8, :] for r0 in range(0, gates.shape[0], CHUNK)],
                                      axis=0))


def _gate_rows_t(gates_t):
    return _gate_scan(jnp.concatenate([gates_t[:, c0:c0 + CHUNK] for c0 in range(0, gates_t.shape[1], CHUNK)],
                                      axis=0))


def _gate_scan(g_t):
    x = _log_sigmoid(g_t)
    lane = _iota2(x.shape, 1)
    shift = 1
    while shift < x.shape[1]:
        x = x + jnp.where(lane >= shift, pltpu.roll(x, shift, 1), 0.0)
        shift *= 2
    return g_t, x


CT_ROWS = M_DH + 8


def _mlstm_scores_and_state(q_tb, kb, v_t, li_row, b_row, m_old, ct_aug, key_ok, n_valid):
    n = kb.shape[0]
    row = _iota2((n, n), 0)
    col = _iota2((n, n), 1)
    a_row = li_row - b_row
    a_keys = jnp.broadcast_to(jnp.sum(jnp.where(row == col, a_row, 0.0), axis=1, keepdims=True), (n, n))
    a_t = jnp.where(key_ok, a_keys, NEG_INF)
    mm = jnp.maximum(m_old, jnp.max(a_t, axis=0, keepdims=True))
    w_t = jnp.exp(a_t - mm)
    lane = _iota2((1, n), 1)
    last = n_valid - 1
    mm_last = jnp.max(jnp.where(lane == last, mm, NEG_INF), axis=1, keepdims=True)
    m_new = jnp.sum(jnp.where(lane == last, b_row, 0.0), axis=1, keepdims=True) + mm_last
    w_state = jnp.exp(a_row - mm_last)
    if n_valid < n:
        w_state = jnp.where(lane < n_valid, w_state, 0.0)
    decay = jnp.exp(m_old - mm_last)
    scores_t = _dot(kb, q_tb)
    inter = _dot(ct_aug.astype(BF16), q_tb)
    ones_row = jnp.where(_iota2((CT_ROWS - M_DH, n), 0) == 0, w_state, 0.0)
    vtw = jnp.concatenate([v_t * w_state, ones_row], axis=0).astype(BF16)
    ct_aug_new = decay * ct_aug + _dot(vtw, kb)
    s_inter = jnp.exp(m_old - mm)
    floor = jnp.exp(-(b_row + mm))
    return (scores_t, w_t, v_t.astype(BF16), inter, s_inter, floor), ct_aug_new, m_new


def _mlstm_weighted_values(scores_t, w_t, vtb, inter, s_inter, floor):
    qkw_t = scores_t * w_t
    den = jnp.sum(qkw_t, axis=0, keepdims=True)
    num_t = _dot(vtb, qkw_t.astype(BF16))
    return num_t, den, inter, s_inter, floor


def _mlstm_finish(num_t, den, inter, s_inter, floor):
    num_t = num_t + inter[0:M_DH] * s_inter
    den = den + inter[M_DH:M_DH + 1] * s_inter
    return num_t * (1.0 / jnp.maximum(jnp.abs(den), floor))


def _mlstm_gate_head(h, half_o, half_z, norm_g, axis):
    hh = _times_sigmoid(h, half_o)
    mu = jnp.mean(hh, axis=axis, keepdims=True)
    hc = hh - mu
    var = jnp.mean(hc * hc, axis=axis, keepdims=True)
    return hc * lax.rsqrt(var + LN_EPS) * norm_g * _silu_of_twice(half_z)


def _gate_mix(h_m, half_o, half_z, o_a, half_az, norm_g):
    parts = []
    for hd in range(M_HEADS):
        sl = slice(hd * M_DH, (hd + 1) * M_DH)
        parts.append(_mlstm_gate_head(h_m[:, sl], half_o[:, sl], half_z[:, sl], norm_g[:, sl], axis=-1).astype(BF16))
    parts.append((o_a * _silu_of_twice(half_az)).astype(BF16))
    return jnp.concatenate(parts, axis=-1)


def _out_and_norm(hp, mix, wo_ref, g, b):
    z = DN_ALPHA * hp + _dot(mix, wo_ref[...])
    return _layer_norm(z, g, b)


def _small_projections_kernel(meta_ref, xs_ref, wt_ref, bias_ref, g0_ref, b0_ref,
                              ct0_ref, m0_ref, km_ref, vm_ref, vmt_ref, hs_ref, us_ref, u_ref):
    hs = _layer_norm(xs_ref[...].reshape(hs_ref.shape), g0_ref[...], b0_ref[...])
    hs_ref[...] = hs
    meta = jnp.concatenate([meta_ref[...], jnp.zeros((CHUNK - N_META, D_MODEL), F32)], axis=0)
    hp = _layer_norm(meta, g0_ref[...], b0_ref[...])
    _project(((hs.astype(BF16), us_ref), (hp.astype(BF16), u_ref)), wt_ref, bias_ref)
    row = _iota2((CHUNK, CHUNK), 0)
    col = _iota2((CHUNK, CHUNK), 1)
    key_ok = (row <= col) & (row < N_META)
    li_rows, b_rows = _gate_rows(u_ref[:, C_G:C_G + 128])
    zero_m = jnp.zeros((1, 128), F32)
    zero_ct = jnp.zeros((CT_ROWS, M_DH), F32)
    m0_ref[...] = jnp.zeros(m0_ref.shape, F32)
    for hd in range(M_HEADS):
        q = u_ref[:, C_Q + hd * M_DH:C_Q + (hd + 1) * M_DH]
        k = u_ref[:, C_K + hd * M_DH:C_K + (hd + 1) * M_DH] * (M_DH ** -0.5)
        v = u_ref[:, C_V + hd * M_DH:C_V + (hd + 1) * M_DH]
        _, ct_new, m_new = _mlstm_scores_and_state(
            q.T.astype(BF16), k.astype(BF16), v.T, li_rows[hd:hd + 1], b_rows[M_HEADS + hd:M_HEADS + hd + 1],
            zero_m, zero_ct, key_ok, N_META)
        ct0_ref[hd] = ct_new
        m0_ref[hd:hd + 1, :] = jnp.broadcast_to(m_new, (1, 128))
    km_ref[...] = u_ref[0:N_META, C_AK:C_AK + KV_W]
    vm_ref[...] = u_ref[0:N_META, C_AV:C_AV + KV_W]
    vmt_ref[...] = u_ref[:, C_AV:C_AV + KV_W].T[:, 0:N_META]


def _keep_kv_half(x, kv):
    low = _iota2(x.shape, 1) < A_DH
    return jnp.where(low if kv == 0 else ~low, x, 0.0)


def _swa_weighted_values_t(scores, values_t, sink, own_ok, prev_ok):
    s_own = jnp.where(own_ok, scores[0], NEG_INF)
    s_prev = jnp.where(prev_ok, scores[1], NEG_INF)
    s_meta = scores[2]
    mx = jnp.maximum(jnp.maximum(jnp.max(s_own, axis=0, keepdims=True),
                                 jnp.max(s_prev, axis=0, keepdims=True)),
                     jnp.maximum(jnp.max(s_meta, axis=0, keepdims=True), sink))
    p_own, p_prev, p_meta = (jnp.exp(s - mx) for s in (s_own, s_prev, s_meta))
    den = (jnp.sum(p_own, axis=0, keepdims=True) + jnp.sum(p_prev, axis=0, keepdims=True)
           + jnp.sum(p_meta, axis=0, keepdims=True) + jnp.exp(sink - mx))
    o_t = (_dot(values_t[0], p_own.astype(BF16)) + _dot(values_t[1], p_prev.astype(BF16))
           + _dot(values_t[2], p_meta.astype(BF16)))
    return o_t, den


def _prompt_kernel(sink_ref, x_ref, wt_ref, bcol_ref, brow_ref, g0_ref, b0_ref,
                   ct0_ref, m0_ref, km_ref, vm_ref, vmt_ref, ng_ref, wo_ref, lng_ref, lnb_ref,
                   y_ref, pk_ref, pv_ref, pc_ref, pn_ref, pm_ref,
                   utb_ref, utf_ref, ukey_ref, hp_ref, mix_ref, ct_ref, kprev_ref, vtprev_ref, li_ref, cumf_ref,
                   *, tb):
    j = pl.program_id(1)
    n_chunks = tb // CHUNK

    @pl.when(j == 0)
    def _():
        ct_ref[...] = ct0_ref[...]
        pm_ref[0] = m0_ref[...]
        kprev_ref[...] = jnp.zeros(kprev_ref.shape, F32)
        vtprev_ref[...] = jnp.zeros(vtprev_ref.shape, F32)
        pk_ref[0, 0:N_META, :] = km_ref[...]
        pv_ref[0, 0:N_META, :] = vm_ref[...]

    for t0 in range(0, tb, tb // TOKEN_SPLITS):
        tok = slice(t0, t0 + tb // TOKEN_SPLITS)
        hp = _layer_norm(x_ref[0, tok, :], g0_ref[...], b0_ref[...])
        hp_ref[tok, :] = hp
        hb = hp.astype(BF16)
        _project_keys(hb, tok, wt_ref, brow_ref, ukey_ref)
        _project_both(hb, tok, wt_ref, bcol_ref, utb_ref, utf_ref)

    def feat(c0, n=M_DH):
        return slice(c0 - N_T_BF16, c0 - N_T_BF16 + n)

    li_ref[...], cumf_ref[...] = _gate_rows_t(utf_ref[feat(C_G, 8), :])

    key = _iota2((CHUNK, CHUNK), 0)
    query = _iota2((CHUNK, CHUNK), 1)
    causal = key <= query

    def chunk_rows(ci):
        return slice(ci * CHUNK, (ci + 1) * CHUNK)

    def attention_scores(ci):
        rows = chunk_rows(ci)
        k_own = ukey_ref[rows, C_AK - C_K:C_AK - C_K + KV_W]
        vt_own = utf_ref[feat(C_AV, KV_W), rows]
        k_prev = kprev_ref[...]
        vt_prev = vtprev_ref[...]
        prev_ok = (key > query) & ((j * n_chunks + ci) > 0)
        keys_kv = [tuple(_keep_kv_half(x, kv).astype(BF16) for x in (k_own, k_prev, km_ref[...]))
                   for kv in range(A_KV_HEADS)]
        values_kv = [tuple(x[kv * A_DH:(kv + 1) * A_DH].astype(BF16) for x in (vt_own, vt_prev, vmt_ref[...]))
                     for kv in range(A_KV_HEADS)]
        a_scores = []
        for hd in range(A_HEADS):
            kv = hd // A_GROUP
            q0 = C_AQ + (hd - kv) * A_DH
            q_win = utb_ref[q0:q0 + 2 * A_DH, rows]
            a_scores.append(tuple(_dot(kk, q_win) for kk in keys_kv[kv]))
        kprev_ref[...] = k_own
        vtprev_ref[...] = vt_own
        return prev_ok, a_scores, values_kv

    def mlstm_first_stage(ci):
        rows = chunk_rows(ci)
        gate_rows = slice(ci * 8, (ci + 1) * 8)
        li_rows = li_ref[gate_rows, :]
        b_rows = cumf_ref[gate_rows, :]
        m_carry = []
        for hd in range(M_HEADS):
            q_tb = utb_ref[C_Q + hd * M_DH:C_Q + (hd + 1) * M_DH, rows]
            kb = (ukey_ref[rows, hd * M_DH:(hd + 1) * M_DH] * (M_DH ** -0.5)).astype(BF16)
            v_t = utf_ref[feat(C_V + hd * M_DH), rows]
            carry_hd, ct_new, m_new = _mlstm_scores_and_state(
                q_tb, kb, v_t, li_rows[hd:hd + 1], b_rows[M_HEADS + hd:M_HEADS + hd + 1],
                pm_ref[0, hd:hd + 1, :], ct_ref[hd], causal, CHUNK)
            ct_ref[hd] = ct_new
            pm_ref[0, hd:hd + 1, :] = jnp.broadcast_to(m_new, (1, 128))
            m_carry.append(carry_hd)
        return m_carry

    def mlstm_second_stage(m_carry):
        return [_mlstm_weighted_values(*c) for c in m_carry]

    def chunk_finish(ci, prev_ok, a_scores, values_kv, m_carry):
        rows = chunk_rows(ci)
        a_out = [_swa_weighted_values_t(a_scores[hd], values_kv[hd // A_GROUP], sink_ref[hd], causal, prev_ok)
                 for hd in range(A_HEADS)]

        mix_t = []
        for hd in range(M_HEADS):
            mix_t.append(_mlstm_gate_head(_mlstm_finish(*m_carry[hd]), utf_ref[feat(C_O + hd * M_DH), rows],
                                          utf_ref[feat(C_Z + hd * M_DH), rows],
                                          ng_ref[hd * M_DH:(hd + 1) * M_DH, :], axis=0))
        for tile in range(A_HEADS // 2):
            o_t = jnp.concatenate([o * (1.0 / den) for o, den in a_out[2 * tile:2 * tile + 2]], axis=0)
            mix_t.append(o_t * _silu_of_twice(utf_ref[feat(C_AZ + tile * 128), rows]))
        mix_rows = [x_t.T.astype(BF16) for x_t in mix_t]
        for i, x in enumerate(mix_rows):
            mix_ref[rows, i * 128:(i + 1) * 128] = x

    for ci in range(n_chunks):
        attn = attention_scores(ci)
        m_carry = mlstm_second_stage(mlstm_first_stage(ci))
        chunk_finish(ci, *attn, m_carry)

    for t0 in range(0, tb, tb // OUT_SPLITS):
        tok = slice(t0, t0 + tb // OUT_SPLITS)
        y_ref[0, tok, :] = _out_and_norm(hp_ref[tok, :], mix_ref[tok, :], wo_ref, lng_ref[...], lnb_ref[...])

    @pl.when(j == pl.num_programs(1) - 1)
    def _():
        pk_ref[0, N_META:N_BUF, :] = ukey_ref[tb - WINDOW:tb, C_AK - C_K:C_AK - C_K + KV_W]
        pv_ref[0, N_META:N_BUF, :] = utf_ref[feat(C_AV, KV_W), tb - WINDOW:tb].T
        for hd in range(M_HEADS):
            pc_ref[0, hd] = ct_ref[hd, 0:M_DH, :].T
            pn_ref[0, hd:hd + 1, :] = ct_ref[hd, M_DH:M_DH + 1, :]
        pn_ref[0, M_HEADS:8, :] = jnp.zeros((8 - M_HEADS, M_DH), F32)


def _const_spec(shape):
    return pl.BlockSpec(shape, lambda *_: (0,) * len(shape))


def _small_projections(meta_tokens, x_sample, ln0_g, ln0_b, w_t, b_all):
    rows = x_sample.shape[0] * x_sample.shape[1]
    return pl.pallas_call(
        _small_projections_kernel,
        out_shape=(jax.ShapeDtypeStruct((M_HEADS, CT_ROWS, M_DH), F32),
                   jax.ShapeDtypeStruct((8, 128), F32),
                   jax.ShapeDtypeStruct((N_META, KV_W), F32),
                   jax.ShapeDtypeStruct((N_META, KV_W), F32),
                   jax.ShapeDtypeStruct((KV_W, N_META), F32),
                   jax.ShapeDtypeStruct((rows, D_MODEL), F32),
                   jax.ShapeDtypeStruct((rows, N_PAD), F32)),
        scratch_shapes=[pltpu.VMEM((CHUNK, N_PAD), F32)],
        compiler_params=pltpu.CompilerParams(vmem_limit_bytes=VMEM_LIMIT_BYTES),
        name="small_projections",
    )(meta_tokens.astype(F32), x_sample, w_t, b_all, ln0_g, ln0_b)


def _prompt_path(x_prompt, meta_state, ln0_g, ln0_b, w_t, b_all, sinks, norm_g, w_o, ln_g, ln_b, tb=512):
    batch, seq, _ = x_prompt.shape
    ct0, m0, km, vm, vmt = meta_state

    nj = seq // tb
    in_specs = [
        pl.BlockSpec(memory_space=pltpu.SMEM),
        pl.BlockSpec((1, tb, D_MODEL), lambda b, j: (b, j, 0)),
        pl.BlockSpec((N_RAW, D_MODEL), lambda b, j: (0, 0), pipeline_mode=pl.Buffered(1)),
        _const_spec((N_RAW, 128)), _const_spec((1, N_RAW)),
        _const_spec((1, D_MODEL)), _const_spec((1, D_MODEL)),
        _const_spec((M_HEADS, CT_ROWS, M_DH)), _const_spec((8, 128)),
        _const_spec((N_META, KV_W)), _const_spec((N_META, KV_W)), _const_spec((KV_W, N_META)),
        _const_spec((D_MLSTM, 128)),
        _const_spec((D_MODEL, D_MODEL)),
        _const_spec((1, D_MODEL)), _const_spec((1, D_MODEL)),
    ]
    out_specs = [
        pl.BlockSpec((1, tb, D_MODEL), lambda b, j: (b, j, 0)),
        pl.BlockSpec((1, N_BUF, KV_W), lambda b, j: (b, 0, 0)),
        pl.BlockSpec((1, N_BUF, KV_W), lambda b, j: (b, 0, 0)),
        pl.BlockSpec((1, M_HEADS, M_DH, M_DH), lambda b, j: (b, 0, 0, 0)),
        pl.BlockSpec((1, 8, M_DH), lambda b, j: (b, 0, 0)),
        pl.BlockSpec((1, 8, 128), lambda b, j: (b, 0, 0)),
    ]
    out_shape = (
        jax.ShapeDtypeStruct((batch, seq, D_MODEL), F32),
        jax.ShapeDtypeStruct((batch, N_BUF, KV_W), F32),
        jax.ShapeDtypeStruct((batch, N_BUF, KV_W), F32),
        jax.ShapeDtypeStruct((batch, M_HEADS, M_DH, M_DH), F32),
        jax.ShapeDtypeStruct((batch, 8, M_DH), F32),
        jax.ShapeDtypeStruct((batch, 8, 128), F32),
    )
    y, pk, pv, pc, pn, pm = pl.pallas_call(
        functools.partial(_prompt_kernel, tb=tb),
        grid=(batch, nj),
        in_specs=in_specs,
        out_specs=out_specs,
        out_shape=out_shape,
        scratch_shapes=[pltpu.VMEM((N_T_BF16, tb), BF16), pltpu.VMEM((N_T - N_T_BF16, tb), F32),
                        pltpu.VMEM((tb, N_KEYS), F32),
                        pltpu.VMEM((tb, D_MODEL), F32), pltpu.VMEM((tb, D_MODEL), BF16),
                        pltpu.VMEM((M_HEADS, CT_ROWS, M_DH), F32),
                        pltpu.VMEM((CHUNK, KV_W), F32), pltpu.VMEM((KV_W, CHUNK), F32),
                        pltpu.VMEM((tb // CHUNK * 8, CHUNK), F32), pltpu.VMEM((tb // CHUNK * 8, CHUNK), F32)],
        compiler_params=pltpu.CompilerParams(dimension_semantics=("arbitrary", "arbitrary"),
                                             vmem_limit_bytes=VMEM_LIMIT_BYTES),
        name="prompt_layer",
    )(sinks, x_prompt, w_t, jnp.broadcast_to(b_all.reshape(N_RAW, 1), (N_RAW, 128)), b_all, ln0_g, ln0_b,
      ct0, m0, km, vm, vmt, jnp.broadcast_to(norm_g.reshape(D_MLSTM, 1), (D_MLSTM, 128)), w_o, ln_g, ln_b)
    pk = pk.reshape(1, batch, N_BUF, A_KV_HEADS, A_DH)
    pv = pv.reshape(1, batch, N_BUF, A_KV_HEADS, A_DH)
    return y, pk, pv, pc[None], pn[:, :M_HEADS][None], pm[:, :M_HEADS, 0][None]


SEQ_PER_GROUP = 32
SWA_SEQ_PER_STEP = 8
SEQ_BATCH = 8


def _segment_last(x, pos, seg_len):
    n = x.shape[1]
    step = 1
    while step < seg_len:
        x = jnp.where((pos // step) % 2 == 0, pltpu.roll(x, n - step, 1), x)
        step *= 2
    return x


def _sample_mlstm_kernel(q_ref, k_ref, v_ref, g_ref, c_ref, n_ref, m_ref,
                         h_ref, cn_ref, nn_ref, mn_ref, inter_ref, *, dec_seq):
    g_t = g_ref[...].T[0:8, :]
    for hd in range(M_HEADS):
        _sample_mlstm_head(hd, g_t, q_ref, k_ref, v_ref, c_ref, n_ref, m_ref,
                           h_ref, cn_ref, nn_ref, mn_ref, inter_ref, dec_seq)


def _sample_mlstm_head(hd, g_t, q_ref, k_ref, v_ref, c_ref, n_ref, m_ref,
                       h_ref, cn_ref, nn_ref, mn_ref, inter_ref, dec_seq):
    sl = slice(hd * M_DH, (hd + 1) * M_DH)
    n = CHUNK
    nb = n // dec_seq
    q = q_ref[:, sl]
    k = k_ref[:, sl] * (M_DH ** -0.5)
    v = v_ref[:, sl]
    qb = q.astype(BF16)
    kb = k.astype(BF16)
    vb = v.astype(BF16)

    li_row = g_t[hd:hd + 1, :]
    lf_row = _log_sigmoid(g_t[M_HEADS + hd:M_HEADS + hd + 1, :])
    lane = _iota2((1, n), 1)
    pos = lane % dec_seq
    b_row = lf_row
    shift = 1
    while shift < dec_seq:
        b_row = b_row + jnp.where(pos >= shift, pltpu.roll(b_row, shift, 1), 0.0)
        shift *= 2

    key = _iota2((n, n), 0)
    query = _iota2((n, n), 1)
    key_ok = (key <= query) & (key // dec_seq == query // dec_seq)
    a_row = li_row - b_row
    a_keys = jnp.broadcast_to(jnp.sum(jnp.where(key == query, a_row, 0.0), axis=1, keepdims=True), (n, n))
    a_t = jnp.where(key_ok, a_keys, NEG_INF)
    seq_of_lane = _iota2((nb, n), 1) // dec_seq == _iota2((nb, n), 0)
    first_lane = _iota2((nb, n), 1) == _iota2((nb, n), 0) * dec_seq
    m_old = jnp.sum(jnp.where(seq_of_lane, m_ref[:, hd:hd + 1], 0.0), axis=0, keepdims=True)
    mm = jnp.maximum(m_old, jnp.max(a_t, axis=0, keepdims=True))
    w_t = jnp.exp(a_t - mm)
    mm_last = _segment_last(mm, pos, dec_seq)
    m_new = _segment_last(b_row, pos, dec_seq) + mm_last
    w_state = jnp.exp(a_row - mm_last)
    decay = jnp.exp(m_old - mm_last)
    mn_ref[:, hd:hd + 1] = jnp.sum(jnp.where(first_lane, m_new, 0.0), axis=1, keepdims=True)

    qkw_t = _dot_nt(kb, qb) * w_t
    den = jnp.sum(qkw_t, axis=0, keepdims=True)
    num_t = _dot(v.T.astype(BF16), qkw_t.astype(BF16))

    n_seq = n_ref[:, hd, :]
    decay_seq = jnp.sum(jnp.where(first_lane, decay, 0.0), axis=1, keepdims=True)
    nn_ref[:, hd, :] = decay_seq * n_seq + _dot(jnp.where(seq_of_lane, w_state, 0.0).astype(BF16), kb)
    expand = jnp.where(_iota2((n, nb), 0) // dec_seq == _iota2((n, nb), 1), 1.0, 0.0).astype(BF16)
    n_rows = _dot(expand, n_seq.astype(BF16))
    qn_col = jnp.sum(q * n_rows, axis=1, keepdims=True)
    qn = jnp.sum(jnp.where(key == query, qn_col, 0.0), axis=0, keepdims=True)

    kwt = k.T * w_state
    decay_rows = jnp.broadcast_to(jnp.sum(jnp.where(key == query, decay, 0.0), axis=1, keepdims=True), (n, n))
    lane_seq = query // dec_seq
    low_rows = _iota2((8, M_DH), 0) < dec_seq
    assert 8 % dec_seq == 0 and 8 // dec_seq == 2
    for pair in range(nb // 2):
        q8 = q[8 * pair:8 * pair + 8].astype(BF16)
        inter_ref[8 * pair:8 * pair + 8, :] = jnp.where(low_rows, _dot(q8, c_ref[2 * pair, hd].astype(BF16)),
                                                        _dot(q8, c_ref[2 * pair + 1, hd].astype(BF16)))
    for s0 in range(0, nb, SEQ_BATCH):
        seqs = range(s0, s0 + SEQ_BATCH)
        updates = [_dot(jnp.where(lane_seq == s, kwt, 0.0).astype(BF16), vb) for s in seqs]
        for s, upd in zip(seqs, updates):
            cn_ref[s, hd] = decay_rows[s * dec_seq:s * dec_seq + 1, :] * c_ref[s, hd] + upd

    s_inter = jnp.exp(m_old - mm)
    num_t = num_t + inter_ref[...].T * s_inter
    den = den + qn * s_inter
    h_ref[:, sl] = (num_t * (1.0 / jnp.maximum(jnp.abs(den), jnp.exp(-(b_row + mm))))).T


def _sample_swa_kernel(aq_ref, kn_ref, vn_ref, ck_ref, cv_ref, sink_ref, o_ref, nk_ref, nv_ref, *, dec_seq):
    n_rows = A_HEADS * dec_seq
    t_c = _iota2((n_rows, N_BUF), 0) % dec_seq
    i_c = _iota2((n_rows, N_BUF), 1)
    ok_c = (i_c < N_META) | (i_c > t_c + N_META)
    t_n = _iota2((n_rows, 8), 0) % dec_seq
    i_n = _iota2((n_rows, 8), 1)
    ok_n = (i_n <= t_n) & (i_n < dec_seq)
    sink = sink_ref[:, 0:1]
    seqs = range(ck_ref.shape[0])
    rows_of = lambda s: slice(s * dec_seq, (s + 1) * dec_seq)

    low = _iota2((len(seqs) * dec_seq, 128), 1) < A_DH
    head_q = []
    for hd in range(A_HEADS):
        tile = aq_ref[:, (hd // 2) * 128:(hd // 2 + 1) * 128]
        piece = jnp.where(low if hd % 2 == 0 else ~low, tile, 0.0)
        head_q.append(piece if hd % 2 == hd // A_GROUP else pltpu.roll(piece, A_DH, 1))
    pad_rows = jnp.zeros((8 - dec_seq, KV_W), F32)
    new_k = [jnp.concatenate([kn_ref[rows_of(s), :], pad_rows], axis=0).astype(BF16) for s in seqs]
    new_v = [jnp.concatenate([vn_ref[rows_of(s), :], pad_rows], axis=0).astype(BF16) for s in seqs]

    scores = []
    for s in seqs:
        qz = jnp.concatenate([q[rows_of(s), :] for q in head_q], axis=0).astype(BF16)
        scores.append((_dot_nt(qz, ck_ref[s].astype(BF16)), _dot_nt(qz, new_k[s])))
    probs = []
    for s_c, s_n in scores:
        s_c = jnp.where(ok_c, s_c, NEG_INF)
        s_n = jnp.where(ok_n, s_n, NEG_INF)
        mx = jnp.maximum(jnp.maximum(jnp.max(s_c, axis=1, keepdims=True),
                                     jnp.max(s_n, axis=1, keepdims=True)), sink)
        p_c = jnp.exp(s_c - mx)
        p_n = jnp.exp(s_n - mx)
        den = jnp.sum(p_c, axis=1, keepdims=True) + jnp.sum(p_n, axis=1, keepdims=True) + jnp.exp(sink - mx)
        probs.append((p_c.astype(BF16), p_n.astype(BF16), den))
    outs = [_dot(p_c, cv_ref[s].astype(BF16)) + _dot(p_n, new_v[s])
            for s, (p_c, p_n, _) in zip(seqs, probs)]
    low4 = _iota2((dec_seq, 128), 1) < A_DH
    for s, o, (_, _, den) in zip(seqs, outs, probs):
        o = o / den
        for pair in range(A_HEADS // 2):
            even, odd = (o[(2 * pair + e) * dec_seq:(2 * pair + e + 1) * dec_seq, :] for e in range(2))
            if pair // (A_GROUP // 2) == 0:
                odd = pltpu.roll(odd, A_DH, 1)
            else:
                even = pltpu.roll(even, A_DH, 1)
            o_ref[rows_of(s), pair * 128:(pair + 1) * 128] = jnp.where(low4, even, odd)
    for s in seqs:
        for cache_ref, new_ref, out_ref in ((ck_ref, kn_ref, nk_ref), (cv_ref, vn_ref, nv_ref)):
            out_ref[s, 0:N_META, :] = cache_ref[s, 0:N_META, :]
            out_ref[s, N_META:N_BUF - dec_seq, :] = cache_ref[s, N_META + dec_seq:N_BUF, :]
            out_ref[s, N_BUF - dec_seq:N_BUF, :] = new_ref[rows_of(s), :]


def _sample_out_kernel(hs_ref, half_o_ref, half_z_ref, half_az_ref, hm_ref, oa_ref, ng_ref, wo_ref, lng_ref, lnb_ref,
                       y_ref):
    mix = _gate_mix(hm_ref[...], half_o_ref[...], half_z_ref[...], oa_ref[...], half_az_ref[...], ng_ref[...])
    y_ref[...] = _out_and_norm(hs_ref[...], mix, wo_ref, lng_ref[...], lnb_ref[...]).reshape(y_ref.shape)


def _sample_path(hs, u, db, dec_seq, cache_k, cache_v, state_c, state_n, state_m, a_sinks, norm_g, w_o, ln_g, ln_b):
    rows = db * dec_seq
    params = pltpu.CompilerParams(vmem_limit_bytes=VMEM_LIMIT_BYTES)

    n_groups = db // SEQ_PER_GROUP

    def col_spec(col0):
        assert col0 % D_MLSTM == 0
        return pl.BlockSpec((CHUNK, D_MLSTM), lambda g: (g, col0 // D_MLSTM))

    state_spec = pl.BlockSpec((SEQ_PER_GROUP, M_HEADS, M_DH, M_DH), lambda g: (g, 0, 0, 0))
    vec_spec = pl.BlockSpec((SEQ_PER_GROUP, M_HEADS, M_DH), lambda g: (g, 0, 0))
    m_spec = pl.BlockSpec((SEQ_PER_GROUP, M_HEADS), lambda g: (g, 0))
    h_m, c_new, n_new, m_new = pl.pallas_call(
        functools.partial(_sample_mlstm_kernel, dec_seq=dec_seq),
        grid=(n_groups,),
        in_specs=[col_spec(C_Q), col_spec(C_K), col_spec(C_V),
                  pl.BlockSpec((CHUNK, 128), lambda g: (g, C_G // 128)),
                  state_spec, vec_spec, m_spec],
        out_specs=[pl.BlockSpec((CHUNK, D_MLSTM), lambda g: (g, 0)), state_spec, vec_spec, m_spec],
        out_shape=(jax.ShapeDtypeStruct((rows, D_MLSTM), F32),
                   jax.ShapeDtypeStruct(state_c.shape, F32),
                   jax.ShapeDtypeStruct(state_n.shape, F32),
                   jax.ShapeDtypeStruct(state_m.shape, F32)),
        scratch_shapes=[pltpu.VMEM((CHUNK, M_DH), F32)],
        compiler_params=pltpu.CompilerParams(dimension_semantics=("arbitrary",),
                                             vmem_limit_bytes=VMEM_LIMIT_BYTES),
        name="sample_mlstm",
    )(u, u, u, u, state_c, state_n, state_m)

    n_qrows = A_HEADS * dec_seq
    sink_rows = jnp.broadcast_to(jnp.repeat(a_sinks.astype(F32), dec_seq)[:, None], (n_qrows, 128))
    ck = cache_k.reshape(db, N_BUF, KV_W)
    cv = cache_v.reshape(db, N_BUF, KV_W)
    sb = SWA_SEQ_PER_STEP

    def token_spec(col0, width):
        assert col0 % width == 0
        return pl.BlockSpec((sb * dec_seq, width), lambda i: (i, col0 // width))

    cache_spec = pl.BlockSpec((sb, N_BUF, KV_W), lambda i: (i, 0, 0))
    o_a, nk, nv = pl.pallas_call(
        functools.partial(_sample_swa_kernel, dec_seq=dec_seq),
        grid=(db // sb,),
        in_specs=[token_spec(C_AQ, D_SWA), token_spec(C_AK, KV_W), token_spec(C_AV, KV_W), cache_spec, cache_spec,
                  pl.BlockSpec((n_qrows, 128), lambda i: (0, 0))],
        out_specs=[pl.BlockSpec((sb * dec_seq, D_SWA), lambda i: (i, 0)), cache_spec, cache_spec],
        out_shape=(jax.ShapeDtypeStruct((rows, D_SWA), F32),
                   jax.ShapeDtypeStruct((db, N_BUF, KV_W), F32),
                   jax.ShapeDtypeStruct((db, N_BUF, KV_W), F32)),
        compiler_params=pltpu.CompilerParams(dimension_semantics=("arbitrary",),
                                             vmem_limit_bytes=VMEM_LIMIT_BYTES),
        name="sample_swa",
    )(u, u, u, ck, cv, sink_rows)

    def gate_spec(col0):
        assert col0 % D_MLSTM == 0
        return pl.BlockSpec((rows, D_MLSTM), lambda i: (0, col0 // D_MLSTM))

    def whole(shape):
        return pl.BlockSpec(shape, lambda i: (0,) * len(shape))

    y = pl.pallas_call(
        _sample_out_kernel,
        grid=(1,),
        in_specs=[whole((rows, D_MODEL)), gate_spec(C_O), gate_spec(C_Z), gate_spec(C_AZ),
                  whole((rows, D_MLSTM)), whole((rows, D_SWA)), whole((1, D_MLSTM)), whole((D_MODEL, D_MODEL)),
                  whole((1, D_MODEL)), whole((1, D_MODEL))],
        out_specs=whole((db, dec_seq, D_MODEL)),
        out_shape=jax.ShapeDtypeStruct((db, dec_seq, D_MODEL), F32),
        compiler_params=params,
        name="sample_out",
    )(hs, u, u, u, h_m, o_a, norm_g, w_o, ln_g, ln_b)

    shape5 = (1, db, N_BUF, A_KV_HEADS, A_DH)
    return (y, nk.reshape(shape5), nv.reshape(shape5), c_new[None],
            n_new[None], m_new[None])


def kernel(x_prompt, x_sample, cache_swa_k, cache_swa_v, state_mlstm_c, state_mlstm_n, state_mlstm_m,
           meta_tokens, ln0_g, ln0_b, w_in, b_in, a_sinks, m_norm_g, w_out, ln_g, ln_b):
    assert w_in.shape[0] == DEPTH and x_prompt.shape[-1] == D_MODEL
    w_t = jnp.transpose(w_in[0].astype(F32))
    b_all = b_in[0].astype(F32)[None]
    w_o = w_out[0].astype(BF16)
    g0 = ln0_g.astype(F32)[None]
    b0 = ln0_b.astype(F32)[None]
    lg = ln_g[0].astype(F32)[None]
    lb = ln_b[0].astype(F32)[None]
    norm_g = m_norm_g[0].astype(F32)[None]
    sinks = a_sinks[0].astype(F32)

    *meta_state, hs, u = _small_projections(meta_tokens, x_sample, g0, b0, w_t, b_all)
    y_p, pk, pv, pc, pn, pm = _prompt_path(x_prompt, meta_state, g0, b0, w_t, b_all, sinks, norm_g, w_o, lg, lb)
    y_s, sk, sv, sc, sn, sm = _sample_path(hs, u, x_sample.shape[0], x_sample.shape[1], cache_swa_k[0], cache_swa_v[0],
                                           state_mlstm_c[0], state_mlstm_n[0], state_mlstm_m[0],
                                           sinks, norm_g, w_o, lg, lb)
    return (y_p, y_s, pk, pv, pc, pn, pm, sk, sv, sc, sn, sm)
```

```python
import functools

import jax
import jax.numpy as jnp
from jax import lax
from jax.experimental import pallas as pl
from jax.experimental.pallas import tpu as pltpu

F32 = jnp.float32
BF16 = jnp.bfloat16

D_MODEL = 1024
N_META = 16
M_HEADS = 4
M_DH = 128
D_MLSTM = M_HEADS * M_DH
A_HEADS = 8
A_KV_HEADS = 2
A_GROUP = A_HEADS // A_KV_HEADS
A_DH = 64
D_SWA = A_HEADS * A_DH
KV_W = A_KV_HEADS * A_DH
WINDOW = 128
CHUNK = 128
LN_EPS = 1e-5
DEPTH = 1
DN_ALPHA = (2.0 * DEPTH) ** 0.25
N_BUF = N_META + WINDOW

C_Q, C_AQ = 0, 512
N_T_BF16 = 1024
C_V, C_O, C_Z, C_AZ, C_AV, C_G = 1024, 1536, 2048, 2560, 3072, 3200
GATE_ROWS = 16
N_T = C_G + GATE_ROWS
C_K, C_AK = 3584, 4096
N_PAD = 4224
N_KEYS = N_PAD - C_K
RAW_Q, RAW_K, RAW_V, RAW_O, RAW_Z, RAW_G = (0, 512), (512, 512), (1024, 512), (1536, 512), (2048, 512), (2560, 8)
RAW_AQ, RAW_AK, RAW_AV, RAW_AZ = (2568, 512), (3080, 128), (3208, 128), (3336, 512)
N_RAW = 3848

TOKEN_SPLITS = 1
OUT_SPLITS = 4
VMEM_LIMIT_BYTES = 56 * 1024 * 1024
NEG_INF = float("-inf")


def _dot(a, b):
    return jnp.dot(a, b, preferred_element_type=F32)


def _dot_nt(a, b):
    return lax.dot_general(a, b, (((1,), (1,)), ((), ())), preferred_element_type=F32)


def _layer_norm(x, g, b):
    mu = jnp.mean(x, axis=-1, keepdims=True)
    xc = x - mu
    var = jnp.mean(xc * xc, axis=-1, keepdims=True)
    return xc * lax.rsqrt(var + LN_EPS) * g + b


def _log_sigmoid(x):
    return jnp.minimum(x, 0.0) - jnp.log1p(jnp.exp(-jnp.abs(x)))


def _times_sigmoid(h, half_x):
    return 0.5 * (h * jnp.tanh(half_x) + h)


def _silu_of_twice(half_x):
    return half_x * jnp.tanh(half_x) + half_x


def _iota2(shape, dim):
    return lax.broadcasted_iota(jnp.int32, shape, dim)


T_FEATURES = ((RAW_G, C_G, 1.0), (RAW_Q, C_Q, 1.0), (RAW_AQ, C_AQ, A_DH ** -0.5), (RAW_AV, C_AV, 1.0), (RAW_V, C_V, 1.0),
              (RAW_O, C_O, 0.5), (RAW_Z, C_Z, 0.5), (RAW_AZ, C_AZ, 0.5))
KEY_FEATURES = ((RAW_K, C_K, 1.0), (RAW_AK, C_AK, 1.0))


def _weights(wt_ref, raw):
    return wt_ref[raw[0]:raw[0] + raw[1], :].astype(BF16)


def _project(token_sets, wt_ref, b_ref):
    for _, u_ref in token_sets:
        u_ref[:, C_G:C_K] = jnp.zeros((u_ref.shape[0], C_K - C_G), F32)
    for raw, dst, scale in T_FEATURES + KEY_FEATURES:
        w = _weights(wt_ref, raw)
        for hb, u_ref in token_sets:
            res = _dot_nt(hb, w) + b_ref[:, raw[0]:raw[0] + raw[1]]
            u_ref[:, dst:dst + raw[1]] = res if scale == 1.0 else res * scale


def _project_both(hb, tok, wt_ref, bcol_ref, utb_ref, utf_ref):
    reps = hb.shape[0] // 128
    assert RAW_Z[0] + RAW_Z[1] == RAW_G[0]
    for raw, dst, scale in ((RAW_Z[0], RAW_Z[1] + RAW_G[1]), C_Z, 0.5), *T_FEATURES:
        if raw in (RAW_Z, RAW_G):
            continue
        res = _dot_nt(_weights(wt_ref, raw), hb) + jnp.concatenate([bcol_ref[raw[0]:raw[0] + raw[1], :]] * reps, axis=1)
        if raw[1] > RAW_Z[1]:
            utf_ref[C_G - N_T_BF16:C_G - N_T_BF16 + RAW_G[1], tok] = res[RAW_Z[1]:]
            res = res[0:RAW_Z[1]]
        if scale != 1.0:
            res = res * scale
        if dst < N_T_BF16:
            utb_ref[dst:dst + res.shape[0], tok] = res.astype(BF16)
        else:
            utf_ref[dst - N_T_BF16:dst - N_T_BF16 + res.shape[0], tok] = res


def _project_keys(hb, tok, wt_ref, brow_ref, ukey_ref):
    for raw, dst, _ in KEY_FEATURES:
        ukey_ref[tok, dst - C_K:dst - C_K + raw[1]] = (_dot_nt(hb, _weights(wt_ref, raw))
                                                       + brow_ref[:, raw[0]:raw[0] + raw[1]])


def _gate_rows(gates):
    return _gate_scan(jnp.concatenate([gates[r0:r0 + CHUNK].T[0:8, :] for r0 in range(0, gates.shape[0], CHUNK)],
                                      axis=0))


def _gate_rows_t(gates_t):
    return _gate_scan(jnp.concatenate([gates_t[:, c0:c0 + CHUNK] for c0 in range(0, gates_t.shape[1], CHUNK)],
                                      axis=0))


def _gate_scan(g_t):
    x = _log_sigmoid(g_t)
    lane = _iota2(x.shape, 1)
    shift = 1
    while shift < x.shape[1]:
        x = x + jnp.where(lane >= shift, pltpu.roll(x, shift, 1), 0.0)
        shift *= 2
    return g_t, x


CT_ROWS = M_DH + 8


def _mlstm_scores_and_state(q_tb, kb, v_t, li_row, b_row, m_old, ct_aug, key_ok, n_valid):
    n = kb.shape[0]
    row = _iota2((n, n), 0)
    col = _iota2((n, n), 1)
    a_row = li_row - b_row
    a_keys = jnp.broadcast_to(jnp.sum(jnp.where(row == col, a_row, 0.0), axis=1, keepdims=True), (n, n))
    a_t = jnp.where(key_ok, a_keys, NEG_INF)
    mm = jnp.maximum(m_old, jnp.max(a_t, axis=0, keepdims=True))
    w_t = jnp.exp(a_t - mm)
    lane = _iota2((1, n), 1)
    last = n_valid - 1
    mm_last = jnp.max(jnp.where(lane == last, mm, NEG_INF), axis=1, keepdims=True)
    m_new = jnp.sum(jnp.where(lane == last, b_row, 0.0), axis=1, keepdims=True) + mm_last
    w_state = jnp.exp(a_row - mm_last)
    if n_valid < n:
        w_state = jnp.where(lane < n_valid, w_state, 0.0)
    decay = jnp.exp(m_old - mm_last)
    scores_t = _dot(kb, q_tb)
    inter = _dot(ct_aug.astype(BF16), q_tb)
    ones_row = jnp.where(_iota2((CT_ROWS - M_DH, n), 0) == 0, w_state, 0.0)
    vtw = jnp.concatenate([v_t * w_state, ones_row], axis=0).astype(BF16)
    ct_aug_new = decay * ct_aug + _dot(vtw, kb)
    s_inter = jnp.exp(m_old - mm)
    floor = jnp.exp(-(b_row + mm))
    return (scores_t, w_t, v_t.astype(BF16), inter, s_inter, floor), ct_aug_new, m_new


def _mlstm_weighted_values(scores_t, w_t, vtb, inter, s_inter, floor):
    qkw_t = scores_t * w_t
    den = jnp.sum(qkw_t, axis=0, keepdims=True)
    num_t = _dot(vtb, qkw_t.astype(BF16))
    return num_t, den, inter, s_inter, floor


def _mlstm_finish(num_t, den, inter, s_inter, floor):
    num_t = num_t + inter[0:M_DH] * s_inter
    den = den + inter[M_DH:M_DH + 1] * s_inter
    return num_t * (1.0 / jnp.maximum(jnp.abs(den), floor))


def _mlstm_gate_head(h, half_o, half_z, norm_g, axis):
    hh = _times_sigmoid(h, half_o)
    mu = jnp.mean(hh, axis=axis, keepdims=True)
    hc = hh - mu
    var = jnp.mean(hc * hc, axis=axis, keepdims=True)
    return hc * lax.rsqrt(var + LN_EPS) * norm_g * _silu_of_twice(half_z)


def _gate_mix(h_m, half_o, half_z, o_a, half_az, norm_g):
    parts = []
    for hd in range(M_HEADS):
        sl = slice(hd * M_DH, (hd + 1) * M_DH)
        parts.append(_mlstm_gate_head(h_m[:, sl], half_o[:, sl], half_z[:, sl], norm_g[:, sl], axis=-1).astype(BF16))
    parts.append((o_a * _silu_of_twice(half_az)).astype(BF16))
    return jnp.concatenate(parts, axis=-1)


def _out_and_norm(hp, mix, wo_ref, g, b):
    z = DN_ALPHA * hp + _dot(mix, wo_ref[...])
    return _layer_norm(z, g, b)


def _small_projections_kernel(meta_ref, xs_ref, wt_ref, bias_ref, g0_ref, b0_ref,
                              ct0_ref, m0_ref, km_ref, vm_ref, vmt_ref, hs_ref, us_ref, u_ref):
    hs = _layer_norm(xs_ref[...].reshape(hs_ref.shape), g0_ref[...], b0_ref[...])
    hs_ref[...] = hs
    meta = jnp.concatenate([meta_ref[...], jnp.zeros((CHUNK - N_META, D_MODEL), F32)], axis=0)
    hp = _layer_norm(meta, g0_ref[...], b0_ref[...])
    _project(((hs.astype(BF16), us_ref), (hp.astype(BF16), u_ref)), wt_ref, bias_ref)
    row = _iota2((CHUNK, CHUNK), 0)
    col = _iota2((CHUNK, CHUNK), 1)
    key_ok = (row <= col) & (row < N_META)
    li_rows, b_rows = _gate_rows(u_ref[:, C_G:C_G + 128])
    zero_m = jnp.zeros((1, 128), F32)
    zero_ct = jnp.zeros((CT_ROWS, M_DH), F32)
    m0_ref[...] = jnp.zeros(m0_ref.shape, F32)
    for hd in range(M_HEADS):
        q = u_ref[:, C_Q + hd * M_DH:C_Q + (hd + 1) * M_DH]
        k = u_ref[:, C_K + hd * M_DH:C_K + (hd + 1) * M_DH] * (M_DH ** -0.5)
        v = u_ref[:, C_V + hd * M_DH:C_V + (hd + 1) * M_DH]
        _, ct_new, m_new = _mlstm_scores_and_state(
            q.T.astype(BF16), k.astype(BF16), v.T, li_rows[hd:hd + 1], b_rows[M_HEADS + hd:M_HEADS + hd + 1],
            zero_m, zero_ct, key_ok, N_META)
        ct0_ref[hd] = ct_new
        m0_ref[hd:hd + 1, :] = jnp.broadcast_to(m_new, (1, 128))
    km_ref[...] = u_ref[0:N_META, C_AK:C_AK + KV_W]
    vm_ref[...] = u_ref[0:N_META, C_AV:C_AV + KV_W]
    vmt_ref[...] = u_ref[:, C_AV:C_AV + KV_W].T[:, 0:N_META]


def _keep_kv_half(x, kv):
    low = _iota2(x.shape, 1) < A_DH
    return jnp.where(low if kv == 0 else ~low, x, 0.0)


def _swa_weighted_values_t(scores, values_t, sink, own_ok, prev_ok):
    s_own = jnp.where(own_ok, scores[0], NEG_INF)
    s_prev = jnp.where(prev_ok, scores[1], NEG_INF)
    s_meta = scores[2]
    mx = jnp.maximum(jnp.maximum(jnp.max(s_own, axis=0, keepdims=True),
                                 jnp.max(s_prev, axis=0, keepdims=True)),
                     jnp.maximum(jnp.max(s_meta, axis=0, keepdims=True), sink))
    p_own, p_prev, p_meta = (jnp.exp(s - mx) for s in (s_own, s_prev, s_meta))
    den = (jnp.sum(p_own, axis=0, keepdims=True) + jnp.sum(p_prev, axis=0, keepdims=True)
           + jnp.sum(p_meta, axis=0, keepdims=True) + jnp.exp(sink - mx))
    o_t = (_dot(values_t[0], p_own.astype(BF16)) + _dot(values_t[1], p_prev.astype(BF16))
           + _dot(values_t[2], p_meta.astype(BF16)))
    return o_t, den


def _prompt_kernel(sink_ref, x_ref, wt_ref, bcol_ref, brow_ref, g0_ref, b0_ref,
                   ct0_ref, m0_ref, km_ref, vm_ref, vmt_ref, ng_ref, wo_ref, lng_ref, lnb_ref,
                   y_ref, pk_ref, pv_ref, pc_ref, pn_ref, pm_ref,
                   utb_ref, utf_ref, ukey_ref, hp_ref, mix_ref, ct_ref, kprev_ref, vtprev_ref, li_ref, cumf_ref,
                   *, tb):
    j = pl.program_id(1)
    n_chunks = tb // CHUNK

    @pl.when(j == 0)
    def _():
        ct_ref[...] = ct0_ref[...]
        pm_ref[0] = m0_ref[...]
        kprev_ref[...] = jnp.zeros(kprev_ref.shape, F32)
        vtprev_ref[...] = jnp.zeros(vtprev_ref.shape, F32)
        pk_ref[0, 0:N_META, :] = km_ref[...]
        pv_ref[0, 0:N_META, :] = vm_ref[...]

    for t0 in range(0, tb, tb // TOKEN_SPLITS):
        tok = slice(t0, t0 + tb // TOKEN_SPLITS)
        hp = _layer_norm(x_ref[0, tok, :], g0_ref[...], b0_ref[...])
        hp_ref[tok, :] = hp
        hb = hp.astype(BF16)
        _project_keys(hb, tok, wt_ref, brow_ref, ukey_ref)
        _project_both(hb, tok, wt_ref, bcol_ref, utb_ref, utf_ref)

    def feat(c0, n=M_DH):
        return slice(c0 - N_T_BF16, c0 - N_T_BF16 + n)

    li_ref[...], cumf_ref[...] = _gate_rows_t(utf_ref[feat(C_G, 8), :])

    key = _iota2((CHUNK, CHUNK), 0)
    query = _iota2((CHUNK, CHUNK), 1)
    causal = key <= query

    def chunk_rows(ci):
        return slice(ci * CHUNK, (ci + 1) * CHUNK)

    def attention_scores(ci):
        rows = chunk_rows(ci)
        k_own = ukey_ref[rows, C_AK - C_K:C_AK - C_K + KV_W]
        vt_own = utf_ref[feat(C_AV, KV_W), rows]
        k_prev = kprev_ref[...]
        vt_prev = vtprev_ref[...]
        prev_ok = (key > query) & ((j * n_chunks + ci) > 0)
        keys_kv = [tuple(_keep_kv_half(x, kv).astype(BF16) for x in (k_own, k_prev, km_ref[...]))
                   for kv in range(A_KV_HEADS)]
        values_kv = [tuple(x[kv * A_DH:(kv + 1) * A_DH].astype(BF16) for x in (vt_own, vt_prev, vmt_ref[...]))
                     for kv in range(A_KV_HEADS)]
        a_scores = []
        for hd in range(A_HEADS):
            kv = hd // A_GROUP
            q0 = C_AQ + (hd - kv) * A_DH
            q_win = utb_ref[q0:q0 + 2 * A_DH, rows]
            a_scores.append(tuple(_dot(kk, q_win) for kk in keys_kv[kv]))
        kprev_ref[...] = k_own
        vtprev_ref[...] = vt_own
        return prev_ok, a_scores, values_kv

    def mlstm_first_stage(ci):
        rows = chunk_rows(ci)
        gate_rows = slice(ci * 8, (ci + 1) * 8)
        li_rows = li_ref[gate_rows, :]
        b_rows = cumf_ref[gate_rows, :]
        m_carry = []
        for hd in range(M_HEADS):
            q_tb = utb_ref[C_Q + hd * M_DH:C_Q + (hd + 1) * M_DH, rows]
            kb = (ukey_ref[rows, hd * M_DH:(hd + 1) * M_DH] * (M_DH ** -0.5)).astype(BF16)
            v_t = utf_ref[feat(C_V + hd * M_DH), rows]
            carry_hd, ct_new, m_new = _mlstm_scores_and_state(
                q_tb, kb, v_t, li_rows[hd:hd + 1], b_rows[M_HEADS + hd:M_HEADS + hd + 1],
                pm_ref[0, hd:hd + 1, :], ct_ref[hd], causal, CHUNK)
            ct_ref[hd] = ct_new
            pm_ref[0, hd:hd + 1, :] = jnp.broadcast_to(m_new, (1, 128))
            m_carry.append(carry_hd)
        return m_carry

    def mlstm_second_stage(m_carry):
        return [_mlstm_weighted_values(*c) for c in m_carry]

    def chunk_finish(ci, prev_ok, a_scores, values_kv, m_carry):
        rows = chunk_rows(ci)
        a_out = [_swa_weighted_values_t(a_scores[hd], values_kv[hd // A_GROUP], sink_ref[hd], causal, prev_ok)
                 for hd in range(A_HEADS)]

        mix_t = []
        for hd in range(M_HEADS):
            mix_t.append(_mlstm_gate_head(_mlstm_finish(*m_carry[hd]), utf_ref[feat(C_O + hd * M_DH), rows],
                                          utf_ref[feat(C_Z + hd * M_DH), rows],
                                          ng_ref[hd * M_DH:(hd + 1) * M_DH, :], axis=0))
        for tile in range(A_HEADS // 2):
            o_t = jnp.concatenate([o * (1.0 / den) for o, den in a_out[2 * tile:2 * tile + 2]], axis=0)
            mix_t.append(o_t * _silu_of_twice(utf_ref[feat(C_AZ + tile * 128), rows]))
        mix_rows = [x_t.T.astype(BF16) for x_t in mix_t]
        for i, x in enumerate(mix_rows):
            mix_ref[rows, i * 128:(i + 1) * 128] = x

    for ci in range(n_chunks):
        attn = attention_scores(ci)
        m_carry = mlstm_second_stage(mlstm_first_stage(ci))
        chunk_finish(ci, *attn, m_carry)

    for t0 in range(0, tb, tb // OUT_SPLITS):
        tok = slice(t0, t0 + tb // OUT_SPLITS)
        y_ref[0, tok, :] = _out_and_norm(hp_ref[tok, :], mix_ref[tok, :], wo_ref, lng_ref[...], lnb_ref[...])

    @pl.when(j == pl.num_programs(1) - 1)
    def _():
        pk_ref[0, N_META:N_BUF, :] = ukey_ref[tb - WINDOW:tb, C_AK - C_K:C_AK - C_K + KV_W]
        pv_ref[0, N_META:N_BUF, :] = utf_ref[feat(C_AV, KV_W), tb - WINDOW:tb].T
        for hd in range(M_HEADS):
            pc_ref[0, hd] = ct_ref[hd, 0:M_DH, :].T
            pn_ref[0, hd:hd + 1, :] = ct_ref[hd, M_DH:M_DH + 1, :]
        pn_ref[0, M_HEADS:8, :] = jnp.zeros((8 - M_HEADS, M_DH), F32)


def _const_spec(shape):
    return pl.BlockSpec(shape, lambda *_: (0,) * len(shape))


def _small_projections(meta_tokens, x_sample, ln0_g, ln0_b, w_t, b_all):
    rows = x_sample.shape[0] * x_sample.shape[1]
    return pl.pallas_call(
        _small_projections_kernel,
        out_shape=(jax.ShapeDtypeStruct((M_HEADS, CT_ROWS, M_DH), F32),
                   jax.ShapeDtypeStruct((8, 128), F32),
                   jax.ShapeDtypeStruct((N_META, KV_W), F32),
                   jax.ShapeDtypeStruct((N_META, KV_W), F32),
                   jax.ShapeDtypeStruct((KV_W, N_META), F32),
                   jax.ShapeDtypeStruct((rows, D_MODEL), F32),
                   jax.ShapeDtypeStruct((rows, N_PAD), F32)),
        scratch_shapes=[pltpu.VMEM((CHUNK, N_PAD), F32)],
        compiler_params=pltpu.CompilerParams(vmem_limit_bytes=VMEM_LIMIT_BYTES),
        name="small_projections",
    )(meta_tokens.astype(F32), x_sample, w_t, b_all, ln0_g, ln0_b)


def _prompt_path(x_prompt, meta_state, ln0_g, ln0_b, w_t, b_all, sinks, norm_g, w_o, ln_g, ln_b, tb=512):
    batch, seq, _ = x_prompt.shape
    ct0, m0, km, vm, vmt = meta_state

    nj = seq // tb
    in_specs = [
        pl.BlockSpec(memory_space=pltpu.SMEM),
        pl.BlockSpec((1, tb, D_MODEL), lambda b, j: (b, j, 0)),
        pl.BlockSpec((N_RAW, D_MODEL), lambda b, j: (0, 0), pipeline_mode=pl.Buffered(1)),
        _const_spec((N_RAW, 128)), _const_spec((1, N_RAW)),
        _const_spec((1, D_MODEL)), _const_spec((1, D_MODEL)),
        _const_spec((M_HEADS, CT_ROWS, M_DH)), _const_spec((8, 128)),
        _const_spec((N_META, KV_W)), _const_spec((N_META, KV_W)), _const_spec((KV_W, N_META)),
        _const_spec((D_MLSTM, 128)),
        _const_spec((D_MODEL, D_MODEL)),
        _const_spec((1, D_MODEL)), _const_spec((1, D_MODEL)),
    ]
    out_specs = [
        pl.BlockSpec((1, tb, D_MODEL), lambda b, j: (b, j, 0)),
        pl.BlockSpec((1, N_BUF, KV_W), lambda b, j: (b, 0, 0)),
        pl.BlockSpec((1, N_BUF, KV_W), lambda b, j: (b, 0, 0)),
        pl.BlockSpec((1, M_HEADS, M_DH, M_DH), lambda b, j: (b, 0, 0, 0)),
        pl.BlockSpec((1, 8, M_DH), lambda b, j: (b, 0, 0)),
        pl.BlockSpec((1, 8, 128), lambda b, j: (b, 0, 0)),
    ]
    out_shape = (
        jax.ShapeDtypeStruct((batch, seq, D_MODEL), F32),
        jax.ShapeDtypeStruct((batch, N_BUF, KV_W), F32),
        jax.ShapeDtypeStruct((batch, N_BUF, KV_W), F32),
        jax.ShapeDtypeStruct((batch, M_HEADS, M_DH, M_DH), F32),
        jax.ShapeDtypeStruct((batch, 8, M_DH), F32),
        jax.ShapeDtypeStruct((batch, 8, 128), F32),
    )
    y, pk, pv, pc, pn, pm = pl.pallas_call(
        functools.partial(_prompt_kernel, tb=tb),
        grid=(batch, nj),
        in_specs=in_specs,
        out_specs=out_specs,
        out_shape=out_shape,
        scratch_shapes=[pltpu.VMEM((N_T_BF16, tb), BF16), pltpu.VMEM((N_T - N_T_BF16, tb), F32),
                        pltpu.VMEM((tb, N_KEYS), F32),
                        pltpu.VMEM((tb, D_MODEL), F32), pltpu.VMEM((tb, D_MODEL), BF16),
                        pltpu.VMEM((M_HEADS, CT_ROWS, M_DH), F32),
                        pltpu.VMEM((CHUNK, KV_W), F32), pltpu.VMEM((KV_W, CHUNK), F32),
                        pltpu.VMEM((tb // CHUNK * 8, CHUNK), F32), pltpu.VMEM((tb // CHUNK * 8, CHUNK), F32)],
        compiler_params=pltpu.CompilerParams(dimension_semantics=("arbitrary", "arbitrary"),
                                             vmem_limit_bytes=VMEM_LIMIT_BYTES),
        name="prompt_layer",
    )(sinks, x_prompt, w_t, jnp.broadcast_to(b_all.reshape(N_RAW, 1), (N_RAW, 128)), b_all, ln0_g, ln0_b,
      ct0, m0, km, vm, vmt, jnp.broadcast_to(norm_g.reshape(D_MLSTM, 1), (D_MLSTM, 128)), w_o, ln_g, ln_b)
    pk = pk.reshape(1, batch, N_BUF, A_KV_HEADS, A_DH)
    pv = pv.reshape(1, batch, N_BUF, A_KV_HEADS, A_DH)
    return y, pk, pv, pc[None], pn[:, :M_HEADS][None], pm[:, :M_HEADS, 0][None]


SEQ_PER_GROUP = 32
SWA_SEQ_PER_STEP = 8
SEQ_BATCH = 8


def _segment_last(x, pos, seg_len):
    n = x.shape[1]
    step = 1
    while step < seg_len:
        x = jnp.where((pos // step) % 2 == 0, pltpu.roll(x, n - step, 1), x)
        step *= 2
    return x


def _sample_mlstm_kernel(q_ref, k_ref, v_ref, g_ref, c_ref, n_ref, m_ref,
                         h_ref, cn_ref, nn_ref, mn_ref, inter_ref, *, dec_seq):
    g_t = g_ref[...].T[0:8, :]
    for hd in range(M_HEADS):
        _sample_mlstm_head(hd, g_t, q_ref, k_ref, v_ref, c_ref, n_ref, m_ref,
                           h_ref, cn_ref, nn_ref, mn_ref, inter_ref, dec_seq)


def _sample_mlstm_head(hd, g_t, q_ref, k_ref, v_ref, c_ref, n_ref, m_ref,
                       h_ref, cn_ref, nn_ref, mn_ref, inter_ref, dec_seq):
    sl = slice(hd * M_DH, (hd + 1) * M_DH)
    n = CHUNK
    nb = n // dec_seq
    q = q_ref[:, sl]
    k = k_ref[:, sl] * (M_DH ** -0.5)
    v = v_ref[:, sl]
    qb = q.astype(BF16)
    kb = k.astype(BF16)
    vb = v.astype(BF16)

    li_row = g_t[hd:hd + 1, :]
    lf_row = _log_sigmoid(g_t[M_HEADS + hd:M_HEADS + hd + 1, :])
    lane = _iota2((1, n), 1)
    pos = lane % dec_seq
    b_row = lf_row
    shift = 1
    while shift < dec_seq:
        b_row = b_row + jnp.where(pos >= shift, pltpu.roll(b_row, shift, 1), 0.0)
        shift *= 2

    key = _iota2((n, n), 0)
    query = _iota2((n, n), 1)
    key_ok = (key <= query) & (key // dec_seq == query // dec_seq)
    a_row = li_row - b_row
    a_keys = jnp.broadcast_to(jnp.sum(jnp.where(key == query, a_row, 0.0), axis=1, keepdims=True), (n, n))
    a_t = jnp.where(key_ok, a_keys, NEG_INF)
    seq_of_lane = _iota2((nb, n), 1) // dec_seq == _iota2((nb, n), 0)
    first_lane = _iota2((nb, n), 1) == _iota2((nb, n), 0) * dec_seq
    m_old = jnp.sum(jnp.where(seq_of_lane, m_ref[:, hd:hd + 1], 0.0), axis=0, keepdims=True)
    mm = jnp.maximum(m_old, jnp.max(a_t, axis=0, keepdims=True))
    w_t = jnp.exp(a_t - mm)
    mm_last = _segment_last(mm, pos, dec_seq)
    m_new = _segment_last(b_row, pos, dec_seq) + mm_last
    w_state = jnp.exp(a_row - mm_last)
    decay = jnp.exp(m_old - mm_last)
    mn_ref[:, hd:hd + 1] = jnp.sum(jnp.where(first_lane, m_new, 0.0), axis=1, keepdims=True)

    qkw_t = _dot_nt(kb, qb) * w_t
    den = jnp.sum(qkw_t, axis=0, keepdims=True)
    num_t = _dot(v.T.astype(BF16), qkw_t.astype(BF16))

    n_seq = n_ref[:, hd, :]
    decay_seq = jnp.sum(jnp.where(first_lane, decay, 0.0), axis=1, keepdims=True)
    nn_ref[:, hd, :] = decay_seq * n_seq + _dot(jnp.where(seq_of_lane, w_state, 0.0).astype(BF16), kb)
    expand = jnp.where(_iota2((n, nb), 0) // dec_seq == _iota2((n, nb), 1), 1.0, 0.0).astype(BF16)
    n_rows = _dot(expand, n_seq.astype(BF16))
    qn_col = jnp.sum(q * n_rows, axis=1, keepdims=True)
    qn = jnp.sum(jnp.where(key == query, qn_col, 0.0), axis=0, keepdims=True)

    kwt = k.T * w_state
    decay_rows = jnp.broadcast_to(jnp.sum(jnp.where(key == query, decay, 0.0), axis=1, keepdims=True), (n, n))
    lane_seq = query // dec_seq
    low_rows = _iota2((8, M_DH), 0) < dec_seq
    assert 8 % dec_seq == 0 and 8 // dec_seq == 2
    for pair in range(nb // 2):
        q8 = q[8 * pair:8 * pair + 8].astype(BF16)
        inter_ref[8 * pair:8 * pair + 8, :] = jnp.where(low_rows, _dot(q8, c_ref[2 * pair, hd].astype(BF16)),
                                                        _dot(q8, c_ref[2 * pair + 1, hd].astype(BF16)))
    for s0 in range(0, nb, SEQ_BATCH):
        seqs = range(s0, s0 + SEQ_BATCH)
        updates = [_dot(jnp.where(lane_seq == s, kwt, 0.0).astype(BF16), vb) for s in seqs]
        for s, upd in zip(seqs, updates):
            cn_ref[s, hd] = decay_rows[s * dec_seq:s * dec_seq + 1, :] * c_ref[s, hd] + upd

    s_inter = jnp.exp(m_old - mm)
    num_t = num_t + inter_ref[...].T * s_inter
    den = den + qn * s_inter
    h_ref[:, sl] = (num_t * (1.0 / jnp.maximum(jnp.abs(den), jnp.exp(-(b_row + mm))))).T


def _sample_swa_kernel(aq_ref, kn_ref, vn_ref, ck_ref, cv_ref, sink_ref, o_ref, nk_ref, nv_ref, *, dec_seq):
    n_rows = A_HEADS * dec_seq
    t_c = _iota2((n_rows, N_BUF), 0) % dec_seq
    i_c = _iota2((n_rows, N_BUF), 1)
    ok_c = (i_c < N_META) | (i_c > t_c + N_META)
    t_n = _iota2((n_rows, 8), 0) % dec_seq
    i_n = _iota2((n_rows, 8), 1)
    ok_n = (i_n <= t_n) & (i_n < dec_seq)
    sink = sink_ref[:, 0:1]
    seqs = range(ck_ref.shape[0])
    rows_of = lambda s: slice(s * dec_seq, (s + 1) * dec_seq)

    low = _iota2((len(seqs) * dec_seq, 128), 1) < A_DH
    head_q = []
    for hd in range(A_HEADS):
        tile = aq_ref[:, (hd // 2) * 128:(hd // 2 + 1) * 128]
        piece = jnp.where(low if hd % 2 == 0 else ~low, tile, 0.0)
        head_q.append(piece if hd % 2 == hd // A_GROUP else pltpu.roll(piece, A_DH, 1))
    pad_rows = jnp.zeros((8 - dec_seq, KV_W), F32)
    new_k = [jnp.concatenate([kn_ref[rows_of(s), :], pad_rows], axis=0).astype(BF16) for s in seqs]
    new_v = [jnp.concatenate([vn_ref[rows_of(s), :], pad_rows], axis=0).astype(BF16) for s in seqs]

    scores = []
    for s in seqs:
        qz = jnp.concatenate([q[rows_of(s), :] for q in head_q], axis=0).astype(BF16)
        scores.append((_dot_nt(qz, ck_ref[s].astype(BF16)), _dot_nt(qz, new_k[s])))
    probs = []
    for s_c, s_n in scores:
        s_c = jnp.where(ok_c, s_c, NEG_INF)
        s_n = jnp.where(ok_n, s_n, NEG_INF)
        mx = jnp.maximum(jnp.maximum(jnp.max(s_c, axis=1, keepdims=True),
                                     jnp.max(s_n, axis=1, keepdims=True)), sink)
        p_c = jnp.exp(s_c - mx)
        p_n = jnp.exp(s_n - mx)
        den = jnp.sum(p_c, axis=1, keepdims=True) + jnp.sum(p_n, axis=1, keepdims=True) + jnp.exp(sink - mx)
        probs.append((p_c.astype(BF16), p_n.astype(BF16), den))
    outs = [_dot(p_c, cv_ref[s].astype(BF16)) + _dot(p_n, new_v[s])
            for s, (p_c, p_n, _) in zip(seqs, probs)]
    low4 = _iota2((dec_seq, 128), 1) < A_DH
    for s, o, (_, _, den) in zip(seqs, outs, probs):
        o = o / den
        for pair in range(A_HEADS // 2):
            even, odd = (o[(2 * pair + e) * dec_seq:(2 * pair + e + 1) * dec_seq, :] for e in range(2))
            if pair // (A_GROUP // 2) == 0:
                odd = pltpu.roll(odd, A_DH, 1)
            else:
                even = pltpu.roll(even, A_DH, 1)
            o_ref[rows_of(s), pair * 128:(pair + 1) * 128] = jnp.where(low4, even, odd)
    for s in seqs:
        for cache_ref, new_ref, out_ref in ((ck_ref, kn_ref, nk_ref), (cv_ref, vn_ref, nv_ref)):
            out_ref[s, 0:N_META, :] = cache_ref[s, 0:N_META, :]
            out_ref[s, N_META:N_BUF - dec_seq, :] = cache_ref[s, N_META + dec_seq:N_BUF, :]
            out_ref[s, N_BUF - dec_seq:N_BUF, :] = new_ref[rows_of(s), :]


def _sample_out_kernel(hs_ref, half_o_ref, half_z_ref, half_az_ref, hm_ref, oa_ref, ng_ref, wo_ref, lng_ref, lnb_ref,
                       y_ref):
    mix = _gate_mix(hm_ref[...], half_o_ref[...], half_z_ref[...], oa_ref[...], half_az_ref[...], ng_ref[...])
    y_ref[...] = _out_and_norm(hs_ref[...], mix, wo_ref, lng_ref[...], lnb_ref[...]).reshape(y_ref.shape)


def _sample_path(hs, u, db, dec_seq, cache_k, cache_v, state_c, state_n, state_m, a_sinks, norm_g, w_o, ln_g, ln_b):
    rows = db * dec_seq
    params = pltpu.CompilerParams(vmem_limit_bytes=VMEM_LIMIT_BYTES)

    n_groups = db // SEQ_PER_GROUP

    def col_spec(col0):
        assert col0 % D_MLSTM == 0
        return pl.BlockSpec((CHUNK, D_MLSTM), lambda g: (g, col0 // D_MLSTM))

    state_spec = pl.BlockSpec((SEQ_PER_GROUP, M_HEADS, M_DH, M_DH), lambda g: (g, 0, 0, 0))
    vec_spec = pl.BlockSpec((SEQ_PER_GROUP, M_HEADS, M_DH), lambda g: (g, 0, 0))
    m_spec = pl.BlockSpec((SEQ_PER_GROUP, M_HEADS), lambda g: (g, 0))
    h_m, c_new, n_new, m_new = pl.pallas_call(
        functools.partial(_sample_mlstm_kernel, dec_seq=dec_seq),
        grid=(n_groups,),
        in_specs=[col_spec(C_Q), col_spec(C_K), col_spec(C_V),
                  pl.BlockSpec((CHUNK, 128), lambda g: (g, C_G // 128)),
                  state_spec, vec_spec, m_spec],
        out_specs=[pl.BlockSpec((CHUNK, D_MLSTM), lambda g: (g, 0)), state_spec, vec_spec, m_spec],
        out_shape=(jax.ShapeDtypeStruct((rows, D_MLSTM), F32),
                   jax.ShapeDtypeStruct(state_c.shape, F32),
                   jax.ShapeDtypeStruct(state_n.shape, F32),
                   jax.ShapeDtypeStruct(state_m.shape, F32)),
        scratch_shapes=[pltpu.VMEM((CHUNK, M_DH), F32)],
        compiler_params=pltpu.CompilerParams(dimension_semantics=("arbitrary",),
                                             vmem_limit_bytes=VMEM_LIMIT_BYTES),
        name="sample_mlstm",
    )(u, u, u, u, state_c, state_n, state_m)

    n_qrows = A_HEADS * dec_seq
    sink_rows = jnp.broadcast_to(jnp.repeat(a_sinks.astype(F32), dec_seq)[:, None], (n_qrows, 128))
    ck = cache_k.reshape(db, N_BUF, KV_W)
    cv = cache_v.reshape(db, N_BUF, KV_W)
    sb = SWA_SEQ_PER_STEP

    def token_spec(col0, width):
        assert col0 % width == 0
        return pl.BlockSpec((sb * dec_seq, width), lambda i: (i, col0 // width))

    cache_spec = pl.BlockSpec((sb, N_BUF, KV_W), lambda i: (i, 0, 0))
    o_a, nk, nv = pl.pallas_call(
        functools.partial(_sample_swa_kernel, dec_seq=dec_seq),
        grid=(db // sb,),
        in_specs=[token_spec(C_AQ, D_SWA), token_spec(C_AK, KV_W), token_spec(C_AV, KV_W), cache_spec, cache_spec,
                  pl.BlockSpec((n_qrows, 128), lambda i: (0, 0))],
        out_specs=[pl.BlockSpec((sb * dec_seq, D_SWA), lambda i: (i, 0)), cache_spec, cache_spec],
        out_shape=(jax.ShapeDtypeStruct((rows, D_SWA), F32),
                   jax.ShapeDtypeStruct((db, N_BUF, KV_W), F32),
                   jax.ShapeDtypeStruct((db, N_BUF, KV_W), F32)),
        compiler_params=pltpu.CompilerParams(dimension_semantics=("arbitrary",),
                                             vmem_limit_bytes=VMEM_LIMIT_BYTES),
        name="sample_swa",
    )(u, u, u, ck, cv, sink_rows)

    def gate_spec(col0):
        assert col0 % D_MLSTM == 0
        return pl.BlockSpec((rows, D_MLSTM), lambda i: (0, col0 // D_MLSTM))

    def whole(shape):
        return pl.BlockSpec(shape, lambda i: (0,) * len(shape))

    y = pl.pallas_call(
        _sample_out_kernel,
        grid=(1,),
        in_specs=[whole((rows, D_MODEL)), gate_spec(C_O), gate_spec(C_Z), gate_spec(C_AZ),
                  whole((rows, D_MLSTM)), whole((rows, D_SWA)), whole((1, D_MLSTM)), whole((D_MODEL, D_MODEL)),
                  whole((1, D_MODEL)), whole((1, D_MODEL))],
        out_specs=whole((db, dec_seq, D_MODEL)),
        out_shape=jax.ShapeDtypeStruct((db, dec_seq, D_MODEL), F32),
        compiler_params=params,
        name="sample_out",
    )(hs, u, u, u, h_m, o_a, norm_g, w_o, ln_g, ln_b)

    shape5 = (1, db, N_BUF, A_KV_HEADS, A_DH)
    return (y, nk.reshape(shape5), nv.reshape(shape5), c_new[None],
            n_new[None], m_new[None])


def kernel(x_prompt, x_sample, cache_swa_k, cache_swa_v, state_mlstm_c, state_mlstm_n, state_mlstm_m,
           meta_tokens, ln0_g, ln0_b, w_in, b_in, a_sinks, m_norm_g, w_out, ln_g, ln_b):
    assert w_in.shape[0] == DEPTH and x_prompt.shape[-1] == D_MODEL
    w_t = jnp.transpose(w_in[0].astype(F32))
    b_all = b_in[0].astype(F32)[None]
    w_o = w_out[0].astype(BF16)
    g0 = ln0_g.astype(F32)[None]
    b0 = ln0_b.astype(F32)[None]
    lg = ln_g[0].astype(F32)[None]
    lb = ln_b[0].astype(F32)[None]
    norm_g = m_norm_g[0].astype(F32)[None]
    sinks = a_sinks[0].astype(F32)

    *meta_state, hs, u = _small_projections(meta_tokens, x_sample, g0, b0, w_t, b_all)
    y_p, pk, pv, pc, pn, pm = _prompt_path(x_prompt, meta_state, g0, b0, w_t, b_all, sinks, norm_g, w_o, lg, lb)
    y_s, sk, sv, sc, sn, sm = _sample_path(hs, u, x_sample.shape[0], x_sample.shape[1], cache_swa_k[0], cache_swa_v[0],
                                           state_mlstm_c[0], state_mlstm_n[0], state_mlstm_m[0],
                                           sinks, norm_g, w_o, lg, lb)
    return (y_p, y_s, pk, pv, pc, pn, pm, sk, sv, sc, sn, sm)
```

```python
import functools

import jax
import jax.numpy as jnp
from jax import lax
from jax.experimental import pallas as pl
from jax.experimental.pallas import tpu as pltpu

F32 = jnp.float32
BF16 = jnp.bfloat16

D_MODEL = 1024
N_META = 16
M_HEADS = 4
M_DH = 128
D_MLSTM = M_HEADS * M_DH
A_HEADS = 8
A_KV_HEADS = 2
A_GROUP = A_HEADS // A_KV_HEADS
A_DH = 64
D_SWA = A_HEADS * A_DH
KV_W = A_KV_HEADS * A_DH
WINDOW = 128
CHUNK = 128
LN_EPS = 1e-5
DEPTH = 1
DN_ALPHA = (2.0 * DEPTH) ** 0.25
N_BUF = N_META + WINDOW

C_Q, C_AQ = 0, 512
N_T_BF16 = 1024
C_V, C_O, C_Z, C_AZ, C_AV, C_G = 1024, 1536, 2048, 2560, 3072, 3200
GATE_ROWS = 16
N_T = C_G + GATE_ROWS
C_K, C_AK = 3584, 4096
N_PAD = 4224
N_KEYS = N_PAD - C_K
RAW_Q, RAW_K, RAW_V, RAW_O, RAW_Z, RAW_G = (0, 512), (512, 512), (1024, 512), (1536, 512), (2048, 512), (2560, 8)
RAW_AQ, RAW_AK, RAW_AV, RAW_AZ = (2568, 512), (3080, 128), (3208, 128), (3336, 512)
N_RAW = 3848

TOKEN_SPLITS = 2
OUT_SPLITS = 4
VMEM_LIMIT_BYTES = 56 * 1024 * 1024
NEG_INF = float("-inf")


def _dot(a, b):
    return jnp.dot(a, b, preferred_element_type=F32)


def _dot_nt(a, b):
    return lax.dot_general(a, b, (((1,), (1,)), ((), ())), preferred_element_type=F32)


def _layer_norm(x, g, b):
    mu = jnp.mean(x, axis=-1, keepdims=True)
    xc = x - mu
    var = jnp.mean(xc * xc, axis=-1, keepdims=True)
    return xc * lax.rsqrt(var + LN_EPS) * g + b


def _log_sigmoid(x):
    return jnp.minimum(x, 0.0) - jnp.log1p(jnp.exp(-jnp.abs(x)))


def _times_sigmoid(h, half_x):
    return 0.5 * (h * jnp.tanh(half_x) + h)


def _silu_of_twice(half_x):
    return half_x * jnp.tanh(half_x) + half_x


def _iota2(shape, dim):
    return lax.broadcasted_iota(jnp.int32, shape, dim)


T_FEATURES = ((RAW_G, C_G, 1.0), (RAW_Q, C_Q, 1.0), (RAW_AQ, C_AQ, A_DH ** -0.5), (RAW_AV, C_AV, 1.0), (RAW_V, C_V, 1.0),
              (RAW_O, C_O, 0.5), (RAW_Z, C_Z, 0.5), (RAW_AZ, C_AZ, 0.5))
KEY_FEATURES = ((RAW_K, C_K, 1.0), (RAW_AK, C_AK, 1.0))


def _weights(wt_ref, raw):
    return wt_ref[raw[0]:raw[0] + raw[1], :].astype(BF16)


def _project(token_sets, wt_ref, b_ref):
    for _, u_ref in token_sets:
        u_ref[:, C_G:C_K] = jnp.zeros((u_ref.shape[0], C_K - C_G), F32)
    for raw, dst, scale in T_FEATURES + KEY_FEATURES:
        w = _weights(wt_ref, raw)
        for hb, u_ref in token_sets:
            res = _dot_nt(hb, w) + b_ref[:, raw[0]:raw[0] + raw[1]]
            u_ref[:, dst:dst + raw[1]] = res if scale == 1.0 else res * scale


def _project_both(hb, tok, wt_ref, bcol_ref, utb_ref, utf_ref):
    reps = hb.shape[0] // 128
    assert RAW_Z[0] + RAW_Z[1] == RAW_G[0]
    for raw, dst, scale in ((RAW_Z[0], RAW_Z[1] + RAW_G[1]), C_Z, 0.5), *T_FEATURES:
        if raw in (RAW_Z, RAW_G):
            continue
        res = _dot_nt(_weights(wt_ref, raw), hb) + jnp.concatenate([bcol_ref[raw[0]:raw[0] + raw[1], :]] * reps, axis=1)
        if raw[1] > RAW_Z[1]:
            utf_ref[C_G - N_T_BF16:C_G - N_T_BF16 + RAW_G[1], tok] = res[RAW_Z[1]:]
            res = res[0:RAW_Z[1]]
        if scale != 1.0:
            res = res * scale
        if dst < N_T_BF16:
            utb_ref[dst:dst + res.shape[0], tok] = res.astype(BF16)
        else:
            utf_ref[dst - N_T_BF16:dst - N_T_BF16 + res.shape[0], tok] = res


def _project_keys(hb, tok, wt_ref, brow_ref, ukey_ref):
    for raw, dst, _ in KEY_FEATURES:
        ukey_ref[tok, dst - C_K:dst - C_K + raw[1]] = (_dot_nt(hb, _weights(wt_ref, raw))
                                                       + brow_ref[:, raw[0]:raw[0] + raw[1]])


def _gate_rows(gates):
    return _gate_scan(jnp.concatenate([gates[r0:r0 + CHUNK].T[0:8, :] for r0 in range(0, gates.shape[0], CHUNK)],
                                      axis=0))


def _gate_rows_t(gates_t):
    return _gate_scan(jnp.concatenate([gates_t[:, c0:c0 + CHUNK] for c0 in range(0, gates_t.shape[1], CHUNK)],
                                      axis=0))


def _gate_scan(g_t):
    x = _log_sigmoid(g_t)
    lane = _iota2(x.shape, 1)
    shift = 1
    while shift < x.shape[1]:
        x = x + jnp.where(lane >= shift, pltpu.roll(x, shift, 1), 0.0)
        shift *= 2
    return g_t, x


CT_ROWS = M_DH + 8


def _mlstm_scores_and_state(q_tb, kb, v_t, li_row, b_row, m_old, ct_aug, key_ok, n_valid):
    n = kb.shape[0]
    row = _iota2((n, n), 0)
    col = _iota2((n, n), 1)
    a_row = li_row - b_row
    a_keys = jnp.broadcast_to(jnp.sum(jnp.where(row == col, a_row, 0.0), axis=1, keepdims=True), (n, n))
    a_t = jnp.where(key_ok, a_keys, NEG_INF)
    mm = jnp.maximum(m_old, jnp.max(a_t, axis=0, keepdims=True))
    w_t = jnp.exp(a_t - mm)
    lane = _iota2((1, n), 1)
    last = n_valid - 1
    mm_last = jnp.max(jnp.where(lane == last, mm, NEG_INF), axis=1, keepdims=True)
    m_new = jnp.sum(jnp.where(lane == last, b_row, 0.0), axis=1, keepdims=True) + mm_last
    w_state = jnp.exp(a_row - mm_last)
    if n_valid < n:
        w_state = jnp.where(lane < n_valid, w_state, 0.0)
    decay = jnp.exp(m_old - mm_last)
    scores_t = _dot(kb, q_tb)
    inter = _dot(ct_aug.astype(BF16), q_tb)
    ones_row = jnp.where(_iota2((CT_ROWS - M_DH, n), 0) == 0, w_state, 0.0)
    vtw = jnp.concatenate([v_t * w_state, ones_row], axis=0).astype(BF16)
    ct_aug_new = decay * ct_aug + _dot(vtw, kb)
    s_inter = jnp.exp(m_old - mm)
    floor = jnp.exp(-(b_row + mm))
    return (scores_t, w_t, v_t.astype(BF16), inter, s_inter, floor), ct_aug_new, m_new


def _mlstm_weighted_values(scores_t, w_t, vtb, inter, s_inter, floor):
    qkw_t = scores_t * w_t
    den = jnp.sum(qkw_t, axis=0, keepdims=True)
    num_t = _dot(vtb, qkw_t.astype(BF16))
    return num_t, den, inter, s_inter, floor


def _mlstm_finish(num_t, den, inter, s_inter, floor):
    num_t = num_t + inter[0:M_DH] * s_inter
    den = den + inter[M_DH:M_DH + 1] * s_inter
    return num_t * (1.0 / jnp.maximum(jnp.abs(den), floor))


def _mlstm_gate_head(h, half_o, half_z, norm_g, axis):
    hh = _times_sigmoid(h, half_o)
    mu = jnp.mean(hh, axis=axis, keepdims=True)
    hc = hh - mu
    var = jnp.mean(hc * hc, axis=axis, keepdims=True)
    return hc * lax.rsqrt(var + LN_EPS) * norm_g * _silu_of_twice(half_z)


def _gate_mix(h_m, half_o, half_z, o_a, half_az, norm_g):
    parts = []
    for hd in range(M_HEADS):
        sl = slice(hd * M_DH, (hd + 1) * M_DH)
        parts.append(_mlstm_gate_head(h_m[:, sl], half_o[:, sl], half_z[:, sl], norm_g[:, sl], axis=-1).astype(BF16))
    parts.append((o_a * _silu_of_twice(half_az)).astype(BF16))
    return jnp.concatenate(parts, axis=-1)


def _out_and_norm(hp, mix, wo_ref, g, b):
    z = DN_ALPHA * hp + _dot(mix, wo_ref[...])
    return _layer_norm(z, g, b)


def _small_projections_kernel(meta_ref, xs_ref, wt_ref, bias_ref, g0_ref, b0_ref,
                              ct0_ref, m0_ref, km_ref, vm_ref, vmt_ref, hs_ref, us_ref, u_ref):
    hs = _layer_norm(xs_ref[...].reshape(hs_ref.shape), g0_ref[...], b0_ref[...])
    hs_ref[...] = hs
    meta = jnp.concatenate([meta_ref[...], jnp.zeros((CHUNK - N_META, D_MODEL), F32)], axis=0)
    hp = _layer_norm(meta, g0_ref[...], b0_ref[...])
    _project(((hs.astype(BF16), us_ref), (hp.astype(BF16), u_ref)), wt_ref, bias_ref)
    row = _iota2((CHUNK, CHUNK), 0)
    col = _iota2((CHUNK, CHUNK), 1)
    key_ok = (row <= col) & (row < N_META)
    li_rows, b_rows = _gate_rows(u_ref[:, C_G:C_G + 128])
    zero_m = jnp.zeros((1, 128), F32)
    zero_ct = jnp.zeros((CT_ROWS, M_DH), F32)
    m0_ref[...] = jnp.zeros(m0_ref.shape, F32)
    for hd in range(M_HEADS):
        q = u_ref[:, C_Q + hd * M_DH:C_Q + (hd + 1) * M_DH]
        k = u_ref[:, C_K + hd * M_DH:C_K + (hd + 1) * M_DH] * (M_DH ** -0.5)
        v = u_ref[:, C_V + hd * M_DH:C_V + (hd + 1) * M_DH]
        _, ct_new, m_new = _mlstm_scores_and_state(
            q.T.astype(BF16), k.astype(BF16), v.T, li_rows[hd:hd + 1], b_rows[M_HEADS + hd:M_HEADS + hd + 1],
            zero_m, zero_ct, key_ok, N_META)
        ct0_ref[hd] = ct_new
        m0_ref[hd:hd + 1, :] = jnp.broadcast_to(m_new, (1, 128))
    km_ref[...] = u_ref[0:N_META, C_AK:C_AK + KV_W]
    vm_ref[...] = u_ref[0:N_META, C_AV:C_AV + KV_W]
    vmt_ref[...] = u_ref[:, C_AV:C_AV + KV_W].T[:, 0:N_META]


def _keep_kv_half(x, kv):
    low = _iota2(x.shape, 1) < A_DH
    return jnp.where(low if kv == 0 else ~low, x, 0.0)


def _swa_weighted_values_t(scores, values_t, sink, own_ok, prev_ok):
    s_own = jnp.where(own_ok, scores[0], NEG_INF)
    s_prev = jnp.where(prev_ok, scores[1], NEG_INF)
    s_meta = scores[2]
    mx = jnp.maximum(jnp.maximum(jnp.max(s_own, axis=0, keepdims=True),
                                 jnp.max(s_prev, axis=0, keepdims=True)),
                     jnp.maximum(jnp.max(s_meta, axis=0, keepdims=True), sink))
    p_own, p_prev, p_meta = (jnp.exp(s - mx) for s in (s_own, s_prev, s_meta))
    den = (jnp.sum(p_own, axis=0, keepdims=True) + jnp.sum(p_prev, axis=0, keepdims=True)
           + jnp.sum(p_meta, axis=0, keepdims=True) + jnp.exp(sink - mx))
    o_t = (_dot(values_t[0], p_own.astype(BF16)) + _dot(values_t[1], p_prev.astype(BF16))
           + _dot(values_t[2], p_meta.astype(BF16)))
    return o_t, den


def _prompt_kernel(sink_ref, x_ref, wt_ref, bcol_ref, brow_ref, g0_ref, b0_ref,
                   ct0_ref, m0_ref, km_ref, vm_ref, vmt_ref, ng_ref, wo_ref, lng_ref, lnb_ref,
                   y_ref, pk_ref, pv_ref, pc_ref, pn_ref, pm_ref,
                   utb_ref, utf_ref, ukey_ref, hp_ref, mix_ref, ct_ref, kprev_ref, vtprev_ref, li_ref, cumf_ref,
                   *, tb):
    j = pl.program_id(1)
    n_chunks = tb // CHUNK

    @pl.when(j == 0)
    def _():
        ct_ref[...] = ct0_ref[...]
        pm_ref[0] = m0_ref[...]
        kprev_ref[...] = jnp.zeros(kprev_ref.shape, F32)
        vtprev_ref[...] = jnp.zeros(vtprev_ref.shape, F32)
        pk_ref[0, 0:N_META, :] = km_ref[...]
        pv_ref[0, 0:N_META, :] = vm_ref[...]

    for t0 in range(0, tb, tb // TOKEN_SPLITS):
        tok = slice(t0, t0 + tb // TOKEN_SPLITS)
        hp = _layer_norm(x_ref[0, tok, :], g0_ref[...], b0_ref[...])
        hp_ref[tok, :] = hp
        hb = hp.astype(BF16)
        _project_keys(hb, tok, wt_ref, brow_ref, ukey_ref)
        _project_both(hb, tok, wt_ref, bcol_ref, utb_ref, utf_ref)

    def feat(c0, n=M_DH):
        return slice(c0 - N_T_BF16, c0 - N_T_BF16 + n)

    li_ref[...], cumf_ref[...] = _gate_rows_t(utf_ref[feat(C_G, 8), :])

    key = _iota2((CHUNK, CHUNK), 0)
    query = _iota2((CHUNK, CHUNK), 1)
    causal = key <= query

    def chunk_rows(ci):
        return slice(ci * CHUNK, (ci + 1) * CHUNK)

    def attention_scores(ci):
        rows = chunk_rows(ci)
        k_own = ukey_ref[rows, C_AK - C_K:C_AK - C_K + KV_W]
        vt_own = utf_ref[feat(C_AV, KV_W), rows]
        k_prev = kprev_ref[...]
        vt_prev = vtprev_ref[...]
        prev_ok = (key > query) & ((j * n_chunks + ci) > 0)
        keys_kv = [tuple(_keep_kv_half(x, kv).astype(BF16) for x in (k_own, k_prev, km_ref[...]))
                   for kv in range(A_KV_HEADS)]
        values_kv = [tuple(x[kv * A_DH:(kv + 1) * A_DH].astype(BF16) for x in (vt_own, vt_prev, vmt_ref[...]))
                     for kv in range(A_KV_HEADS)]
        a_scores = []
        for hd in range(A_HEADS):
            kv = hd // A_GROUP
            q0 = C_AQ + (hd - kv) * A_DH
            q_win = utb_ref[q0:q0 + 2 * A_DH, rows]
            a_scores.append(tuple(_dot(kk, q_win) for kk in keys_kv[kv]))
        kprev_ref[...] = k_own
        vtprev_ref[...] = vt_own
        return prev_ok, a_scores, values_kv

    def mlstm_first_stage(ci):
        rows = chunk_rows(ci)
        gate_rows = slice(ci * 8, (ci + 1) * 8)
        li_rows = li_ref[gate_rows, :]
        b_rows = cumf_ref[gate_rows, :]
        m_carry = []
        for hd in range(M_HEADS):
            q_tb = utb_ref[C_Q + hd * M_DH:C_Q + (hd + 1) * M_DH, rows]
            kb = (ukey_ref[rows, hd * M_DH:(hd + 1) * M_DH] * (M_DH ** -0.5)).astype(BF16)
            v_t = utf_ref[feat(C_V + hd * M_DH), rows]
            carry_hd, ct_new, m_new = _mlstm_scores_and_state(
                q_tb, kb, v_t, li_rows[hd:hd + 1], b_rows[M_HEADS + hd:M_HEADS + hd + 1],
                pm_ref[0, hd:hd + 1, :], ct_ref[hd], causal, CHUNK)
            ct_ref[hd] = ct_new
            pm_ref[0, hd:hd + 1, :] = jnp.broadcast_to(m_new, (1, 128))
            m_carry.append(carry_hd)
        return m_carry

    def mlstm_second_stage(m_carry):
        return [_mlstm_weighted_values(*c) for c in m_carry]

    def chunk_finish(ci, prev_ok, a_scores, values_kv, m_carry):
        rows = chunk_rows(ci)
        a_out = [_swa_weighted_values_t(a_scores[hd], values_kv[hd // A_GROUP], sink_ref[hd], causal, prev_ok)
                 for hd in range(A_HEADS)]

        mix_t = []
        for hd in range(M_HEADS):
            mix_t.append(_mlstm_gate_head(_mlstm_finish(*m_carry[hd]), utf_ref[feat(C_O + hd * M_DH), rows],
                                          utf_ref[feat(C_Z + hd * M_DH), rows],
                                          ng_ref[hd * M_DH:(hd + 1) * M_DH, :], axis=0))
        for tile in range(A_HEADS // 2):
            o_t = jnp.concatenate([o * (1.0 / den) for o, den in a_out[2 * tile:2 * tile + 2]], axis=0)
            mix_t.append(o_t * _silu_of_twice(utf_ref[feat(C_AZ + tile * 128), rows]))
        mix_rows = [x_t.T.astype(BF16) for x_t in mix_t]
        for i, x in enumerate(mix_rows):
            mix_ref[rows, i * 128:(i + 1) * 128] = x

    for ci in range(n_chunks):
        attn = attention_scores(ci)
        m_carry = mlstm_second_stage(mlstm_first_stage(ci))
        chunk_finish(ci, *attn, m_carry)

    for t0 in range(0, tb, tb // OUT_SPLITS):
        tok = slice(t0, t0 + tb // OUT_SPLITS)
        y_ref[0, tok, :] = _out_and_norm(hp_ref[tok, :], mix_ref[tok, :], wo_ref, lng_ref[...], lnb_ref[...])

    @pl.when(j == pl.num_programs(1) - 1)
    def _():
        pk_ref[0, N_META:N_BUF, :] = ukey_ref[tb - WINDOW:tb, C_AK - C_K:C_AK - C_K + KV_W]
        pv_ref[0, N_META:N_BUF, :] = utf_ref[feat(C_AV, KV_W), tb - WINDOW:tb].T
        for hd in range(M_HEADS):
            pc_ref[0, hd] = ct_ref[hd, 0:M_DH, :].T
            pn_ref[0, hd:hd + 1, :] = ct_ref[hd, M_DH:M_DH + 1, :]
        pn_ref[0, M_HEADS:8, :] = jnp.zeros((8 - M_HEADS, M_DH), F32)


def _const_spec(shape):
    return pl.BlockSpec(shape, lambda *_: (0,) * len(shape))


def _small_projections(meta_tokens, x_sample, ln0_g, ln0_b, w_t, b_all):
    rows = x_sample.shape[0] * x_sample.shape[1]
    return pl.pallas_call(
        _small_projections_kernel,
        out_shape=(jax.ShapeDtypeStruct((M_HEADS, CT_ROWS, M_DH), F32),
                   jax.ShapeDtypeStruct((8, 128), F32),
                   jax.ShapeDtypeStruct((N_META, KV_W), F32),
                   jax.ShapeDtypeStruct((N_META, KV_W), F32),
                   jax.ShapeDtypeStruct((KV_W, N_META), F32),
                   jax.ShapeDtypeStruct((rows, D_MODEL), F32),
                   jax.ShapeDtypeStruct((rows, N_PAD), F32)),
        scratch_shapes=[pltpu.VMEM((CHUNK, N_PAD), F32)],
        compiler_params=pltpu.CompilerParams(vmem_limit_bytes=VMEM_LIMIT_BYTES),
        name="small_projections",
    )(meta_tokens.astype(F32), x_sample, w_t, b_all, ln0_g, ln0_b)


def _prompt_path(x_prompt, meta_state, ln0_g, ln0_b, w_t, b_all, sinks, norm_g, w_o, ln_g, ln_b, tb=512):
    batch, seq, _ = x_prompt.shape
    ct0, m0, km, vm, vmt = meta_state

    nj = seq // tb
    in_specs = [
        pl.BlockSpec(memory_space=pltpu.SMEM),
        pl.BlockSpec((1, tb, D_MODEL), lambda b, j: (b, j, 0)),
        pl.BlockSpec((N_RAW, D_MODEL), lambda b, j: (0, 0), pipeline_mode=pl.Buffered(1)),
        _const_spec((N_RAW, 128)), _const_spec((1, N_RAW)),
        _const_spec((1, D_MODEL)), _const_spec((1, D_MODEL)),
        _const_spec((M_HEADS, CT_ROWS, M_DH)), _const_spec((8, 128)),
        _const_spec((N_META, KV_W)), _const_spec((N_META, KV_W)), _const_spec((KV_W, N_META)),
        _const_spec((D_MLSTM, 128)),
        _const_spec((D_MODEL, D_MODEL)),
        _const_spec((1, D_MODEL)), _const_spec((1, D_MODEL)),
    ]
    out_specs = [
        pl.BlockSpec((1, tb, D_MODEL), lambda b, j: (b, j, 0)),
        pl.BlockSpec((1, N_BUF, KV_W), lambda b, j: (b, 0, 0)),
        pl.BlockSpec((1, N_BUF, KV_W), lambda b, j: (b, 0, 0)),
        pl.BlockSpec((1, M_HEADS, M_DH, M_DH), lambda b, j: (b, 0, 0, 0)),
        pl.BlockSpec((1, 8, M_DH), lambda b, j: (b, 0, 0)),
        pl.BlockSpec((1, 8, 128), lambda b, j: (b, 0, 0)),
    ]
    out_shape = (
        jax.ShapeDtypeStruct((batch, seq, D_MODEL), F32),
        jax.ShapeDtypeStruct((batch, N_BUF, KV_W), F32),
        jax.ShapeDtypeStruct((batch, N_BUF, KV_W), F32),
        jax.ShapeDtypeStruct((batch, M_HEADS, M_DH, M_DH), F32),
        jax.ShapeDtypeStruct((batch, 8, M_DH), F32),
        jax.ShapeDtypeStruct((batch, 8, 128), F32),
    )
    y, pk, pv, pc, pn, pm = pl.pallas_call(
        functools.partial(_prompt_kernel, tb=tb),
        grid=(batch, nj),
        in_specs=in_specs,
        out_specs=out_specs,
        out_shape=out_shape,
        scratch_shapes=[pltpu.VMEM((N_T_BF16, tb), BF16), pltpu.VMEM((N_T - N_T_BF16, tb), F32),
                        pltpu.VMEM((tb, N_KEYS), F32),
                        pltpu.VMEM((tb, D_MODEL), F32), pltpu.VMEM((tb, D_MODEL), BF16),
                        pltpu.VMEM((M_HEADS, CT_ROWS, M_DH), F32),
                        pltpu.VMEM((CHUNK, KV_W), F32), pltpu.VMEM((KV_W, CHUNK), F32),
                        pltpu.VMEM((tb // CHUNK * 8, CHUNK), F32), pltpu.VMEM((tb // CHUNK * 8, CHUNK), F32)],
        compiler_params=pltpu.CompilerParams(dimension_semantics=("arbitrary", "arbitrary"),
                                             vmem_limit_bytes=VMEM_LIMIT_BYTES),
        name="prompt_layer",
    )(sinks, x_prompt, w_t, jnp.broadcast_to(b_all.reshape(N_RAW, 1), (N_RAW, 128)), b_all, ln0_g, ln0_b,
      ct0, m0, km, vm, vmt, jnp.broadcast_to(norm_g.reshape(D_MLSTM, 1), (D_MLSTM, 128)), w_o, ln_g, ln_b)
    pk = pk.reshape(1, batch, N_BUF, A_KV_HEADS, A_DH)
    pv = pv.reshape(1, batch, N_BUF, A_KV_HEADS, A_DH)
    return y, pk, pv, pc[None], pn[:, :M_HEADS][None], pm[:, :M_HEADS, 0][None]


SEQ_PER_GROUP = 32
SWA_SEQ_PER_STEP = 8
SEQ_BATCH = 8


def _segment_last(x, pos, seg_len):
    n = x.shape[1]
    step = 1
    while step < seg_len:
        x = jnp.where((pos // step) % 2 == 0, pltpu.roll(x, n - step, 1), x)
        step *= 2
    return x


def _sample_mlstm_kernel(q_ref, k_ref, v_ref, g_ref, c_ref, n_ref, m_ref,
                         h_ref, cn_ref, nn_ref, mn_ref, inter_ref, *, dec_seq):
    g_t = g_ref[...].T[0:8, :]
    for hd in range(M_HEADS):
        _sample_mlstm_head(hd, g_t, q_ref, k_ref, v_ref, c_ref, n_ref, m_ref,
                           h_ref, cn_ref, nn_ref, mn_ref, inter_ref, dec_seq)


def _sample_mlstm_head(hd, g_t, q_ref, k_ref, v_ref, c_ref, n_ref, m_ref,
                       h_ref, cn_ref, nn_ref, mn_ref, inter_ref, dec_seq):
    sl = slice(hd * M_DH, (hd + 1) * M_DH)
    n = CHUNK
    nb = n // dec_seq
    q = q_ref[:, sl]
    k = k_ref[:, sl] * (M_DH ** -0.5)
    v = v_ref[:, sl]
    qb = q.astype(BF16)
    kb = k.astype(BF16)
    vb = v.astype(BF16)

    li_row = g_t[hd:hd + 1, :]
    lf_row = _log_sigmoid(g_t[M_HEADS + hd:M_HEADS + hd + 1, :])
    lane = _iota2((1, n), 1)
    pos = lane % dec_seq
    b_row = lf_row
    shift = 1
    while shift < dec_seq:
        b_row = b_row + jnp.where(pos >= shift, pltpu.roll(b_row, shift, 1), 0.0)
        shift *= 2

    key = _iota2((n, n), 0)
    query = _iota2((n, n), 1)
    key_ok = (key <= query) & (key // dec_seq == query // dec_seq)
    a_row = li_row - b_row
    a_keys = jnp.broadcast_to(jnp.sum(jnp.where(key == query, a_row, 0.0), axis=1, keepdims=True), (n, n))
    a_t = jnp.where(key_ok, a_keys, NEG_INF)
    seq_of_lane = _iota2((nb, n), 1) // dec_seq == _iota2((nb, n), 0)
    first_lane = _iota2((nb, n), 1) == _iota2((nb, n), 0) * dec_seq
    m_old = jnp.sum(jnp.where(seq_of_lane, m_ref[:, hd:hd + 1], 0.0), axis=0, keepdims=True)
    mm = jnp.maximum(m_old, jnp.max(a_t, axis=0, keepdims=True))
    w_t = jnp.exp(a_t - mm)
    mm_last = _segment_last(mm, pos, dec_seq)
    m_new = _segment_last(b_row, pos, dec_seq) + mm_last
    w_state = jnp.exp(a_row - mm_last)
    decay = jnp.exp(m_old - mm_last)
    mn_ref[:, hd:hd + 1] = jnp.sum(jnp.where(first_lane, m_new, 0.0), axis=1, keepdims=True)

    qkw_t = _dot_nt(kb, qb) * w_t
    den = jnp.sum(qkw_t, axis=0, keepdims=True)
    num_t = _dot(v.T.astype(BF16), qkw_t.astype(BF16))

    n_seq = n_ref[:, hd, :]
    decay_seq = jnp.sum(jnp.where(first_lane, decay, 0.0), axis=1, keepdims=True)
    nn_ref[:, hd, :] = decay_seq * n_seq + _dot(jnp.where(seq_of_lane, w_state, 0.0).astype(BF16), kb)
    expand = jnp.where(_iota2((n, nb), 0) // dec_seq == _iota2((n, nb), 1), 1.0, 0.0).astype(BF16)
    n_rows = _dot(expand, n_seq.astype(BF16))
    qn_col = jnp.sum(q * n_rows, axis=1, keepdims=True)
    qn = jnp.sum(jnp.where(key == query, qn_col, 0.0), axis=0, keepdims=True)

    kwt = k.T * w_state
    decay_rows = jnp.broadcast_to(jnp.sum(jnp.where(key == query, decay, 0.0), axis=1, keepdims=True), (n, n))
    lane_seq = query // dec_seq
    low_rows = _iota2((8, M_DH), 0) < dec_seq
    assert 8 % dec_seq == 0 and 8 // dec_seq == 2
    for pair in range(nb // 2):
        q8 = q[8 * pair:8 * pair + 8].astype(BF16)
        inter_ref[8 * pair:8 * pair + 8, :] = jnp.where(low_rows, _dot(q8, c_ref[2 * pair, hd].astype(BF16)),
                                                        _dot(q8, c_ref[2 * pair + 1, hd].astype(BF16)))
    for s0 in range(0, nb, SEQ_BATCH):
        seqs = range(s0, s0 + SEQ_BATCH)
        updates = [_dot(jnp.where(lane_seq == s, kwt, 0.0).astype(BF16), vb) for s in seqs]
        for s, upd in zip(seqs, updates):
            cn_ref[s, hd] = decay_rows[s * dec_seq:s * dec_seq + 1, :] * c_ref[s, hd] + upd

    s_inter = jnp.exp(m_old - mm)
    num_t = num_t + inter_ref[...].T * s_inter
    den = den + qn * s_inter
    h_ref[:, sl] = (num_t * (1.0 / jnp.maximum(jnp.abs(den), jnp.exp(-(b_row + mm))))).T


def _sample_swa_kernel(aq_ref, kn_ref, vn_ref, ck_ref, cv_ref, sink_ref, o_ref, nk_ref, nv_ref, *, dec_seq):
    n_rows = A_HEADS * dec_seq
    t_c = _iota2((n_rows, N_BUF), 0) % dec_seq
    i_c = _iota2((n_rows, N_BUF), 1)
    ok_c = (i_c < N_META) | (i_c > t_c + N_META)
    t_n = _iota2((n_rows, 8), 0) % dec_seq
    i_n = _iota2((n_rows, 8), 1)
    ok_n = (i_n <= t_n) & (i_n < dec_seq)
    sink = sink_ref[:, 0:1]
    seqs = range(ck_ref.shape[0])
    rows_of = lambda s: slice(s * dec_seq, (s + 1) * dec_seq)

    low = _iota2((len(seqs) * dec_seq, 128), 1) < A_DH
    head_q = []
    for hd in range(A_HEADS):
        tile = aq_ref[:, (hd // 2) * 128:(hd // 2 + 1) * 128]
        piece = jnp.where(low if hd % 2 == 0 else ~low, tile, 0.0)
        head_q.append(piece if hd % 2 == hd // A_GROUP else pltpu.roll(piece, A_DH, 1))
    pad_rows = jnp.zeros((8 - dec_seq, KV_W), F32)
    new_k = [jnp.concatenate([kn_ref[rows_of(s), :], pad_rows], axis=0).astype(BF16) for s in seqs]
    new_v = [jnp.concatenate([vn_ref[rows_of(s), :], pad_rows], axis=0).astype(BF16) for s in seqs]

    scores = []
    for s in seqs:
        qz = jnp.concatenate([q[rows_of(s), :] for q in head_q], axis=0).astype(BF16)
        scores.append((_dot_nt(qz, ck_ref[s].astype(BF16)), _dot_nt(qz, new_k[s])))
    probs = []
    for s_c, s_n in scores:
        s_c = jnp.where(ok_c, s_c, NEG_INF)
        s_n = jnp.where(ok_n, s_n, NEG_INF)
        mx = jnp.maximum(jnp.maximum(jnp.max(s_c, axis=1, keepdims=True),
                                     jnp.max(s_n, axis=1, keepdims=True)), sink)
        p_c = jnp.exp(s_c - mx)
        p_n = jnp.exp(s_n - mx)
        den = jnp.sum(p_c, axis=1, keepdims=True) + jnp.sum(p_n, axis=1, keepdims=True) + jnp.exp(sink - mx)
        probs.append((p_c.astype(BF16), p_n.astype(BF16), den))
    outs = [_dot(p_c, cv_ref[s].astype(BF16)) + _dot(p_n, new_v[s])
            for s, (p_c, p_n, _) in zip(seqs, probs)]
    low4 = _iota2((dec_seq, 128), 1) < A_DH
    for s, o, (_, _, den) in zip(seqs, outs, probs):
        o = o / den
        for pair in range(A_HEADS // 2):
            even, odd = (o[(2 * pair + e) * dec_seq:(2 * pair + e + 1) * dec_seq, :] for e in range(2))
            if pair // (A_GROUP // 2) == 0:
                odd = pltpu.roll(odd, A_DH, 1)
            else:
                even = pltpu.roll(even, A_DH, 1)
            o_ref[rows_of(s), pair * 128:(pair + 1) * 128] = jnp.where(low4, even, odd)
    for s in seqs:
        for cache_ref, new_ref, out_ref in ((ck_ref, kn_ref, nk_ref), (cv_ref, vn_ref, nv_ref)):
            out_ref[s, 0:N_META, :] = cache_ref[s, 0:N_META, :]
            out_ref[s, N_META:N_BUF - dec_seq, :] = cache_ref[s, N_META + dec_seq:N_BUF, :]
            out_ref[s, N_BUF - dec_seq:N_BUF, :] = new_ref[rows_of(s), :]


def _sample_out_kernel(hs_ref, half_o_ref, half_z_ref, half_az_ref, hm_ref, oa_ref, ng_ref, wo_ref, lng_ref, lnb_ref,
                       y_ref):
    mix = _gate_mix(hm_ref[...], half_o_ref[...], half_z_ref[...], oa_ref[...], half_az_ref[...], ng_ref[...])
    y_ref[...] = _out_and_norm(hs_ref[...], mix, wo_ref, lng_ref[...], lnb_ref[...]).reshape(y_ref.shape)


def _sample_path(hs, u, db, dec_seq, cache_k, cache_v, state_c, state_n, state_m, a_sinks, norm_g, w_o, ln_g, ln_b):
    rows = db * dec_seq
    params = pltpu.CompilerParams(vmem_limit_bytes=VMEM_LIMIT_BYTES)

    n_groups = db // SEQ_PER_GROUP

    def col_spec(col0):
        assert col0 % D_MLSTM == 0
        return pl.BlockSpec((CHUNK, D_MLSTM), lambda g: (g, col0 // D_MLSTM))

    state_spec = pl.BlockSpec((SEQ_PER_GROUP, M_HEADS, M_DH, M_DH), lambda g: (g, 0, 0, 0))
    vec_spec = pl.BlockSpec((SEQ_PER_GROUP, M_HEADS, M_DH), lambda g: (g, 0, 0))
    m_spec = pl.BlockSpec((SEQ_PER_GROUP, M_HEADS), lambda g: (g, 0))
    h_m, c_new, n_new, m_new = pl.pallas_call(
        functools.partial(_sample_mlstm_kernel, dec_seq=dec_seq),
        grid=(n_groups,),
        in_specs=[col_spec(C_Q), col_spec(C_K), col_spec(C_V),
                  pl.BlockSpec((CHUNK, 128), lambda g: (g, C_G // 128)),
                  state_spec, vec_spec, m_spec],
        out_specs=[pl.BlockSpec((CHUNK, D_MLSTM), lambda g: (g, 0)), state_spec, vec_spec, m_spec],
        out_shape=(jax.ShapeDtypeStruct((rows, D_MLSTM), F32),
                   jax.ShapeDtypeStruct(state_c.shape, F32),
                   jax.ShapeDtypeStruct(state_n.shape, F32),
                   jax.ShapeDtypeStruct(state_m.shape, F32)),
        scratch_shapes=[pltpu.VMEM((CHUNK, M_DH), F32)],
        compiler_params=pltpu.CompilerParams(dimension_semantics=("arbitrary",),
                                             vmem_limit_bytes=VMEM_LIMIT_BYTES),
        name="sample_mlstm",
    )(u, u, u, u, state_c, state_n, state_m)

    n_qrows = A_HEADS * dec_seq
    sink_rows = jnp.broadcast_to(jnp.repeat(a_sinks.astype(F32), dec_seq)[:, None], (n_qrows, 128))
    ck = cache_k.reshape(db, N_BUF, KV_W)
    cv = cache_v.reshape(db, N_BUF, KV_W)
    sb = SWA_SEQ_PER_STEP

    def token_spec(col0, width):
        assert col0 % width == 0
        return pl.BlockSpec((sb * dec_seq, width), lambda i: (i, col0 // width))

    cache_spec = pl.BlockSpec((sb, N_BUF, KV_W), lambda i: (i, 0, 0))
    o_a, nk, nv = pl.pallas_call(
        functools.partial(_sample_swa_kernel, dec_seq=dec_seq),
        grid=(db // sb,),
        in_specs=[token_spec(C_AQ, D_SWA), token_spec(C_AK, KV_W), token_spec(C_AV, KV_W), cache_spec, cache_spec,
                  pl.BlockSpec((n_qrows, 128), lambda i: (0, 0))],
        out_specs=[pl.BlockSpec((sb * dec_seq, D_SWA), lambda i: (i, 0)), cache_spec, cache_spec],
        out_shape=(jax.ShapeDtypeStruct((rows, D_SWA), F32),
                   jax.ShapeDtypeStruct((db, N_BUF, KV_W), F32),
                   jax.ShapeDtypeStruct((db, N_BUF, KV_W), F32)),
        compiler_params=pltpu.CompilerParams(dimension_semantics=("arbitrary",),
                                             vmem_limit_bytes=VMEM_LIMIT_BYTES,
                                             allow_input_fusion=[False, False, False, True, True, False]),
        name="sample_swa",
    )(u, u, u, ck, cv, sink_rows)

    def gate_spec(col0):
        assert col0 % D_MLSTM == 0
        return pl.BlockSpec((rows, D_MLSTM), lambda i: (0, col0 // D_MLSTM))

    def whole(shape):
        return pl.BlockSpec(shape, lambda i: (0,) * len(shape))

    y = pl.pallas_call(
        _sample_out_kernel,
        grid=(1,),
        in_specs=[whole((rows, D_MODEL)), gate_spec(C_O), gate_spec(C_Z), gate_spec(C_AZ),
                  whole((rows, D_MLSTM)), whole((rows, D_SWA)), whole((1, D_MLSTM)), whole((D_MODEL, D_MODEL)),
                  whole((1, D_MODEL)), whole((1, D_MODEL))],
        out_specs=whole((db, dec_seq, D_MODEL)),
        out_shape=jax.ShapeDtypeStruct((db, dec_seq, D_MODEL), F32),
        compiler_params=params,
        name="sample_out",
    )(hs, u, u, u, h_m, o_a, norm_g, w_o, ln_g, ln_b)

    shape5 = (1, db, N_BUF, A_KV_HEADS, A_DH)
    return (y, nk.reshape(shape5), nv.reshape(shape5), c_new[None],
            n_new[None], m_new[None])


def kernel(x_prompt, x_sample, cache_swa_k, cache_swa_v, state_mlstm_c, state_mlstm_n, state_mlstm_m,
           meta_tokens, ln0_g, ln0_b, w_in, b_in, a_sinks, m_norm_g, w_out, ln_g, ln_b):
    assert w_in.shape[0] == DEPTH and x_prompt.shape[-1] == D_MODEL
    w_t = jnp.transpose(w_in[0].astype(F32))
    b_all = b_in[0].astype(F32)[None]
    w_o = w_out[0].astype(BF16)
    g0 = ln0_g.astype(F32)[None]
    b0 = ln0_b.astype(F32)[None]
    lg = ln_g[0].astype(F32)[None]
    lb = ln_b[0].astype(F32)[None]
    norm_g = m_norm_g[0].astype(F32)[None]
    sinks = a_sinks[0].astype(F32)

    *meta_state, hs, u = _small_projections(meta_tokens, x_sample, g0, b0, w_t, b_all)
    y_p, pk, pv, pc, pn, pm = _prompt_path(x_prompt, meta_state, g0, b0, w_t, b_all, sinks, norm_g, w_o, lg, lb)
    y_s, sk, sv, sc, sn, sm = _sample_path(hs, u, x_sample.shape[0], x_sample.shape[1], cache_swa_k[0], cache_swa_v[0],
                                           state_mlstm_c[0], state_mlstm_n[0], state_mlstm_m[0],
                                           sinks, norm_g, w_o, lg, lb)
    return (y_p, y_s, pk, pv, pc, pn, pm, sk, sv, sc, sn, sm)
```
